```python
import jax, jax.numpy as jnp
from jax import lax
import numpy as np

D_MODEL = 2048
BATCH = 8
SEQ = 4096
DEPTH = 1

CHUNK = 64
N_MEM = 256
EPS = 1e-6
D_MIX = D_MODEL
D_POOL = D_MIX // 2
POOL_WINDOWS = (2, 4, 8, 16)
N_POOL_GROUPS = len(POOL_WINDOWS)
POOL_GROUP_DIM = D_POOL // N_POOL_GROUPS
D_SGU = D_MIX - D_POOL
SGU_BLOCK = 128
N_SGU_HEADS = 8
SGU_HEAD_DIM = D_SGU // N_SGU_HEADS
D_IN = D_POOL + 2 * D_SGU
N_XATTN_HEADS = 4
XATTN_HEAD_DIM = D_MODEL // N_XATTN_HEADS
D_FF = ((8 * D_MODEL // 3 + 255) // 256) * 256

kernel_name = "hybrid_pool_sgu_memxattn_block"


def rmsnorm(x, g):
    x32 = x.astype(jnp.float32)
    y = x32 * lax.rsqrt(jnp.mean(x32 * x32, axis=-1, keepdims=True) + EPS)
    return (y * g.astype(jnp.float32)).astype(x.dtype)


def multiscale_pool(a, pool_w, pool_scale):
    B, S, _ = a.shape
    a32 = a.astype(jnp.float32)
    csum = jnp.cumsum(a32, axis=1)
    pos = jnp.arange(1, S + 1, dtype=jnp.float32)[None, :, None]
    outs = []
    for g, w in enumerate(POOL_WINDOWS):
        sl = slice(g * POOL_GROUP_DIM, (g + 1) * POOL_GROUP_DIM)
        c = csum[..., sl]
        prev = jnp.pad(c, ((0, 0), (w, 0), (0, 0)))[:, :S]
        mean = (c - prev) / jnp.minimum(pos, float(w))
        outs.append(mean - a32[..., sl])
    p = jnp.stack(outs, axis=2).astype(a.dtype)
    y = jnp.einsum('bsgc,gcd->bsgd', p, pool_w)
    return y.reshape(B, S, D_POOL) * pool_scale


def spatial_gating(uv, sgu_norm_g, w_spatial, b_spatial):
    B, S, _ = uv.shape
    u, v = uv[..., :D_SGU], uv[..., D_SGU:]
    v = rmsnorm(v, sgu_norm_g)
    v = v.reshape(B, S // SGU_BLOCK, SGU_BLOCK, N_SGU_HEADS, SGU_HEAD_DIM)
    t = jnp.arange(SGU_BLOCK)
    mask = (t[None, :] // CHUNK) <= (t[:, None] // CHUNK)
    ws = jnp.where(mask[None], w_spatial, 0.0)
    mixed = jnp.einsum('hts,bnshc->bnthc', ws, v)
    mixed = mixed + b_spatial.T[None, None, :, :, None]
    return u * mixed.reshape(B, S, D_SGU)


def memory_cross_attention(h, m, w_q, w_k, w_v, w_o):
    B, S, _ = h.shape
    M = m.shape[1]
    q = (h @ w_q).reshape(B, S, N_XATTN_HEADS, XATTN_HEAD_DIM)
    k = (m @ w_k).reshape(B, M, N_XATTN_HEADS, XATTN_HEAD_DIM)
    v = (m @ w_v).reshape(B, M, N_XATTN_HEADS, XATTN_HEAD_DIM)
    s = jnp.einsum('bshd,bmhd->bhsm', q, k).astype(jnp.float32) * (XATTN_HEAD_DIM ** -0.5)
    p = jax.nn.softmax(s, axis=-1).astype(v.dtype)
    o = jnp.einsum('bhsm,bmhd->bshd', p, v).reshape(B, S, D_MODEL)
    return o @ w_o


def swiglu(h, w_gate, w_up, w_down):
    return (jax.nn.silu(h @ w_gate) * (h @ w_up)) @ w_down


def _fwd_setup_inputs(seed: int = 0) -> dict:
    key = jax.random.key(seed)
    ks = jax.random.split(key, 24)
    L = DEPTH
    f32 = jnp.float32

    def nrm(k, shape, scale):
        return jax.random.normal(k, shape, f32) * scale

    def gain(k, shape):
        return 1.0 + 0.02 * jax.random.normal(k, shape, f32)

    return {
        "x": jax.random.normal(ks[0], (BATCH, SEQ, D_MODEL), f32),
        "mem": jax.random.normal(ks[1], (BATCH, N_MEM, D_MODEL), f32),
        "norm_mix_g": gain(ks[2], (L, D_MODEL)),
        "w_in": nrm(ks[3], (L, D_MODEL, D_IN), D_MODEL ** -0.5),
        "pool_w": nrm(ks[4], (L, N_POOL_GROUPS, POOL_GROUP_DIM, POOL_GROUP_DIM), POOL_GROUP_DIM ** -0.5),
        "pool_scale": 1.0 + 0.1 * jax.random.normal(ks[5], (L, D_POOL), f32),
        "sgu_norm_g": gain(ks[6], (L, D_SGU)),
        "w_spatial": nrm(ks[7], (L, N_SGU_HEADS, SGU_BLOCK, SGU_BLOCK), SGU_BLOCK ** -0.5),
        "b_spatial": 1.0 + 0.1 * jax.random.normal(ks[8], (L, N_SGU_HEADS, SGU_BLOCK), f32),
        "w_out": nrm(ks[9], (L, D_MIX, D_MODEL), D_MIX ** -0.5),
        "norm_xattn_g": gain(ks[10], (L, D_MODEL)),
        "norm_mem_g": gain(ks[11], (L, D_MODEL)),
        "w_q": nrm(ks[12], (L, D_MODEL, D_MODEL), D_MODEL ** -0.5),
        "w_k": nrm(ks[13], (L, D_MODEL, D_MODEL), D_MODEL ** -0.5),
        "w_v": nrm(ks[14], (L, D_MODEL, D_MODEL), D_MODEL ** -0.5),
        "w_o": nrm(ks[15], (L, D_MODEL, D_MODEL), D_MODEL ** -0.5),
        "norm_ffn_g": gain(ks[16], (L, D_MODEL)),
        "w_gate": nrm(ks[17], (L, D_MODEL, D_FF), D_MODEL ** -0.5),
        "w_up": nrm(ks[18], (L, D_MODEL, D_FF), D_MODEL ** -0.5),
        "w_down": nrm(ks[19], (L, D_FF, D_MODEL), D_FF ** -0.5),
        "final_norm_g": gain(ks[20], (D_MODEL,)),
    }


def _fwd_reference(x, mem, norm_mix_g, w_in, pool_w, pool_scale, sgu_norm_g, w_spatial,
              b_spatial, w_out, norm_xattn_g, norm_mem_g, w_q, w_k, w_v, w_o,
              norm_ffn_g, w_gate, w_up, w_down, final_norm_g):
    for l in range(DEPTH):
        h = rmsnorm(x, norm_mix_g[l])
        proj = h @ w_in[l]
        y_pool = multiscale_pool(proj[..., :D_POOL], pool_w[l], pool_scale[l])
        y_sgu = spatial_gating(proj[..., D_POOL:], sgu_norm_g[l], w_spatial[l], b_spatial[l])
        x = x + jnp.concatenate([y_pool, y_sgu], axis=-1) @ w_out[l]
        h = rmsnorm(x, norm_xattn_g[l])
        m = rmsnorm(mem, norm_mem_g[l])
        x = x + memory_cross_attention(h, m, w_q[l], w_k[l], w_v[l], w_o[l])
        h = rmsnorm(x, norm_ffn_g[l])
        x = x + swiglu(h, w_gate[l], w_up[l], w_down[l])
    return rmsnorm(x, final_norm_g)


import jax as _jax
import jax.numpy as _jnp

TWIN_FORMAT = 'train_step'
FWD_PARAMS = ['x', 'mem', 'norm_mix_g', 'w_in', 'pool_w', 'pool_scale', 'sgu_norm_g', 'w_spatial', 'b_spatial', 'w_out', 'norm_xattn_g', 'norm_mem_g', 'w_q', 'w_k', 'w_v', 'w_o', 'norm_ffn_g', 'w_gate', 'w_up', 'w_down', 'final_norm_g']
TWIN_WEIGHTS = ['norm_mix_g', 'w_in', 'pool_w', 'pool_scale', 'sgu_norm_g', 'w_spatial', 'b_spatial', 'w_out', 'norm_xattn_g', 'norm_mem_g', 'w_q', 'w_k', 'w_v', 'w_o', 'norm_ffn_g', 'w_gate', 'w_up', 'w_down', 'final_norm_g']
TWIN_DIFF_INPUT = 'x'
TWIN_INPUTS = ['x', 'mem', 'norm_mix_g', 'w_in', 'pool_w', 'pool_scale', 'sgu_norm_g', 'w_spatial', 'b_spatial', 'w_out', 'norm_xattn_g', 'norm_mem_g', 'w_q', 'w_k', 'w_v', 'w_o', 'norm_ffn_g', 'w_gate', 'w_up', 'w_down', 'final_norm_g', 'loss_target', 'm_norm_mix_g', 'm_w_in', 'm_pool_w', 'm_pool_scale', 'm_sgu_norm_g', 'm_w_spatial', 'm_b_spatial', 'm_w_out', 'm_norm_xattn_g', 'm_norm_mem_g', 'm_w_q', 'm_w_k', 'm_w_v', 'm_w_o', 'm_norm_ffn_g', 'm_w_gate', 'm_w_up', 'm_w_down', 'm_final_norm_g', 'v_norm_mix_g', 'v_w_in', 'v_pool_w', 'v_pool_scale', 'v_sgu_norm_g', 'v_w_spatial', 'v_b_spatial', 'v_w_out', 'v_norm_xattn_g', 'v_norm_mem_g', 'v_w_q', 'v_w_k', 'v_w_v', 'v_w_o', 'v_norm_ffn_g', 'v_w_gate', 'v_w_up', 'v_w_down', 'v_final_norm_g']
TWIN_OUTPUTS = ['loss', 'grad_x', 'grad_norm_mix_g', 'grad_w_in', 'grad_pool_w', 'grad_pool_scale', 'grad_sgu_norm_g', 'grad_w_spatial', 'grad_b_spatial', 'grad_w_out', 'grad_norm_xattn_g', 'grad_norm_mem_g', 'grad_w_q', 'grad_w_k', 'grad_w_v', 'grad_w_o', 'grad_norm_ffn_g', 'grad_w_gate', 'grad_w_up', 'grad_w_down', 'grad_final_norm_g', 'delta_norm_mix_g', 'delta_w_in', 'delta_pool_w', 'delta_pool_scale', 'delta_sgu_norm_g', 'delta_w_spatial', 'delta_b_spatial', 'delta_w_out', 'delta_norm_xattn_g', 'delta_norm_mem_g', 'delta_w_q', 'delta_w_k', 'delta_w_v', 'delta_w_o', 'delta_norm_ffn_g', 'delta_w_gate', 'delta_w_up', 'delta_w_down', 'delta_final_norm_g', 'new_m_norm_mix_g', 'new_m_w_in', 'new_m_pool_w', 'new_m_pool_scale', 'new_m_sgu_norm_g', 'new_m_w_spatial', 'new_m_b_spatial', 'new_m_w_out', 'new_m_norm_xattn_g', 'new_m_norm_mem_g', 'new_m_w_q', 'new_m_w_k', 'new_m_w_v', 'new_m_w_o', 'new_m_norm_ffn_g', 'new_m_w_gate', 'new_m_w_up', 'new_m_w_down', 'new_m_final_norm_g', 'new_v_norm_mix_g', 'new_v_w_in', 'new_v_pool_w', 'new_v_pool_scale', 'new_v_sgu_norm_g', 'new_v_w_spatial', 'new_v_b_spatial', 'new_v_w_out', 'new_v_norm_xattn_g', 'new_v_norm_mem_g', 'new_v_w_q', 'new_v_w_k', 'new_v_w_v', 'new_v_w_o', 'new_v_norm_ffn_g', 'new_v_w_gate', 'new_v_w_up', 'new_v_w_down', 'new_v_final_norm_g']
TWIN_LEAF_KINDS = {'loss': 'loss', 'grad_x': 'grad_x', 'grad_norm_mix_g': 'grad_w', 'grad_w_in': 'grad_w', 'grad_pool_w': 'grad_w', 'grad_pool_scale': 'grad_w', 'grad_sgu_norm_g': 'grad_w', 'grad_w_spatial': 'grad_w', 'grad_b_spatial': 'grad_w', 'grad_w_out': 'grad_w', 'grad_norm_xattn_g': 'grad_w', 'grad_norm_mem_g': 'grad_w', 'grad_w_q': 'grad_w', 'grad_w_k': 'grad_w', 'grad_w_v': 'grad_w', 'grad_w_o': 'grad_w', 'grad_norm_ffn_g': 'grad_w', 'grad_w_gate': 'grad_w', 'grad_w_up': 'grad_w', 'grad_w_down': 'grad_w', 'grad_final_norm_g': 'grad_w', 'delta_norm_mix_g': 'delta_w', 'delta_w_in': 'delta_w', 'delta_pool_w': 'delta_w', 'delta_pool_scale': 'delta_w', 'delta_sgu_norm_g': 'delta_w', 'delta_w_spatial': 'delta_w', 'delta_b_spatial': 'delta_w', 'delta_w_out': 'delta_w', 'delta_norm_xattn_g': 'delta_w', 'delta_norm_mem_g': 'delta_w', 'delta_w_q': 'delta_w', 'delta_w_k': 'delta_w', 'delta_w_v': 'delta_w', 'delta_w_o': 'delta_w', 'delta_norm_ffn_g': 'delta_w', 'delta_w_gate': 'delta_w', 'delta_w_up': 'delta_w', 'delta_w_down': 'delta_w', 'delta_final_norm_g': 'delta_w', 'new_m_norm_mix_g': 'new_m', 'new_m_w_in': 'new_m', 'new_m_pool_w': 'new_m', 'new_m_pool_scale': 'new_m', 'new_m_sgu_norm_g': 'new_m', 'new_m_w_spatial': 'new_m', 'new_m_b_spatial': 'new_m', 'new_m_w_out': 'new_m', 'new_m_norm_xattn_g': 'new_m', 'new_m_norm_mem_g': 'new_m', 'new_m_w_q': 'new_m', 'new_m_w_k': 'new_m', 'new_m_w_v': 'new_m', 'new_m_w_o': 'new_m', 'new_m_norm_ffn_g': 'new_m', 'new_m_w_gate': 'new_m', 'new_m_w_up': 'new_m', 'new_m_w_down': 'new_m', 'new_m_final_norm_g': 'new_m', 'new_v_norm_mix_g': 'new_v', 'new_v_w_in': 'new_v', 'new_v_pool_w': 'new_v', 'new_v_pool_scale': 'new_v', 'new_v_sgu_norm_g': 'new_v', 'new_v_w_spatial': 'new_v', 'new_v_b_spatial': 'new_v', 'new_v_w_out': 'new_v', 'new_v_norm_xattn_g': 'new_v', 'new_v_norm_mem_g': 'new_v', 'new_v_w_q': 'new_v', 'new_v_w_k': 'new_v', 'new_v_w_v': 'new_v', 'new_v_w_o': 'new_v', 'new_v_norm_ffn_g': 'new_v', 'new_v_w_gate': 'new_v', 'new_v_w_up': 'new_v', 'new_v_w_down': 'new_v', 'new_v_final_norm_g': 'new_v'}


def _forward(args):
    return _fwd_reference(*[args[k] for k in FWD_PARAMS])


def _output_shape():
    def fwd():
        inp = _fwd_setup_inputs(0)
        return _fwd_reference(*[inp[k] for k in FWD_PARAMS])
    out = _jax.eval_shape(fwd)
    return out.shape, out.dtype

N_MICROBATCH = 1
ADAM_LR = 0.001
ADAM_B1 = 0.9
ADAM_B2 = 0.999
ADAM_EPS = 1e-08
ADAM_WD = 0.01
ADAM_STEP = 10
PER_EXAMPLE_BATCH_AXIS = {'x': 0, 'mem': 0, 'loss_target': 0}
SHARED_INPUTS = []
_WEIGHT_DTYPES = {'norm_mix_g': _jnp.float32, 'w_in': _jnp.float32, 'pool_w': _jnp.float32, 'pool_scale': _jnp.float32, 'sgu_norm_g': _jnp.float32, 'w_spatial': _jnp.float32, 'b_spatial': _jnp.float32, 'w_out': _jnp.float32, 'norm_xattn_g': _jnp.float32, 'norm_mem_g': _jnp.float32, 'w_q': _jnp.float32, 'w_k': _jnp.float32, 'w_v': _jnp.float32, 'w_o': _jnp.float32, 'norm_ffn_g': _jnp.float32, 'w_gate': _jnp.float32, 'w_up': _jnp.float32, 'w_down': _jnp.float32, 'final_norm_g': _jnp.float32}
MOMENT_SCALE = {'norm_mix_g': 8.146338e-02, 'w_in': 6.585873e-02, 'pool_w': 5.569563e-02, 'pool_scale': 5.549467e-02, 'sgu_norm_g': 5.303249e-02, 'w_spatial': 5.386017e-02, 'b_spatial': 6.208083e-02, 'w_out': 7.112465e-02, 'norm_xattn_g': 6.304012e-03, 'norm_mem_g': 9.596139e-03, 'w_q': 6.486517e-03, 'w_k': 6.494033e-03, 'w_v': 6.559713e-03, 'w_o': 6.558209e-03, 'norm_ffn_g': 4.631332e-02, 'w_gate': 2.036582e-02, 'w_up': 1.967479e-02, 'w_down': 3.265713e-02, 'final_norm_g': 1.600653e+01}


def _to_microbatches(a, axis):
    t = _jnp.moveaxis(a, axis, 0)
    t = t.reshape((N_MICROBATCH, t.shape[0] // N_MICROBATCH) + t.shape[1:])
    return _jnp.moveaxis(t, 1, axis + 1)


def setup_inputs(seed: int = 0) -> dict:
    inp = _fwd_setup_inputs(seed)
    key = _jax.random.fold_in(_jax.random.key(seed), 7919)
    shape, _ = _output_shape()
    out = dict(inp)
    out["loss_target"] = _jax.random.normal(_jax.random.fold_in(key, 0), shape, _jnp.float32)
    for i, name in enumerate(TWIN_WEIGHTS):
        w = inp[name].astype(_jnp.float32)
        if MOMENT_SCALE is None:
            s = _jnp.sqrt(_jnp.mean(_jnp.square(w)) + 1e-30)
        else:
            s = MOMENT_SCALE[name]
        km, kv = _jax.random.split(_jax.random.fold_in(key, i + 1))
        out[name] = w
        out["m_" + name] = s * _jax.random.normal(km, w.shape, _jnp.float32)
        out["v_" + name] = (s * s) * _jax.random.uniform(kv, w.shape, _jnp.float32, 0.5, 1.5)
    if N_MICROBATCH > 1:
        for name, axis in PER_EXAMPLE_BATCH_AXIS.items():
            out[name] = _to_microbatches(out[name], axis)
    return {'x': out['x'], 'mem': out['mem'], 'norm_mix_g': out['norm_mix_g'], 'w_in': out['w_in'], 'pool_w': out['pool_w'], 'pool_scale': out['pool_scale'], 'sgu_norm_g': out['sgu_norm_g'], 'w_spatial': out['w_spatial'], 'b_spatial': out['b_spatial'], 'w_out': out['w_out'], 'norm_xattn_g': out['norm_xattn_g'], 'norm_mem_g': out['norm_mem_g'], 'w_q': out['w_q'], 'w_k': out['w_k'], 'w_v': out['w_v'], 'w_o': out['w_o'], 'norm_ffn_g': out['norm_ffn_g'], 'w_gate': out['w_gate'], 'w_up': out['w_up'], 'w_down': out['w_down'], 'final_norm_g': out['final_norm_g'], 'loss_target': out['loss_target'], 'm_norm_mix_g': out['m_norm_mix_g'], 'm_w_in': out['m_w_in'], 'm_pool_w': out['m_pool_w'], 'm_pool_scale': out['m_pool_scale'], 'm_sgu_norm_g': out['m_sgu_norm_g'], 'm_w_spatial': out['m_w_spatial'], 'm_b_spatial': out['m_b_spatial'], 'm_w_out': out['m_w_out'], 'm_norm_xattn_g': out['m_norm_xattn_g'], 'm_norm_mem_g': out['m_norm_mem_g'], 'm_w_q': out['m_w_q'], 'm_w_k': out['m_w_k'], 'm_w_v': out['m_w_v'], 'm_w_o': out['m_w_o'], 'm_norm_ffn_g': out['m_norm_ffn_g'], 'm_w_gate': out['m_w_gate'], 'm_w_up': out['m_w_up'], 'm_w_down': out['m_w_down'], 'm_final_norm_g': out['m_final_norm_g'], 'v_norm_mix_g': out['v_norm_mix_g'], 'v_w_in': out['v_w_in'], 'v_pool_w': out['v_pool_w'], 'v_pool_scale': out['v_pool_scale'], 'v_sgu_norm_g': out['v_sgu_norm_g'], 'v_w_spatial': out['v_w_spatial'], 'v_b_spatial': out['v_b_spatial'], 'v_w_out': out['v_w_out'], 'v_norm_xattn_g': out['v_norm_xattn_g'], 'v_norm_mem_g': out['v_norm_mem_g'], 'v_w_q': out['v_w_q'], 'v_w_k': out['v_w_k'], 'v_w_v': out['v_w_v'], 'v_w_o': out['v_w_o'], 'v_norm_ffn_g': out['v_norm_ffn_g'], 'v_w_gate': out['v_w_gate'], 'v_w_up': out['v_w_up'], 'v_w_down': out['v_w_down'], 'v_final_norm_g': out['v_final_norm_g']}


def _loss(weights, diff, rest, loss_target):
    with _jax.named_scope("forward"):
        args = {**rest, TWIN_DIFF_INPUT: diff, **{k: w.astype(_WEIGHT_DTYPES[k]) for k, w in weights.items()}}
        y = _forward(args)
    with _jax.named_scope("loss_head"):
        err = _jnp.square(y.astype(_jnp.float32) - loss_target)
        return 0.5 * _jnp.sum(_jnp.mean(err, axis=-1)) if err.ndim else 0.5 * err


def _adamw(w, g, m, v):
    m = ADAM_B1 * m + (1.0 - ADAM_B1) * g
    v = ADAM_B2 * v + (1.0 - ADAM_B2) * _jnp.square(g)
    m_hat = m / (1.0 - ADAM_B1 ** ADAM_STEP)
    v_hat = v / (1.0 - ADAM_B2 ** ADAM_STEP)
    delta = -ADAM_LR * (m_hat / (_jnp.sqrt(v_hat) + ADAM_EPS) + ADAM_WD * w)
    return delta, m, v


def reference(x, mem, norm_mix_g, w_in, pool_w, pool_scale, sgu_norm_g, w_spatial, b_spatial, w_out, norm_xattn_g, norm_mem_g, w_q, w_k, w_v, w_o, norm_ffn_g, w_gate, w_up, w_down, final_norm_g, loss_target, m_norm_mix_g, m_w_in, m_pool_w, m_pool_scale, m_sgu_norm_g, m_w_spatial, m_b_spatial, m_w_out, m_norm_xattn_g, m_norm_mem_g, m_w_q, m_w_k, m_w_v, m_w_o, m_norm_ffn_g, m_w_gate, m_w_up, m_w_down, m_final_norm_g, v_norm_mix_g, v_w_in, v_pool_w, v_pool_scale, v_sgu_norm_g, v_w_spatial, v_b_spatial, v_w_out, v_norm_xattn_g, v_norm_mem_g, v_w_q, v_w_k, v_w_v, v_w_o, v_norm_ffn_g, v_w_gate, v_w_up, v_w_down, v_final_norm_g):
    given = dict(x=x, mem=mem, norm_mix_g=norm_mix_g, w_in=w_in, pool_w=pool_w, pool_scale=pool_scale, sgu_norm_g=sgu_norm_g, w_spatial=w_spatial, b_spatial=b_spatial, w_out=w_out, norm_xattn_g=norm_xattn_g, norm_mem_g=norm_mem_g, w_q=w_q, w_k=w_k, w_v=w_v, w_o=w_o, norm_ffn_g=norm_ffn_g, w_gate=w_gate, w_up=w_up, w_down=w_down, final_norm_g=final_norm_g, loss_target=loss_target, m_norm_mix_g=m_norm_mix_g, m_w_in=m_w_in, m_pool_w=m_pool_w, m_pool_scale=m_pool_scale, m_sgu_norm_g=m_sgu_norm_g, m_w_spatial=m_w_spatial, m_b_spatial=m_b_spatial, m_w_out=m_w_out, m_norm_xattn_g=m_norm_xattn_g, m_norm_mem_g=m_norm_mem_g, m_w_q=m_w_q, m_w_k=m_w_k, m_w_v=m_w_v, m_w_o=m_w_o, m_norm_ffn_g=m_norm_ffn_g, m_w_gate=m_w_gate, m_w_up=m_w_up, m_w_down=m_w_down, m_final_norm_g=m_final_norm_g, v_norm_mix_g=v_norm_mix_g, v_w_in=v_w_in, v_pool_w=v_pool_w, v_pool_scale=v_pool_scale, v_sgu_norm_g=v_sgu_norm_g, v_w_spatial=v_w_spatial, v_b_spatial=v_b_spatial, v_w_out=v_w_out, v_norm_xattn_g=v_norm_xattn_g, v_norm_mem_g=v_norm_mem_g, v_w_q=v_w_q, v_w_k=v_w_k, v_w_v=v_w_v, v_w_o=v_w_o, v_norm_ffn_g=v_norm_ffn_g, v_w_gate=v_w_gate, v_w_up=v_w_up, v_w_down=v_w_down, v_final_norm_g=v_final_norm_g)
    weights = {n: given[n] for n in TWIN_WEIGHTS}
    shared = {n: given[n] for n in SHARED_INPUTS}
    per_example = {n: given[n] for n in ['x', 'mem']}
    grad_fn = _jax.value_and_grad(_loss, argnums=(0, 1))

    def one_microbatch(ex, loss_target):
        ex = dict(ex)
        diff = ex.pop(TWIN_DIFF_INPUT)
        return grad_fn(weights, diff, {**shared, **ex}, loss_target)

    if N_MICROBATCH == 1:
        loss, (grad_w, grad_x) = one_microbatch(per_example, given["loss_target"])
    else:
        def body(carry, xs):
            loss_sum, grad_sum = carry
            l_k, (gw_k, gx_k) = one_microbatch(xs[0], xs[1])
            with _jax.named_scope("update"):
                return (loss_sum + l_k, _jax.tree.map(_jnp.add, grad_sum, gw_k)), gx_k

        init = (_jnp.zeros((), _jnp.float32), _jax.tree.map(_jnp.zeros_like, weights))
        (loss, grad_w), grad_x = _jax.lax.scan(body, init, (per_example, given["loss_target"]))
    with _jax.named_scope("update"):
        delta_w, new_m, new_v = {}, {}, {}
        for n in TWIN_WEIGHTS:
            delta_w[n], new_m[n], new_v[n] = _adamw(weights[n], grad_w[n], given["m_" + n], given["v_" + n])
    return (loss, grad_x, *[grad_w[n] for n in TWIN_WEIGHTS], *[delta_w[n] for n in TWIN_WEIGHTS],
            *[new_m[n] for n in TWIN_WEIGHTS], *[new_v[n] for n in TWIN_WEIGHTS])
```

```python
import functools

import jax
import jax.numpy as jnp
from jax import lax
from jax.experimental import pallas as pl
from jax.experimental.pallas import tpu as pltpu

F32 = jnp.float32
BF16 = jnp.bfloat16
MESH = pl.DeviceIdType.MESH

EPS = 1e-6
D_MODEL = 2048
D_POOL = 1024
D_SGU = 1024
POOL_WINDOWS = (2, 4, 8, 16)
POOL_GROUP = 256
POOL_HALO = 16
SGU_BLOCK = 128
SGU_CHUNK = 64
N_SGU_HEADS = 8
N_HEADS = 4
HEAD_DIM = 512
N_DEV = 8

ADAM_LR = 0.001
ADAM_B1 = 0.9
ADAM_B2 = 0.999
ADAM_EPS = 1e-08
ADAM_WD = 0.01
ADAM_STEP = 10

VMEM_LIMIT = 56 * 1024 * 1024


def _cparams(sem=None):
    return pltpu.CompilerParams(dimension_semantics=sem, vmem_limit_bytes=VMEM_LIMIT)


def _sds(shape, dtype):
    return jax.ShapeDtypeStruct(shape, dtype)


def _rowsum8(v):
    r, c = v.shape
    return v.reshape(r // 8, 8, c).sum(axis=0)


_DN = {
    "nn": (((1,), (0,)), ((), ())),
    "nt": (((1,), (1,)), ((), ())),
    "tn": (((0,), (0,)), ((), ())),
}


def _mm(name, a_list, b_list, terms, mode, tm, tn, tk, out_dtypes, epilogue=None, extras=(), n_acc=1):
    a0, b0 = a_list[0], b_list[0]
    if mode == "tn":
        K, M = a0.shape
    else:
        M, K = a0.shape
    N = b0.shape[0] if mode == "nt" else b0.shape[1]
    assert M % tm == 0 and N % tn == 0 and K % tk == 0, (name, M, N, K, tm, tn, tk)
    nk = K // tk
    na, nb, ne, no = len(a_list), len(b_list), len(extras), len(out_dtypes)
    dn = _DN[mode]

    if mode == "tn":
        a_spec = pl.BlockSpec((tk, tm), lambda i, j, k: (k, i))
    else:
        a_spec = pl.BlockSpec((tm, tk), lambda i, j, k: (i, k))
    if mode == "nt":
        b_spec = pl.BlockSpec((tn, tk), lambda i, j, k: (j, k))
    else:
        b_spec = pl.BlockSpec((tk, tn), lambda i, j, k: (k, j))
    o_spec = pl.BlockSpec((tm, tn), lambda i, j, k: (i, j))

    def body(*refs):
        a_refs = refs[:na]
        b_refs = refs[na:na + nb]
        e_refs = refs[na + nb:na + nb + ne]
        o_refs = refs[na + nb + ne:na + nb + ne + no]
        acc_refs = refs[na + nb + ne + no:]
        parts = [None] * n_acc
        for ai, bi, ci in terms:
            d = lax.dot_general(a_refs[ai][...].astype(BF16), b_refs[bi][...].astype(BF16), dn,
                                preferred_element_type=F32)
            parts[ci] = d if parts[ci] is None else parts[ci] + d

        def finish(accs):
            outs = epilogue(accs, [e[...] for e in e_refs]) if epilogue is not None else accs
            for o_ref, v in zip(o_refs, outs):
                o_ref[...] = v.astype(o_ref.dtype)

        if nk == 1:
            finish(parts)
        else:
            k = pl.program_id(2)

            @pl.when(k == 0)
            def _():
                for c in range(n_acc):
                    acc_refs[c][...] = parts[c]

            @pl.when(k > 0)
            def _():
                for c in range(n_acc):
                    acc_refs[c][...] += parts[c]

            @pl.when(k == nk - 1)
            def _():
                finish([acc_refs[c][...] for c in range(n_acc)])

    scratch = [pltpu.VMEM((tm, tn), F32) for _ in range(n_acc)] if nk > 1 else []
    res = pl.pallas_call(
        body, name=name, grid=(M // tm, N // tn, nk),
        in_specs=[a_spec] * na + [b_spec] * nb + [o_spec] * ne,
        out_specs=[o_spec] * no,
        out_shape=[_sds((M, N), dt) for dt in out_dtypes],
        scratch_shapes=scratch,
        compiler_params=_cparams(("parallel", "parallel", "arbitrary")),
    )(*a_list, *b_list, *extras)
    return res


def _mm1(name, a, b, mode, tm, tn, tk, out_dtype, **kw):
    return _mm(name, [a], [b], [(0, 0, 0)], mode, tm, tn, tk, [out_dtype], **kw)[0]


def _rms_fwd(name, x, g, tr):
    S, Dm = x.shape

    def body(x_ref, g_ref, h_ref):
        xv = x_ref[...]
        r = lax.rsqrt(jnp.mean(xv * xv, axis=-1, keepdims=True) + EPS)
        h_ref[...] = (xv * r * g_ref[...]).astype(h_ref.dtype)

    return pl.pallas_call(
        body, name=name, grid=(S // tr,),
        in_specs=[pl.BlockSpec((tr, Dm), lambda i: (i, 0)), pl.BlockSpec((1, Dm), lambda i: (0, 0))],
        out_specs=pl.BlockSpec((tr, Dm), lambda i: (i, 0)),
        out_shape=_sds((S, Dm), BF16),
        compiler_params=_cparams(("parallel",)),
    )(x, g)


def _rms_bwd(name, dh, x, g, dres, tr, want_dx=True):
    S, Dm = x.shape
    nsteps = S // tr

    def body(*refs):
        if want_dx:
            dh_ref, x_ref, g_ref, dres_ref, dx_ref, dxb_ref, dg_ref, acc_ref = refs
        else:
            dh_ref, x_ref, g_ref, dg_ref, acc_ref = refs
        i = pl.program_id(0)
        xv = x_ref[...]
        r = lax.rsqrt(jnp.mean(xv * xv, axis=-1, keepdims=True) + EPS)
        xh = xv * r
        dhv = dh_ref[...]
        part = _rowsum8(dhv * xh)

        @pl.when(i == 0)
        def _():
            acc_ref[...] = part

        @pl.when(i > 0)
        def _():
            acc_ref[...] += part

        @pl.when(i == nsteps - 1)
        def _():
            dg_ref[...] = jnp.sum(acc_ref[...], axis=0, keepdims=True)

        if want_dx:
            dxh = dhv * g_ref[...]
            m = jnp.mean(dxh * xh, axis=-1, keepdims=True)
            dx = dres_ref[...] + r * (dxh - xh * m)
            dx_ref[...] = dx
            dxb_ref[...] = dx.astype(BF16)

    row = pl.BlockSpec((tr, Dm), lambda i: (i, 0))
    vec = pl.BlockSpec((1, Dm), lambda i: (0, 0))
    if want_dx:
        in_specs = [row, row, vec, row]
        out_specs = [row, row, vec]
        out_shape = [_sds((S, Dm), F32), _sds((S, Dm), BF16), _sds((1, Dm), F32)]
        args = (dh, x, g, dres)
    else:
        in_specs = [row, row, vec]
        out_specs = [vec]
        out_shape = [_sds((1, Dm), F32)]
        args = (dh, x, g)
    return pl.pallas_call(
        body, name=name, grid=(nsteps,), in_specs=in_specs, out_specs=out_specs, out_shape=out_shape,
        scratch_shapes=[pltpu.VMEM((8, Dm), F32)],
        compiler_params=_cparams(("arbitrary",)),
    )(*args)


def _final_loss(name, x3, g, target, tr):
    S, Dm = x3.shape
    nsteps = S // tr

    def body(x_ref, g_ref, t_ref, dx_ref, dxb_ref, dg_ref, loss_ref, acc_g, acc_l):
        i = pl.program_id(0)
        xv = x_ref[...]
        gv = g_ref[...]
        r = lax.rsqrt(jnp.mean(xv * xv, axis=-1, keepdims=True) + EPS)
        xh = xv * r
        e = xh * gv - t_ref[...]
        dy = e * (1.0 / Dm)
        lpart = _rowsum8(e * e)
        gpart = _rowsum8(dy * xh)

        @pl.when(i == 0)
        def _():
            acc_g[...] = gpart
            acc_l[...] = lpart

        @pl.when(i > 0)
        def _():
            acc_g[...] += gpart
            acc_l[...] += lpart

        @pl.when(i == nsteps - 1)
        def _():
            dg_ref[...] = jnp.sum(acc_g[...], axis=0, keepdims=True)
            tot = jnp.sum(jnp.sum(acc_l[...], axis=1, keepdims=True), axis=0, keepdims=True)
            loss_ref[...] = tot * (0.5 / Dm)

        dxh = dy * gv
        m = jnp.mean(dxh * xh, axis=-1, keepdims=True)
        dx = r * (dxh - xh * m)
        dx_ref[...] = dx
        dxb_ref[...] = dx.astype(BF16)

    row = pl.BlockSpec((tr, Dm), lambda i: (i, 0))
    vec = pl.BlockSpec((1, Dm), lambda i: (0, 0))
    return pl.pallas_call(
        body, name=name, grid=(nsteps,),
        in_specs=[row, vec, row],
        out_specs=[row, row, vec, pl.BlockSpec((1, 1), lambda i: (0, 0))],
        out_shape=[_sds((S, Dm), F32), _sds((S, Dm), BF16), _sds((1, Dm), F32), _sds((1, 1), F32)],
        scratch_shapes=[pltpu.VMEM((8, Dm), F32), pltpu.VMEM((8, Dm), F32)],
        compiler_params=_cparams(("arbitrary",)),
    )(x3, g, target)


def _softmax_rows(s):
    e = jnp.exp(s - jnp.max(s, axis=-1, keepdims=True))
    return e / jnp.sum(e, axis=-1, keepdims=True)


def _attn_fwd(name, q, k, v, ts):
    S, Dm = q.shape
    M = k.shape[0]
    scale = HEAD_DIM ** -0.5

    def body(q_ref, k_ref, v_ref, o_ref):
        for h in range(N_HEADS):
            sl = slice(h * HEAD_DIM, (h + 1) * HEAD_DIM)
            s = lax.dot_general(q_ref[:, sl], k_ref[:, sl], _DN["nt"], preferred_element_type=F32) * scale
            p = _softmax_rows(s)
            o_ref[:, sl] = jnp.dot(p.astype(BF16), v_ref[:, sl], preferred_element_type=F32).astype(o_ref.dtype)

    row = pl.BlockSpec((ts, Dm), lambda i: (i, 0))
    mem = pl.BlockSpec((M, Dm), lambda i: (0, 0))
    return pl.pallas_call(
        body, name=name, grid=(S // ts,), in_specs=[row, mem, mem], out_specs=row,
        out_shape=_sds((S, Dm), BF16), compiler_params=_cparams(("parallel",)),
    )(q, k, v)


def _attn_bwd(name, q, k, v, do, ts):
    S, Dm = q.shape
    M = k.shape[0]
    scale = HEAD_DIM ** -0.5

    def body(q_ref, k_ref, v_ref, do_ref, dq_ref, dk_ref, dv_ref):
        i = pl.program_id(0)

        @pl.when(i == 0)
        def _():
            dk_ref[...] = jnp.zeros_like(dk_ref)
            dv_ref[...] = jnp.zeros_like(dv_ref)

        for h in range(N_HEADS):
            sl = slice(h * HEAD_DIM, (h + 1) * HEAD_DIM)
            qh = q_ref[:, sl]
            kh = k_ref[:, sl]
            doh = do_ref[:, sl]
            s = lax.dot_general(qh, kh, _DN["nt"], preferred_element_type=F32) * scale
            p = _softmax_rows(s)
            dp = lax.dot_general(doh, v_ref[:, sl], _DN["nt"], preferred_element_type=F32)
            ds = p * (dp - jnp.sum(dp * p, axis=-1, keepdims=True)) * scale
            dsb = ds.astype(BF16)
            dq_ref[:, sl] = jnp.dot(dsb, kh, preferred_element_type=F32).astype(dq_ref.dtype)
            dk_ref[:, sl] += lax.dot_general(dsb, qh, _DN["tn"], preferred_element_type=F32)
            dv_ref[:, sl] += lax.dot_general(p.astype(BF16), doh, _DN["tn"], preferred_element_type=F32)

    row = pl.BlockSpec((ts, Dm), lambda i: (i, 0))
    mem = pl.BlockSpec((M, Dm), lambda i: (0, 0))
    return pl.pallas_call(
        body, name=name, grid=(S // ts,), in_specs=[row, mem, mem, row], out_specs=[row, mem, mem],
        out_shape=[_sds((S, Dm), BF16), _sds((M, Dm), F32), _sds((M, Dm), F32)],
        compiler_params=_cparams(("arbitrary",)),
    )(q, k, v, do)


def _pool_denominators(row0, ts):
    return (row0 + lax.broadcasted_iota(jnp.int32, (ts, 1), 0) + 1).astype(F32)


def _mixer_fwd(name, proj, pool_w, pool_scale, sgu_g, ws, bias_full, ts):
    S = proj.shape[0]
    nblk = ts // SGU_BLOCK
    halo_blocks = ts // POOL_HALO

    def body(proj_ref, halo_ref, pw_ref, sc_ref, g_ref, ws_ref, b_ref, y_ref, p_ref, vn_ref, ext_ref):
        i = pl.program_id(0)
        a = proj_ref[:, 0:D_POOL]
        ext_ref[0:POOL_HALO, :] = jnp.where(i > 0, halo_ref[...], 0.0)
        ext_ref[POOL_HALO:POOL_HALO + ts, :] = a
        pos = _pool_denominators(i * ts, ts)
        for gi, w in enumerate(POOL_WINDOWS):
            cs = slice(gi * POOL_GROUP, (gi + 1) * POOL_GROUP)
            acc = a[:, cs]
            for j in range(1, w):
                acc = acc + ext_ref[POOL_HALO - j:POOL_HALO - j + ts, cs]
            pg = (acc / jnp.minimum(pos, float(w)) - a[:, cs]).astype(BF16)
            p_ref[:, cs] = pg
            ypre = jnp.dot(pg, pw_ref[gi], preferred_element_type=F32)
            y_ref[:, cs] = (ypre * sc_ref[:, cs]).astype(y_ref.dtype)

        v = proj_ref[:, D_POOL + D_SGU:D_POOL + 2 * D_SGU]
        r = lax.rsqrt(jnp.mean(v * v, axis=-1, keepdims=True) + EPS)
        vn_ref[...] = (v * r * g_ref[...]).astype(BF16)
        for n in range(nblk):
            rs = slice(n * SGU_BLOCK, (n + 1) * SGU_BLOCK)
            for h in range(N_SGU_HEADS):
                cs = slice(h * SGU_BLOCK, (h + 1) * SGU_BLOCK)
                mixed = jnp.dot(ws_ref[h], vn_ref[rs, cs], preferred_element_type=F32) + b_ref[:, cs]
                u = proj_ref[rs, D_POOL + h * SGU_BLOCK:D_POOL + (h + 1) * SGU_BLOCK]
                y_ref[rs, D_POOL + h * SGU_BLOCK:D_POOL + (h + 1) * SGU_BLOCK] = (u * mixed).astype(y_ref.dtype)

    return pl.pallas_call(
        body, name=name, grid=(S // ts,),
        in_specs=[
            pl.BlockSpec((ts, D_POOL + 2 * D_SGU), lambda i: (i, 0)),
            pl.BlockSpec((POOL_HALO, D_POOL), lambda i: (jnp.maximum(i * halo_blocks - 1, 0), 0)),
            pl.BlockSpec((4, POOL_GROUP, POOL_GROUP), lambda i: (0, 0, 0)),
            pl.BlockSpec((1, D_POOL), lambda i: (0, 0)),
            pl.BlockSpec((1, D_SGU), lambda i: (0, 0)),
            pl.BlockSpec((N_SGU_HEADS, SGU_BLOCK, SGU_BLOCK), lambda i: (0, 0, 0)),
            pl.BlockSpec((SGU_BLOCK, D_SGU), lambda i: (0, 0)),
        ],
        out_specs=[
            pl.BlockSpec((ts, D_MODEL), lambda i: (i, 0)),
            pl.BlockSpec((ts, D_POOL), lambda i: (i, 0)),
            pl.BlockSpec((ts, D_SGU), lambda i: (i, 0)),
        ],
        out_shape=[_sds((S, D_MODEL), BF16), _sds((S, D_POOL), BF16), _sds((S, D_SGU), BF16)],
        scratch_shapes=[pltpu.VMEM((ts + POOL_HALO, D_POOL), F32)],
        compiler_params=_cparams(("parallel",)),
    )(proj, proj, pool_w, pool_scale, sgu_g, ws, bias_full)


def _mixer_bwd(name, dymix, proj, p, vn, pool_w, pool_scale, sgu_g, ws, bias_full, ts):
    S = proj.shape[0]
    nsteps = S // ts
    nblk = ts // SGU_BLOCK
    halo_blocks = ts // POOL_HALO

    def body(dy_ref, dyh_ref, u_ref, v_ref, p_ref, vn_ref, pw_ref, sc_ref, g_ref, ws_ref, b_ref,
             dproj_ref, dpw_ref, dsc_ref, dg_ref, dws_ref, db_ref,
             ext_ref, dvn_ref, acc_sc, acc_g, acc_b):
        i = pl.program_id(0)

        @pl.when(i == 0)
        def _():
            dpw_ref[...] = jnp.zeros_like(dpw_ref)
            dws_ref[...] = jnp.zeros_like(dws_ref)
            acc_sc[...] = jnp.zeros_like(acc_sc)
            acc_g[...] = jnp.zeros_like(acc_g)
            acc_b[...] = jnp.zeros_like(acc_b)

        pos = _pool_denominators(i * ts, ts)
        pos_h = _pool_denominators((i + 1) * ts, POOL_HALO)
        for gi, w in enumerate(POOL_WINDOWS):
            cs = slice(gi * POOL_GROUP, (gi + 1) * POOL_GROUP)
            pg = p_ref[:, cs]
            wg = pw_ref[gi]
            dyp = dy_ref[:, cs]
            ypre = jnp.dot(pg, wg, preferred_element_type=F32)
            acc_sc[:, cs] += _rowsum8(dyp * ypre)
            dz = (dyp * sc_ref[:, cs]).astype(BF16)
            dpw_ref[gi] += lax.dot_general(pg, dz, _DN["tn"], preferred_element_type=F32)
            dp = lax.dot_general(dz, wg, _DN["nt"], preferred_element_type=F32)
            dzh = (dyh_ref[:, cs] * sc_ref[:, cs]).astype(BF16)
            dph = lax.dot_general(dzh, wg, _DN["nt"], preferred_element_type=F32)
            ext_ref[0:ts, cs] = dp / jnp.minimum(pos, float(w))
            ext_ref[ts:ts + POOL_HALO, cs] = jnp.where(i < nsteps - 1, dph / jnp.minimum(pos_h, float(w)), 0.0)
            acc = ext_ref[0:ts, cs]
            for j in range(1, w):
                acc = acc + ext_ref[j:j + ts, cs]
            dproj_ref[:, cs] = (acc - dp).astype(dproj_ref.dtype)

        for n in range(nblk):
            rs = slice(n * SGU_BLOCK, (n + 1) * SGU_BLOCK)
            for h in range(N_SGU_HEADS):
                cs = slice(h * SGU_BLOCK, (h + 1) * SGU_BLOCK)
                vnb = vn_ref[rs, cs]
                wh = ws_ref[h]
                mixed = jnp.dot(wh, vnb, preferred_element_type=F32) + b_ref[:, cs]
                dys = dy_ref[rs, D_POOL + h * SGU_BLOCK:D_POOL + (h + 1) * SGU_BLOCK]
                dproj_ref[rs, D_POOL + h * SGU_BLOCK:D_POOL + (h + 1) * SGU_BLOCK] = (dys * mixed).astype(dproj_ref.dtype)
                dmix = dys * u_ref[rs, cs]
                acc_b[:, cs] += dmix
                dmb = dmix.astype(BF16)
                dws_ref[h] += lax.dot_general(dmb, vnb, _DN["nt"], preferred_element_type=F32)
                dvn_ref[rs, cs] = lax.dot_general(wh, dmb, _DN["tn"], preferred_element_type=F32)
        v = v_ref[...]
        r = lax.rsqrt(jnp.mean(v * v, axis=-1, keepdims=True) + EPS)
        vh = v * r
        dvn = dvn_ref[...]
        acc_g[...] += _rowsum8(dvn * vh)
        dxh = dvn * g_ref[...]
        m = jnp.mean(dxh * vh, axis=-1, keepdims=True)
        dproj_ref[:, D_POOL + D_SGU:D_POOL + 2 * D_SGU] = (r * (dxh - vh * m)).astype(dproj_ref.dtype)

        @pl.when(i == nsteps - 1)
        def _():
            dsc_ref[...] = jnp.sum(acc_sc[...], axis=0, keepdims=True)
            dg_ref[...] = jnp.sum(acc_g[...], axis=0, keepdims=True)
            t_idx = lax.broadcasted_iota(jnp.int32, (SGU_BLOCK, SGU_BLOCK), 0) // SGU_CHUNK
            s_idx = lax.broadcasted_iota(jnp.int32, (SGU_BLOCK, SGU_BLOCK), 1) // SGU_CHUNK
            mask = s_idx <= t_idx
            for h in range(N_SGU_HEADS):
                cs = slice(h * SGU_BLOCK, (h + 1) * SGU_BLOCK)
                dws_ref[h] = jnp.where(mask, dws_ref[h], 0.0)
                col = jnp.sum(acc_b[:, cs], axis=1, keepdims=True)
                db_ref[h] = jnp.broadcast_to(col, (SGU_BLOCK, SGU_BLOCK))

    const2 = lambda i: (0, 0)
    const3 = lambda i: (0, 0, 0)
    last_halo = S // POOL_HALO - 1
    return pl.pallas_call(
        body, name=name, grid=(nsteps,),
        in_specs=[
            pl.BlockSpec((ts, D_MODEL), lambda i: (i, 0)),
            pl.BlockSpec((POOL_HALO, D_POOL), lambda i: (jnp.minimum((i + 1) * halo_blocks, last_halo), 0)),
            pl.BlockSpec((ts, D_SGU), lambda i: (i, 1)),
            pl.BlockSpec((ts, D_SGU), lambda i: (i, 2)),
            pl.BlockSpec((ts, D_POOL), lambda i: (i, 0)),
            pl.BlockSpec((ts, D_SGU), lambda i: (i, 0)),
            pl.BlockSpec((4, POOL_GROUP, POOL_GROUP), const3),
            pl.BlockSpec((1, D_POOL), const2),
            pl.BlockSpec((1, D_SGU), const2),
            pl.BlockSpec((N_SGU_HEADS, SGU_BLOCK, SGU_BLOCK), const3),
            pl.BlockSpec((SGU_BLOCK, D_SGU), const2),
        ],
        out_specs=[
            pl.BlockSpec((ts, D_POOL + 2 * D_SGU), lambda i: (i, 0)),
            pl.BlockSpec((4, POOL_GROUP, POOL_GROUP), const3),
            pl.BlockSpec((1, D_POOL), const2),
            pl.BlockSpec((1, D_SGU), const2),
            pl.BlockSpec((N_SGU_HEADS, SGU_BLOCK, SGU_BLOCK), const3),
            pl.BlockSpec((N_SGU_HEADS, SGU_BLOCK, SGU_BLOCK), const3),
        ],
        out_shape=[
            _sds((S, D_POOL + 2 * D_SGU), BF16),
            _sds((4, POOL_GROUP, POOL_GROUP), F32),
            _sds((1, D_POOL), F32),
            _sds((1, D_SGU), F32),
            _sds((N_SGU_HEADS, SGU_BLOCK, SGU_BLOCK), F32),
            _sds((N_SGU_HEADS, SGU_BLOCK, SGU_BLOCK), F32),
        ],
        scratch_shapes=[
            pltpu.VMEM((ts + POOL_HALO, D_POOL), F32),
            pltpu.VMEM((ts, D_SGU), F32),
            pltpu.VMEM((8, D_POOL), F32),
            pltpu.VMEM((8, D_SGU), F32),
            pltpu.VMEM((SGU_BLOCK, D_SGU), F32),
        ],
        compiler_params=_cparams(("arbitrary",)),
    )(dymix, dymix, proj, proj, p, vn, pool_w, pool_scale, sgu_g, ws, bias_full)


def _silu_mul(accs, _):
    gt, up = accs
    sig = 1.0 / (1.0 + jnp.exp(-gt))
    return gt, up, gt * sig * up


def _silu_mul_bwd(accs, extras):
    (dact,) = accs
    gt = extras[0].astype(F32)
    up = extras[1].astype(F32)
    sig = 1.0 / (1.0 + jnp.exp(-gt))
    silu = gt * sig
    dgt = dact * up * (sig * (1.0 + gt * (1.0 - sig)))
    dup = dact * silu
    return dgt, dup


def _add_residual(accs, extras):
    return (extras[0] + accs[0],)


def _add_residual_and_cast(accs, extras):
    y = extras[0] + accs[0]
    return y, y


def _local_step(x, mem, target, W, sm):
    S = x.shape[0]
    tm = min(1024, S)
    th = min(512, S)
    ts = min(512, S)
    tr = min(512, S)
    tk_s = min(2048, S)
    M = mem.shape[0]

    h1 = _rms_fwd("rms_mix", x, sm["norm_mix_g"], tr)
    proj = _mm1("proj_in", h1, W["w_in_t"], "nt", tm, 1024, 2048, F32)
    ymix, p, vn = _mixer_fwd("mixer_fwd", proj, W["pool_w"], sm["pool_scale"], sm["sgu_norm_g"],
                             sm["ws_masked"], sm["bias_full"], ts)
    x1 = _mm1("proj_out", ymix, W["w_out"], "nn", tm, 1024, 2048, F32, epilogue=_add_residual, extras=(x,))

    h2 = _rms_fwd("rms_xattn", x1, sm["norm_xattn_g"], tr)
    mb = _rms_fwd("rms_mem", mem, sm["norm_mem_g"], M)
    q = _mm1("proj_q", h2, W["w_q"], "nn", tm, 1024, 2048, BF16)
    kk = _mm1("proj_k", mb, W["w_k"], "nn", M, 1024, 2048, BF16)
    vv = _mm1("proj_v", mb, W["w_v"], "nn", M, 1024, 2048, BF16)
    o = _attn_fwd("attn_fwd", q, kk, vv, ts)
    x2 = _mm1("proj_o", o, W["w_o"], "nn", tm, 1024, 2048, F32, epilogue=_add_residual, extras=(x1,))

    h3 = _rms_fwd("rms_ffn", x2, sm["norm_ffn_g"], tr)
    gt, up, act = _mm("ffn_gate_up", [h3], [W["w_gate_t"], W["w_up_t"]], [(0, 0, 0), (0, 1, 1)], "nt",
                      tm, 512, 2048, [BF16, BF16, BF16], epilogue=_silu_mul, n_acc=2)
    x3 = _mm1("ffn_down", act, W["w_down"], "nn", th, 1024, 2816, F32, epilogue=_add_residual, extras=(x2,))

    dx3, dx3b, d_final_g, loss = _final_loss("final_loss", x3, sm["final_norm_g"], target, tr)

    dgt, dup = _mm("ffn_down_dgrad", [dx3b], [W["w_down"]], [(0, 0, 0)], "nt", tm, 512, 2048, [BF16, BF16],
                   epilogue=_silu_mul_bwd, extras=(gt, up))
    g_w_down = _mm1("ffn_down_wgrad", act, dx3b, "tn", 1408, 1024, tk_s, BF16)
    g_w_gate_t = _mm1("ffn_gate_wgrad", dgt, h3, "tn", 1408, 1024, tk_s, BF16)
    g_w_up_t = _mm1("ffn_up_wgrad", dup, h3, "tn", 1408, 1024, tk_s, BF16)
    dh3 = _mm("ffn_gate_up_dgrad", [dgt, dup], [W["w_gate_t"], W["w_up_t"]], [(0, 0, 0), (1, 1, 0)], "nn",
              th, 1024, 1408, [F32])[0]
    dx2, dx2b, d_ffn_g = _rms_bwd("rms_ffn_bwd", dh3, x2, sm["norm_ffn_g"], dx3, tr)

    do = _mm1("proj_o_dgrad", dx2b, W["w_o"], "nt", tm, 1024, 2048, BF16)
    g_w_o = _mm1("proj_o_wgrad", o, dx2b, "tn", 1024, 1024, tk_s, BF16)
    dq, dk, dv = _attn_bwd("attn_bwd", q, kk, vv, do, ts)
    g_w_q = _mm1("proj_q_wgrad", h2, dq, "tn", 1024, 1024, tk_s, BF16)
    dh2 = _mm1("proj_q_dgrad", dq, W["w_q"], "nt", tm, 1024, 2048, F32)
    dx1, dx1b, d_xattn_g = _rms_bwd("rms_xattn_bwd", dh2, x1, sm["norm_xattn_g"], dx2, tr)
    g_w_k = _mm1("proj_k_wgrad", mb, dk, "tn", 1024, 1024, M, BF16)
    g_w_v = _mm1("proj_v_wgrad", mb, dv, "tn", 1024, 1024, M, BF16)
    dmb = _mm("proj_kv_dgrad", [dk, dv], [W["w_k"], W["w_v"]], [(0, 0, 0), (1, 1, 0)], "nt",
              M, 1024, 2048, [F32])[0]
    (d_mem_g,) = _rms_bwd("rms_mem_bwd", dmb, mem, sm["norm_mem_g"], None, M, want_dx=False)

    dymix = _mm1("proj_out_dgrad", dx1b, W["w_out"], "nt", tm, 1024, 2048, F32)
    g_w_out = _mm1("proj_out_wgrad", ymix, dx1b, "tn", 1024, 1024, tk_s, BF16)
    dproj, d_pool_w, d_pool_scale, d_sgu_g, d_ws, d_b = _mixer_bwd(
        "mixer_bwd", dymix, proj, p, vn, W["pool_w"], sm["pool_scale"], sm["sgu_norm_g"],
        sm["ws_masked"], sm["bias_full"], ts)
    g_w_in_t = _mm1("proj_in_wgrad", dproj, h1, "tn", 1024, 1024, tk_s, BF16)
    dh1 = _mm1("proj_in_dgrad", dproj, W["w_in_t"], "nn", tm, 1024, 3072, F32)
    grad_x, _, d_mix_g = _rms_bwd("rms_mix_bwd", dh1, x, sm["norm_mix_g"], dx1, tr)

    big = dict(w_in_t=g_w_in_t, w_out=g_w_out, w_q=g_w_q, w_k=g_w_k, w_v=g_w_v, w_o=g_w_o,
               w_gate_t=g_w_gate_t, w_up_t=g_w_up_t, w_down=g_w_down)
    small = dict(norm_mix_g=d_mix_g, pool_w=d_pool_w, pool_scale=d_pool_scale, sgu_norm_g=d_sgu_g,
                 w_spatial=d_ws, b_spatial=d_b[:, :, 0], norm_xattn_g=d_xattn_g, norm_mem_g=d_mem_g,
                 norm_ffn_g=d_ffn_g, final_norm_g=d_final_g)
    return loss, grad_x, big, small


_ANY = pl.BlockSpec(memory_space=pl.ANY)


def _mesh_pos():
    return lax.axis_index("x"), lax.axis_index("y"), lax.axis_index("c")


def _all_gather(name, shards):
    n = len(shards)

    def body(*refs):
        ins = refs[:n]
        outs = refs[n:2 * n]
        send_sems, recv_sems, local_sems = refs[2 * n:]
        x, y, c = _mesh_pos()
        me, sibling = (x, y, c), (x, y, 1 - c)
        chips = [(1 - x, y), (x, 1 - y), (1 - x, 1 - y)]

        def copy(a, k, block, to, src=None):
            bx, by, bc = block
            dst = outs[a].at[4 * bx + 2 * by + bc]
            return pltpu.make_async_remote_copy(
                src_ref=dst if src is None else src, dst_ref=dst,
                send_sem=send_sems.at[a, k], recv_sem=recv_sems.at[a, k],
                device_id=to, device_id_type=MESH)

        mine = [pltpu.make_async_copy(ins[a], outs[a].at[4 * x + 2 * y + c], local_sems.at[a]) for a in range(n)]
        for cp in mine:
            cp.start()
        started = []
        for a in range(n):
            first = [copy(a, 0, me, sibling, src=ins[a])]
            first += [copy(a, 1 + j, me, (*chip, c), src=ins[a]) for j, chip in enumerate(chips)]
            for cp in first:
                cp.start()
            started += first
        for a in range(n):
            for j, chip in enumerate(chips):
                copy(a, 1 + j, (*chip, c), me).wait_recv()
                fwd = copy(a, 4 + j, (*chip, c), sibling)
                fwd.start()
                started.append(fwd)
        for a in range(n):
            copy(a, 0, sibling, me).wait_recv()
            for j, chip in enumerate(chips):
                copy(a, 4 + j, (*chip, 1 - c), me).wait_recv()
        for cp in started:
            cp.wait_send()
        for cp in mine:
            cp.wait()

    return pl.pallas_call(
        body, name=name,
        in_specs=[_ANY] * n, out_specs=[_ANY] * n,
        out_shape=[_sds((N_DEV,) + s.shape, s.dtype) for s in shards],
        scratch_shapes=[pltpu.SemaphoreType.DMA((n, 7)), pltpu.SemaphoreType.DMA((n, 7)),
                        pltpu.SemaphoreType.DMA((n,))],
    )(*shards)


def _pair_exchange(name, grads):
    n = len(grads)
    views = [g.reshape(4, 2, g.shape[0] // N_DEV, g.shape[1]) for g in grads]

    def body(*refs):
        ins = refs[:n]
        mine = refs[n:2 * n]
        theirs = refs[2 * n:3 * n]
        send_sems, recv_sems, local_sems = refs[3 * n:]
        x, y, c = _mesh_pos()
        remote, local = [], []
        for a in range(n):
            for k in range(4):
                local.append(pltpu.make_async_copy(ins[a].at[k, c], mine[a].at[k], local_sems.at[a, k]))
                remote.append(pltpu.make_async_remote_copy(
                    src_ref=ins[a].at[k, 1 - c], dst_ref=theirs[a].at[k],
                    send_sem=send_sems.at[a, k], recv_sem=recv_sems.at[a, k],
                    device_id=(x, y, 1 - c), device_id_type=MESH))
        for cp in remote:
            cp.start()
        for cp in local:
            cp.start()
        for cp in remote:
            cp.wait()
        for cp in local:
            cp.wait()

    half = [_sds((4,) + v.shape[2:], v.dtype) for v in views]
    res = pl.pallas_call(
        body, name=name,
        in_specs=[_ANY] * n, out_specs=[_ANY] * (2 * n), out_shape=half + half,
        scratch_shapes=[pltpu.SemaphoreType.DMA((n, 4)), pltpu.SemaphoreType.DMA((n, 4)),
                        pltpu.SemaphoreType.DMA((n, 4))],
    )(*views)
    return res[:n], res[n:]


def _chip_exchange(name, pair_sums):
    n = len(pair_sums)

    def body(*refs):
        ins = refs[:n]
        outs = refs[n:2 * n]
        send_sems, recv_sems, local_sems = refs[2 * n:]
        x, y, c = _mesh_pos()
        my_chip = 2 * x + y
        chips = [(1 - x, y), (x, 1 - y), (1 - x, 1 - y)]
        remote, local = [], []
        for a in range(n):
            local.append(pltpu.make_async_copy(ins[a].at[my_chip], outs[a].at[my_chip], local_sems.at[a]))
            for j, (cx, cy) in enumerate(chips):
                remote.append(pltpu.make_async_remote_copy(
                    src_ref=ins[a].at[2 * cx + cy], dst_ref=outs[a].at[my_chip],
                    send_sem=send_sems.at[a, j], recv_sem=recv_sems.at[a, j],
                    device_id=(cx, cy, c), device_id_type=MESH))
        for cp in remote:
            cp.start()
        for cp in local:
            cp.start()
        for cp in remote:
            cp.wait_send()
        for a in range(n):
            for j, (cx, cy) in enumerate(chips):
                pltpu.make_async_remote_copy(
                    src_ref=ins[a].at[my_chip], dst_ref=outs[a].at[2 * cx + cy],
                    send_sem=send_sems.at[a, j], recv_sem=recv_sems.at[a, j],
                    device_id=(cx, cy, c), device_id_type=MESH).wait_recv()
        for cp in local:
            cp.wait()

    return pl.pallas_call(
        body, name=name,
        in_specs=[_ANY] * n, out_specs=[_ANY] * n,
        out_shape=[_sds(s.shape, s.dtype) for s in pair_sums],
        scratch_shapes=[pltpu.SemaphoreType.DMA((n, 3)), pltpu.SemaphoreType.DMA((n, 3)),
                        pltpu.SemaphoreType.DMA((n,))],
    )(*pair_sums)


def _add_pairs(name, a, b, tr):
    _, r, C = a.shape
    spec = pl.BlockSpec((1, tr, C), lambda k, t: (k, t, 0))

    def body(a_ref, b_ref, o_ref):
        o_ref[...] = (a_ref[...].astype(F32) + b_ref[...].astype(F32)).astype(o_ref.dtype)

    return pl.pallas_call(
        body, name=name, grid=(4, r // tr), in_specs=[spec, spec], out_specs=spec,
        out_shape=_sds(a.shape, a.dtype), compiler_params=_cparams(("parallel", "parallel")),
    )(a, b)


def _sum_leading(name, parts, tr, out_dtype=F32):
    n, r, C = parts.shape

    def body(p_ref, o_ref):
        acc = p_ref[0].astype(F32)
        for k in range(1, n):
            acc = acc + p_ref[k].astype(F32)
        o_ref[...] = acc.astype(o_ref.dtype)

    return pl.pallas_call(
        body, name=name, grid=(r // tr,),
        in_specs=[pl.BlockSpec((n, tr, C), lambda t: (0, t, 0))],
        out_specs=pl.BlockSpec((tr, C), lambda t: (t, 0)),
        out_shape=_sds((r, C), out_dtype), compiler_params=_cparams(("parallel",)),
    )(parts)


def _row_tile(r):
    for t in (512, 384, 352, 256, 128, 64, 32, 16, 8):
        if r % t == 0:
            return t
    return r


def _adamw(name, w, g, m, v):
    R, C = w.shape
    tr = _row_tile(R)
    c1 = 1.0 - ADAM_B1 ** ADAM_STEP
    c2 = 1.0 - ADAM_B2 ** ADAM_STEP

    def body(w_ref, g_ref, m_ref, v_ref, d_ref, nm_ref, nv_ref):
        gv = g_ref[...]
        nm = ADAM_B1 * m_ref[...] + (1.0 - ADAM_B1) * gv
        nv = ADAM_B2 * v_ref[...] + (1.0 - ADAM_B2) * (gv * gv)
        m_hat = nm / c1
        v_hat = nv / c2
        d_ref[...] = -ADAM_LR * (m_hat / (jnp.sqrt(v_hat) + ADAM_EPS) + ADAM_WD * w_ref[...])
        nm_ref[...] = nm
        nv_ref[...] = nv

    spec = pl.BlockSpec((tr, C), lambda i: (i, 0))
    return pl.pallas_call(
        body, name=name, grid=(R // tr,), in_specs=[spec] * 4, out_specs=[spec] * 3,
        out_shape=[_sds((R, C), F32)] * 3, compiler_params=_cparams(("parallel",)),
    )(w, g, m, v)


_BIG = ("w_in_t", "w_out", "w_q", "w_k", "w_v", "w_o", "w_gate_t", "w_up_t", "w_down")
_SMALL = ("norm_mix_g", "pool_scale", "sgu_norm_g", "w_spatial", "b_spatial", "norm_xattn_g",
          "norm_mem_g", "norm_ffn_g", "final_norm_g")
_LANES = 128


_PACK_ROWS = 512


def _pack(parts):
    rows = [p.reshape(-1, _LANES) for p in parts]
    n = sum(r.shape[0] for r in rows)
    pad = -n % _PACK_ROWS
    if pad:
        rows.append(jnp.zeros((pad, _LANES), rows[0].dtype))
    return jnp.concatenate(rows, axis=0)


def _unpack(packed, like):
    out, row = [], 0
    for ref in like:
        rows = ref.size // _LANES
        out.append(packed[row:row + rows].reshape(ref.shape))
        row += rows
    return out


def kernel(x, mem, norm_mix_g, w_in, pool_w, pool_scale, sgu_norm_g, w_spatial, b_spatial, w_out, norm_xattn_g, norm_mem_g, w_q, w_k, w_v, w_o, norm_ffn_g, w_gate, w_up, w_down, final_norm_g, loss_target, m_norm_mix_g, m_w_in, m_pool_w, m_pool_scale, m_sgu_norm_g, m_w_spatial, m_b_spatial, m_w_out, m_norm_xattn_g, m_norm_mem_g, m_w_q, m_w_k, m_w_v, m_w_o, m_norm_ffn_g, m_w_gate, m_w_up, m_w_down, m_final_norm_g, v_norm_mix_g, v_w_in, v_pool_w, v_pool_scale, v_sgu_norm_g, v_w_spatial, v_b_spatial, v_w_out, v_norm_xattn_g, v_norm_mem_g, v_w_q, v_w_k, v_w_v, v_w_o, v_norm_ffn_g, v_w_gate, v_w_up, v_w_down, v_final_norm_g):
    args = dict(locals())
    names = ("norm_mix_g", "w_in", "pool_w", "pool_scale", "sgu_norm_g", "w_spatial", "b_spatial", "w_out",
             "norm_xattn_g", "norm_mem_g", "w_q", "w_k", "w_v", "w_o", "norm_ffn_g", "w_gate", "w_up",
             "w_down", "final_norm_g")
    w = {k: args[k] for k in names}
    m = {k: args["m_" + k] for k in names}
    v = {k: args["v_" + k] for k in names}
    my_dev = 4 * lax.axis_index("x") + 2 * lax.axis_index("y") + lax.axis_index("c")

    shards = dict(
        w_in_t=w["w_in"][0].T, w_out=w["w_out"][0], w_q=w["w_q"][0], w_k=w["w_k"][0], w_v=w["w_v"][0],
        w_o=w["w_o"][0], w_gate_t=w["w_gate"][0].T, w_up_t=w["w_up"][0].T, w_down=w["w_down"][0])
    order = _BIG + ("pool_w",)
    send = [shards[k].astype(BF16) for k in _BIG] + [w["pool_w"][0].reshape(4 * 32, POOL_GROUP).astype(BF16)]
    gathered = _all_gather("gather_weights", send)
    W = {k: g.reshape(-1, g.shape[-1]) for k, g in zip(order, gathered)}
    W["pool_w"] = W["pool_w"].reshape(N_DEV, 4, 32, POOL_GROUP).transpose(1, 0, 2, 3).reshape(4, POOL_GROUP, POOL_GROUP)

    t = jnp.arange(SGU_BLOCK)
    mask = (t[None, :] // SGU_CHUNK) <= (t[:, None] // SGU_CHUNK)
    sm = dict(
        norm_mix_g=w["norm_mix_g"], pool_scale=w["pool_scale"], sgu_norm_g=w["sgu_norm_g"],
        norm_xattn_g=w["norm_xattn_g"], norm_mem_g=w["norm_mem_g"], norm_ffn_g=w["norm_ffn_g"],
        final_norm_g=w["final_norm_g"].reshape(1, D_MODEL),
        ws_masked=jnp.where(mask[None], w["w_spatial"][0], 0.0).astype(BF16),
        bias_full=jnp.repeat(w["b_spatial"][0].T, SGU_BLOCK, axis=1))

    loss_part, grad_x, big, small = _local_step(x[0], mem[0], loss_target[0], W, sm)
    loss = lax.psum(loss_part[0, 0], ("x", "y", "c"))

    glist = [big[k] for k in _BIG]
    mine, theirs = _pair_exchange("grad_pair_exchange", glist)
    pair = [_add_pairs("grad_pair_sum_" + k, a, b, _row_tile(a.shape[1])) for k, a, b in zip(_BIG, mine, theirs)]
    landed = _chip_exchange("grad_chip_exchange", pair)
    gsum = {k: _sum_leading("grad_chip_sum_" + k, l, _row_tile(l.shape[1])) for k, l in zip(_BIG, landed)}
    grads = dict(
        w_in=gsum["w_in_t"].T[None], w_out=gsum["w_out"][None], w_q=gsum["w_q"][None], w_k=gsum["w_k"][None],
        w_v=gsum["w_v"][None], w_o=gsum["w_o"][None], w_gate=gsum["w_gate_t"].T[None],
        w_up=gsum["w_up_t"].T[None], w_down=gsum["w_down"][None])

    small_parts = [small[k] for k in _SMALL] + [small["pool_w"]]
    packed = _pack(small_parts)
    (allp,) = _all_gather("gather_small_grads", [packed])
    total = _sum_leading("sum_small_grads", allp, _row_tile(packed.shape[0]))
    small_sum = _unpack(total, [w[k] for k in _SMALL] + [small["pool_w"]])
    for k, g in zip(_SMALL, small_sum[:-1]):
        grads[k] = g
    full_pool = small_sum[-1]
    grads["pool_w"] = lax.dynamic_slice_in_dim(full_pool, my_dev * 32, 32, axis=1)[None]

    delta, new_m, new_v = {}, {}, {}
    for k in ("w_in", "w_out", "w_q", "w_k", "w_v", "w_o", "w_gate", "w_up", "w_down", "pool_w"):
        shp = w[k].shape
        two_d = (-1, shp[-1])
        d_, m_, v_ = _adamw("adamw_" + k, w[k].reshape(two_d), grads[k].reshape(two_d), m[k].reshape(two_d),
                            v[k].reshape(two_d))
        delta[k], new_m[k], new_v[k] = d_.reshape(shp), m_.reshape(shp), v_.reshape(shp)
    d_, m_, v_ = _adamw("adamw_small", _pack([w[k] for k in _SMALL]), _pack([grads[k] for k in _SMALL]),
                        _pack([m[k] for k in _SMALL]), _pack([v[k] for k in _SMALL]))
    like = [w[k] for k in _SMALL]
    for k, a, b, c_ in zip(_SMALL, _unpack(d_, like), _unpack(m_, like), _unpack(v_, like)):
        delta[k], new_m[k], new_v[k] = a, b, c_

    outs = [loss, grad_x[None]]
    outs += [grads[k].reshape(w[k].shape) for k in names]
    outs += [delta[k] for k in names]
    outs += [new_m[k] for k in names]
    outs += [new_v[k] for k in names]
    return tuple(outs)
```

```python
import functools

import jax
import jax.numpy as jnp
from jax import lax
from jax.experimental import pallas as pl
from jax.experimental.pallas import tpu as pltpu
from jax.experimental.pallas import tpu_sc as plsc

F32 = jnp.float32
BF16 = jnp.bfloat16
MESH = pl.DeviceIdType.MESH

EPS = 1e-6
D_MODEL = 2048
D_POOL = 1024
D_SGU = 1024
POOL_WINDOWS = (2, 4, 8, 16)
POOL_GROUP = 256
POOL_HALO = 16
SGU_BLOCK = 128
SGU_CHUNK = 64
N_SGU_HEADS = 8
N_HEADS = 4
HEAD_DIM = 512
N_DEV = 8

ADAM_LR = 0.001
ADAM_B1 = 0.9
ADAM_B2 = 0.999
ADAM_EPS = 1e-08
ADAM_WD = 0.01
ADAM_STEP = 10

VMEM_LIMIT = 56 * 1024 * 1024


def _cparams(sem=None):
    return pltpu.CompilerParams(dimension_semantics=sem, vmem_limit_bytes=VMEM_LIMIT)


def _sds(shape, dtype):
    return jax.ShapeDtypeStruct(shape, dtype)


def _rowsum8(v):
    r, c = v.shape
    return v.reshape(r // 8, 8, c).sum(axis=0)


_DN = {
    "nn": (((1,), (0,)), ((), ())),
    "nt": (((1,), (1,)), ((), ())),
    "tn": (((0,), (0,)), ((), ())),
}


def _mm(name, a_list, b_list, terms, mode, tm, tn, tk, out_dtypes, epilogue=None, extras=(), n_acc=1):
    a0, b0 = a_list[0], b_list[0]
    if mode == "tn":
        K, M = a0.shape
    else:
        M, K = a0.shape
    N = b0.shape[0] if mode == "nt" else b0.shape[1]
    assert M % tm == 0 and N % tn == 0 and K % tk == 0, (name, M, N, K, tm, tn, tk)
    nk = K // tk
    na, nb, ne, no = len(a_list), len(b_list), len(extras), len(out_dtypes)
    dn = _DN[mode]

    if mode == "tn":
        a_spec = pl.BlockSpec((tk, tm), lambda i, j, k: (k, i))
    else:
        a_spec = pl.BlockSpec((tm, tk), lambda i, j, k: (i, k))
    if mode == "nt":
        b_spec = pl.BlockSpec((tn, tk), lambda i, j, k: (j, k))
    else:
        b_spec = pl.BlockSpec((tk, tn), lambda i, j, k: (k, j))
    o_spec = pl.BlockSpec((tm, tn), lambda i, j, k: (i, j))

    def body(*refs):
        a_refs = refs[:na]
        b_refs = refs[na:na + nb]
        e_refs = refs[na + nb:na + nb + ne]
        o_refs = refs[na + nb + ne:na + nb + ne + no]
        acc_refs = refs[na + nb + ne + no:]
        parts = [None] * n_acc
        for ai, bi, ci in terms:
            d = lax.dot_general(a_refs[ai][...].astype(BF16), b_refs[bi][...].astype(BF16), dn,
                                preferred_element_type=F32)
            parts[ci] = d if parts[ci] is None else parts[ci] + d

        def finish(accs):
            outs = epilogue(accs, [e[...] for e in e_refs]) if epilogue is not None else accs
            for o_ref, v in zip(o_refs, outs):
                o_ref[...] = v.astype(o_ref.dtype)

        if nk == 1:
            finish(parts)
        else:
            k = pl.program_id(2)

            @pl.when(k == 0)
            def _():
                for c in range(n_acc):
                    acc_refs[c][...] = parts[c]

            @pl.when(k > 0)
            def _():
                for c in range(n_acc):
                    acc_refs[c][...] += parts[c]

            @pl.when(k == nk - 1)
            def _():
                finish([acc_refs[c][...] for c in range(n_acc)])

    scratch = [pltpu.VMEM((tm, tn), F32) for _ in range(n_acc)] if nk > 1 else []
    res = pl.pallas_call(
        body, name=name, grid=(M // tm, N // tn, nk),
        in_specs=[a_spec] * na + [b_spec] * nb + [o_spec] * ne,
        out_specs=[o_spec] * no,
        out_shape=[_sds((M, N), dt) for dt in out_dtypes],
        scratch_shapes=scratch,
        compiler_params=_cparams(("parallel", "parallel", "arbitrary")),
    )(*a_list, *b_list, *extras)
    return res


def _mm1(name, a, b, mode, tm, tn, tk, out_dtype, **kw):
    return _mm(name, [a], [b], [(0, 0, 0)], mode, tm, tn, tk, [out_dtype], **kw)[0]


def _rms_fwd(name, x, g, tr):
    S, Dm = x.shape

    def body(x_ref, g_ref, h_ref):
        xv = x_ref[...]
        r = lax.rsqrt(jnp.mean(xv * xv, axis=-1, keepdims=True) + EPS)
        h_ref[...] = (xv * r * g_ref[...]).astype(h_ref.dtype)

    return pl.pallas_call(
        body, name=name, grid=(S // tr,),
        in_specs=[pl.BlockSpec((tr, Dm), lambda i: (i, 0)), pl.BlockSpec((1, Dm), lambda i: (0, 0))],
        out_specs=pl.BlockSpec((tr, Dm), lambda i: (i, 0)),
        out_shape=_sds((S, Dm), BF16),
        compiler_params=_cparams(("parallel",)),
    )(x, g)


def _rms_bwd(name, dh, x, g, dres, tr, want_dx=True):
    S, Dm = x.shape
    nsteps = S // tr

    def body(*refs):
        if want_dx:
            dh_ref, x_ref, g_ref, dres_ref, dx_ref, dxb_ref, dg_ref, acc_ref = refs
        else:
            dh_ref, x_ref, g_ref, dg_ref, acc_ref = refs
        i = pl.program_id(0)
        xv = x_ref[...]
        r = lax.rsqrt(jnp.mean(xv * xv, axis=-1, keepdims=True) + EPS)
        xh = xv * r
        dhv = dh_ref[...]
        part = _rowsum8(dhv * xh)

        @pl.when(i == 0)
        def _():
            acc_ref[...] = part

        @pl.when(i > 0)
        def _():
            acc_ref[...] += part

        @pl.when(i == nsteps - 1)
        def _():
            dg_ref[...] = jnp.sum(acc_ref[...], axis=0, keepdims=True)

        if want_dx:
            dxh = dhv * g_ref[...]
            m = jnp.mean(dxh * xh, axis=-1, keepdims=True)
            dx = dres_ref[...] + r * (dxh - xh * m)
            dx_ref[...] = dx
            dxb_ref[...] = dx.astype(BF16)

    row = pl.BlockSpec((tr, Dm), lambda i: (i, 0))
    vec = pl.BlockSpec((1, Dm), lambda i: (0, 0))
    if want_dx:
        in_specs = [row, row, vec, row]
        out_specs = [row, row, vec]
        out_shape = [_sds((S, Dm), F32), _sds((S, Dm), BF16), _sds((1, Dm), F32)]
        args = (dh, x, g, dres)
    else:
        in_specs = [row, row, vec]
        out_specs = [vec]
        out_shape = [_sds((1, Dm), F32)]
        args = (dh, x, g)
    return pl.pallas_call(
        body, name=name, grid=(nsteps,), in_specs=in_specs, out_specs=out_specs, out_shape=out_shape,
        scratch_shapes=[pltpu.VMEM((8, Dm), F32)],
        compiler_params=_cparams(("arbitrary",)),
    )(*args)


def _final_loss(name, x3, g, target, tr):
    S, Dm = x3.shape
    nsteps = S // tr

    def body(x_ref, g_ref, t_ref, dx_ref, dxb_ref, dg_ref, loss_ref, acc_g, acc_l):
        i = pl.program_id(0)
        xv = x_ref[...]
        gv = g_ref[...]
        r = lax.rsqrt(jnp.mean(xv * xv, axis=-1, keepdims=True) + EPS)
        xh = xv * r
        e = xh * gv - t_ref[...]
        dy = e * (1.0 / Dm)
        lpart = _rowsum8(e * e)
        gpart = _rowsum8(dy * xh)

        @pl.when(i == 0)
        def _():
            acc_g[...] = gpart
            acc_l[...] = lpart

        @pl.when(i > 0)
        def _():
            acc_g[...] += gpart
            acc_l[...] += lpart

        @pl.when(i == nsteps - 1)
        def _():
            dg_ref[...] = jnp.sum(acc_g[...], axis=0, keepdims=True)
            tot = jnp.sum(jnp.sum(acc_l[...], axis=1, keepdims=True), axis=0, keepdims=True)
            loss_ref[...] = tot * (0.5 / Dm)

        dxh = dy * gv
        m = jnp.mean(dxh * xh, axis=-1, keepdims=True)
        dx = r * (dxh - xh * m)
        dx_ref[...] = dx
        dxb_ref[...] = dx.astype(BF16)

    row = pl.BlockSpec((tr, Dm), lambda i: (i, 0))
    vec = pl.BlockSpec((1, Dm), lambda i: (0, 0))
    return pl.pallas_call(
        body, name=name, grid=(nsteps,),
        in_specs=[row, vec, row],
        out_specs=[row, row, vec, pl.BlockSpec((1, 1), lambda i: (0, 0))],
        out_shape=[_sds((S, Dm), F32), _sds((S, Dm), BF16), _sds((1, Dm), F32), _sds((1, 1), F32)],
        scratch_shapes=[pltpu.VMEM((8, Dm), F32), pltpu.VMEM((8, Dm), F32)],
        compiler_params=_cparams(("arbitrary",)),
    )(x3, g, target)


def _softmax_rows(s):
    e = jnp.exp(s - jnp.max(s, axis=-1, keepdims=True))
    return e / jnp.sum(e, axis=-1, keepdims=True)


def _attn_fwd(name, q, k, v, ts):
    S, Dm = q.shape
    M = k.shape[0]
    scale = HEAD_DIM ** -0.5

    def body(q_ref, k_ref, v_ref, o_ref):
        for h in range(N_HEADS):
            sl = slice(h * HEAD_DIM, (h + 1) * HEAD_DIM)
            s = lax.dot_general(q_ref[:, sl], k_ref[:, sl], _DN["nt"], preferred_element_type=F32) * scale
            p = _softmax_rows(s)
            o_ref[:, sl] = jnp.dot(p.astype(BF16), v_ref[:, sl], preferred_element_type=F32).astype(o_ref.dtype)

    row = pl.BlockSpec((ts, Dm), lambda i: (i, 0))
    mem = pl.BlockSpec((M, Dm), lambda i: (0, 0))
    return pl.pallas_call(
        body, name=name, grid=(S // ts,), in_specs=[row, mem, mem], out_specs=row,
        out_shape=_sds((S, Dm), BF16), compiler_params=_cparams(("parallel",)),
    )(q, k, v)


def _attn_bwd(name, q, k, v, do, ts):
    S, Dm = q.shape
    M = k.shape[0]
    scale = HEAD_DIM ** -0.5

    def body(q_ref, k_ref, v_ref, do_ref, dq_ref, dk_ref, dv_ref):
        i = pl.program_id(0)

        @pl.when(i == 0)
        def _():
            dk_ref[...] = jnp.zeros_like(dk_ref)
            dv_ref[...] = jnp.zeros_like(dv_ref)

        for h in range(N_HEADS):
            sl = slice(h * HEAD_DIM, (h + 1) * HEAD_DIM)
            qh = q_ref[:, sl]
            kh = k_ref[:, sl]
            doh = do_ref[:, sl]
            s = lax.dot_general(qh, kh, _DN["nt"], preferred_element_type=F32) * scale
            p = _softmax_rows(s)
            dp = lax.dot_general(doh, v_ref[:, sl], _DN["nt"], preferred_element_type=F32)
            ds = p * (dp - jnp.sum(dp * p, axis=-1, keepdims=True)) * scale
            dsb = ds.astype(BF16)
            dq_ref[:, sl] = jnp.dot(dsb, kh, preferred_element_type=F32).astype(dq_ref.dtype)
            dk_ref[:, sl] += lax.dot_general(dsb, qh, _DN["tn"], preferred_element_type=F32)
            dv_ref[:, sl] += lax.dot_general(p.astype(BF16), doh, _DN["tn"], preferred_element_type=F32)

    row = pl.BlockSpec((ts, Dm), lambda i: (i, 0))
    mem = pl.BlockSpec((M, Dm), lambda i: (0, 0))
    return pl.pallas_call(
        body, name=name, grid=(S // ts,), in_specs=[row, mem, mem, row], out_specs=[row, mem, mem],
        out_shape=[_sds((S, Dm), BF16), _sds((M, Dm), F32), _sds((M, Dm), F32)],
        compiler_params=_cparams(("arbitrary",)),
    )(q, k, v, do)


def _pool_denominators(row0, ts):
    return (row0 + lax.broadcasted_iota(jnp.int32, (ts, 1), 0) + 1).astype(F32)


def _mixer_fwd(name, proj, pool_w, pool_scale, sgu_g, ws, bias_full, ts):
    S = proj.shape[0]
    nblk = ts // SGU_BLOCK
    halo_blocks = ts // POOL_HALO

    def body(proj_ref, halo_ref, pw_ref, sc_ref, g_ref, ws_ref, b_ref, y_ref, p_ref, vn_ref, ext_ref):
        i = pl.program_id(0)
        a = proj_ref[:, 0:D_POOL]
        ext_ref[0:POOL_HALO, :] = jnp.where(i > 0, halo_ref[...], 0.0)
        ext_ref[POOL_HALO:POOL_HALO + ts, :] = a
        pos = _pool_denominators(i * ts, ts)
        for gi, w in enumerate(POOL_WINDOWS):
            cs = slice(gi * POOL_GROUP, (gi + 1) * POOL_GROUP)
            acc = a[:, cs]
            for j in range(1, w):
                acc = acc + ext_ref[POOL_HALO - j:POOL_HALO - j + ts, cs]
            pg = (acc / jnp.minimum(pos, float(w)) - a[:, cs]).astype(BF16)
            p_ref[:, cs] = pg
            ypre = jnp.dot(pg, pw_ref[gi], preferred_element_type=F32)
            y_ref[:, cs] = (ypre * sc_ref[:, cs]).astype(y_ref.dtype)

        v = proj_ref[:, D_POOL + D_SGU:D_POOL + 2 * D_SGU]
        r = lax.rsqrt(jnp.mean(v * v, axis=-1, keepdims=True) + EPS)
        vn_ref[...] = (v * r * g_ref[...]).astype(BF16)
        for n in range(nblk):
            rs = slice(n * SGU_BLOCK, (n + 1) * SGU_BLOCK)
            for h in range(N_SGU_HEADS):
                cs = slice(h * SGU_BLOCK, (h + 1) * SGU_BLOCK)
                mixed = jnp.dot(ws_ref[h], vn_ref[rs, cs], preferred_element_type=F32) + b_ref[:, cs]
                u = proj_ref[rs, D_POOL + h * SGU_BLOCK:D_POOL + (h + 1) * SGU_BLOCK]
                y_ref[rs, D_POOL + h * SGU_BLOCK:D_POOL + (h + 1) * SGU_BLOCK] = (u * mixed).astype(y_ref.dtype)

    return pl.pallas_call(
        body, name=name, grid=(S // ts,),
        in_specs=[
            pl.BlockSpec((ts, D_POOL + 2 * D_SGU), lambda i: (i, 0)),
            pl.BlockSpec((POOL_HALO, D_POOL), lambda i: (jnp.maximum(i * halo_blocks - 1, 0), 0)),
            pl.BlockSpec((4, POOL_GROUP, POOL_GROUP), lambda i: (0, 0, 0)),
            pl.BlockSpec((1, D_POOL), lambda i: (0, 0)),
            pl.BlockSpec((1, D_SGU), lambda i: (0, 0)),
            pl.BlockSpec((N_SGU_HEADS, SGU_BLOCK, SGU_BLOCK), lambda i: (0, 0, 0)),
            pl.BlockSpec((SGU_BLOCK, D_SGU), lambda i: (0, 0)),
        ],
        out_specs=[
            pl.BlockSpec((ts, D_MODEL), lambda i: (i, 0)),
            pl.BlockSpec((ts, D_POOL), lambda i: (i, 0)),
            pl.BlockSpec((ts, D_SGU), lambda i: (i, 0)),
        ],
        out_shape=[_sds((S, D_MODEL), BF16), _sds((S, D_POOL), BF16), _sds((S, D_SGU), BF16)],
        scratch_shapes=[pltpu.VMEM((ts + POOL_HALO, D_POOL), F32)],
        compiler_params=_cparams(("parallel",)),
    )(proj, proj, pool_w, pool_scale, sgu_g, ws, bias_full)


def _mixer_bwd(name, dymix, proj, p, vn, pool_w, pool_scale, sgu_g, ws, bias_full, ts):
    S = proj.shape[0]
    nsteps = S // ts
    nblk = ts // SGU_BLOCK
    halo_blocks = ts // POOL_HALO

    def body(dy_ref, dyh_ref, u_ref, v_ref, p_ref, vn_ref, pw_ref, sc_ref, g_ref, ws_ref, b_ref,
             dproj_ref, dpw_ref, dsc_ref, dg_ref, dws_ref, db_ref,
             ext_ref, dvn_ref, acc_sc, acc_g, acc_b):
        i = pl.program_id(0)

        @pl.when(i == 0)
        def _():
            dpw_ref[...] = jnp.zeros_like(dpw_ref)
            dws_ref[...] = jnp.zeros_like(dws_ref)
            acc_sc[...] = jnp.zeros_like(acc_sc)
            acc_g[...] = jnp.zeros_like(acc_g)
            acc_b[...] = jnp.zeros_like(acc_b)

        pos = _pool_denominators(i * ts, ts)
        pos_h = _pool_denominators((i + 1) * ts, POOL_HALO)
        for gi, w in enumerate(POOL_WINDOWS):
            cs = slice(gi * POOL_GROUP, (gi + 1) * POOL_GROUP)
            pg = p_ref[:, cs]
            wg = pw_ref[gi]
            dyp = dy_ref[:, cs]
            ypre = jnp.dot(pg, wg, preferred_element_type=F32)
            acc_sc[:, cs] += _rowsum8(dyp * ypre)
            dz = (dyp * sc_ref[:, cs]).astype(BF16)
            dpw_ref[gi] += lax.dot_general(pg, dz, _DN["tn"], preferred_element_type=F32)
            dp = lax.dot_general(dz, wg, _DN["nt"], preferred_element_type=F32)
            dzh = (dyh_ref[:, cs] * sc_ref[:, cs]).astype(BF16)
            dph = lax.dot_general(dzh, wg, _DN["nt"], preferred_element_type=F32)
            ext_ref[0:ts, cs] = dp / jnp.minimum(pos, float(w))
            ext_ref[ts:ts + POOL_HALO, cs] = jnp.where(i < nsteps - 1, dph / jnp.minimum(pos_h, float(w)), 0.0)
            acc = ext_ref[0:ts, cs]
            for j in range(1, w):
                acc = acc + ext_ref[j:j + ts, cs]
            dproj_ref[:, cs] = (acc - dp).astype(dproj_ref.dtype)

        for n in range(nblk):
            rs = slice(n * SGU_BLOCK, (n + 1) * SGU_BLOCK)
            for h in range(N_SGU_HEADS):
                cs = slice(h * SGU_BLOCK, (h + 1) * SGU_BLOCK)
                vnb = vn_ref[rs, cs]
                wh = ws_ref[h]
                mixed = jnp.dot(wh, vnb, preferred_element_type=F32) + b_ref[:, cs]
                dys = dy_ref[rs, D_POOL + h * SGU_BLOCK:D_POOL + (h + 1) * SGU_BLOCK]
                dproj_ref[rs, D_POOL + h * SGU_BLOCK:D_POOL + (h + 1) * SGU_BLOCK] = (dys * mixed).astype(dproj_ref.dtype)
                dmix = dys * u_ref[rs, cs]
                acc_b[:, cs] += dmix
                dmb = dmix.astype(BF16)
                dws_ref[h] += lax.dot_general(dmb, vnb, _DN["nt"], preferred_element_type=F32)
                dvn_ref[rs, cs] = lax.dot_general(wh, dmb, _DN["tn"], preferred_element_type=F32)
        v = v_ref[...]
        r = lax.rsqrt(jnp.mean(v * v, axis=-1, keepdims=True) + EPS)
        vh = v * r
        dvn = dvn_ref[...]
        acc_g[...] += _rowsum8(dvn * vh)
        dxh = dvn * g_ref[...]
        m = jnp.mean(dxh * vh, axis=-1, keepdims=True)
        dproj_ref[:, D_POOL + D_SGU:D_POOL + 2 * D_SGU] = (r * (dxh - vh * m)).astype(dproj_ref.dtype)

        @pl.when(i == nsteps - 1)
        def _():
            dsc_ref[...] = jnp.sum(acc_sc[...], axis=0, keepdims=True)
            dg_ref[...] = jnp.sum(acc_g[...], axis=0, keepdims=True)
            t_idx = lax.broadcasted_iota(jnp.int32, (SGU_BLOCK, SGU_BLOCK), 0) // SGU_CHUNK
            s_idx = lax.broadcasted_iota(jnp.int32, (SGU_BLOCK, SGU_BLOCK), 1) // SGU_CHUNK
            mask = s_idx <= t_idx
            for h in range(N_SGU_HEADS):
                cs = slice(h * SGU_BLOCK, (h + 1) * SGU_BLOCK)
                dws_ref[h] = jnp.where(mask, dws_ref[h], 0.0)
                col = jnp.sum(acc_b[:, cs], axis=1, keepdims=True)
                db_ref[h] = jnp.broadcast_to(col, (SGU_BLOCK, SGU_BLOCK))

    const2 = lambda i: (0, 0)
    const3 = lambda i: (0, 0, 0)
    last_halo = S // POOL_HALO - 1
    return pl.pallas_call(
        body, name=name, grid=(nsteps,),
        in_specs=[
            pl.BlockSpec((ts, D_MODEL), lambda i: (i, 0)),
            pl.BlockSpec((POOL_HALO, D_POOL), lambda i: (jnp.minimum((i + 1) * halo_blocks, last_halo), 0)),
            pl.BlockSpec((ts, D_SGU), lambda i: (i, 1)),
            pl.BlockSpec((ts, D_SGU), lambda i: (i, 2)),
            pl.BlockSpec((ts, D_POOL), lambda i: (i, 0)),
            pl.BlockSpec((ts, D_SGU), lambda i: (i, 0)),
            pl.BlockSpec((4, POOL_GROUP, POOL_GROUP), const3),
            pl.BlockSpec((1, D_POOL), const2),
            pl.BlockSpec((1, D_SGU), const2),
            pl.BlockSpec((N_SGU_HEADS, SGU_BLOCK, SGU_BLOCK), const3),
            pl.BlockSpec((SGU_BLOCK, D_SGU), const2),
        ],
        out_specs=[
            pl.BlockSpec((ts, D_POOL + 2 * D_SGU), lambda i: (i, 0)),
            pl.BlockSpec((4, POOL_GROUP, POOL_GROUP), const3),
            pl.BlockSpec((1, D_POOL), const2),
            pl.BlockSpec((1, D_SGU), const2),
            pl.BlockSpec((N_SGU_HEADS, SGU_BLOCK, SGU_BLOCK), const3),
            pl.BlockSpec((N_SGU_HEADS, SGU_BLOCK, SGU_BLOCK), const3),
        ],
        out_shape=[
            _sds((S, D_POOL + 2 * D_SGU), BF16),
            _sds((4, POOL_GROUP, POOL_GROUP), F32),
            _sds((1, D_POOL), F32),
            _sds((1, D_SGU), F32),
            _sds((N_SGU_HEADS, SGU_BLOCK, SGU_BLOCK), F32),
            _sds((N_SGU_HEADS, SGU_BLOCK, SGU_BLOCK), F32),
        ],
        scratch_shapes=[
            pltpu.VMEM((ts + POOL_HALO, D_POOL), F32),
            pltpu.VMEM((ts, D_SGU), F32),
            pltpu.VMEM((8, D_POOL), F32),
            pltpu.VMEM((8, D_SGU), F32),
            pltpu.VMEM((SGU_BLOCK, D_SGU), F32),
        ],
        compiler_params=_cparams(("arbitrary",)),
    )(dymix, dymix, proj, proj, p, vn, pool_w, pool_scale, sgu_g, ws, bias_full)


def _silu_mul(accs, _):
    gt, up = accs
    sig = 1.0 / (1.0 + jnp.exp(-gt))
    return gt, up, gt * sig * up


def _silu_mul_bwd(accs, extras):
    (dact,) = accs
    gt = extras[0].astype(F32)
    up = extras[1].astype(F32)
    sig = 1.0 / (1.0 + jnp.exp(-gt))
    silu = gt * sig
    dgt = dact * up * (sig * (1.0 + gt * (1.0 - sig)))
    dup = dact * silu
    return dgt, dup


def _add_residual(accs, extras):
    return (extras[0] + accs[0],)


def _add_residual_and_cast(accs, extras):
    y = extras[0] + accs[0]
    return y, y


def _local_step(x, mem, target, W, sm):
    S = x.shape[0]
    tm = min(1024, S)
    th = min(512, S)
    ts = min(512, S)
    tr = min(512, S)
    tk_s = min(2048, S)
    M = mem.shape[0]

    h1 = _rms_fwd("rms_mix", x, sm["norm_mix_g"], tr)
    proj = _mm1("proj_in", h1, W["w_in_t"], "nt", tm, 1024, 2048, F32)
    ymix, p, vn = _mixer_fwd("mixer_fwd", proj, W["pool_w"], sm["pool_scale"], sm["sgu_norm_g"],
                             sm["ws_masked"], sm["bias_full"], ts)
    x1 = _mm1("proj_out", ymix, W["w_out"], "nn", tm, 1024, 2048, F32, epilogue=_add_residual, extras=(x,))

    h2 = _rms_fwd("rms_xattn", x1, sm["norm_xattn_g"], tr)
    mb = _rms_fwd("rms_mem", mem, sm["norm_mem_g"], M)
    q = _mm1("proj_q", h2, W["w_q"], "nn", tm, 1024, 2048, BF16)
    kk = _mm1("proj_k", mb, W["w_k"], "nn", M, 1024, 2048, BF16)
    vv = _mm1("proj_v", mb, W["w_v"], "nn", M, 1024, 2048, BF16)
    o = _attn_fwd("attn_fwd", q, kk, vv, ts)
    x2 = _mm1("proj_o", o, W["w_o"], "nn", tm, 1024, 2048, F32, epilogue=_add_residual, extras=(x1,))

    h3 = _rms_fwd("rms_ffn", x2, sm["norm_ffn_g"], tr)
    gt, up, act = _mm("ffn_gate_up", [h3], [W["w_gate_t"], W["w_up_t"]], [(0, 0, 0), (0, 1, 1)], "nt",
                      tm, 512, 2048, [BF16, BF16, BF16], epilogue=_silu_mul, n_acc=2)
    x3 = _mm1("ffn_down", act, W["w_down"], "nn", th, 1024, 2816, F32, epilogue=_add_residual, extras=(x2,))

    dx3, dx3b, d_final_g, loss = _final_loss("final_loss", x3, sm["final_norm_g"], target, tr)

    dgt, dup = _mm("ffn_down_dgrad", [dx3b], [W["w_down"]], [(0, 0, 0)], "nt", tm, 512, 2048, [BF16, BF16],
                   epilogue=_silu_mul_bwd, extras=(gt, up))
    g_w_down = _mm1("ffn_down_wgrad", act, dx3b, "tn", 1408, 1024, tk_s, BF16)
    g_w_gate_t = _mm1("ffn_gate_wgrad", dgt, h3, "tn", 1408, 1024, tk_s, BF16)
    g_w_up_t = _mm1("ffn_up_wgrad", dup, h3, "tn", 1408, 1024, tk_s, BF16)
    dh3 = _mm("ffn_gate_up_dgrad", [dgt, dup], [W["w_gate_t"], W["w_up_t"]], [(0, 0, 0), (1, 1, 0)], "nn",
              th, 1024, 1408, [F32])[0]
    dx2, dx2b, d_ffn_g = _rms_bwd("rms_ffn_bwd", dh3, x2, sm["norm_ffn_g"], dx3, tr)

    do = _mm1("proj_o_dgrad", dx2b, W["w_o"], "nt", tm, 1024, 2048, BF16)
    g_w_o = _mm1("proj_o_wgrad", o, dx2b, "tn", 1024, 1024, tk_s, BF16)
    dq, dk, dv = _attn_bwd("attn_bwd", q, kk, vv, do, ts)
    g_w_q = _mm1("proj_q_wgrad", h2, dq, "tn", 1024, 1024, tk_s, BF16)
    dh2 = _mm1("proj_q_dgrad", dq, W["w_q"], "nt", tm, 1024, 2048, F32)
    dx1, dx1b, d_xattn_g = _rms_bwd("rms_xattn_bwd", dh2, x1, sm["norm_xattn_g"], dx2, tr)
    g_w_k = _mm1("proj_k_wgrad", mb, dk, "tn", 1024, 1024, M, BF16)
    g_w_v = _mm1("proj_v_wgrad", mb, dv, "tn", 1024, 1024, M, BF16)
    dmb = _mm("proj_kv_dgrad", [dk, dv], [W["w_k"], W["w_v"]], [(0, 0, 0), (1, 1, 0)], "nt",
              M, 1024, 2048, [F32])[0]
    (d_mem_g,) = _rms_bwd("rms_mem_bwd", dmb, mem, sm["norm_mem_g"], None, M, want_dx=False)

    dymix = _mm1("proj_out_dgrad", dx1b, W["w_out"], "nt", tm, 1024, 2048, F32)
    g_w_out = _mm1("proj_out_wgrad", ymix, dx1b, "tn", 1024, 1024, tk_s, BF16)
    dproj, d_pool_w, d_pool_scale, d_sgu_g, d_ws, d_b = _mixer_bwd(
        "mixer_bwd", dymix, proj, p, vn, W["pool_w"], sm["pool_scale"], sm["sgu_norm_g"],
        sm["ws_masked"], sm["bias_full"], ts)
    g_w_in_t = _mm1("proj_in_wgrad", dproj, h1, "tn", 1024, 1024, tk_s, BF16)
    dh1 = _mm1("proj_in_dgrad", dproj, W["w_in_t"], "nn", tm, 1024, 3072, F32)
    grad_x, _, d_mix_g = _rms_bwd("rms_mix_bwd", dh1, x, sm["norm_mix_g"], dx1, tr)

    big = dict(w_in_t=g_w_in_t, w_out=g_w_out, w_q=g_w_q, w_k=g_w_k, w_v=g_w_v, w_o=g_w_o,
               w_gate_t=g_w_gate_t, w_up_t=g_w_up_t, w_down=g_w_down)
    small = dict(norm_mix_g=d_mix_g, pool_w=d_pool_w, pool_scale=d_pool_scale, sgu_norm_g=d_sgu_g,
                 w_spatial=d_ws, b_spatial=d_b[:, :, 0], norm_xattn_g=d_xattn_g, norm_mem_g=d_mem_g,
                 norm_ffn_g=d_ffn_g, final_norm_g=d_final_g)
    return loss, grad_x, big, small


_ANY = pl.BlockSpec(memory_space=pl.ANY)


def _mesh_pos():
    return lax.axis_index("x"), lax.axis_index("y"), lax.axis_index("c")


def _all_gather(name, shards):
    n = len(shards)

    def body(*refs):
        ins = refs[:n]
        outs = refs[n:2 * n]
        send_sems, recv_sems, local_sems = refs[2 * n:]
        x, y, c = _mesh_pos()
        me, sibling = (x, y, c), (x, y, 1 - c)
        chips = [(1 - x, y), (x, 1 - y), (1 - x, 1 - y)]

        def copy(a, k, block, to, src=None):
            bx, by, bc = block
            dst = outs[a].at[4 * bx + 2 * by + bc]
            return pltpu.make_async_remote_copy(
                src_ref=dst if src is None else src, dst_ref=dst,
                send_sem=send_sems.at[a, k], recv_sem=recv_sems.at[a, k],
                device_id=to, device_id_type=MESH)

        mine = [pltpu.make_async_copy(ins[a], outs[a].at[4 * x + 2 * y + c], local_sems.at[a]) for a in range(n)]
        for cp in mine:
            cp.start()
        started = []
        for a in range(n):
            first = [copy(a, 0, me, sibling, src=ins[a])]
            first += [copy(a, 1 + j, me, (*chip, c), src=ins[a]) for j, chip in enumerate(chips)]
            for cp in first:
                cp.start()
            started += first
        for a in range(n):
            for j, chip in enumerate(chips):
                copy(a, 1 + j, (*chip, c), me).wait_recv()
                fwd = copy(a, 4 + j, (*chip, c), sibling)
                fwd.start()
                started.append(fwd)
        for a in range(n):
            copy(a, 0, sibling, me).wait_recv()
            for j, chip in enumerate(chips):
                copy(a, 4 + j, (*chip, 1 - c), me).wait_recv()
        for cp in started:
            cp.wait_send()
        for cp in mine:
            cp.wait()

    return pl.pallas_call(
        body, name=name,
        in_specs=[_ANY] * n, out_specs=[_ANY] * n,
        out_shape=[_sds((N_DEV,) + s.shape, s.dtype) for s in shards],
        scratch_shapes=[pltpu.SemaphoreType.DMA((n, 7)), pltpu.SemaphoreType.DMA((n, 7)),
                        pltpu.SemaphoreType.DMA((n,))],
    )(*shards)


def _handshake(peers):
    barrier = pltpu.get_barrier_semaphore()
    for peer in peers:
        pl.semaphore_signal(barrier, inc=1, device_id=peer, device_id_type=MESH)
    pl.semaphore_wait(barrier, len(peers))


def _seq_all_gather(name, shards, collective_id):
    n = len(shards)

    def body(*refs):
        ins = refs[:n]
        outs = refs[n:2 * n]
        send_sems, recv_sems, local_sems = refs[2 * n:]
        x, y, c = _mesh_pos()
        me, sibling = (x, y, c), (x, y, 1 - c)
        chips = [(1 - x, y), (x, 1 - y), (1 - x, 1 - y)]
        _handshake([sibling] + [(*chip, c) for chip in chips])

        def copy(a, k, block, to, src=None):
            bx, by, bc = block
            dst = outs[a].at[4 * bx + 2 * by + bc]
            return pltpu.make_async_remote_copy(
                src_ref=dst if src is None else src, dst_ref=dst,
                send_sem=send_sems.at[a, k], recv_sem=recv_sems.at[a, k],
                device_id=to, device_id_type=MESH)

        mine = [pltpu.make_async_copy(ins[a], outs[a].at[4 * x + 2 * y + c], local_sems.at[a]) for a in range(n)]
        for cp in mine:
            cp.start()
        started = []
        for a in range(n):
            first = [copy(a, 0, me, sibling, src=ins[a])]
            first += [copy(a, 1 + j, me, (*chip, c), src=ins[a]) for j, chip in enumerate(chips)]
            for cp in first:
                cp.start()
            started += first
        for a in range(n):
            for j, chip in enumerate(chips):
                copy(a, 1 + j, (*chip, c), me).wait_recv()
                fwd = copy(a, 4 + j, (*chip, c), sibling)
                fwd.start()
                started.append(fwd)
        for a in range(n):
            copy(a, 0, sibling, me).wait_recv()
            for j, chip in enumerate(chips):
                copy(a, 4 + j, (*chip, 1 - c), me).wait_recv()
        for cp in started:
            cp.wait_send()
        for cp in mine:
            cp.wait()

    return pl.kernel(
        body, name=name,
        out_type=[_sds((N_DEV,) + s.shape, s.dtype) for s in shards],
        mesh=plsc.ScalarSubcoreMesh(axis_name="seq", num_cores=1),
        scratch_types=[pltpu.SemaphoreType.DMA((n, 7)), pltpu.SemaphoreType.DMA((n, 7)),
                       pltpu.SemaphoreType.DMA((n,))],
        compiler_params=pltpu.CompilerParams(collective_id=collective_id),
    )(*shards)


_SEQ_MESH = dict(axis_name="seq", num_cores=1)


def _seq_pair_exchange(name, gview, collective_id):
    def body(g_ref, theirs_ref, send_sems, recv_sems):
        x, y, c = _mesh_pos()
        sibling = (x, y, 1 - c)
        _handshake([sibling])
        copies = [pltpu.make_async_remote_copy(
            src_ref=g_ref.at[k, 1 - c], dst_ref=theirs_ref.at[k],
            send_sem=send_sems.at[k], recv_sem=recv_sems.at[k],
            device_id=sibling, device_id_type=MESH) for k in range(4)]
        for cp in copies:
            cp.start()
        for cp in copies:
            cp.wait()

    return pl.kernel(
        body, name=name, out_type=_sds((4,) + gview.shape[2:], gview.dtype),
        mesh=plsc.ScalarSubcoreMesh(**_SEQ_MESH),
        scratch_types=[pltpu.SemaphoreType.DMA((4,)), pltpu.SemaphoreType.DMA((4,))],
        compiler_params=pltpu.CompilerParams(collective_id=collective_id),
    )(gview)


def _pair_sum(name, gview, theirs, pos, tr):
    _, _, r, C = gview.shape

    def body(pos_ref, a_ref, b_ref, o_ref):
        o_ref[...] = (a_ref[...].astype(F32) + b_ref[...].astype(F32)).astype(o_ref.dtype)

    grid_spec = pltpu.PrefetchScalarGridSpec(
        num_scalar_prefetch=1, grid=(4, r // tr),
        in_specs=[pl.BlockSpec((None, None, tr, C), lambda k, t, pos_ref: (k, pos_ref[0], t, 0)),
                  pl.BlockSpec((None, tr, C), lambda k, t, pos_ref: (k, t, 0))],
        out_specs=pl.BlockSpec((None, tr, C), lambda k, t, pos_ref: (k, t, 0)))
    return pl.pallas_call(
        body, name=name, grid_spec=grid_spec, out_shape=_sds(theirs.shape, theirs.dtype),
        compiler_params=_cparams(("parallel", "parallel")),
    )(pos, gview, theirs)


def _seq_chip_exchange(name, pair, collective_id):
    def body(p_ref, land_ref, send_sems, recv_sems):
        x, y, c = _mesh_pos()
        my_chip = 2 * x + y
        chips = [(1 - x, y), (x, 1 - y), (1 - x, 1 - y)]
        _handshake([(cx, cy, c) for cx, cy in chips])
        copies = [pltpu.make_async_remote_copy(
            src_ref=p_ref.at[2 * cx + cy], dst_ref=land_ref.at[my_chip],
            send_sem=send_sems.at[j], recv_sem=recv_sems.at[j],
            device_id=(cx, cy, c), device_id_type=MESH) for j, (cx, cy) in enumerate(chips)]
        for cp in copies:
            cp.start()
        for cp in copies:
            cp.wait_send()
        for j, (cx, cy) in enumerate(chips):
            pltpu.make_async_remote_copy(
                src_ref=p_ref.at[my_chip], dst_ref=land_ref.at[2 * cx + cy],
                send_sem=send_sems.at[j], recv_sem=recv_sems.at[j],
                device_id=(cx, cy, c), device_id_type=MESH).wait_recv()

    return pl.kernel(
        body, name=name, out_type=_sds(pair.shape, pair.dtype),
        mesh=plsc.ScalarSubcoreMesh(**_SEQ_MESH),
        scratch_types=[pltpu.SemaphoreType.DMA((3,)), pltpu.SemaphoreType.DMA((3,))],
        compiler_params=pltpu.CompilerParams(collective_id=collective_id),
    )(pair)


def _chip_sum(name, pair, landed, pos, tr):
    _, r, C = pair.shape

    def body(pos_ref, own_ref, land_ref, o_ref, acc_ref):
        k = pl.program_id(1)
        val = jnp.where(k == pos_ref[1], own_ref[...], land_ref[...]).astype(F32)

        @pl.when(k == 0)
        def _():
            acc_ref[...] = val

        @pl.when(k > 0)
        def _():
            acc_ref[...] += val

        @pl.when(k == 3)
        def _():
            o_ref[...] = acc_ref[...]

    def land_index(t, k, pos_ref):
        return (jnp.where(k == pos_ref[1], (k + 1) % 4, k), t, 0)

    grid_spec = pltpu.PrefetchScalarGridSpec(
        num_scalar_prefetch=1, grid=(r // tr, 4),
        in_specs=[pl.BlockSpec((None, tr, C), lambda t, k, pos_ref: (pos_ref[1], t, 0)),
                  pl.BlockSpec((None, tr, C), land_index)],
        out_specs=pl.BlockSpec((tr, C), lambda t, k, pos_ref: (t, 0)),
        scratch_shapes=[pltpu.VMEM((tr, C), F32)])
    return pl.pallas_call(
        body, name=name, grid_spec=grid_spec, out_shape=_sds((r, C), F32),
        compiler_params=_cparams(("parallel", "arbitrary")),
    )(pos, pair, landed)


def _sum_leading(name, parts, tr, out_dtype=F32):
    n, r, C = parts.shape

    def body(p_ref, o_ref):
        acc = p_ref[0].astype(F32)
        for k in range(1, n):
            acc = acc + p_ref[k].astype(F32)
        o_ref[...] = acc.astype(o_ref.dtype)

    return pl.pallas_call(
        body, name=name, grid=(r // tr,),
        in_specs=[pl.BlockSpec((n, tr, C), lambda t: (0, t, 0))],
        out_specs=pl.BlockSpec((tr, C), lambda t: (t, 0)),
        out_shape=_sds((r, C), out_dtype), compiler_params=_cparams(("parallel",)),
    )(parts)


def _row_tile(r):
    for t in (512, 384, 352, 256, 128, 64, 32, 16, 8):
        if r % t == 0:
            return t
    return r


def _adamw(name, w, g, m, v):
    R, C = w.shape
    tr = _row_tile(R)
    c1 = 1.0 - ADAM_B1 ** ADAM_STEP
    c2 = 1.0 - ADAM_B2 ** ADAM_STEP

    def body(w_ref, g_ref, m_ref, v_ref, d_ref, nm_ref, nv_ref):
        gv = g_ref[...]
        nm = ADAM_B1 * m_ref[...] + (1.0 - ADAM_B1) * gv
        nv = ADAM_B2 * v_ref[...] + (1.0 - ADAM_B2) * (gv * gv)
        m_hat = nm / c1
        v_hat = nv / c2
        d_ref[...] = -ADAM_LR * (m_hat / (jnp.sqrt(v_hat) + ADAM_EPS) + ADAM_WD * w_ref[...])
        nm_ref[...] = nm
        nv_ref[...] = nv

    spec = pl.BlockSpec((tr, C), lambda i: (i, 0))
    return pl.pallas_call(
        body, name=name, grid=(R // tr,), in_specs=[spec] * 4, out_specs=[spec] * 3,
        out_shape=[_sds((R, C), F32)] * 3, compiler_params=_cparams(("parallel",)),
    )(w, g, m, v)


_BIG = ("w_in_t", "w_out", "w_q", "w_k", "w_v", "w_o", "w_gate_t", "w_up_t", "w_down")
_SMALL = ("norm_mix_g", "pool_scale", "sgu_norm_g", "w_spatial", "b_spatial", "norm_xattn_g",
          "norm_mem_g", "norm_ffn_g", "final_norm_g")
_LANES = 128
_GATHER_GROUPS = (("w_in_t", "pool_w"), ("w_out", "w_q", "w_k", "w_v", "w_o"), ("w_gate_t", "w_up_t"), ("w_down",))
_RS_ORDER = ("w_down", "w_gate_t", "w_up_t", "w_o", "w_q", "w_k", "w_v", "w_out", "w_in_t")
_ID_GATHER, _ID_PAIR, _ID_CHIP = 0, 1, 2


_PACK_ROWS = 512


def _pack(parts):
    rows = [p.reshape(-1, _LANES) for p in parts]
    n = sum(r.shape[0] for r in rows)
    pad = -n % _PACK_ROWS
    if pad:
        rows.append(jnp.zeros((pad, _LANES), rows[0].dtype))
    return jnp.concatenate(rows, axis=0)


def _unpack(packed, like):
    out, row = [], 0
    for ref in like:
        rows = ref.size // _LANES
        out.append(packed[row:row + rows].reshape(ref.shape))
        row += rows
    return out


def kernel(x, mem, norm_mix_g, w_in, pool_w, pool_scale, sgu_norm_g, w_spatial, b_spatial, w_out, norm_xattn_g, norm_mem_g, w_q, w_k, w_v, w_o, norm_ffn_g, w_gate, w_up, w_down, final_norm_g, loss_target, m_norm_mix_g, m_w_in, m_pool_w, m_pool_scale, m_sgu_norm_g, m_w_spatial, m_b_spatial, m_w_out, m_norm_xattn_g, m_norm_mem_g, m_w_q, m_w_k, m_w_v, m_w_o, m_norm_ffn_g, m_w_gate, m_w_up, m_w_down, m_final_norm_g, v_norm_mix_g, v_w_in, v_pool_w, v_pool_scale, v_sgu_norm_g, v_w_spatial, v_b_spatial, v_w_out, v_norm_xattn_g, v_norm_mem_g, v_w_q, v_w_k, v_w_v, v_w_o, v_norm_ffn_g, v_w_gate, v_w_up, v_w_down, v_final_norm_g):
    args = dict(locals())
    names = ("norm_mix_g", "w_in", "pool_w", "pool_scale", "sgu_norm_g", "w_spatial", "b_spatial", "w_out",
             "norm_xattn_g", "norm_mem_g", "w_q", "w_k", "w_v", "w_o", "norm_ffn_g", "w_gate", "w_up",
             "w_down", "final_norm_g")
    w = {k: args[k] for k in names}
    m = {k: args["m_" + k] for k in names}
    v = {k: args["v_" + k] for k in names}
    my_dev = 4 * lax.axis_index("x") + 2 * lax.axis_index("y") + lax.axis_index("c")

    shards = dict(
        w_in_t=w["w_in"][0].T, w_out=w["w_out"][0], w_q=w["w_q"][0], w_k=w["w_k"][0], w_v=w["w_v"][0],
        w_o=w["w_o"][0], w_gate_t=w["w_gate"][0].T, w_up_t=w["w_up"][0].T, w_down=w["w_down"][0])
    send = {k: shards[k].astype(BF16) for k in _BIG}
    send["pool_w"] = w["pool_w"][0].reshape(4 * 32, POOL_GROUP).astype(BF16)
    W = {}
    for gi, group in enumerate(_GATHER_GROUPS):
        gathered = _seq_all_gather("gather_weights_%d" % gi, [send[k] for k in group], _ID_GATHER)
        for k, g in zip(group, gathered):
            W[k] = g.reshape(-1, g.shape[-1])
    W["pool_w"] = W["pool_w"].reshape(N_DEV, 4, 32, POOL_GROUP).transpose(1, 0, 2, 3).reshape(4, POOL_GROUP, POOL_GROUP)

    t = jnp.arange(SGU_BLOCK)
    mask = (t[None, :] // SGU_CHUNK) <= (t[:, None] // SGU_CHUNK)
    sm = dict(
        norm_mix_g=w["norm_mix_g"], pool_scale=w["pool_scale"], sgu_norm_g=w["sgu_norm_g"],
        norm_xattn_g=w["norm_xattn_g"], norm_mem_g=w["norm_mem_g"], norm_ffn_g=w["norm_ffn_g"],
        final_norm_g=w["final_norm_g"].reshape(1, D_MODEL),
        ws_masked=jnp.where(mask[None], w["w_spatial"][0], 0.0).astype(BF16),
        bias_full=jnp.repeat(w["b_spatial"][0].T, SGU_BLOCK, axis=1))

    loss_part, grad_x, big, small = _local_step(x[0], mem[0], loss_target[0], W, sm)
    loss = lax.psum(loss_part[0, 0], ("x", "y", "c"))

    pos = jnp.stack([lax.axis_index("c"), 2 * lax.axis_index("x") + lax.axis_index("y")]).astype(jnp.int32)
    gsum = {}
    for k in _RS_ORDER:
        g = big[k]
        r = g.shape[0] // N_DEV
        gview = g.reshape(4, 2, r, g.shape[1])
        theirs = _seq_pair_exchange("grad_pair_exchange_" + k, gview, _ID_PAIR)
        pair = _pair_sum("grad_pair_sum_" + k, gview, theirs, pos, _row_tile(r))
        landed = _seq_chip_exchange("grad_chip_exchange_" + k, pair, _ID_CHIP)
        gsum[k] = _chip_sum("grad_chip_sum_" + k, pair, landed, pos, _row_tile(r))
    grads = dict(
        w_in=gsum["w_in_t"].T[None], w_out=gsum["w_out"][None], w_q=gsum["w_q"][None], w_k=gsum["w_k"][None],
        w_v=gsum["w_v"][None], w_o=gsum["w_o"][None], w_gate=gsum["w_gate_t"].T[None],
        w_up=gsum["w_up_t"].T[None], w_down=gsum["w_down"][None])

    small_parts = [small[k] for k in _SMALL] + [small["pool_w"]]
    packed = _pack(small_parts)
    (allp,) = _all_gather("gather_small_grads", [packed])
    total = _sum_leading("sum_small_grads", allp, _row_tile(packed.shape[0]))
    small_sum = _unpack(total, [w[k] for k in _SMALL] + [small["pool_w"]])
    for k, g in zip(_SMALL, small_sum[:-1]):
        grads[k] = g
    full_pool = small_sum[-1]
    grads["pool_w"] = lax.dynamic_slice_in_dim(full_pool, my_dev * 32, 32, axis=1)[None]

    delta, new_m, new_v = {}, {}, {}
    for k in ("w_in", "w_out", "w_q", "w_k", "w_v", "w_o", "w_gate", "w_up", "w_down", "pool_w"):
        shp = w[k].shape
        two_d = (-1, shp[-1])
        d_, m_, v_ = _adamw("adamw_" + k, w[k].reshape(two_d), grads[k].reshape(two_d), m[k].reshape(two_d),
                            v[k].reshape(two_d))
        delta[k], new_m[k], new_v[k] = d_.reshape(shp), m_.reshape(shp), v_.reshape(shp)
    d_, m_, v_ = _adamw("adamw_small", _pack([w[k] for k in _SMALL]), _pack([grads[k] for k in _SMALL]),
                        _pack([m[k] for k in _SMALL]), _pack([v[k] for k in _SMALL]))
    like = [w[k] for k in _SMALL]
    for k, a, b, c_ in zip(_SMALL, _unpack(d_, like), _unpack(m_, like), _unpack(v_, like)):
        delta[k], new_m[k], new_v[k] = a, b, c_

    outs = [loss, grad_x[None]]
    outs += [grads[k].reshape(w[k].shape) for k in names]
    outs += [delta[k] for k in names]
    outs += [new_m[k] for k in names]
    outs += [new_v[k] for k in names]
    return tuple(outs)
```

```python
import functools

import jax
import jax.numpy as jnp
from jax import lax
from jax.experimental import pallas as pl
from jax.experimental.pallas import tpu as pltpu
from jax.experimental.pallas import tpu_sc as plsc

F32 = jnp.float32
BF16 = jnp.bfloat16
MESH = pl.DeviceIdType.MESH

EPS = 1e-6
D_MODEL = 2048
D_POOL = 1024
D_SGU = 1024
POOL_WINDOWS = (2, 4, 8, 16)
POOL_GROUP = 256
POOL_HALO = 16
SGU_BLOCK = 128
SGU_CHUNK = 64
N_SGU_HEADS = 8
N_HEADS = 4
HEAD_DIM = 512
N_DEV = 8

ADAM_LR = 0.001
ADAM_B1 = 0.9
ADAM_B2 = 0.999
ADAM_EPS = 1e-08
ADAM_WD = 0.01
ADAM_STEP = 10

VMEM_LIMIT = 56 * 1024 * 1024


def _cparams(sem=None):
    return pltpu.CompilerParams(dimension_semantics=sem, vmem_limit_bytes=VMEM_LIMIT)


def _sds(shape, dtype):
    return jax.ShapeDtypeStruct(shape, dtype)


_ANY = pl.BlockSpec(memory_space=pl.ANY)


class _Chain:
    def __init__(self):
        self.tc = None
        self.sc = None


_CHAIN = _Chain()


def _first(out):
    return out[0] if isinstance(out, (list, tuple)) else out


def _tc_call(body, *, in_specs=None, grid_spec=None, **kw):
    def run(*args):
        prev, n = _CHAIN.tc, len(args)
        fn, specs, spec, operands = body, in_specs, grid_spec, args
        if prev is not None:
            def fn(*refs):
                return body(*refs[:n], *refs[n + 1:])
            operands = args + (prev,)
            if grid_spec is None:
                specs = list(in_specs) + [_ANY]
            else:
                spec = pltpu.PrefetchScalarGridSpec(
                    num_scalar_prefetch=grid_spec.num_scalar_prefetch, grid=grid_spec.grid,
                    in_specs=list(grid_spec.in_specs) + [_ANY], out_specs=grid_spec.out_specs,
                    scratch_shapes=grid_spec.scratch_shapes)
        if spec is None:
            out = pl.pallas_call(fn, in_specs=specs, **kw)(*operands)
        else:
            out = pl.pallas_call(fn, grid_spec=spec, **kw)(*operands)
        _CHAIN.tc = _first(out)
        return out
    return run


def _sc_call(body, **kw):
    return pl.kernel(body, mesh=plsc.ScalarSubcoreMesh(axis_name="seq", num_cores=1), **kw)


def _rowsum8(v):
    r, c = v.shape
    return v.reshape(r // 8, 8, c).sum(axis=0)


_DN = {
    "nn": (((1,), (0,)), ((), ())),
    "nt": (((1,), (1,)), ((), ())),
    "tn": (((0,), (0,)), ((), ())),
}


def _mm(name, a_list, b_list, terms, mode, tm, tn, tk, out_dtypes, epilogue=None, extras=(), n_acc=1):
    a0, b0 = a_list[0], b_list[0]
    if mode == "tn":
        K, M = a0.shape
    else:
        M, K = a0.shape
    N = b0.shape[0] if mode == "nt" else b0.shape[1]
    assert M % tm == 0 and N % tn == 0 and K % tk == 0, (name, M, N, K, tm, tn, tk)
    nk = K // tk
    na, nb, ne, no = len(a_list), len(b_list), len(extras), len(out_dtypes)
    dn = _DN[mode]

    if mode == "tn":
        a_spec = pl.BlockSpec((tk, tm), lambda i, j, k: (k, i))
    else:
        a_spec = pl.BlockSpec((tm, tk), lambda i, j, k: (i, k))
    if mode == "nt":
        b_spec = pl.BlockSpec((tn, tk), lambda i, j, k: (j, k))
    else:
        b_spec = pl.BlockSpec((tk, tn), lambda i, j, k: (k, j))
    o_spec = pl.BlockSpec((tm, tn), lambda i, j, k: (i, j))

    def body(*refs):
        a_refs = refs[:na]
        b_refs = refs[na:na + nb]
        e_refs = refs[na + nb:na + nb + ne]
        o_refs = refs[na + nb + ne:na + nb + ne + no]
        acc_refs = refs[na + nb + ne + no:]
        parts = [None] * n_acc
        for ai, bi, ci in terms:
            d = lax.dot_general(a_refs[ai][...].astype(BF16), b_refs[bi][...].astype(BF16), dn,
                                preferred_element_type=F32)
            parts[ci] = d if parts[ci] is None else parts[ci] + d

        def finish(accs):
            outs = epilogue(accs, [e[...] for e in e_refs]) if epilogue is not None else accs
            for o_ref, v in zip(o_refs, outs):
                o_ref[...] = v.astype(o_ref.dtype)

        if nk == 1:
            finish(parts)
        else:
            k = pl.program_id(2)

            @pl.when(k == 0)
            def _():
                for c in range(n_acc):
                    acc_refs[c][...] = parts[c]

            @pl.when(k > 0)
            def _():
                for c in range(n_acc):
                    acc_refs[c][...] += parts[c]

            @pl.when(k == nk - 1)
            def _():
                finish([acc_refs[c][...] for c in range(n_acc)])

    scratch = [pltpu.VMEM((tm, tn), F32) for _ in range(n_acc)] if nk > 1 else []
    res = _tc_call(
        body, name=name, grid=(M // tm, N // tn, nk),
        in_specs=[a_spec] * na + [b_spec] * nb + [o_spec] * ne,
        out_specs=[o_spec] * no,
        out_shape=[_sds((M, N), dt) for dt in out_dtypes],
        scratch_shapes=scratch,
        compiler_params=_cparams(("parallel", "parallel", "arbitrary")),
    )(*a_list, *b_list, *extras)
    return res


def _mm1(name, a, b, mode, tm, tn, tk, out_dtype, **kw):
    return _mm(name, [a], [b], [(0, 0, 0)], mode, tm, tn, tk, [out_dtype], **kw)[0]


def _rms_fwd(name, x, g, tr):
    S, Dm = x.shape

    def body(x_ref, g_ref, h_ref):
        xv = x_ref[...]
        r = lax.rsqrt(jnp.mean(xv * xv, axis=-1, keepdims=True) + EPS)
        h_ref[...] = (xv * r * g_ref[...]).astype(h_ref.dtype)

    return _tc_call(
        body, name=name, grid=(S // tr,),
        in_specs=[pl.BlockSpec((tr, Dm), lambda i: (i, 0)), pl.BlockSpec((1, Dm), lambda i: (0, 0))],
        out_specs=pl.BlockSpec((tr, Dm), lambda i: (i, 0)),
        out_shape=_sds((S, Dm), BF16),
        compiler_params=_cparams(("parallel",)),
    )(x, g)


def _rms_bwd(name, dh, x, g, dres, tr, want_dx=True):
    S, Dm = x.shape
    nsteps = S // tr

    def body(*refs):
        if want_dx:
            dh_ref, x_ref, g_ref, dres_ref, dx_ref, dxb_ref, dg_ref, acc_ref = refs
        else:
            dh_ref, x_ref, g_ref, dg_ref, acc_ref = refs
        i = pl.program_id(0)
        xv = x_ref[...]
        r = lax.rsqrt(jnp.mean(xv * xv, axis=-1, keepdims=True) + EPS)
        xh = xv * r
        dhv = dh_ref[...]
        part = _rowsum8(dhv * xh)

        @pl.when(i == 0)
        def _():
            acc_ref[...] = part

        @pl.when(i > 0)
        def _():
            acc_ref[...] += part

        @pl.when(i == nsteps - 1)
        def _():
            dg_ref[...] = jnp.sum(acc_ref[...], axis=0, keepdims=True)

        if want_dx:
            dxh = dhv * g_ref[...]
            m = jnp.mean(dxh * xh, axis=-1, keepdims=True)
            dx = dres_ref[...] + r * (dxh - xh * m)
            dx_ref[...] = dx
            dxb_ref[...] = dx.astype(BF16)

    row = pl.BlockSpec((tr, Dm), lambda i: (i, 0))
    vec = pl.BlockSpec((1, Dm), lambda i: (0, 0))
    if want_dx:
        in_specs = [row, row, vec, row]
        out_specs = [row, row, vec]
        out_shape = [_sds((S, Dm), F32), _sds((S, Dm), BF16), _sds((1, Dm), F32)]
        args = (dh, x, g, dres)
    else:
        in_specs = [row, row, vec]
        out_specs = [vec]
        out_shape = [_sds((1, Dm), F32)]
        args = (dh, x, g)
    return _tc_call(
        body, name=name, grid=(nsteps,), in_specs=in_specs, out_specs=out_specs, out_shape=out_shape,
        scratch_shapes=[pltpu.VMEM((8, Dm), F32)],
        compiler_params=_cparams(("arbitrary",)),
    )(*args)


def _final_loss(name, x3, g, target, tr):
    S, Dm = x3.shape
    nsteps = S // tr

    def body(x_ref, g_ref, t_ref, dx_ref, dxb_ref, dg_ref, loss_ref, acc_g, acc_l):
        i = pl.program_id(0)
        xv = x_ref[...]
        gv = g_ref[...]
        r = lax.rsqrt(jnp.mean(xv * xv, axis=-1, keepdims=True) + EPS)
        xh = xv * r
        e = xh * gv - t_ref[...]
        dy = e * (1.0 / Dm)
        lpart = _rowsum8(e * e)
        gpart = _rowsum8(dy * xh)

        @pl.when(i == 0)
        def _():
            acc_g[...] = gpart
            acc_l[...] = lpart

        @pl.when(i > 0)
        def _():
            acc_g[...] += gpart
            acc_l[...] += lpart

        @pl.when(i == nsteps - 1)
        def _():
            dg_ref[...] = jnp.sum(acc_g[...], axis=0, keepdims=True)
            tot = jnp.sum(jnp.sum(acc_l[...], axis=1, keepdims=True), axis=0, keepdims=True)
            loss_ref[...] = tot * (0.5 / Dm)

        dxh = dy * gv
        m = jnp.mean(dxh * xh, axis=-1, keepdims=True)
        dx = r * (dxh - xh * m)
        dx_ref[...] = dx
        dxb_ref[...] = dx.astype(BF16)

    row = pl.BlockSpec((tr, Dm), lambda i: (i, 0))
    vec = pl.BlockSpec((1, Dm), lambda i: (0, 0))
    return _tc_call(
        body, name=name, grid=(nsteps,),
        in_specs=[row, vec, row],
        out_specs=[row, row, vec, pl.BlockSpec((1, 1), lambda i: (0, 0))],
        out_shape=[_sds((S, Dm), F32), _sds((S, Dm), BF16), _sds((1, Dm), F32), _sds((1, 1), F32)],
        scratch_shapes=[pltpu.VMEM((8, Dm), F32), pltpu.VMEM((8, Dm), F32)],
        compiler_params=_cparams(("arbitrary",)),
    )(x3, g, target)


def _softmax_rows(s):
    e = jnp.exp(s - jnp.max(s, axis=-1, keepdims=True))
    return e / jnp.sum(e, axis=-1, keepdims=True)


def _attn_fwd(name, q, k, v, ts):
    S, Dm = q.shape
    M = k.shape[0]
    scale = HEAD_DIM ** -0.5

    def body(q_ref, k_ref, v_ref, o_ref):
        for h in range(N_HEADS):
            sl = slice(h * HEAD_DIM, (h + 1) * HEAD_DIM)
            s = lax.dot_general(q_ref[:, sl], k_ref[:, sl], _DN["nt"], preferred_element_type=F32) * scale
            p = _softmax_rows(s)
            o_ref[:, sl] = jnp.dot(p.astype(BF16), v_ref[:, sl], preferred_element_type=F32).astype(o_ref.dtype)

    row = pl.BlockSpec((ts, Dm), lambda i: (i, 0))
    mem = pl.BlockSpec((M, Dm), lambda i: (0, 0))
    return _tc_call(
        body, name=name, grid=(S // ts,), in_specs=[row, mem, mem], out_specs=row,
        out_shape=_sds((S, Dm), BF16), compiler_params=_cparams(("parallel",)),
    )(q, k, v)


def _attn_bwd(name, q, k, v, do, ts):
    S, Dm = q.shape
    M = k.shape[0]
    scale = HEAD_DIM ** -0.5

    def body(q_ref, k_ref, v_ref, do_ref, dq_ref, dk_ref, dv_ref):
        i = pl.program_id(0)

        @pl.when(i == 0)
        def _():
            dk_ref[...] = jnp.zeros_like(dk_ref)
            dv_ref[...] = jnp.zeros_like(dv_ref)

        for h in range(N_HEADS):
            sl = slice(h * HEAD_DIM, (h + 1) * HEAD_DIM)
            qh = q_ref[:, sl]
            kh = k_ref[:, sl]
            doh = do_ref[:, sl]
            s = lax.dot_general(qh, kh, _DN["nt"], preferred_element_type=F32) * scale
            p = _softmax_rows(s)
            dp = lax.dot_general(doh, v_ref[:, sl], _DN["nt"], preferred_element_type=F32)
            ds = p * (dp - jnp.sum(dp * p, axis=-1, keepdims=True)) * scale
            dsb = ds.astype(BF16)
            dq_ref[:, sl] = jnp.dot(dsb, kh, preferred_element_type=F32).astype(dq_ref.dtype)
            dk_ref[:, sl] += lax.dot_general(dsb, qh, _DN["tn"], preferred_element_type=F32)
            dv_ref[:, sl] += lax.dot_general(p.astype(BF16), doh, _DN["tn"], preferred_element_type=F32)

    row = pl.BlockSpec((ts, Dm), lambda i: (i, 0))
    mem = pl.BlockSpec((M, Dm), lambda i: (0, 0))
    return _tc_call(
        body, name=name, grid=(S // ts,), in_specs=[row, mem, mem, row], out_specs=[row, mem, mem],
        out_shape=[_sds((S, Dm), BF16), _sds((M, Dm), F32), _sds((M, Dm), F32)],
        compiler_params=_cparams(("arbitrary",)),
    )(q, k, v, do)


def _pool_denominators(row0, ts):
    return (row0 + lax.broadcasted_iota(jnp.int32, (ts, 1), 0) + 1).astype(F32)


def _mixer_fwd(name, proj, pool_w, pool_scale, sgu_g, ws, bias_full, ts):
    S = proj.shape[0]
    nblk = ts // SGU_BLOCK
    halo_blocks = ts // POOL_HALO

    def body(proj_ref, halo_ref, pw_ref, sc_ref, g_ref, ws_ref, b_ref, y_ref, p_ref, vn_ref, ext_ref):
        i = pl.program_id(0)
        a = proj_ref[:, 0:D_POOL]
        ext_ref[0:POOL_HALO, :] = jnp.where(i > 0, halo_ref[...], 0.0)
        ext_ref[POOL_HALO:POOL_HALO + ts, :] = a
        pos = _pool_denominators(i * ts, ts)
        for gi, w in enumerate(POOL_WINDOWS):
            cs = slice(gi * POOL_GROUP, (gi + 1) * POOL_GROUP)
            acc = a[:, cs]
            for j in range(1, w):
                acc = acc + ext_ref[POOL_HALO - j:POOL_HALO - j + ts, cs]
            pg = (acc / jnp.minimum(pos, float(w)) - a[:, cs]).astype(BF16)
            p_ref[:, cs] = pg
            ypre = jnp.dot(pg, pw_ref[gi], preferred_element_type=F32)
            y_ref[:, cs] = (ypre * sc_ref[:, cs]).astype(y_ref.dtype)

        v = proj_ref[:, D_POOL + D_SGU:D_POOL + 2 * D_SGU]
        r = lax.rsqrt(jnp.mean(v * v, axis=-1, keepdims=True) + EPS)
        vn_ref[...] = (v * r * g_ref[...]).astype(BF16)
        for n in range(nblk):
            rs = slice(n * SGU_BLOCK, (n + 1) * SGU_BLOCK)
            for h in range(N_SGU_HEADS):
                cs = slice(h * SGU_BLOCK, (h + 1) * SGU_BLOCK)
                mixed = jnp.dot(ws_ref[h], vn_ref[rs, cs], preferred_element_type=F32) + b_ref[:, cs]
                u = proj_ref[rs, D_POOL + h * SGU_BLOCK:D_POOL + (h + 1) * SGU_BLOCK]
                y_ref[rs, D_POOL + h * SGU_BLOCK:D_POOL + (h + 1) * SGU_BLOCK] = (u * mixed).astype(y_ref.dtype)

    return _tc_call(
        body, name=name, grid=(S // ts,),
        in_specs=[
            pl.BlockSpec((ts, D_POOL + 2 * D_SGU), lambda i: (i, 0)),
            pl.BlockSpec((POOL_HALO, D_POOL), lambda i: (jnp.maximum(i * halo_blocks - 1, 0), 0)),
            pl.BlockSpec((4, POOL_GROUP, POOL_GROUP), lambda i: (0, 0, 0)),
            pl.BlockSpec((1, D_POOL), lambda i: (0, 0)),
            pl.BlockSpec((1, D_SGU), lambda i: (0, 0)),
            pl.BlockSpec((N_SGU_HEADS, SGU_BLOCK, SGU_BLOCK), lambda i: (0, 0, 0)),
            pl.BlockSpec((SGU_BLOCK, D_SGU), lambda i: (0, 0)),
        ],
        out_specs=[
            pl.BlockSpec((ts, D_MODEL), lambda i: (i, 0)),
            pl.BlockSpec((ts, D_POOL), lambda i: (i, 0)),
            pl.BlockSpec((ts, D_SGU), lambda i: (i, 0)),
        ],
        out_shape=[_sds((S, D_MODEL), BF16), _sds((S, D_POOL), BF16), _sds((S, D_SGU), BF16)],
        scratch_shapes=[pltpu.VMEM((ts + POOL_HALO, D_POOL), F32)],
        compiler_params=_cparams(("parallel",)),
    )(proj, proj, pool_w, pool_scale, sgu_g, ws, bias_full)


def _mixer_bwd(name, dymix, proj, p, vn, pool_w, pool_scale, sgu_g, ws, bias_full, ts):
    S = proj.shape[0]
    nsteps = S // ts
    nblk = ts // SGU_BLOCK
    halo_blocks = ts // POOL_HALO

    def body(dy_ref, dyh_ref, u_ref, v_ref, p_ref, vn_ref, pw_ref, sc_ref, g_ref, ws_ref, b_ref,
             dproj_ref, dpw_ref, dsc_ref, dg_ref, dws_ref, db_ref,
             ext_ref, dvn_ref, acc_sc, acc_g, acc_b):
        i = pl.program_id(0)

        @pl.when(i == 0)
        def _():
            dpw_ref[...] = jnp.zeros_like(dpw_ref)
            dws_ref[...] = jnp.zeros_like(dws_ref)
            acc_sc[...] = jnp.zeros_like(acc_sc)
            acc_g[...] = jnp.zeros_like(acc_g)
            acc_b[...] = jnp.zeros_like(acc_b)

        pos = _pool_denominators(i * ts, ts)
        pos_h = _pool_denominators((i + 1) * ts, POOL_HALO)
        for gi, w in enumerate(POOL_WINDOWS):
            cs = slice(gi * POOL_GROUP, (gi + 1) * POOL_GROUP)
            pg = p_ref[:, cs]
            wg = pw_ref[gi]
            dyp = dy_ref[:, cs]
            ypre = jnp.dot(pg, wg, preferred_element_type=F32)
            acc_sc[:, cs] += _rowsum8(dyp * ypre)
            dz = (dyp * sc_ref[:, cs]).astype(BF16)
            dpw_ref[gi] += lax.dot_general(pg, dz, _DN["tn"], preferred_element_type=F32)
            dp = lax.dot_general(dz, wg, _DN["nt"], preferred_element_type=F32)
            dzh = (dyh_ref[:, cs] * sc_ref[:, cs]).astype(BF16)
            dph = lax.dot_general(dzh, wg, _DN["nt"], preferred_element_type=F32)
            ext_ref[0:ts, cs] = dp / jnp.minimum(pos, float(w))
            ext_ref[ts:ts + POOL_HALO, cs] = jnp.where(i < nsteps - 1, dph / jnp.minimum(pos_h, float(w)), 0.0)
            acc = ext_ref[0:ts, cs]
            for j in range(1, w):
                acc = acc + ext_ref[j:j + ts, cs]
            dproj_ref[:, cs] = (acc - dp).astype(dproj_ref.dtype)

        for n in range(nblk):
            rs = slice(n * SGU_BLOCK, (n + 1) * SGU_BLOCK)
            for h in range(N_SGU_HEADS):
                cs = slice(h * SGU_BLOCK, (h + 1) * SGU_BLOCK)
                vnb = vn_ref[rs, cs]
                wh = ws_ref[h]
                mixed = jnp.dot(wh, vnb, preferred_element_type=F32) + b_ref[:, cs]
                dys = dy_ref[rs, D_POOL + h * SGU_BLOCK:D_POOL + (h + 1) * SGU_BLOCK]
                dproj_ref[rs, D_POOL + h * SGU_BLOCK:D_POOL + (h + 1) * SGU_BLOCK] = (dys * mixed).astype(dproj_ref.dtype)
                dmix = dys * u_ref[rs, cs]
                acc_b[:, cs] += dmix
                dmb = dmix.astype(BF16)
                dws_ref[h] += lax.dot_general(dmb, vnb, _DN["nt"], preferred_element_type=F32)
                dvn_ref[rs, cs] = lax.dot_general(wh, dmb, _DN["tn"], preferred_element_type=F32)
        v = v_ref[...]
        r = lax.rsqrt(jnp.mean(v * v, axis=-1, keepdims=True) + EPS)
        vh = v * r
        dvn = dvn_ref[...]
        acc_g[...] += _rowsum8(dvn * vh)
        dxh = dvn * g_ref[...]
        m = jnp.mean(dxh * vh, axis=-1, keepdims=True)
        dproj_ref[:, D_POOL + D_SGU:D_POOL + 2 * D_SGU] = (r * (dxh - vh * m)).astype(dproj_ref.dtype)

        @pl.when(i == nsteps - 1)
        def _():
            dsc_ref[...] = jnp.sum(acc_sc[...], axis=0, keepdims=True)
            dg_ref[...] = jnp.sum(acc_g[...], axis=0, keepdims=True)
            t_idx = lax.broadcasted_iota(jnp.int32, (SGU_BLOCK, SGU_BLOCK), 0) // SGU_CHUNK
            s_idx = lax.broadcasted_iota(jnp.int32, (SGU_BLOCK, SGU_BLOCK), 1) // SGU_CHUNK
            mask = s_idx <= t_idx
            for h in range(N_SGU_HEADS):
                cs = slice(h * SGU_BLOCK, (h + 1) * SGU_BLOCK)
                dws_ref[h] = jnp.where(mask, dws_ref[h], 0.0)
                col = jnp.sum(acc_b[:, cs], axis=1, keepdims=True)
                db_ref[h] = jnp.broadcast_to(col, (SGU_BLOCK, SGU_BLOCK))

    const2 = lambda i: (0, 0)
    const3 = lambda i: (0, 0, 0)
    last_halo = S // POOL_HALO - 1
    return _tc_call(
        body, name=name, grid=(nsteps,),
        in_specs=[
            pl.BlockSpec((ts, D_MODEL), lambda i: (i, 0)),
            pl.BlockSpec((POOL_HALO, D_POOL), lambda i: (jnp.minimum((i + 1) * halo_blocks, last_halo), 0)),
            pl.BlockSpec((ts, D_SGU), lambda i: (i, 1)),
            pl.BlockSpec((ts, D_SGU), lambda i: (i, 2)),
            pl.BlockSpec((ts, D_POOL), lambda i: (i, 0)),
            pl.BlockSpec((ts, D_SGU), lambda i: (i, 0)),
            pl.BlockSpec((4, POOL_GROUP, POOL_GROUP), const3),
            pl.BlockSpec((1, D_POOL), const2),
            pl.BlockSpec((1, D_SGU), const2),
            pl.BlockSpec((N_SGU_HEADS, SGU_BLOCK, SGU_BLOCK), const3),
            pl.BlockSpec((SGU_BLOCK, D_SGU), const2),
        ],
        out_specs=[
            pl.BlockSpec((ts, D_POOL + 2 * D_SGU), lambda i: (i, 0)),
            pl.BlockSpec((4, POOL_GROUP, POOL_GROUP), const3),
            pl.BlockSpec((1, D_POOL), const2),
            pl.BlockSpec((1, D_SGU), const2),
            pl.BlockSpec((N_SGU_HEADS, SGU_BLOCK, SGU_BLOCK), const3),
            pl.BlockSpec((N_SGU_HEADS, SGU_BLOCK, SGU_BLOCK), const3),
        ],
        out_shape=[
            _sds((S, D_POOL + 2 * D_SGU), BF16),
            _sds((4, POOL_GROUP, POOL_GROUP), F32),
            _sds((1, D_POOL), F32),
            _sds((1, D_SGU), F32),
            _sds((N_SGU_HEADS, SGU_BLOCK, SGU_BLOCK), F32),
            _sds((N_SGU_HEADS, SGU_BLOCK, SGU_BLOCK), F32),
        ],
        scratch_shapes=[
            pltpu.VMEM((ts + POOL_HALO, D_POOL), F32),
            pltpu.VMEM((ts, D_SGU), F32),
            pltpu.VMEM((8, D_POOL), F32),
            pltpu.VMEM((8, D_SGU), F32),
            pltpu.VMEM((SGU_BLOCK, D_SGU), F32),
        ],
        compiler_params=_cparams(("arbitrary",)),
    )(dymix, dymix, proj, proj, p, vn, pool_w, pool_scale, sgu_g, ws, bias_full)


def _silu_mul(accs, _):
    gt, up = accs
    sig = 1.0 / (1.0 + jnp.exp(-gt))
    return gt, up, gt * sig * up


def _silu_mul_bwd(accs, extras):
    (dact,) = accs
    gt = extras[0].astype(F32)
    up = extras[1].astype(F32)
    sig = 1.0 / (1.0 + jnp.exp(-gt))
    silu = gt * sig
    dgt = dact * up * (sig * (1.0 + gt * (1.0 - sig)))
    dup = dact * silu
    return dgt, dup


def _add_residual(accs, extras):
    return (extras[0] + accs[0],)


def _add_residual_and_cast(accs, extras):
    y = extras[0] + accs[0]
    return y, y


def _local_step(x, mem, target, W, sm, rs):
    S = x.shape[0]
    tm = min(1024, S)
    th = min(512, S)
    ts = min(512, S)
    tr = min(512, S)
    tk_s = min(2048, S)
    M = mem.shape[0]

    h1 = _rms_fwd("rms_mix", x, sm["norm_mix_g"], tr)
    proj = _mm1("proj_in", h1, W["w_in_t"], "nt", tm, 1024, 2048, F32)
    ymix, p, vn = _mixer_fwd("mixer_fwd", proj, W["pool_w"], sm["pool_scale"], sm["sgu_norm_g"],
                             sm["ws_masked"], sm["bias_full"], ts)
    x1 = _mm1("proj_out", ymix, W["w_out"], "nn", tm, 1024, 2048, F32, epilogue=_add_residual, extras=(x,))

    h2 = _rms_fwd("rms_xattn", x1, sm["norm_xattn_g"], tr)
    mb = _rms_fwd("rms_mem", mem, sm["norm_mem_g"], M)
    q = _mm1("proj_q", h2, W["w_q"], "nn", tm, 1024, 2048, BF16)
    kk = _mm1("proj_k", mb, W["w_k"], "nn", M, 1024, 2048, BF16)
    vv = _mm1("proj_v", mb, W["w_v"], "nn", M, 1024, 2048, BF16)
    o = _attn_fwd("attn_fwd", q, kk, vv, ts)
    x2 = _mm1("proj_o", o, W["w_o"], "nn", tm, 1024, 2048, F32, epilogue=_add_residual, extras=(x1,))

    h3 = _rms_fwd("rms_ffn", x2, sm["norm_ffn_g"], tr)
    gt, up, act = _mm("ffn_gate_up", [h3], [W["w_gate_t"], W["w_up_t"]], [(0, 0, 0), (0, 1, 1)], "nt",
                      tm, 512, 2048, [BF16, BF16, BF16], epilogue=_silu_mul, n_acc=2)
    x3 = _mm1("ffn_down", act, W["w_down"], "nn", th, 1024, 2816, F32, epilogue=_add_residual, extras=(x2,))

    dx3, dx3b, d_final_g, loss = _final_loss("final_loss", x3, sm["final_norm_g"], target, tr)

    dgt, dup = _mm("ffn_down_dgrad", [dx3b], [W["w_down"]], [(0, 0, 0)], "nt", tm, 512, 2048, [BF16, BF16],
                   epilogue=_silu_mul_bwd, extras=(gt, up))
    rs.push("w_down", _mm1("ffn_down_wgrad", act, dx3b, "tn", 1408, 1024, tk_s, BF16))
    rs.push("w_gate_t", _mm1("ffn_gate_wgrad", dgt, h3, "tn", 1408, 1024, tk_s, BF16))
    rs.reduce("w_down")
    rs.push("w_up_t", _mm1("ffn_up_wgrad", dup, h3, "tn", 1408, 1024, tk_s, BF16))
    rs.reduce("w_gate_t")
    dh3 = _mm("ffn_gate_up_dgrad", [dgt, dup], [W["w_gate_t"], W["w_up_t"]], [(0, 0, 0), (1, 1, 0)], "nn",
              th, 1024, 1408, [F32])[0]
    rs.reduce("w_up_t")
    dx2, dx2b, d_ffn_g = _rms_bwd("rms_ffn_bwd", dh3, x2, sm["norm_ffn_g"], dx3, tr)
    rs.finish("w_down")

    rs.push("w_o", _mm1("proj_o_wgrad", o, dx2b, "tn", 1024, 1024, tk_s, BF16))
    rs.finish("w_gate_t")
    do = _mm1("proj_o_dgrad", dx2b, W["w_o"], "nt", tm, 1024, 2048, BF16)
    rs.reduce("w_o")
    dq, dk, dv = _attn_bwd("attn_bwd", q, kk, vv, do, ts)
    rs.push("w_q", _mm1("proj_q_wgrad", h2, dq, "tn", 1024, 1024, tk_s, BF16))
    rs.push("w_k", _mm1("proj_k_wgrad", mb, dk, "tn", 1024, 1024, M, BF16))
    rs.push("w_v", _mm1("proj_v_wgrad", mb, dv, "tn", 1024, 1024, M, BF16))
    rs.finish("w_up_t")
    dh2 = _mm1("proj_q_dgrad", dq, W["w_q"], "nt", tm, 1024, 2048, F32)
    rs.reduce("w_q")
    rs.reduce("w_k")
    rs.reduce("w_v")
    dmb = _mm("proj_kv_dgrad", [dk, dv], [W["w_k"], W["w_v"]], [(0, 0, 0), (1, 1, 0)], "nt",
              M, 1024, 2048, [F32])[0]
    (d_mem_g,) = _rms_bwd("rms_mem_bwd", dmb, mem, sm["norm_mem_g"], None, M, want_dx=False)
    dx1, dx1b, d_xattn_g = _rms_bwd("rms_xattn_bwd", dh2, x1, sm["norm_xattn_g"], dx2, tr)
    rs.finish("w_o")

    rs.push("w_out", _mm1("proj_out_wgrad", ymix, dx1b, "tn", 1024, 1024, tk_s, BF16))
    dymix = _mm1("proj_out_dgrad", dx1b, W["w_out"], "nt", tm, 1024, 2048, F32)
    rs.reduce("w_out")
    dproj, d_pool_w, d_pool_scale, d_sgu_g, d_ws, d_b = _mixer_bwd(
        "mixer_bwd", dymix, proj, p, vn, W["pool_w"], sm["pool_scale"], sm["sgu_norm_g"],
        sm["ws_masked"], sm["bias_full"], ts)
    rs.small("early", dict(
        pool_w=d_pool_w, pool_scale=d_pool_scale, sgu_norm_g=d_sgu_g, w_spatial=d_ws, b_spatial=d_b[:, :, 0],
        norm_xattn_g=d_xattn_g, norm_mem_g=d_mem_g, norm_ffn_g=d_ffn_g, final_norm_g=d_final_g))
    rs.push("w_in_t", _mm1("proj_in_wgrad", dproj, h1, "tn", 1024, 1024, tk_s, BF16))
    dh1 = _mm1("proj_in_dgrad", dproj, W["w_in_t"], "nn", tm, 1024, 3072, F32)
    rs.finish("w_q")
    rs.finish("w_k")
    rs.finish("w_v")
    rs.reduce("w_in_t")
    grad_x, _, d_mix_g = _rms_bwd("rms_mix_bwd", dh1, x, sm["norm_mix_g"], dx1, tr)
    rs.small("late", dict(norm_mix_g=d_mix_g))
    rs.finish("w_out")
    rs.finish_small("early")
    rs.finish("w_in_t")
    rs.finish_small("late")
    return loss, grad_x


def _mesh_pos():
    return lax.axis_index("x"), lax.axis_index("y"), lax.axis_index("c")


def _handshake(peers):
    barrier = pltpu.get_barrier_semaphore()
    for peer in peers:
        pl.semaphore_signal(barrier, inc=1, device_id=peer, device_id_type=MESH)
    pl.semaphore_wait(barrier, len(peers))


def _seq_all_gather(name, shards, collective_id):
    n = len(shards)

    def body(*refs):
        ins = refs[:n]
        outs = refs[n:2 * n]
        send_sems, recv_sems, local_sems = refs[2 * n:]
        x, y, c = _mesh_pos()
        me, sibling = (x, y, c), (x, y, 1 - c)
        chips = [(1 - x, y), (x, 1 - y), (1 - x, 1 - y)]
        _handshake([sibling] + [(*chip, c) for chip in chips])

        def copy(a, k, block, to, src=None):
            bx, by, bc = block
            dst = outs[a].at[4 * bx + 2 * by + bc]
            return pltpu.make_async_remote_copy(
                src_ref=dst if src is None else src, dst_ref=dst,
                send_sem=send_sems.at[a, k], recv_sem=recv_sems.at[a, k],
                device_id=to, device_id_type=MESH)

        mine = [pltpu.make_async_copy(ins[a], outs[a].at[4 * x + 2 * y + c], local_sems.at[a]) for a in range(n)]
        for cp in mine:
            cp.start()
        started = []
        for a in range(n):
            first = [copy(a, 0, me, sibling, src=ins[a])]
            first += [copy(a, 1 + j, me, (*chip, c), src=ins[a]) for j, chip in enumerate(chips)]
            for cp in first:
                cp.start()
            started += first
        for a in range(n):
            for j, chip in enumerate(chips):
                copy(a, 1 + j, (*chip, c), me).wait_recv()
                fwd = copy(a, 4 + j, (*chip, c), sibling)
                fwd.start()
                started.append(fwd)
        for a in range(n):
            copy(a, 0, sibling, me).wait_recv()
            for j, chip in enumerate(chips):
                copy(a, 4 + j, (*chip, 1 - c), me).wait_recv()
        for cp in started:
            cp.wait_send()
        for cp in mine:
            cp.wait()

    return _sc_call(
        body, name=name,
        out_type=[_sds((N_DEV,) + s.shape, s.dtype) for s in shards],
        scratch_types=[pltpu.SemaphoreType.DMA((n, 7)), pltpu.SemaphoreType.DMA((n, 7)),
                       pltpu.SemaphoreType.DMA((n,))],
        compiler_params=pltpu.CompilerParams(collective_id=collective_id),
    )(*shards)


def _seq_pair_exchange(name, gview, collective_id):
    def body(g_ref, theirs_ref, send_sems, recv_sems):
        x, y, c = _mesh_pos()
        sibling = (x, y, 1 - c)
        _handshake([sibling])
        copies = [pltpu.make_async_remote_copy(
            src_ref=g_ref.at[k, 1 - c], dst_ref=theirs_ref.at[k],
            send_sem=send_sems.at[k], recv_sem=recv_sems.at[k],
            device_id=sibling, device_id_type=MESH) for k in range(4)]
        for cp in copies:
            cp.start()
        for cp in copies:
            cp.wait()

    return _sc_call(
        body, name=name, out_type=_sds((4,) + gview.shape[2:], gview.dtype),
        scratch_types=[pltpu.SemaphoreType.DMA((4,)), pltpu.SemaphoreType.DMA((4,))],
        compiler_params=pltpu.CompilerParams(collective_id=collective_id),
    )(gview)


def _pair_sum(name, gview, theirs, pos, tr):
    _, _, r, C = gview.shape

    def body(pos_ref, a_ref, b_ref, o_ref):
        o_ref[...] = (a_ref[...].astype(F32) + b_ref[...].astype(F32)).astype(o_ref.dtype)

    grid_spec = pltpu.PrefetchScalarGridSpec(
        num_scalar_prefetch=1, grid=(4, r // tr),
        in_specs=[pl.BlockSpec((None, None, tr, C), lambda k, t, pos_ref: (k, pos_ref[0], t, 0)),
                  pl.BlockSpec((None, tr, C), lambda k, t, pos_ref: (k, t, 0))],
        out_specs=pl.BlockSpec((None, tr, C), lambda k, t, pos_ref: (k, t, 0)))
    return _tc_call(
        body, name=name, grid_spec=grid_spec, out_shape=_sds(theirs.shape, theirs.dtype),
        compiler_params=_cparams(("parallel", "parallel")),
    )(pos, gview, theirs)


def _seq_chip_exchange(name, pair, collective_id):
    def body(p_ref, land_ref, send_sems, recv_sems):
        x, y, c = _mesh_pos()
        my_chip = 2 * x + y
        chips = [(1 - x, y), (x, 1 - y), (1 - x, 1 - y)]
        _handshake([(cx, cy, c) for cx, cy in chips])
        copies = [pltpu.make_async_remote_copy(
            src_ref=p_ref.at[2 * cx + cy], dst_ref=land_ref.at[my_chip],
            send_sem=send_sems.at[j], recv_sem=recv_sems.at[j],
            device_id=(cx, cy, c), device_id_type=MESH) for j, (cx, cy) in enumerate(chips)]
        for cp in copies:
            cp.start()
        for cp in copies:
            cp.wait_send()
        for j, (cx, cy) in enumerate(chips):
            pltpu.make_async_remote_copy(
                src_ref=p_ref.at[my_chip], dst_ref=land_ref.at[2 * cx + cy],
                send_sem=send_sems.at[j], recv_sem=recv_sems.at[j],
                device_id=(cx, cy, c), device_id_type=MESH).wait_recv()

    return _sc_call(
        body, name=name, out_type=_sds(pair.shape, pair.dtype),
        scratch_types=[pltpu.SemaphoreType.DMA((3,)), pltpu.SemaphoreType.DMA((3,))],
        compiler_params=pltpu.CompilerParams(collective_id=collective_id),
    )(pair)


def _chip_sum(name, pair, landed, pos, tr):
    _, r, C = pair.shape

    def body(pos_ref, own_ref, land_ref, o_ref, acc_ref):
        k = pl.program_id(1)
        val = jnp.where(k == pos_ref[1], own_ref[...], land_ref[...]).astype(F32)

        @pl.when(k == 0)
        def _():
            acc_ref[...] = val

        @pl.when(k > 0)
        def _():
            acc_ref[...] += val

        @pl.when(k == 3)
        def _():
            o_ref[...] = acc_ref[...]

    def land_index(t, k, pos_ref):
        return (jnp.where(k == pos_ref[1], (k + 1) % 4, k), t, 0)

    grid_spec = pltpu.PrefetchScalarGridSpec(
        num_scalar_prefetch=1, grid=(r // tr, 4),
        in_specs=[pl.BlockSpec((None, tr, C), lambda t, k, pos_ref: (pos_ref[1], t, 0)),
                  pl.BlockSpec((None, tr, C), land_index)],
        out_specs=pl.BlockSpec((tr, C), lambda t, k, pos_ref: (t, 0)),
        scratch_shapes=[pltpu.VMEM((tr, C), F32)])
    return _tc_call(
        body, name=name, grid_spec=grid_spec, out_shape=_sds((r, C), F32),
        compiler_params=_cparams(("parallel", "arbitrary")),
    )(pos, pair, landed)


def _sum_leading(name, parts, tr, out_dtype=F32):
    n, r, C = parts.shape

    def body(p_ref, o_ref):
        acc = p_ref[0].astype(F32)
        for k in range(1, n):
            acc = acc + p_ref[k].astype(F32)
        o_ref[...] = acc.astype(o_ref.dtype)

    return _tc_call(
        body, name=name, grid=(r // tr,),
        in_specs=[pl.BlockSpec((n, tr, C), lambda t: (0, t, 0))],
        out_specs=pl.BlockSpec((tr, C), lambda t: (t, 0)),
        out_shape=_sds((r, C), out_dtype), compiler_params=_cparams(("parallel",)),
    )(parts)


def _row_tile(r):
    for t in (512, 384, 352, 256, 128, 64, 32, 16, 8):
        if r % t == 0:
            return t
    return r


def _adamw(name, w, g, m, v):
    R, C = w.shape
    tr = _row_tile(R)
    c1 = 1.0 - ADAM_B1 ** ADAM_STEP
    c2 = 1.0 - ADAM_B2 ** ADAM_STEP

    def body(w_ref, g_ref, m_ref, v_ref, d_ref, nm_ref, nv_ref):
        gv = g_ref[...]
        nm = ADAM_B1 * m_ref[...] + (1.0 - ADAM_B1) * gv
        nv = ADAM_B2 * v_ref[...] + (1.0 - ADAM_B2) * (gv * gv)
        m_hat = nm / c1
        v_hat = nv / c2
        d_ref[...] = -ADAM_LR * (m_hat / (jnp.sqrt(v_hat) + ADAM_EPS) + ADAM_WD * w_ref[...])
        nm_ref[...] = nm
        nv_ref[...] = nv

    spec = pl.BlockSpec((tr, C), lambda i: (i, 0))
    return _tc_call(
        body, name=name, grid=(R // tr,), in_specs=[spec] * 4, out_specs=[spec] * 3,
        out_shape=[_sds((R, C), F32)] * 3, compiler_params=_cparams(("parallel",)),
    )(w, g, m, v)


_BIG = ("w_in_t", "w_out", "w_q", "w_k", "w_v", "w_o", "w_gate_t", "w_up_t", "w_down")
_SMALL = ("norm_mix_g", "pool_scale", "sgu_norm_g", "w_spatial", "b_spatial", "norm_xattn_g",
          "norm_mem_g", "norm_ffn_g", "final_norm_g")
_LANES = 128
_GATHER_GROUPS = (("w_in_t", "pool_w"), ("w_out",), ("w_q",), ("w_k", "w_v"), ("w_o",), ("w_gate_t",),
                  ("w_up_t",), ("w_down",))
_RS_ORDER = ("w_down", "w_gate_t", "w_up_t", "w_o", "w_q", "w_k", "w_v", "w_out", "w_in_t")
_SMALL_GROUPS = dict(
    early=("pool_w", "pool_scale", "sgu_norm_g", "w_spatial", "b_spatial", "norm_xattn_g", "norm_mem_g",
           "norm_ffn_g", "final_norm_g"),
    late=("norm_mix_g",))
_ID_GATHER, _ID_PAIR, _ID_CHIP = 0, 1, 2


_PACK_ROWS = 512


def _pack(parts):
    rows = [p.reshape(-1, _LANES) for p in parts]
    n = sum(r.shape[0] for r in rows)
    pad = -n % (_PACK_ROWS if n > _PACK_ROWS else 8)
    if pad:
        rows.append(jnp.zeros((pad, _LANES), rows[0].dtype))
    return jnp.concatenate(rows, axis=0)


class _GradReducer:
    def __init__(self, pos, apply, apply_small):
        self.pos, self.apply, self.apply_small = pos, apply, apply_small
        self.view, self.theirs, self.pair, self.landed = {}, {}, {}, {}
        self.small_gathered = {}

    def push(self, k, g):
        r = g.shape[0] // N_DEV
        self.view[k] = g.reshape(4, 2, r, g.shape[1])
        self.theirs[k] = _seq_pair_exchange("grad_pair_exchange_" + k, self.view[k], _ID_PAIR)

    def reduce(self, k):
        r = self.view[k].shape[2]
        self.pair[k] = _pair_sum("grad_pair_sum_" + k, self.view[k], self.theirs[k], self.pos, _row_tile(r))
        self.landed[k] = _seq_chip_exchange("grad_chip_exchange_" + k, self.pair[k], _ID_CHIP)

    def finish(self, k):
        r = self.pair[k].shape[1]
        self.apply(k, _chip_sum("grad_chip_sum_" + k, self.pair[k], self.landed[k], self.pos, _row_tile(r)))

    def small(self, tag, parts):
        packed = _pack([parts[k] for k in _SMALL_GROUPS[tag]])
        (self.small_gathered[tag],) = _seq_all_gather("gather_small_grads_" + tag, [packed], _ID_GATHER)

    def finish_small(self, tag):
        allp = self.small_gathered[tag]
        self.apply_small(tag, _sum_leading("sum_small_grads_" + tag, allp, min(_PACK_ROWS, allp.shape[1])))


def _unpack(packed, like):
    out, row = [], 0
    for ref in like:
        rows = ref.size // _LANES
        out.append(packed[row:row + rows].reshape(ref.shape))
        row += rows
    return out


def kernel(x, mem, norm_mix_g, w_in, pool_w, pool_scale, sgu_norm_g, w_spatial, b_spatial, w_out, norm_xattn_g, norm_mem_g, w_q, w_k, w_v, w_o, norm_ffn_g, w_gate, w_up, w_down, final_norm_g, loss_target, m_norm_mix_g, m_w_in, m_pool_w, m_pool_scale, m_sgu_norm_g, m_w_spatial, m_b_spatial, m_w_out, m_norm_xattn_g, m_norm_mem_g, m_w_q, m_w_k, m_w_v, m_w_o, m_norm_ffn_g, m_w_gate, m_w_up, m_w_down, m_final_norm_g, v_norm_mix_g, v_w_in, v_pool_w, v_pool_scale, v_sgu_norm_g, v_w_spatial, v_b_spatial, v_w_out, v_norm_xattn_g, v_norm_mem_g, v_w_q, v_w_k, v_w_v, v_w_o, v_norm_ffn_g, v_w_gate, v_w_up, v_w_down, v_final_norm_g):
    args = dict(locals())
    names = ("norm_mix_g", "w_in", "pool_w", "pool_scale", "sgu_norm_g", "w_spatial", "b_spatial", "w_out",
             "norm_xattn_g", "norm_mem_g", "w_q", "w_k", "w_v", "w_o", "norm_ffn_g", "w_gate", "w_up",
             "w_down", "final_norm_g")
    w = {k: args[k] for k in names}
    m = {k: args["m_" + k] for k in names}
    v = {k: args["v_" + k] for k in names}
    my_dev = 4 * lax.axis_index("x") + 2 * lax.axis_index("y") + lax.axis_index("c")
    _CHAIN.__init__()

    shards = dict(
        w_in_t=w["w_in"][0].T, w_out=w["w_out"][0], w_q=w["w_q"][0], w_k=w["w_k"][0], w_v=w["w_v"][0],
        w_o=w["w_o"][0], w_gate_t=w["w_gate"][0].T, w_up_t=w["w_up"][0].T, w_down=w["w_down"][0])
    send = {k: shards[k].astype(BF16) for k in _BIG}
    send["pool_w"] = w["pool_w"][0].reshape(4 * 32, POOL_GROUP).astype(BF16)
    W = {}
    for gi, group in enumerate(_GATHER_GROUPS):
        gathered = _seq_all_gather("gather_weights_%d" % gi, [send[k] for k in group], _ID_GATHER)
        for k, g in zip(group, gathered):
            W[k] = g.reshape(-1, g.shape[-1])
    W["pool_w"] = W["pool_w"].reshape(N_DEV, 4, 32, POOL_GROUP).transpose(1, 0, 2, 3).reshape(4, POOL_GROUP, POOL_GROUP)

    t = jnp.arange(SGU_BLOCK)
    mask = (t[None, :] // SGU_CHUNK) <= (t[:, None] // SGU_CHUNK)
    sm = dict(
        norm_mix_g=w["norm_mix_g"], pool_scale=w["pool_scale"], sgu_norm_g=w["sgu_norm_g"],
        norm_xattn_g=w["norm_xattn_g"], norm_mem_g=w["norm_mem_g"], norm_ffn_g=w["norm_ffn_g"],
        final_norm_g=w["final_norm_g"].reshape(1, D_MODEL),
        ws_masked=jnp.where(mask[None], w["w_spatial"][0], 0.0).astype(BF16),
        bias_full=jnp.repeat(w["b_spatial"][0].T, SGU_BLOCK, axis=1))

    natural = dict(w_in_t="w_in", w_gate_t="w_gate", w_up_t="w_up")
    grads, delta, new_m, new_v = {}, {}, {}, {}

    def adamw(k):
        shp = w[k].shape
        two_d = (-1, shp[-1])
        d_, m_, v_ = _adamw("adamw_" + k, w[k].reshape(two_d), grads[k].reshape(two_d), m[k].reshape(two_d),
                            v[k].reshape(two_d))
        delta[k], new_m[k], new_v[k] = d_.reshape(shp), m_.reshape(shp), v_.reshape(shp)

    def apply(k, g):
        name = natural.get(k, k)
        grads[name] = (g.T if k in natural else g)[None]
        adamw(name)

    like = dict(w)
    like["pool_w"] = _sds((4, POOL_GROUP, POOL_GROUP), F32)

    def apply_small(tag, total):
        group = _SMALL_GROUPS[tag]
        grads.update(zip(group, _unpack(total, [like[k] for k in group])))
        if tag == "early":
            grads["pool_w"] = lax.dynamic_slice_in_dim(grads["pool_w"], my_dev * 32, 32, axis=1)[None]
            adamw("pool_w")
        else:
            d_, m_, v_ = _adamw("adamw_small", _pack([w[k] for k in _SMALL]), _pack([grads[k] for k in _SMALL]),
                                _pack([m[k] for k in _SMALL]), _pack([v[k] for k in _SMALL]))
            shapes = [w[k] for k in _SMALL]
            for k, a, b, c_ in zip(_SMALL, _unpack(d_, shapes), _unpack(m_, shapes), _unpack(v_, shapes)):
                delta[k], new_m[k], new_v[k] = a, b, c_

    pos = jnp.stack([lax.axis_index("c"), 2 * lax.axis_index("x") + lax.axis_index("y")]).astype(jnp.int32)
    rs = _GradReducer(pos, apply, apply_small)
    loss_part, grad_x = _local_step(x[0], mem[0], loss_target[0], W, sm, rs)
    loss = lax.psum(loss_part[0, 0], ("x", "y", "c"))

    outs = [loss, grad_x[None]]
    outs += [grads[k].reshape(w[k].shape) for k in names]
    outs += [delta[k] for k in names]
    outs += [new_m[k] for k in names]
    outs += [new_v[k] for k in names]
    return tuple(outs)
```

```python
import functools

import jax
import jax.numpy as jnp
from jax import lax
from jax.experimental import pallas as pl
from jax.experimental.pallas import tpu as pltpu
from jax.experimental.pallas import tpu_sc as plsc

F32 = jnp.float32
BF16 = jnp.bfloat16
MESH = pl.DeviceIdType.MESH

EPS = 1e-6
D_MODEL = 2048
D_POOL = 1024
D_SGU = 1024
POOL_WINDOWS = (2, 4, 8, 16)
POOL_GROUP = 256
POOL_HALO = 16
SGU_BLOCK = 128
SGU_CHUNK = 64
N_SGU_HEADS = 8
N_HEADS = 4
HEAD_DIM = 512
N_DEV = 8

ADAM_LR = 0.001
ADAM_B1 = 0.9
ADAM_B2 = 0.999
ADAM_EPS = 1e-08
ADAM_WD = 0.01
ADAM_STEP = 10

VMEM_LIMIT = 56 * 1024 * 1024


def _cparams(sem=None):
    return pltpu.CompilerParams(dimension_semantics=sem, vmem_limit_bytes=VMEM_LIMIT)


def _sds(shape, dtype):
    return jax.ShapeDtypeStruct(shape, dtype)


_ANY = pl.BlockSpec(memory_space=pl.ANY)


class _Chain:
    def __init__(self):
        self.tc = None
        self.sc = None


_CHAIN = _Chain()


def _first(out):
    return out[0] if isinstance(out, (list, tuple)) else out


def _tc_call(body, *, in_specs=None, grid_spec=None, **kw):
    def run(*args):
        prev, n = _CHAIN.tc, len(args)
        fn, specs, spec, operands = body, in_specs, grid_spec, args
        if prev is not None:
            def fn(*refs):
                return body(*refs[:n], *refs[n + 1:])
            operands = args + (prev,)
            if grid_spec is None:
                specs = list(in_specs) + [_ANY]
            else:
                spec = pltpu.PrefetchScalarGridSpec(
                    num_scalar_prefetch=grid_spec.num_scalar_prefetch, grid=grid_spec.grid,
                    in_specs=list(grid_spec.in_specs) + [_ANY], out_specs=grid_spec.out_specs,
                    scratch_shapes=grid_spec.scratch_shapes)
        if spec is None:
            out = pl.pallas_call(fn, in_specs=specs, **kw)(*operands)
        else:
            out = pl.pallas_call(fn, grid_spec=spec, **kw)(*operands)
        _CHAIN.tc = _first(out)
        return out
    return run


def _sc_call(body, **kw):
    return pl.kernel(body, mesh=plsc.ScalarSubcoreMesh(axis_name="seq", num_cores=1), **kw)


def _rowsum8(v):
    r, c = v.shape
    return v.reshape(r // 8, 8, c).sum(axis=0)


_EPILOGUE_COLS = 256
_DN = {
    "nn": (((1,), (0,)), ((), ())),
    "nt": (((1,), (1,)), ((), ())),
    "tn": (((0,), (0,)), ((), ())),
}


def _mm(name, a_list, b_list, terms, mode, tm, tn, tk, out_dtypes, epilogue=None, extras=(), n_acc=1):
    a0, b0 = a_list[0], b_list[0]
    if mode == "tn":
        K, M = a0.shape
    else:
        M, K = a0.shape
    N = b0.shape[0] if mode == "nt" else b0.shape[1]
    assert M % tm == 0 and N % tn == 0 and K % tk == 0, (name, M, N, K, tm, tn, tk)
    nk = K // tk
    na, nb, ne, no = len(a_list), len(b_list), len(extras), len(out_dtypes)
    dn = _DN[mode]

    if mode == "tn":
        a_spec = pl.BlockSpec((tk, tm), lambda i, j, k: (k, i))
    else:
        a_spec = pl.BlockSpec((tm, tk), lambda i, j, k: (i, k))
    if mode == "nt":
        b_spec = pl.BlockSpec((tn, tk), lambda i, j, k: (j, k))
    else:
        b_spec = pl.BlockSpec((tk, tn), lambda i, j, k: (k, j))
    o_spec = pl.BlockSpec((tm, tn), lambda i, j, k: (i, j))

    def body(*refs):
        a_refs = refs[:na]
        b_refs = refs[na:na + nb]
        e_refs = refs[na + nb:na + nb + ne]
        o_refs = refs[na + nb + ne:na + nb + ne + no]
        acc_refs = refs[na + nb + ne + no:]

        def products(cols):
            parts = [None] * n_acc
            for ai, bi, ci in terms:
                b = b_refs[bi][cols, :] if mode == "nt" else b_refs[bi][:, cols]
                d = lax.dot_general(a_refs[ai][...].astype(BF16), b.astype(BF16), dn, preferred_element_type=F32)
                parts[ci] = d if parts[ci] is None else parts[ci] + d
            return parts

        def finish(accs, cols=slice(None)):
            outs = epilogue(accs, [e[:, cols] for e in e_refs]) if epilogue is not None else accs
            for o_ref, v in zip(o_refs, outs):
                o_ref[:, cols] = v.astype(o_ref.dtype)

        if nk == 1 and epilogue is not None and tn > _EPILOGUE_COLS:
            for c0 in range(0, tn, _EPILOGUE_COLS):
                cols = slice(c0, c0 + _EPILOGUE_COLS)
                finish(products(cols), cols)
            return
        parts = products(slice(None))
        if nk == 1:
            finish(parts)
        else:
            k = pl.program_id(2)

            @pl.when(k == 0)
            def _():
                for c in range(n_acc):
                    acc_refs[c][...] = parts[c]

            @pl.when(k > 0)
            def _():
                for c in range(n_acc):
                    acc_refs[c][...] += parts[c]

            @pl.when(k == nk - 1)
            def _():
                finish([acc_refs[c][...] for c in range(n_acc)])

    scratch = [pltpu.VMEM((tm, tn), F32) for _ in range(n_acc)] if nk > 1 else []
    res = _tc_call(
        body, name=name, grid=(M // tm, N // tn, nk),
        in_specs=[a_spec] * na + [b_spec] * nb + [o_spec] * ne,
        out_specs=[o_spec] * no,
        out_shape=[_sds((M, N), dt) for dt in out_dtypes],
        scratch_shapes=scratch,
        compiler_params=_cparams(("parallel", "parallel", "arbitrary")),
    )(*a_list, *b_list, *extras)
    return res


def _mm_rows(name, a_list, b_list, terms, mode, tm, tk, rows, vecs, row_dtypes, n_vec_out, epilogue,
             n_scalar_out=0):
    a0, b0 = a_list[0], b_list[0]
    M, K = a0.shape
    N = b0.shape[0] if mode == "nt" else b0.shape[1]
    assert mode in ("nn", "nt") and M % tm == 0 and K % tk == 0, (name, M, N, K, tm, tk)
    nm, nk = M // tm, K // tk
    slab = min(128, tm)
    na, nb, nr, nv, no = len(a_list), len(b_list), len(rows), len(vecs), len(row_dtypes)
    dn = _DN[mode]
    a_spec = pl.BlockSpec((tm, tk), lambda i, k: (i, k))
    b_spec = pl.BlockSpec((N, tk), lambda i, k: (0, k)) if mode == "nt" else pl.BlockSpec((tk, N), lambda i, k: (k, 0))
    row_spec = pl.BlockSpec((tm, N), lambda i, k: (i, 0))
    vec_spec = pl.BlockSpec((1, N), lambda i, k: (0, 0))
    one_spec = pl.BlockSpec((1, 1), lambda i, k: (0, 0))

    def body(*refs):
        pos = 0
        a_refs = refs[pos:pos + na]; pos += na
        b_refs = refs[pos:pos + nb]; pos += nb
        r_refs = refs[pos:pos + nr]; pos += nr
        v_refs = refs[pos:pos + nv]; pos += nv
        o_refs = refs[pos:pos + no]; pos += no
        s_refs = refs[pos:pos + n_vec_out]; pos += n_vec_out
        vacc_refs = refs[pos:pos + n_vec_out]; pos += n_vec_out
        acc_ref = refs[pos] if nk > 1 else None
        i, k = pl.program_id(0), pl.program_id(1)
        part = None
        for ai, bi, _ in terms:
            d = lax.dot_general(a_refs[ai][...].astype(BF16), b_refs[bi][...].astype(BF16), dn,
                                preferred_element_type=F32)
            part = d if part is None else part + d

        def finish(acc):
            vecs_now = [v[...] for v in v_refs]
            vparts = None
            for r0 in range(0, tm, slab):
                rs_ = slice(r0, r0 + slab)
                outs, vp = epilogue(acc[rs_, :], [r[rs_, :] for r in r_refs], vecs_now)
                for o_ref, val in zip(o_refs, outs):
                    o_ref[rs_, :] = val.astype(o_ref.dtype)
                vparts = vp if vparts is None else [a + b for a, b in zip(vparts, vp)]

            @pl.when(i == 0)
            def _():
                for vacc, vp in zip(vacc_refs, vparts):
                    vacc[...] = vp

            @pl.when(i > 0)
            def _():
                for vacc, vp in zip(vacc_refs, vparts):
                    vacc[...] += vp

            @pl.when(i == nm - 1)
            def _():
                for j, (s_ref, vacc) in enumerate(zip(s_refs, vacc_refs)):
                    col = jnp.sum(vacc[...], axis=0, keepdims=True)
                    s_ref[...] = jnp.sum(col, axis=1, keepdims=True) if j >= n_vec_out - n_scalar_out else col

        if nk == 1:
            finish(part)
        else:
            @pl.when(k == 0)
            def _():
                acc_ref[...] = part

            @pl.when(k > 0)
            def _():
                acc_ref[...] += part

            @pl.when(k == nk - 1)
            def _():
                finish(acc_ref)

    n_plain = n_vec_out - n_scalar_out
    return _tc_call(
        body, name=name, grid=(nm, nk),
        in_specs=[a_spec] * na + [b_spec] * nb + [row_spec] * nr + [vec_spec] * nv,
        out_specs=[row_spec] * no + [vec_spec] * n_plain + [one_spec] * n_scalar_out,
        out_shape=[_sds((M, N), dt) for dt in row_dtypes] + [_sds((1, N), F32)] * n_plain
        + [_sds((1, 1), F32)] * n_scalar_out,
        scratch_shapes=[pltpu.VMEM((8, N), F32)] * n_vec_out + ([pltpu.VMEM((tm, N), F32)] if nk > 1 else []),
        compiler_params=_cparams(("arbitrary", "arbitrary")),
    )(*a_list, *b_list, *rows, *vecs)


def _ep_residual_norm(acc, rows, vecs):
    x_new = rows[0] + acc
    r = lax.rsqrt(jnp.mean(x_new * x_new, axis=-1, keepdims=True) + EPS)
    return [x_new, x_new * r * vecs[0]], []


def _ep_norm_bwd(acc, rows, vecs):
    xv, dres = rows
    r = lax.rsqrt(jnp.mean(xv * xv, axis=-1, keepdims=True) + EPS)
    xh = xv * r
    dxh = acc * vecs[0]
    m = jnp.mean(dxh * xh, axis=-1, keepdims=True)
    dx = dres + r * (dxh - xh * m)
    return [dx, dx], [_rowsum8(acc * xh)]


def _ep_final_loss(acc, rows, vecs):
    x2, target = rows
    gv = vecs[0]
    xv = x2 + acc
    inv_d = 1.0 / xv.shape[-1]
    r = lax.rsqrt(jnp.mean(xv * xv, axis=-1, keepdims=True) + EPS)
    xh = xv * r
    e = xh * gv - target
    dy = e * inv_d
    dxh = dy * gv
    m = jnp.mean(dxh * xh, axis=-1, keepdims=True)
    dx = r * (dxh - xh * m)
    return [dx, dx], [_rowsum8(dy * xh), _rowsum8(e * e) * (0.5 * inv_d)]


def _mm1(name, a, b, mode, tm, tn, tk, out_dtype, **kw):
    return _mm(name, [a], [b], [(0, 0, 0)], mode, tm, tn, tk, [out_dtype], **kw)[0]


def _rms_fwd(name, x, g, tr):
    S, Dm = x.shape

    def body(x_ref, g_ref, h_ref):
        xv = x_ref[...]
        r = lax.rsqrt(jnp.mean(xv * xv, axis=-1, keepdims=True) + EPS)
        h_ref[...] = (xv * r * g_ref[...]).astype(h_ref.dtype)

    return _tc_call(
        body, name=name, grid=(S // tr,),
        in_specs=[pl.BlockSpec((tr, Dm), lambda i: (i, 0)), pl.BlockSpec((1, Dm), lambda i: (0, 0))],
        out_specs=pl.BlockSpec((tr, Dm), lambda i: (i, 0)),
        out_shape=_sds((S, Dm), BF16),
        compiler_params=_cparams(("parallel",)),
    )(x, g)


def _rms_bwd(name, dh, x, g, dres, tr, want_dx=True):
    S, Dm = x.shape
    nsteps = S // tr

    def body(*refs):
        if want_dx:
            dh_ref, x_ref, g_ref, dres_ref, dx_ref, dxb_ref, dg_ref, acc_ref = refs
        else:
            dh_ref, x_ref, g_ref, dg_ref, acc_ref = refs
        i = pl.program_id(0)
        xv = x_ref[...]
        r = lax.rsqrt(jnp.mean(xv * xv, axis=-1, keepdims=True) + EPS)
        xh = xv * r
        dhv = dh_ref[...]
        part = _rowsum8(dhv * xh)

        @pl.when(i == 0)
        def _():
            acc_ref[...] = part

        @pl.when(i > 0)
        def _():
            acc_ref[...] += part

        @pl.when(i == nsteps - 1)
        def _():
            dg_ref[...] = jnp.sum(acc_ref[...], axis=0, keepdims=True)

        if want_dx:
            dxh = dhv * g_ref[...]
            m = jnp.mean(dxh * xh, axis=-1, keepdims=True)
            dx = dres_ref[...] + r * (dxh - xh * m)
            dx_ref[...] = dx
            dxb_ref[...] = dx.astype(BF16)

    row = pl.BlockSpec((tr, Dm), lambda i: (i, 0))
    vec = pl.BlockSpec((1, Dm), lambda i: (0, 0))
    if want_dx:
        in_specs = [row, row, vec, row]
        out_specs = [row, row, vec]
        out_shape = [_sds((S, Dm), F32), _sds((S, Dm), BF16), _sds((1, Dm), F32)]
        args = (dh, x, g, dres)
    else:
        in_specs = [row, row, vec]
        out_specs = [vec]
        out_shape = [_sds((1, Dm), F32)]
        args = (dh, x, g)
    return _tc_call(
        body, name=name, grid=(nsteps,), in_specs=in_specs, out_specs=out_specs, out_shape=out_shape,
        scratch_shapes=[pltpu.VMEM((8, Dm), F32)],
        compiler_params=_cparams(("arbitrary",)),
    )(*args)


def _final_loss(name, x3, g, target, tr):
    S, Dm = x3.shape
    nsteps = S // tr

    def body(x_ref, g_ref, t_ref, dx_ref, dxb_ref, dg_ref, loss_ref, acc_g, acc_l):
        i = pl.program_id(0)
        xv = x_ref[...]
        gv = g_ref[...]
        r = lax.rsqrt(jnp.mean(xv * xv, axis=-1, keepdims=True) + EPS)
        xh = xv * r
        e = xh * gv - t_ref[...]
        dy = e * (1.0 / Dm)
        lpart = _rowsum8(e * e)
        gpart = _rowsum8(dy * xh)

        @pl.when(i == 0)
        def _():
            acc_g[...] = gpart
            acc_l[...] = lpart

        @pl.when(i > 0)
        def _():
            acc_g[...] += gpart
            acc_l[...] += lpart

        @pl.when(i == nsteps - 1)
        def _():
            dg_ref[...] = jnp.sum(acc_g[...], axis=0, keepdims=True)
            tot = jnp.sum(jnp.sum(acc_l[...], axis=1, keepdims=True), axis=0, keepdims=True)
            loss_ref[...] = tot * (0.5 / Dm)

        dxh = dy * gv
        m = jnp.mean(dxh * xh, axis=-1, keepdims=True)
        dx = r * (dxh - xh * m)
        dx_ref[...] = dx
        dxb_ref[...] = dx.astype(BF16)

    row = pl.BlockSpec((tr, Dm), lambda i: (i, 0))
    vec = pl.BlockSpec((1, Dm), lambda i: (0, 0))
    return _tc_call(
        body, name=name, grid=(nsteps,),
        in_specs=[row, vec, row],
        out_specs=[row, row, vec, pl.BlockSpec((1, 1), lambda i: (0, 0))],
        out_shape=[_sds((S, Dm), F32), _sds((S, Dm), BF16), _sds((1, Dm), F32), _sds((1, 1), F32)],
        scratch_shapes=[pltpu.VMEM((8, Dm), F32), pltpu.VMEM((8, Dm), F32)],
        compiler_params=_cparams(("arbitrary",)),
    )(x3, g, target)


def _softmax_rows(s):
    e = jnp.exp(s - jnp.max(s, axis=-1, keepdims=True))
    return e / jnp.sum(e, axis=-1, keepdims=True)


def _attn_fwd(name, q, k, v, ts):
    S, Dm = q.shape
    M = k.shape[0]
    scale = HEAD_DIM ** -0.5

    def body(q_ref, k_ref, v_ref, o_ref):
        for h in range(N_HEADS):
            sl = slice(h * HEAD_DIM, (h + 1) * HEAD_DIM)
            s = lax.dot_general(q_ref[:, sl], k_ref[:, sl], _DN["nt"], preferred_element_type=F32) * scale
            p = _softmax_rows(s)
            o_ref[:, sl] = jnp.dot(p.astype(BF16), v_ref[:, sl], preferred_element_type=F32).astype(o_ref.dtype)

    row = pl.BlockSpec((ts, Dm), lambda i: (i, 0))
    mem = pl.BlockSpec((M, Dm), lambda i: (0, 0))
    return _tc_call(
        body, name=name, grid=(S // ts,), in_specs=[row, mem, mem], out_specs=row,
        out_shape=_sds((S, Dm), BF16), compiler_params=_cparams(("parallel",)),
    )(q, k, v)


def _attn_bwd(name, q, k, v, do, ts):
    S, Dm = q.shape
    M = k.shape[0]
    scale = HEAD_DIM ** -0.5

    def body(q_ref, k_ref, v_ref, do_ref, dq_ref, dk_ref, dv_ref):
        i = pl.program_id(0)

        @pl.when(i == 0)
        def _():
            dk_ref[...] = jnp.zeros_like(dk_ref)
            dv_ref[...] = jnp.zeros_like(dv_ref)

        for h in range(N_HEADS):
            sl = slice(h * HEAD_DIM, (h + 1) * HEAD_DIM)
            qh = q_ref[:, sl]
            kh = k_ref[:, sl]
            doh = do_ref[:, sl]
            s = lax.dot_general(qh, kh, _DN["nt"], preferred_element_type=F32) * scale
            p = _softmax_rows(s)
            dp = lax.dot_general(doh, v_ref[:, sl], _DN["nt"], preferred_element_type=F32)
            ds = p * (dp - jnp.sum(dp * p, axis=-1, keepdims=True)) * scale
            dsb = ds.astype(BF16)
            dq_ref[:, sl] = jnp.dot(dsb, kh, preferred_element_type=F32).astype(dq_ref.dtype)
            dk_ref[:, sl] += lax.dot_general(dsb, qh, _DN["tn"], preferred_element_type=F32)
            dv_ref[:, sl] += lax.dot_general(p.astype(BF16), doh, _DN["tn"], preferred_element_type=F32)

    row = pl.BlockSpec((ts, Dm), lambda i: (i, 0))
    mem = pl.BlockSpec((M, Dm), lambda i: (0, 0))
    return _tc_call(
        body, name=name, grid=(S // ts,), in_specs=[row, mem, mem, row], out_specs=[row, mem, mem],
        out_shape=[_sds((S, Dm), BF16), _sds((M, Dm), F32), _sds((M, Dm), F32)],
        compiler_params=_cparams(("arbitrary",)),
    )(q, k, v, do)


def _pool_denominators(row0, ts):
    return (row0 + lax.broadcasted_iota(jnp.int32, (ts, 1), 0) + 1).astype(F32)


def _mixer_fwd(name, proj, pool_w, pool_scale, sgu_g, ws, bias_full, ts):
    S = proj.shape[0]
    nblk = ts // SGU_BLOCK
    halo_blocks = ts // POOL_HALO

    def body(proj_ref, halo_ref, pw_ref, sc_ref, g_ref, ws_ref, b_ref, y_ref, p_ref, vn_ref, ext_ref):
        i = pl.program_id(0)
        a = proj_ref[:, 0:D_POOL]
        ext_ref[0:POOL_HALO, :] = jnp.where(i > 0, halo_ref[...], 0.0)
        ext_ref[POOL_HALO:POOL_HALO + ts, :] = a
        pos = _pool_denominators(i * ts, ts)
        for gi, w in enumerate(POOL_WINDOWS):
            cs = slice(gi * POOL_GROUP, (gi + 1) * POOL_GROUP)
            acc = a[:, cs]
            for j in range(1, w):
                acc = acc + ext_ref[POOL_HALO - j:POOL_HALO - j + ts, cs]
            pg = (acc / jnp.minimum(pos, float(w)) - a[:, cs]).astype(BF16)
            p_ref[:, cs] = pg
            ypre = jnp.dot(pg, pw_ref[gi], preferred_element_type=F32)
            y_ref[:, cs] = (ypre * sc_ref[:, cs]).astype(y_ref.dtype)

        v = proj_ref[:, D_POOL + D_SGU:D_POOL + 2 * D_SGU]
        r = lax.rsqrt(jnp.mean(v * v, axis=-1, keepdims=True) + EPS)
        vn_ref[...] = (v * r * g_ref[...]).astype(BF16)
        for n in range(nblk):
            rs = slice(n * SGU_BLOCK, (n + 1) * SGU_BLOCK)
            for h in range(N_SGU_HEADS):
                cs = slice(h * SGU_BLOCK, (h + 1) * SGU_BLOCK)
                mixed = jnp.dot(ws_ref[h], vn_ref[rs, cs], preferred_element_type=F32) + b_ref[:, cs]
                u = proj_ref[rs, D_POOL + h * SGU_BLOCK:D_POOL + (h + 1) * SGU_BLOCK]
                y_ref[rs, D_POOL + h * SGU_BLOCK:D_POOL + (h + 1) * SGU_BLOCK] = (u * mixed).astype(y_ref.dtype)

    return _tc_call(
        body, name=name, grid=(S // ts,),
        in_specs=[
            pl.BlockSpec((ts, D_POOL + 2 * D_SGU), lambda i: (i, 0)),
            pl.BlockSpec((POOL_HALO, D_POOL), lambda i: (jnp.maximum(i * halo_blocks - 1, 0), 0)),
            pl.BlockSpec((4, POOL_GROUP, POOL_GROUP), lambda i: (0, 0, 0)),
            pl.BlockSpec((1, D_POOL), lambda i: (0, 0)),
            pl.BlockSpec((1, D_SGU), lambda i: (0, 0)),
            pl.BlockSpec((N_SGU_HEADS, SGU_BLOCK, SGU_BLOCK), lambda i: (0, 0, 0)),
            pl.BlockSpec((SGU_BLOCK, D_SGU), lambda i: (0, 0)),
        ],
        out_specs=[
            pl.BlockSpec((ts, D_MODEL), lambda i: (i, 0)),
            pl.BlockSpec((ts, D_POOL), lambda i: (i, 0)),
            pl.BlockSpec((ts, D_SGU), lambda i: (i, 0)),
        ],
        out_shape=[_sds((S, D_MODEL), BF16), _sds((S, D_POOL), BF16), _sds((S, D_SGU), BF16)],
        scratch_shapes=[pltpu.VMEM((ts + POOL_HALO, D_POOL), F32)],
        compiler_params=_cparams(("parallel",)),
    )(proj, proj, pool_w, pool_scale, sgu_g, ws, bias_full)


def _mixer_bwd(name, dymix, proj, p, vn, pool_w, pool_scale, sgu_g, ws, bias_full, ts):
    S = proj.shape[0]
    nsteps = S // ts
    nblk = ts // SGU_BLOCK
    halo_blocks = ts // POOL_HALO

    def body(dy_ref, dyh_ref, u_ref, v_ref, p_ref, vn_ref, pw_ref, sc_ref, g_ref, ws_ref, b_ref,
             dproj_ref, dpw_ref, dsc_ref, dg_ref, dws_ref, db_ref,
             ext_ref, dvn_ref, acc_sc, acc_g, acc_b):
        i = pl.program_id(0)

        @pl.when(i == 0)
        def _():
            dpw_ref[...] = jnp.zeros_like(dpw_ref)
            dws_ref[...] = jnp.zeros_like(dws_ref)
            acc_sc[...] = jnp.zeros_like(acc_sc)
            acc_g[...] = jnp.zeros_like(acc_g)
            acc_b[...] = jnp.zeros_like(acc_b)

        pos = _pool_denominators(i * ts, ts)
        pos_h = _pool_denominators((i + 1) * ts, POOL_HALO)
        for gi, w in enumerate(POOL_WINDOWS):
            cs = slice(gi * POOL_GROUP, (gi + 1) * POOL_GROUP)
            pg = p_ref[:, cs]
            wg = pw_ref[gi]
            dyp = dy_ref[:, cs]
            ypre = jnp.dot(pg, wg, preferred_element_type=F32)
            acc_sc[:, cs] += _rowsum8(dyp * ypre)
            dz = (dyp * sc_ref[:, cs]).astype(BF16)
            dpw_ref[gi] += lax.dot_general(pg, dz, _DN["tn"], preferred_element_type=F32)
            dp = lax.dot_general(dz, wg, _DN["nt"], preferred_element_type=F32)
            dzh = (dyh_ref[:, cs] * sc_ref[:, cs]).astype(BF16)
            dph = lax.dot_general(dzh, wg, _DN["nt"], preferred_element_type=F32)
            ext_ref[0:ts, cs] = dp / jnp.minimum(pos, float(w))
            ext_ref[ts:ts + POOL_HALO, cs] = jnp.where(i < nsteps - 1, dph / jnp.minimum(pos_h, float(w)), 0.0)
            acc = ext_ref[0:ts, cs]
            for j in range(1, w):
                acc = acc + ext_ref[j:j + ts, cs]
            dproj_ref[:, cs] = (acc - dp).astype(dproj_ref.dtype)

        for n in range(nblk):
            rs = slice(n * SGU_BLOCK, (n + 1) * SGU_BLOCK)
            for h in range(N_SGU_HEADS):
                cs = slice(h * SGU_BLOCK, (h + 1) * SGU_BLOCK)
                vnb = vn_ref[rs, cs]
                wh = ws_ref[h]
                mixed = jnp.dot(wh, vnb, preferred_element_type=F32) + b_ref[:, cs]
                dys = dy_ref[rs, D_POOL + h * SGU_BLOCK:D_POOL + (h + 1) * SGU_BLOCK]
                dproj_ref[rs, D_POOL + h * SGU_BLOCK:D_POOL + (h + 1) * SGU_BLOCK] = (dys * mixed).astype(dproj_ref.dtype)
                dmix = dys * u_ref[rs, cs]
                acc_b[:, cs] += dmix
                dmb = dmix.astype(BF16)
                dws_ref[h] += lax.dot_general(dmb, vnb, _DN["nt"], preferred_element_type=F32)
                dvn_ref[rs, cs] = lax.dot_general(wh, dmb, _DN["tn"], preferred_element_type=F32)
        v = v_ref[...]
        r = lax.rsqrt(jnp.mean(v * v, axis=-1, keepdims=True) + EPS)
        vh = v * r
        dvn = dvn_ref[...]
        acc_g[...] += _rowsum8(dvn * vh)
        dxh = dvn * g_ref[...]
        m = jnp.mean(dxh * vh, axis=-1, keepdims=True)
        dproj_ref[:, D_POOL + D_SGU:D_POOL + 2 * D_SGU] = (r * (dxh - vh * m)).astype(dproj_ref.dtype)

        @pl.when(i == nsteps - 1)
        def _():
            dsc_ref[...] = jnp.sum(acc_sc[...], axis=0, keepdims=True)
            dg_ref[...] = jnp.sum(acc_g[...], axis=0, keepdims=True)
            t_idx = lax.broadcasted_iota(jnp.int32, (SGU_BLOCK, SGU_BLOCK), 0) // SGU_CHUNK
            s_idx = lax.broadcasted_iota(jnp.int32, (SGU_BLOCK, SGU_BLOCK), 1) // SGU_CHUNK
            mask = s_idx <= t_idx
            for h in range(N_SGU_HEADS):
                cs = slice(h * SGU_BLOCK, (h + 1) * SGU_BLOCK)
                dws_ref[h] = jnp.where(mask, dws_ref[h], 0.0)
                col = jnp.sum(acc_b[:, cs], axis=1, keepdims=True)
                db_ref[h] = jnp.broadcast_to(col, (SGU_BLOCK, SGU_BLOCK))

    const2 = lambda i: (0, 0)
    const3 = lambda i: (0, 0, 0)
    last_halo = S // POOL_HALO - 1
    return _tc_call(
        body, name=name, grid=(nsteps,),
        in_specs=[
            pl.BlockSpec((ts, D_MODEL), lambda i: (i, 0)),
            pl.BlockSpec((POOL_HALO, D_POOL), lambda i: (jnp.minimum((i + 1) * halo_blocks, last_halo), 0)),
            pl.BlockSpec((ts, D_SGU), lambda i: (i, 1)),
            pl.BlockSpec((ts, D_SGU), lambda i: (i, 2)),
            pl.BlockSpec((ts, D_POOL), lambda i: (i, 0)),
            pl.BlockSpec((ts, D_SGU), lambda i: (i, 0)),
            pl.BlockSpec((4, POOL_GROUP, POOL_GROUP), const3),
            pl.BlockSpec((1, D_POOL), const2),
            pl.BlockSpec((1, D_SGU), const2),
            pl.BlockSpec((N_SGU_HEADS, SGU_BLOCK, SGU_BLOCK), const3),
            pl.BlockSpec((SGU_BLOCK, D_SGU), const2),
        ],
        out_specs=[
            pl.BlockSpec((ts, D_POOL + 2 * D_SGU), lambda i: (i, 0)),
            pl.BlockSpec((4, POOL_GROUP, POOL_GROUP), const3),
            pl.BlockSpec((1, D_POOL), const2),
            pl.BlockSpec((1, D_SGU), const2),
            pl.BlockSpec((N_SGU_HEADS, SGU_BLOCK, SGU_BLOCK), const3),
            pl.BlockSpec((N_SGU_HEADS, SGU_BLOCK, SGU_BLOCK), const3),
        ],
        out_shape=[
            _sds((S, D_POOL + 2 * D_SGU), BF16),
            _sds((4, POOL_GROUP, POOL_GROUP), F32),
            _sds((1, D_POOL), F32),
            _sds((1, D_SGU), F32),
            _sds((N_SGU_HEADS, SGU_BLOCK, SGU_BLOCK), F32),
            _sds((N_SGU_HEADS, SGU_BLOCK, SGU_BLOCK), F32),
        ],
        scratch_shapes=[
            pltpu.VMEM((ts + POOL_HALO, D_POOL), F32),
            pltpu.VMEM((ts, D_SGU), F32),
            pltpu.VMEM((8, D_POOL), F32),
            pltpu.VMEM((8, D_SGU), F32),
            pltpu.VMEM((SGU_BLOCK, D_SGU), F32),
        ],
        compiler_params=_cparams(("arbitrary",)),
    )(dymix, dymix, proj, proj, p, vn, pool_w, pool_scale, sgu_g, ws, bias_full)


def _silu_mul(accs, extras):
    (up,) = accs
    gt = extras[0]
    sig = 1.0 / (1.0 + jnp.exp(-gt))
    return gt, up, gt * sig * up


def _silu_mul_bwd(accs, extras):
    (dact,) = accs
    gt = extras[0].astype(F32)
    up = extras[1].astype(F32)
    sig = 1.0 / (1.0 + jnp.exp(-gt))
    silu = gt * sig
    dgt = dact * up * (sig * (1.0 + gt * (1.0 - sig)))
    dup = dact * silu
    return dgt, dup


def _add_residual(accs, extras):
    return (extras[0] + accs[0],)


def _add_residual_and_cast(accs, extras):
    y = extras[0] + accs[0]
    return y, y


def _local_step(x, mem, target, W, sm, rs):
    S = x.shape[0]
    tm = min(1024, S)
    th = min(512, S)
    tq = min(256, S)
    ts = min(512, S)
    tr = min(512, S)
    tk_s = min(2048, S)
    M = mem.shape[0]

    h1 = _rms_fwd("rms_mix", x, sm["norm_mix_g"], tr)
    proj = _mm1("proj_in", h1, W["w_in_t"], "nt", tm, 1024, 2048, F32)
    ymix, p, vn = _mixer_fwd("mixer_fwd", proj, W["pool_w"], sm["pool_scale"], sm["sgu_norm_g"],
                             sm["ws_masked"], sm["bias_full"], ts)
    x1, h2 = _mm_rows("proj_out", [ymix], [W["w_out"]], [(0, 0, 0)], "nn", tq, 2048, [x], [sm["norm_xattn_g"]],
                      [F32, BF16], 0, _ep_residual_norm)

    mb = _rms_fwd("rms_mem", mem, sm["norm_mem_g"], M)
    q = _mm1("proj_q", h2, W["w_q"], "nn", tm, 1024, 2048, BF16)
    kk = _mm1("proj_k", mb, W["w_k"], "nn", M, 1024, 2048, BF16)
    vv = _mm1("proj_v", mb, W["w_v"], "nn", M, 1024, 2048, BF16)
    o = _attn_fwd("attn_fwd", q, kk, vv, ts)
    x2, h3 = _mm_rows("proj_o", [o], [W["w_o"]], [(0, 0, 0)], "nn", tq, 2048, [x1], [sm["norm_ffn_g"]],
                      [F32, BF16], 0, _ep_residual_norm)

    gt32 = _mm1("ffn_gate", h3, W["w_gate_t"], "nt", tm, 512, 2048, F32)
    gt, up, act = _mm("ffn_up", [h3], [W["w_up_t"]], [(0, 0, 0)], "nt", tm, 512, 2048, [BF16, BF16, BF16],
                      epilogue=_silu_mul, extras=(gt32,))
    x3 = _mm1("ffn_down", act, W["w_down"], "nn", th, 512, 5632, F32, epilogue=_add_residual, extras=(x2,))
    dx3, dx3b, d_final_g, loss = _final_loss("final_loss", x3, sm["final_norm_g"], target, tr)

    dgt, dup = _mm("ffn_down_dgrad", [dx3b], [W["w_down"]], [(0, 0, 0)], "nt", tm, 512, 2048, [BF16, BF16],
                   epilogue=_silu_mul_bwd, extras=(gt, up))
    rs.push("w_down", _mm1("ffn_down_wgrad", act, dx3b, "tn", 1408, 1024, tk_s, BF16))
    rs.push("w_gate_t", _mm1("ffn_gate_wgrad", dgt, h3, "tn", 1408, 1024, tk_s, BF16))
    rs.reduce("w_down")
    rs.push("w_up_t", _mm1("ffn_up_wgrad", dup, h3, "tn", 1408, 1024, tk_s, BF16))
    rs.reduce("w_gate_t")
    dh3 = _mm("ffn_gate_up_dgrad", [dgt, dup], [W["w_gate_t"], W["w_up_t"]], [(0, 0, 0), (1, 1, 0)], "nn",
              th, 256, 5632, [F32])[0]
    rs.reduce("w_up_t")
    dx2, dx2b, d_ffn_g = _rms_bwd("rms_ffn_bwd", dh3, x2, sm["norm_ffn_g"], dx3, tr)
    rs.finish("w_down")

    rs.push("w_o", _mm1("proj_o_wgrad", o, dx2b, "tn", 1024, 1024, tk_s, BF16))
    rs.finish("w_gate_t")
    do = _mm1("proj_o_dgrad", dx2b, W["w_o"], "nt", tm, 1024, 2048, BF16)
    rs.reduce("w_o")
    dq, dk, dv = _attn_bwd("attn_bwd", q, kk, vv, do, ts)
    rs.push("w_q", _mm1("proj_q_wgrad", h2, dq, "tn", 1024, 1024, tk_s, BF16))
    rs.push("w_k", _mm1("proj_k_wgrad", mb, dk, "tn", 1024, 1024, M, BF16))
    rs.push("w_v", _mm1("proj_v_wgrad", mb, dv, "tn", 1024, 1024, M, BF16))
    rs.finish("w_up_t")
    dx1, dx1b, d_xattn_g = _mm_rows(
        "proj_q_dgrad", [dq], [W["w_q"]], [(0, 0, 0)], "nt", tq, 2048, [x1, dx2], [sm["norm_xattn_g"]],
        [F32, BF16], 1, _ep_norm_bwd)
    rs.reduce("w_q")
    rs.reduce("w_k")
    rs.reduce("w_v")
    dmb = _mm("proj_kv_dgrad", [dk, dv], [W["w_k"], W["w_v"]], [(0, 0, 0), (1, 1, 0)], "nt",
              M, 1024, 2048, [F32])[0]
    (d_mem_g,) = _rms_bwd("rms_mem_bwd", dmb, mem, sm["norm_mem_g"], None, M, want_dx=False)
    rs.finish("w_o")

    rs.push("w_out", _mm1("proj_out_wgrad", ymix, dx1b, "tn", 1024, 1024, tk_s, BF16))
    dymix = _mm1("proj_out_dgrad", dx1b, W["w_out"], "nt", tm, 1024, 2048, F32)
    rs.reduce("w_out")
    dproj, d_pool_w, d_pool_scale, d_sgu_g, d_ws, d_b = _mixer_bwd(
        "mixer_bwd", dymix, proj, p, vn, W["pool_w"], sm["pool_scale"], sm["sgu_norm_g"],
        sm["ws_masked"], sm["bias_full"], ts)
    rs.small("early", dict(
        pool_w=d_pool_w, pool_scale=d_pool_scale, sgu_norm_g=d_sgu_g, w_spatial=d_ws, b_spatial=d_b[:, :, 0],
        norm_xattn_g=d_xattn_g, norm_mem_g=d_mem_g, norm_ffn_g=d_ffn_g, final_norm_g=d_final_g))
    rs.push("w_in_t", _mm1("proj_in_wgrad", dproj, h1, "tn", 1024, 1024, tk_s, BF16))
    rs.finish("w_q")
    rs.finish("w_k")
    rs.finish("w_v")
    rs.reduce("w_in_t")
    grad_x, d_mix_g = _mm_rows(
        "proj_in_dgrad", [dproj], [W["w_in_t"]], [(0, 0, 0)], "nn", tq, 3072, [x, dx1], [sm["norm_mix_g"]],
        [F32], 1, _ep_norm_bwd)
    rs.small("late", dict(norm_mix_g=d_mix_g))
    rs.finish("w_out")
    rs.finish_small("early")
    rs.finish("w_in_t")
    rs.finish_small("late")
    return loss, grad_x


def _mesh_pos():
    return lax.axis_index("x"), lax.axis_index("y"), lax.axis_index("c")


def _handshake(peers):
    barrier = pltpu.get_barrier_semaphore()
    for peer in peers:
        pl.semaphore_signal(barrier, inc=1, device_id=peer, device_id_type=MESH)
    pl.semaphore_wait(barrier, len(peers))


def _seq_all_gather(name, shards, collective_id):
    n = len(shards)

    def body(*refs):
        ins = refs[:n]
        outs = refs[n:2 * n]
        send_sems, recv_sems, local_sems = refs[2 * n:]
        x, y, c = _mesh_pos()
        me, sibling = (x, y, c), (x, y, 1 - c)
        chips = [(1 - x, y), (x, 1 - y), (1 - x, 1 - y)]
        _handshake([sibling] + [(*chip, c) for chip in chips])

        def copy(a, k, block, to, src=None):
            bx, by, bc = block
            dst = outs[a].at[4 * bx + 2 * by + bc]
            return pltpu.make_async_remote_copy(
                src_ref=dst if src is None else src, dst_ref=dst,
                send_sem=send_sems.at[a, k], recv_sem=recv_sems.at[a, k],
                device_id=to, device_id_type=MESH)

        mine = [pltpu.make_async_copy(ins[a], outs[a].at[4 * x + 2 * y + c], local_sems.at[a]) for a in range(n)]
        for cp in mine:
            cp.start()
        started = []
        for a in range(n):
            first = [copy(a, 0, me, sibling, src=ins[a])]
            first += [copy(a, 1 + j, me, (*chip, c), src=ins[a]) for j, chip in enumerate(chips)]
            for cp in first:
                cp.start()
            started += first
        for a in range(n):
            for j, chip in enumerate(chips):
                copy(a, 1 + j, (*chip, c), me).wait_recv()
                fwd = copy(a, 4 + j, (*chip, c), sibling)
                fwd.start()
                started.append(fwd)
        for a in range(n):
            copy(a, 0, sibling, me).wait_recv()
            for j, chip in enumerate(chips):
                copy(a, 4 + j, (*chip, 1 - c), me).wait_recv()
        for cp in started:
            cp.wait_send()
        for cp in mine:
            cp.wait()

    return _sc_call(
        body, name=name,
        out_type=[_sds((N_DEV,) + s.shape, s.dtype) for s in shards],
        scratch_types=[pltpu.SemaphoreType.DMA((n, 7)), pltpu.SemaphoreType.DMA((n, 7)),
                       pltpu.SemaphoreType.DMA((n,))],
        compiler_params=pltpu.CompilerParams(collective_id=collective_id),
    )(*shards)


def _seq_pair_exchange(name, gview, collective_id):
    def body(g_ref, theirs_ref, send_sems, recv_sems):
        x, y, c = _mesh_pos()
        sibling = (x, y, 1 - c)
        _handshake([sibling])
        copies = [pltpu.make_async_remote_copy(
            src_ref=g_ref.at[k, 1 - c], dst_ref=theirs_ref.at[k],
            send_sem=send_sems.at[k], recv_sem=recv_sems.at[k],
            device_id=sibling, device_id_type=MESH) for k in range(4)]
        for cp in copies:
            cp.start()
        for cp in copies:
            cp.wait()

    return _sc_call(
        body, name=name, out_type=_sds((4,) + gview.shape[2:], gview.dtype),
        scratch_types=[pltpu.SemaphoreType.DMA((4,)), pltpu.SemaphoreType.DMA((4,))],
        compiler_params=pltpu.CompilerParams(collective_id=collective_id),
    )(gview)


def _pair_sum(name, gview, theirs, pos, tr):
    _, _, r, C = gview.shape

    def body(pos_ref, a_ref, b_ref, o_ref):
        o_ref[...] = (a_ref[...].astype(F32) + b_ref[...].astype(F32)).astype(o_ref.dtype)

    grid_spec = pltpu.PrefetchScalarGridSpec(
        num_scalar_prefetch=1, grid=(4, r // tr),
        in_specs=[pl.BlockSpec((None, None, tr, C), lambda k, t, pos_ref: (k, pos_ref[0], t, 0)),
                  pl.BlockSpec((None, tr, C), lambda k, t, pos_ref: (k, t, 0))],
        out_specs=pl.BlockSpec((None, tr, C), lambda k, t, pos_ref: (k, t, 0)))
    return _tc_call(
        body, name=name, grid_spec=grid_spec, out_shape=_sds(theirs.shape, theirs.dtype),
        compiler_params=_cparams(("parallel", "parallel")),
    )(pos, gview, theirs)


def _seq_chip_exchange(name, pair, collective_id):
    def body(p_ref, land_ref, send_sems, recv_sems):
        x, y, c = _mesh_pos()
        my_chip = 2 * x + y
        chips = [(1 - x, y), (x, 1 - y), (1 - x, 1 - y)]
        _handshake([(cx, cy, c) for cx, cy in chips])
        copies = [pltpu.make_async_remote_copy(
            src_ref=p_ref.at[2 * cx + cy], dst_ref=land_ref.at[my_chip],
            send_sem=send_sems.at[j], recv_sem=recv_sems.at[j],
            device_id=(cx, cy, c), device_id_type=MESH) for j, (cx, cy) in enumerate(chips)]
        for cp in copies:
            cp.start()
        for cp in copies:
            cp.wait_send()
        for j, (cx, cy) in enumerate(chips):
            pltpu.make_async_remote_copy(
                src_ref=p_ref.at[my_chip], dst_ref=land_ref.at[2 * cx + cy],
                send_sem=send_sems.at[j], recv_sem=recv_sems.at[j],
                device_id=(cx, cy, c), device_id_type=MESH).wait_recv()

    return _sc_call(
        body, name=name, out_type=_sds(pair.shape, pair.dtype),
        scratch_types=[pltpu.SemaphoreType.DMA((3,)), pltpu.SemaphoreType.DMA((3,))],
        compiler_params=pltpu.CompilerParams(collective_id=collective_id),
    )(pair)


def _chip_sum(name, pair, landed, pos, tr):
    _, r, C = pair.shape

    def body(pos_ref, own_ref, land_ref, o_ref, acc_ref):
        k = pl.program_id(1)
        val = jnp.where(k == pos_ref[1], own_ref[...], land_ref[...]).astype(F32)

        @pl.when(k == 0)
        def _():
            acc_ref[...] = val

        @pl.when(k > 0)
        def _():
            acc_ref[...] += val

        @pl.when(k == 3)
        def _():
            o_ref[...] = acc_ref[...]

    def land_index(t, k, pos_ref):
        return (jnp.where(k == pos_ref[1], (k + 1) % 4, k), t, 0)

    grid_spec = pltpu.PrefetchScalarGridSpec(
        num_scalar_prefetch=1, grid=(r // tr, 4),
        in_specs=[pl.BlockSpec((None, tr, C), lambda t, k, pos_ref: (pos_ref[1], t, 0)),
                  pl.BlockSpec((None, tr, C), land_index)],
        out_specs=pl.BlockSpec((tr, C), lambda t, k, pos_ref: (t, 0)),
        scratch_shapes=[pltpu.VMEM((tr, C), F32)])
    return _tc_call(
        body, name=name, grid_spec=grid_spec, out_shape=_sds((r, C), F32),
        compiler_params=_cparams(("parallel", "arbitrary")),
    )(pos, pair, landed)


def _sum_leading(name, parts, tr, out_dtype=F32):
    n, r, C = parts.shape

    def body(p_ref, o_ref):
        acc = p_ref[0].astype(F32)
        for k in range(1, n):
            acc = acc + p_ref[k].astype(F32)
        o_ref[...] = acc.astype(o_ref.dtype)

    return _tc_call(
        body, name=name, grid=(r // tr,),
        in_specs=[pl.BlockSpec((n, tr, C), lambda t: (0, t, 0))],
        out_specs=pl.BlockSpec((tr, C), lambda t: (t, 0)),
        out_shape=_sds((r, C), out_dtype), compiler_params=_cparams(("parallel",)),
    )(parts)


def _row_tile(r):
    for t in (512, 384, 352, 256, 128, 64, 32, 16, 8):
        if r % t == 0:
            return t
    return r


def _adamw(name, w, g, m, v):
    R, C = w.shape
    tr = _row_tile(R)
    c1 = 1.0 - ADAM_B1 ** ADAM_STEP
    c2 = 1.0 - ADAM_B2 ** ADAM_STEP

    def body(w_ref, g_ref, m_ref, v_ref, d_ref, nm_ref, nv_ref):
        gv = g_ref[...]
        nm = ADAM_B1 * m_ref[...] + (1.0 - ADAM_B1) * gv
        nv = ADAM_B2 * v_ref[...] + (1.0 - ADAM_B2) * (gv * gv)
        m_hat = nm / c1
        v_hat = nv / c2
        d_ref[...] = -ADAM_LR * (m_hat / (jnp.sqrt(v_hat) + ADAM_EPS) + ADAM_WD * w_ref[...])
        nm_ref[...] = nm
        nv_ref[...] = nv

    spec = pl.BlockSpec((tr, C), lambda i: (i, 0))
    return _tc_call(
        body, name=name, grid=(R // tr,), in_specs=[spec] * 4, out_specs=[spec] * 3,
        out_shape=[_sds((R, C), F32)] * 3, compiler_params=_cparams(("parallel",)),
    )(w, g, m, v)


_BIG = ("w_in_t", "w_out", "w_q", "w_k", "w_v", "w_o", "w_gate_t", "w_up_t", "w_down")
_SMALL = ("norm_mix_g", "pool_scale", "sgu_norm_g", "w_spatial", "b_spatial", "norm_xattn_g",
          "norm_mem_g", "norm_ffn_g", "final_norm_g")
_LANES = 128
_GATHER_GROUPS = (("w_in_t", "pool_w"), ("w_out",), ("w_q",), ("w_k", "w_v"), ("w_o",), ("w_gate_t",),
                  ("w_up_t",), ("w_down",))
_RS_ORDER = ("w_down", "w_gate_t", "w_up_t", "w_o", "w_q", "w_k", "w_v", "w_out", "w_in_t")
_SMALL_GROUPS = dict(
    early=("pool_w", "pool_scale", "sgu_norm_g", "w_spatial", "b_spatial", "norm_xattn_g", "norm_mem_g",
           "norm_ffn_g", "final_norm_g"),
    late=("norm_mix_g",))
_ID_GATHER, _ID_PAIR, _ID_CHIP = 0, 1, 2


_PACK_ROWS = 512


def _pack(parts):
    rows = [p.reshape(-1, _LANES) for p in parts]
    n = sum(r.shape[0] for r in rows)
    pad = -n % (_PACK_ROWS if n > _PACK_ROWS else 8)
    if pad:
        rows.append(jnp.zeros((pad, _LANES), rows[0].dtype))
    return jnp.concatenate(rows, axis=0)


class _GradReducer:
    def __init__(self, pos, apply, apply_small):
        self.pos, self.apply, self.apply_small = pos, apply, apply_small
        self.view, self.theirs, self.pair, self.landed = {}, {}, {}, {}
        self.small_gathered = {}

    def push(self, k, g):
        r = g.shape[0] // N_DEV
        self.view[k] = g.reshape(4, 2, r, g.shape[1])
        self.theirs[k] = _seq_pair_exchange("grad_pair_exchange_" + k, self.view[k], _ID_PAIR)

    def reduce(self, k):
        r = self.view[k].shape[2]
        self.pair[k] = _pair_sum("grad_pair_sum_" + k, self.view[k], self.theirs[k], self.pos, _row_tile(r))
        self.landed[k] = _seq_chip_exchange("grad_chip_exchange_" + k, self.pair[k], _ID_CHIP)

    def finish(self, k):
        r = self.pair[k].shape[1]
        self.apply(k, _chip_sum("grad_chip_sum_" + k, self.pair[k], self.landed[k], self.pos, _row_tile(r)))

    def small(self, tag, parts):
        packed = _pack([parts[k] for k in _SMALL_GROUPS[tag]])
        (self.small_gathered[tag],) = _seq_all_gather("gather_small_grads_" + tag, [packed], _ID_GATHER)

    def finish_small(self, tag):
        allp = self.small_gathered[tag]
        self.apply_small(tag, _sum_leading("sum_small_grads_" + tag, allp, min(_PACK_ROWS, allp.shape[1])))


def _unpack(packed, like):
    out, row = [], 0
    for ref in like:
        rows = ref.size // _LANES
        out.append(packed[row:row + rows].reshape(ref.shape))
        row += rows
    return out


def kernel(x, mem, norm_mix_g, w_in, pool_w, pool_scale, sgu_norm_g, w_spatial, b_spatial, w_out, norm_xattn_g, norm_mem_g, w_q, w_k, w_v, w_o, norm_ffn_g, w_gate, w_up, w_down, final_norm_g, loss_target, m_norm_mix_g, m_w_in, m_pool_w, m_pool_scale, m_sgu_norm_g, m_w_spatial, m_b_spatial, m_w_out, m_norm_xattn_g, m_norm_mem_g, m_w_q, m_w_k, m_w_v, m_w_o, m_norm_ffn_g, m_w_gate, m_w_up, m_w_down, m_final_norm_g, v_norm_mix_g, v_w_in, v_pool_w, v_pool_scale, v_sgu_norm_g, v_w_spatial, v_b_spatial, v_w_out, v_norm_xattn_g, v_norm_mem_g, v_w_q, v_w_k, v_w_v, v_w_o, v_norm_ffn_g, v_w_gate, v_w_up, v_w_down, v_final_norm_g):
    args = dict(locals())
    names = ("norm_mix_g", "w_in", "pool_w", "pool_scale", "sgu_norm_g", "w_spatial", "b_spatial", "w_out",
             "norm_xattn_g", "norm_mem_g", "w_q", "w_k", "w_v", "w_o", "norm_ffn_g", "w_gate", "w_up",
             "w_down", "final_norm_g")
    w = {k: args[k] for k in names}
    m = {k: args["m_" + k] for k in names}
    v = {k: args["v_" + k] for k in names}
    my_dev = 4 * lax.axis_index("x") + 2 * lax.axis_index("y") + lax.axis_index("c")
    _CHAIN.__init__()

    shards = dict(
        w_in_t=w["w_in"][0].T, w_out=w["w_out"][0], w_q=w["w_q"][0], w_k=w["w_k"][0], w_v=w["w_v"][0],
        w_o=w["w_o"][0], w_gate_t=w["w_gate"][0].T, w_up_t=w["w_up"][0].T, w_down=w["w_down"][0])
    send = {k: shards[k].astype(BF16) for k in _BIG}
    send["pool_w"] = w["pool_w"][0].reshape(4 * 32, POOL_GROUP).astype(BF16)
    W = {}
    for gi, group in enumerate(_GATHER_GROUPS):
        gathered = _seq_all_gather("gather_weights_%d" % gi, [send[k] for k in group], _ID_GATHER)
        for k, g in zip(group, gathered):
            W[k] = g.reshape(-1, g.shape[-1])
    W["pool_w"] = W["pool_w"].reshape(N_DEV, 4, 32, POOL_GROUP).transpose(1, 0, 2, 3).reshape(4, POOL_GROUP, POOL_GROUP)

    t = jnp.arange(SGU_BLOCK)
    mask = (t[None, :] // SGU_CHUNK) <= (t[:, None] // SGU_CHUNK)
    sm = dict(
        norm_mix_g=w["norm_mix_g"], pool_scale=w["pool_scale"], sgu_norm_g=w["sgu_norm_g"],
        norm_xattn_g=w["norm_xattn_g"], norm_mem_g=w["norm_mem_g"], norm_ffn_g=w["norm_ffn_g"],
        final_norm_g=w["final_norm_g"].reshape(1, D_MODEL),
        ws_masked=jnp.where(mask[None], w["w_spatial"][0], 0.0).astype(BF16),
        bias_full=jnp.repeat(w["b_spatial"][0].T, SGU_BLOCK, axis=1))

    natural = dict(w_in_t="w_in", w_gate_t="w_gate", w_up_t="w_up")
    grads, delta, new_m, new_v = {}, {}, {}, {}

    def adamw(k):
        shp = w[k].shape
        two_d = (-1, shp[-1])
        d_, m_, v_ = _adamw("adamw_" + k, w[k].reshape(two_d), grads[k].reshape(two_d), m[k].reshape(two_d),
                            v[k].reshape(two_d))
        delta[k], new_m[k], new_v[k] = d_.reshape(shp), m_.reshape(shp), v_.reshape(shp)

    def apply(k, g):
        name = natural.get(k, k)
        grads[name] = (g.T if k in natural else g)[None]
        adamw(name)

    like = dict(w)
    like["pool_w"] = _sds((4, POOL_GROUP, POOL_GROUP), F32)

    def apply_small(tag, total):
        group = _SMALL_GROUPS[tag]
        grads.update(zip(group, _unpack(total, [like[k] for k in group])))
        if tag == "early":
            grads["pool_w"] = lax.dynamic_slice_in_dim(grads["pool_w"], my_dev * 32, 32, axis=1)[None]
            adamw("pool_w")
        else:
            d_, m_, v_ = _adamw("adamw_small", _pack([w[k] for k in _SMALL]), _pack([grads[k] for k in _SMALL]),
                                _pack([m[k] for k in _SMALL]), _pack([v[k] for k in _SMALL]))
            shapes = [w[k] for k in _SMALL]
            for k, a, b, c_ in zip(_SMALL, _unpack(d_, shapes), _unpack(m_, shapes), _unpack(v_, shapes)):
                delta[k], new_m[k], new_v[k] = a, b, c_

    pos = jnp.stack([lax.axis_index("c"), 2 * lax.axis_index("x") + lax.axis_index("y")]).astype(jnp.int32)
    rs = _GradReducer(pos, apply, apply_small)
    loss_part, grad_x = _local_step(x[0], mem[0], loss_target[0], W, sm, rs)
    loss = lax.psum(loss_part[0, 0], ("x", "y", "c"))

    outs = [loss, grad_x[None]]
    outs += [grads[k].reshape(w[k].shape) for k in names]
    outs += [delta[k] for k in names]
    outs += [new_m[k] for k in names]
    outs += [new_v[k] for k in names]
    return tuple(outs)
```

```python
import functools

import jax
import jax.numpy as jnp
from jax import lax
from jax.experimental import pallas as pl
from jax.experimental.pallas import tpu as pltpu
from jax.experimental.pallas import tpu_sc as plsc

F32 = jnp.float32
BF16 = jnp.bfloat16
MESH = pl.DeviceIdType.MESH

EPS = 1e-6
D_MODEL = 2048
D_POOL = 1024
D_SGU = 1024
POOL_WINDOWS = (2, 4, 8, 16)
POOL_GROUP = 256
POOL_HALO = 16
SGU_BLOCK = 128
SGU_CHUNK = 64
N_SGU_HEADS = 8
N_HEADS = 4
HEAD_DIM = 512
N_DEV = 8

ADAM_LR = 0.001
ADAM_B1 = 0.9
ADAM_B2 = 0.999
ADAM_EPS = 1e-08
ADAM_WD = 0.01
ADAM_STEP = 10

VMEM_LIMIT = 56 * 1024 * 1024


def _cparams(sem=None):
    return pltpu.CompilerParams(dimension_semantics=sem, vmem_limit_bytes=VMEM_LIMIT)


def _sds(shape, dtype):
    return jax.ShapeDtypeStruct(shape, dtype)


_ANY = pl.BlockSpec(memory_space=pl.ANY)


class _Chain:
    def __init__(self):
        self.tc = None
        self.sc = None


_CHAIN = _Chain()


def _first(out):
    return out[0] if isinstance(out, (list, tuple)) else out


def _tc_call(body, *, in_specs=None, grid_spec=None, **kw):
    def run(*args):
        prev, n = _CHAIN.tc, len(args)
        fn, specs, spec, operands = body, in_specs, grid_spec, args
        if prev is not None:
            def fn(*refs):
                return body(*refs[:n], *refs[n + 1:])
            operands = args + (prev,)
            if grid_spec is None:
                specs = list(in_specs) + [_ANY]
            else:
                spec = pltpu.PrefetchScalarGridSpec(
                    num_scalar_prefetch=grid_spec.num_scalar_prefetch, grid=grid_spec.grid,
                    in_specs=list(grid_spec.in_specs) + [_ANY], out_specs=grid_spec.out_specs,
                    scratch_shapes=grid_spec.scratch_shapes)
        if spec is None:
            out = pl.pallas_call(fn, in_specs=specs, **kw)(*operands)
        else:
            out = pl.pallas_call(fn, grid_spec=spec, **kw)(*operands)
        _CHAIN.tc = _first(out)
        return out
    return run


def _sc_call(body, **kw):
    return pl.kernel(body, mesh=plsc.ScalarSubcoreMesh(axis_name="seq", num_cores=1), **kw)


def _rowsum8(v):
    r, c = v.shape
    return v.reshape(r // 8, 8, c).sum(axis=0)


_EPILOGUE_COLS = 256
_DN = {
    "nn": (((1,), (0,)), ((), ())),
    "nt": (((1,), (1,)), ((), ())),
    "tn": (((0,), (0,)), ((), ())),
}


def _mm(name, a_list, b_list, terms, mode, tm, tn, tk, out_dtypes, epilogue=None, extras=(), n_acc=1):
    a0, b0 = a_list[0], b_list[0]
    if mode == "tn":
        K, M = a0.shape
    else:
        M, K = a0.shape
    N = b0.shape[0] if mode == "nt" else b0.shape[1]
    assert M % tm == 0 and N % tn == 0 and K % tk == 0, (name, M, N, K, tm, tn, tk)
    nk = K // tk
    na, nb, ne, no = len(a_list), len(b_list), len(extras), len(out_dtypes)
    dn = _DN[mode]

    if mode == "tn":
        a_spec = pl.BlockSpec((tk, tm), lambda i, j, k: (k, i))
    else:
        a_spec = pl.BlockSpec((tm, tk), lambda i, j, k: (i, k))
    if mode == "nt":
        b_spec = pl.BlockSpec((tn, tk), lambda i, j, k: (j, k))
    else:
        b_spec = pl.BlockSpec((tk, tn), lambda i, j, k: (k, j))
    o_spec = pl.BlockSpec((tm, tn), lambda i, j, k: (i, j))

    def body(*refs):
        a_refs = refs[:na]
        b_refs = refs[na:na + nb]
        e_refs = refs[na + nb:na + nb + ne]
        o_refs = refs[na + nb + ne:na + nb + ne + no]
        acc_refs = refs[na + nb + ne + no:]

        def products(cols):
            parts = [None] * n_acc
            for ai, bi, ci in terms:
                b = b_refs[bi][cols, :] if mode == "nt" else b_refs[bi][:, cols]
                d = lax.dot_general(a_refs[ai][...].astype(BF16), b.astype(BF16), dn, preferred_element_type=F32)
                parts[ci] = d if parts[ci] is None else parts[ci] + d
            return parts

        def finish(accs, cols=slice(None)):
            outs = epilogue(accs, [e[:, cols] for e in e_refs]) if epilogue is not None else accs
            for o_ref, v in zip(o_refs, outs):
                o_ref[:, cols] = v.astype(o_ref.dtype)

        if nk == 1 and epilogue is not None and tn > _EPILOGUE_COLS:
            for c0 in range(0, tn, _EPILOGUE_COLS):
                cols = slice(c0, c0 + _EPILOGUE_COLS)
                finish(products(cols), cols)
            return
        parts = products(slice(None))
        if nk == 1:
            finish(parts)
        else:
            k = pl.program_id(2)

            @pl.when(k == 0)
            def _():
                for c in range(n_acc):
                    acc_refs[c][...] = parts[c]

            @pl.when(k > 0)
            def _():
                for c in range(n_acc):
                    acc_refs[c][...] += parts[c]

            @pl.when(k == nk - 1)
            def _():
                finish([acc_refs[c][...] for c in range(n_acc)])

    scratch = [pltpu.VMEM((tm, tn), F32) for _ in range(n_acc)] if nk > 1 else []
    res = _tc_call(
        body, name=name, grid=(M // tm, N // tn, nk),
        in_specs=[a_spec] * na + [b_spec] * nb + [o_spec] * ne,
        out_specs=[o_spec] * no,
        out_shape=[_sds((M, N), dt) for dt in out_dtypes],
        scratch_shapes=scratch,
        compiler_params=_cparams(("parallel", "parallel", "arbitrary")),
    )(*a_list, *b_list, *extras)
    return res


def _mm_rows(name, a_list, b_list, terms, mode, tm, tk, rows, vecs, row_dtypes, n_vec_out, epilogue,
             n_scalar_out=0):
    a0, b0 = a_list[0], b_list[0]
    M, K = a0.shape
    N = b0.shape[0] if mode == "nt" else b0.shape[1]
    assert mode in ("nn", "nt") and M % tm == 0 and K % tk == 0, (name, M, N, K, tm, tk)
    nm, nk = M // tm, K // tk
    slab = min(128, tm)
    na, nb, nr, nv, no = len(a_list), len(b_list), len(rows), len(vecs), len(row_dtypes)
    dn = _DN[mode]
    a_spec = pl.BlockSpec((tm, tk), lambda i, k: (i, k))
    b_spec = pl.BlockSpec((N, tk), lambda i, k: (0, k)) if mode == "nt" else pl.BlockSpec((tk, N), lambda i, k: (k, 0))
    row_spec = pl.BlockSpec((tm, N), lambda i, k: (i, 0))
    vec_spec = pl.BlockSpec((1, N), lambda i, k: (0, 0))
    one_spec = pl.BlockSpec((1, 1), lambda i, k: (0, 0))

    def body(*refs):
        pos = 0
        a_refs = refs[pos:pos + na]; pos += na
        b_refs = refs[pos:pos + nb]; pos += nb
        r_refs = refs[pos:pos + nr]; pos += nr
        v_refs = refs[pos:pos + nv]; pos += nv
        o_refs = refs[pos:pos + no]; pos += no
        s_refs = refs[pos:pos + n_vec_out]; pos += n_vec_out
        vacc_refs = refs[pos:pos + n_vec_out]; pos += n_vec_out
        acc_ref = refs[pos] if nk > 1 else None
        i, k = pl.program_id(0), pl.program_id(1)
        part = None
        for ai, bi, _ in terms:
            d = lax.dot_general(a_refs[ai][...].astype(BF16), b_refs[bi][...].astype(BF16), dn,
                                preferred_element_type=F32)
            part = d if part is None else part + d

        def finish(acc):
            vecs_now = [v[...] for v in v_refs]
            vparts = None
            for r0 in range(0, tm, slab):
                rs_ = slice(r0, r0 + slab)
                outs, vp = epilogue(acc[rs_, :], [r[rs_, :] for r in r_refs], vecs_now)
                for o_ref, val in zip(o_refs, outs):
                    o_ref[rs_, :] = val.astype(o_ref.dtype)
                vparts = vp if vparts is None else [a + b for a, b in zip(vparts, vp)]

            @pl.when(i == 0)
            def _():
                for vacc, vp in zip(vacc_refs, vparts):
                    vacc[...] = vp

            @pl.when(i > 0)
            def _():
                for vacc, vp in zip(vacc_refs, vparts):
                    vacc[...] += vp

            @pl.when(i == nm - 1)
            def _():
                for j, (s_ref, vacc) in enumerate(zip(s_refs, vacc_refs)):
                    col = jnp.sum(vacc[...], axis=0, keepdims=True)
                    s_ref[...] = jnp.sum(col, axis=1, keepdims=True) if j >= n_vec_out - n_scalar_out else col

        if nk == 1:
            finish(part)
        else:
            @pl.when(k == 0)
            def _():
                acc_ref[...] = part

            @pl.when(k > 0)
            def _():
                acc_ref[...] += part

            @pl.when(k == nk - 1)
            def _():
                finish(acc_ref)

    n_plain = n_vec_out - n_scalar_out
    return _tc_call(
        body, name=name, grid=(nm, nk),
        in_specs=[a_spec] * na + [b_spec] * nb + [row_spec] * nr + [vec_spec] * nv,
        out_specs=[row_spec] * no + [vec_spec] * n_plain + [one_spec] * n_scalar_out,
        out_shape=[_sds((M, N), dt) for dt in row_dtypes] + [_sds((1, N), F32)] * n_plain
        + [_sds((1, 1), F32)] * n_scalar_out,
        scratch_shapes=[pltpu.VMEM((8, N), F32)] * n_vec_out + ([pltpu.VMEM((tm, N), F32)] if nk > 1 else []),
        compiler_params=_cparams(("arbitrary", "arbitrary")),
    )(*a_list, *b_list, *rows, *vecs)


def _ep_residual_norm(acc, rows, vecs):
    x_new = rows[0] + acc
    r = lax.rsqrt(jnp.mean(x_new * x_new, axis=-1, keepdims=True) + EPS)
    return [x_new, x_new * r * vecs[0]], []


def _ep_norm_bwd(acc, rows, vecs):
    xv, dres = rows
    r = lax.rsqrt(jnp.mean(xv * xv, axis=-1, keepdims=True) + EPS)
    xh = xv * r
    dxh = acc * vecs[0]
    m = jnp.mean(dxh * xh, axis=-1, keepdims=True)
    dx = dres + r * (dxh - xh * m)
    return [dx, dx], [_rowsum8(acc * xh)]


def _ep_final_loss(acc, rows, vecs):
    x2, target = rows
    gv = vecs[0]
    xv = x2 + acc
    inv_d = 1.0 / xv.shape[-1]
    r = lax.rsqrt(jnp.mean(xv * xv, axis=-1, keepdims=True) + EPS)
    xh = xv * r
    e = xh * gv - target
    dy = e * inv_d
    dxh = dy * gv
    m = jnp.mean(dxh * xh, axis=-1, keepdims=True)
    dx = r * (dxh - xh * m)
    return [dx, dx], [_rowsum8(dy * xh), _rowsum8(e * e) * (0.5 * inv_d)]


def _mm1(name, a, b, mode, tm, tn, tk, out_dtype, **kw):
    return _mm(name, [a], [b], [(0, 0, 0)], mode, tm, tn, tk, [out_dtype], **kw)[0]


def _rms_fwd(name, x, g, tr):
    S, Dm = x.shape

    def body(x_ref, g_ref, h_ref):
        xv = x_ref[...]
        r = lax.rsqrt(jnp.mean(xv * xv, axis=-1, keepdims=True) + EPS)
        h_ref[...] = (xv * r * g_ref[...]).astype(h_ref.dtype)

    return _tc_call(
        body, name=name, grid=(S // tr,),
        in_specs=[pl.BlockSpec((tr, Dm), lambda i: (i, 0)), pl.BlockSpec((1, Dm), lambda i: (0, 0))],
        out_specs=pl.BlockSpec((tr, Dm), lambda i: (i, 0)),
        out_shape=_sds((S, Dm), BF16),
        compiler_params=_cparams(("parallel",)),
    )(x, g)


def _rms_bwd(name, dh, x, g, dres, tr, want_dx=True):
    S, Dm = x.shape
    nsteps = S // tr

    def body(*refs):
        if want_dx:
            dh_ref, x_ref, g_ref, dres_ref, dx_ref, dxb_ref, dg_ref, acc_ref = refs
        else:
            dh_ref, x_ref, g_ref, dg_ref, acc_ref = refs
        i = pl.program_id(0)
        xv = x_ref[...]
        r = lax.rsqrt(jnp.mean(xv * xv, axis=-1, keepdims=True) + EPS)
        xh = xv * r
        dhv = dh_ref[...]
        part = _rowsum8(dhv * xh)

        @pl.when(i == 0)
        def _():
            acc_ref[...] = part

        @pl.when(i > 0)
        def _():
            acc_ref[...] += part

        @pl.when(i == nsteps - 1)
        def _():
            dg_ref[...] = jnp.sum(acc_ref[...], axis=0, keepdims=True)

        if want_dx:
            dxh = dhv * g_ref[...]
            m = jnp.mean(dxh * xh, axis=-1, keepdims=True)
            dx = dres_ref[...] + r * (dxh - xh * m)
            dx_ref[...] = dx
            dxb_ref[...] = dx.astype(BF16)

    row = pl.BlockSpec((tr, Dm), lambda i: (i, 0))
    vec = pl.BlockSpec((1, Dm), lambda i: (0, 0))
    if want_dx:
        in_specs = [row, row, vec, row]
        out_specs = [row, row, vec]
        out_shape = [_sds((S, Dm), F32), _sds((S, Dm), BF16), _sds((1, Dm), F32)]
        args = (dh, x, g, dres)
    else:
        in_specs = [row, row, vec]
        out_specs = [vec]
        out_shape = [_sds((1, Dm), F32)]
        args = (dh, x, g)
    return _tc_call(
        body, name=name, grid=(nsteps,), in_specs=in_specs, out_specs=out_specs, out_shape=out_shape,
        scratch_shapes=[pltpu.VMEM((8, Dm), F32)],
        compiler_params=_cparams(("arbitrary",)),
    )(*args)


def _final_loss(name, x3, g, target, tr):
    S, Dm = x3.shape
    nsteps = S // tr

    def body(x_ref, g_ref, t_ref, dx_ref, dxb_ref, dg_ref, loss_ref, acc_g, acc_l):
        i = pl.program_id(0)
        xv = x_ref[...]
        gv = g_ref[...]
        r = lax.rsqrt(jnp.mean(xv * xv, axis=-1, keepdims=True) + EPS)
        xh = xv * r
        e = xh * gv - t_ref[...]
        dy = e * (1.0 / Dm)
        lpart = _rowsum8(e * e)
        gpart = _rowsum8(dy * xh)

        @pl.when(i == 0)
        def _():
            acc_g[...] = gpart
            acc_l[...] = lpart

        @pl.when(i > 0)
        def _():
            acc_g[...] += gpart
            acc_l[...] += lpart

        @pl.when(i == nsteps - 1)
        def _():
            dg_ref[...] = jnp.sum(acc_g[...], axis=0, keepdims=True)
            tot = jnp.sum(jnp.sum(acc_l[...], axis=1, keepdims=True), axis=0, keepdims=True)
            loss_ref[...] = tot * (0.5 / Dm)

        dxh = dy * gv
        m = jnp.mean(dxh * xh, axis=-1, keepdims=True)
        dx = r * (dxh - xh * m)
        dx_ref[...] = dx
        dxb_ref[...] = dx.astype(BF16)

    row = pl.BlockSpec((tr, Dm), lambda i: (i, 0))
    vec = pl.BlockSpec((1, Dm), lambda i: (0, 0))
    return _tc_call(
        body, name=name, grid=(nsteps,),
        in_specs=[row, vec, row],
        out_specs=[row, row, vec, pl.BlockSpec((1, 1), lambda i: (0, 0))],
        out_shape=[_sds((S, Dm), F32), _sds((S, Dm), BF16), _sds((1, Dm), F32), _sds((1, 1), F32)],
        scratch_shapes=[pltpu.VMEM((8, Dm), F32), pltpu.VMEM((8, Dm), F32)],
        compiler_params=_cparams(("arbitrary",)),
    )(x3, g, target)


def _softmax_rows(s):
    e = jnp.exp(s - jnp.max(s, axis=-1, keepdims=True))
    return e / jnp.sum(e, axis=-1, keepdims=True)


def _attn_fwd(name, q, k, v, ts):
    S, Dm = q.shape
    M = k.shape[0]
    scale = HEAD_DIM ** -0.5

    def body(q_ref, k_ref, v_ref, o_ref):
        for h in range(N_HEADS):
            sl = slice(h * HEAD_DIM, (h + 1) * HEAD_DIM)
            s = lax.dot_general(q_ref[:, sl], k_ref[:, sl], _DN["nt"], preferred_element_type=F32) * scale
            p = _softmax_rows(s)
            o_ref[:, sl] = jnp.dot(p.astype(BF16), v_ref[:, sl], preferred_element_type=F32).astype(o_ref.dtype)

    row = pl.BlockSpec((ts, Dm), lambda i: (i, 0))
    mem = pl.BlockSpec((M, Dm), lambda i: (0, 0))
    return _tc_call(
        body, name=name, grid=(S // ts,), in_specs=[row, mem, mem], out_specs=row,
        out_shape=_sds((S, Dm), BF16), compiler_params=_cparams(("parallel",)),
    )(q, k, v)


def _attn_bwd(name, q, k, v, do, ts):
    S, Dm = q.shape
    M = k.shape[0]
    scale = HEAD_DIM ** -0.5

    def body(q_ref, k_ref, v_ref, do_ref, dq_ref, dk_ref, dv_ref):
        i = pl.program_id(0)

        @pl.when(i == 0)
        def _():
            dk_ref[...] = jnp.zeros_like(dk_ref)
            dv_ref[...] = jnp.zeros_like(dv_ref)

        for h in range(N_HEADS):
            sl = slice(h * HEAD_DIM, (h + 1) * HEAD_DIM)
            qh = q_ref[:, sl]
            kh = k_ref[:, sl]
            doh = do_ref[:, sl]
            s = lax.dot_general(qh, kh, _DN["nt"], preferred_element_type=F32) * scale
            p = _softmax_rows(s)
            dp = lax.dot_general(doh, v_ref[:, sl], _DN["nt"], preferred_element_type=F32)
            ds = p * (dp - jnp.sum(dp * p, axis=-1, keepdims=True)) * scale
            dsb = ds.astype(BF16)
            dq_ref[:, sl] = jnp.dot(dsb, kh, preferred_element_type=F32).astype(dq_ref.dtype)
            dk_ref[:, sl] += lax.dot_general(dsb, qh, _DN["tn"], preferred_element_type=F32)
            dv_ref[:, sl] += lax.dot_general(p.astype(BF16), doh, _DN["tn"], preferred_element_type=F32)

    row = pl.BlockSpec((ts, Dm), lambda i: (i, 0))
    mem = pl.BlockSpec((M, Dm), lambda i: (0, 0))
    return _tc_call(
        body, name=name, grid=(S // ts,), in_specs=[row, mem, mem, row], out_specs=[row, mem, mem],
        out_shape=[_sds((S, Dm), BF16), _sds((M, Dm), F32), _sds((M, Dm), F32)],
        compiler_params=_cparams(("arbitrary",)),
    )(q, k, v, do)


def _pool_denominators(row0, ts):
    return (row0 + lax.broadcasted_iota(jnp.int32, (ts, 1), 0) + 1).astype(F32)


def _mixer_fwd(name, proj, pool_w, pool_scale, sgu_g, ws, bias_full, ts):
    S = proj.shape[0]
    nblk = ts // SGU_BLOCK
    halo_blocks = ts // POOL_HALO

    def body(proj_ref, halo_ref, pw_ref, sc_ref, g_ref, ws_ref, b_ref, y_ref, p_ref, vn_ref, ext_ref):
        i = pl.program_id(0)
        a = proj_ref[:, 0:D_POOL]
        ext_ref[0:POOL_HALO, :] = jnp.where(i > 0, halo_ref[...], 0.0)
        ext_ref[POOL_HALO:POOL_HALO + ts, :] = a
        pos = _pool_denominators(i * ts, ts)
        for gi, w in enumerate(POOL_WINDOWS):
            cs = slice(gi * POOL_GROUP, (gi + 1) * POOL_GROUP)
            acc = a[:, cs]
            for j in range(1, w):
                acc = acc + ext_ref[POOL_HALO - j:POOL_HALO - j + ts, cs]
            pg = (acc / jnp.minimum(pos, float(w)) - a[:, cs]).astype(BF16)
            p_ref[:, cs] = pg
            ypre = jnp.dot(pg, pw_ref[gi], preferred_element_type=F32)
            y_ref[:, cs] = (ypre * sc_ref[:, cs]).astype(y_ref.dtype)

        v = proj_ref[:, D_POOL + D_SGU:D_POOL + 2 * D_SGU]
        r = lax.rsqrt(jnp.mean(v * v, axis=-1, keepdims=True) + EPS)
        vn_ref[...] = (v * r * g_ref[...]).astype(BF16)
        for n in range(nblk):
            rs = slice(n * SGU_BLOCK, (n + 1) * SGU_BLOCK)
            for h in range(N_SGU_HEADS):
                cs = slice(h * SGU_BLOCK, (h + 1) * SGU_BLOCK)
                mixed = jnp.dot(ws_ref[h], vn_ref[rs, cs], preferred_element_type=F32) + b_ref[:, cs]
                u = proj_ref[rs, D_POOL + h * SGU_BLOCK:D_POOL + (h + 1) * SGU_BLOCK]
                y_ref[rs, D_POOL + h * SGU_BLOCK:D_POOL + (h + 1) * SGU_BLOCK] = (u * mixed).astype(y_ref.dtype)

    return _tc_call(
        body, name=name, grid=(S // ts,),
        in_specs=[
            pl.BlockSpec((ts, D_POOL + 2 * D_SGU), lambda i: (i, 0)),
            pl.BlockSpec((POOL_HALO, D_POOL), lambda i: (jnp.maximum(i * halo_blocks - 1, 0), 0)),
            pl.BlockSpec((4, POOL_GROUP, POOL_GROUP), lambda i: (0, 0, 0)),
            pl.BlockSpec((1, D_POOL), lambda i: (0, 0)),
            pl.BlockSpec((1, D_SGU), lambda i: (0, 0)),
            pl.BlockSpec((N_SGU_HEADS, SGU_BLOCK, SGU_BLOCK), lambda i: (0, 0, 0)),
            pl.BlockSpec((SGU_BLOCK, D_SGU), lambda i: (0, 0)),
        ],
        out_specs=[
            pl.BlockSpec((ts, D_MODEL), lambda i: (i, 0)),
            pl.BlockSpec((ts, D_POOL), lambda i: (i, 0)),
            pl.BlockSpec((ts, D_SGU), lambda i: (i, 0)),
        ],
        out_shape=[_sds((S, D_MODEL), BF16), _sds((S, D_POOL), BF16), _sds((S, D_SGU), BF16)],
        scratch_shapes=[pltpu.VMEM((ts + POOL_HALO, D_POOL), F32)],
        compiler_params=_cparams(("parallel",)),
    )(proj, proj, pool_w, pool_scale, sgu_g, ws, bias_full)


def _mixer_bwd(name, dymix, proj, p, vn, pool_w, pool_scale, sgu_g, ws, bias_full, ts):
    S = proj.shape[0]
    nsteps = S // ts
    nblk = ts // SGU_BLOCK
    halo_blocks = ts // POOL_HALO

    def body(dy_ref, dyh_ref, u_ref, v_ref, p_ref, vn_ref, pw_ref, sc_ref, g_ref, ws_ref, b_ref,
             dproj_ref, dpw_ref, dsc_ref, dg_ref, dws_ref, db_ref,
             ext_ref, dvn_ref, acc_sc, acc_g, acc_b):
        i = pl.program_id(0)

        @pl.when(i == 0)
        def _():
            dpw_ref[...] = jnp.zeros_like(dpw_ref)
            dws_ref[...] = jnp.zeros_like(dws_ref)
            acc_sc[...] = jnp.zeros_like(acc_sc)
            acc_g[...] = jnp.zeros_like(acc_g)
            acc_b[...] = jnp.zeros_like(acc_b)

        pos = _pool_denominators(i * ts, ts)
        pos_h = _pool_denominators((i + 1) * ts, POOL_HALO)
        for gi, w in enumerate(POOL_WINDOWS):
            cs = slice(gi * POOL_GROUP, (gi + 1) * POOL_GROUP)
            pg = p_ref[:, cs]
            wg = pw_ref[gi]
            dyp = dy_ref[:, cs]
            ypre = jnp.dot(pg, wg, preferred_element_type=F32)
            acc_sc[:, cs] += _rowsum8(dyp * ypre)
            dz = (dyp * sc_ref[:, cs]).astype(BF16)
            dpw_ref[gi] += lax.dot_general(pg, dz, _DN["tn"], preferred_element_type=F32)
            dp = lax.dot_general(dz, wg, _DN["nt"], preferred_element_type=F32)
            dzh = (dyh_ref[:, cs] * sc_ref[:, cs]).astype(BF16)
            dph = lax.dot_general(dzh, wg, _DN["nt"], preferred_element_type=F32)
            ext_ref[0:ts, cs] = dp / jnp.minimum(pos, float(w))
            ext_ref[ts:ts + POOL_HALO, cs] = jnp.where(i < nsteps - 1, dph / jnp.minimum(pos_h, float(w)), 0.0)
            acc = ext_ref[0:ts, cs]
            for j in range(1, w):
                acc = acc + ext_ref[j:j + ts, cs]
            dproj_ref[:, cs] = (acc - dp).astype(dproj_ref.dtype)

        for n in range(nblk):
            rs = slice(n * SGU_BLOCK, (n + 1) * SGU_BLOCK)
            for h in range(N_SGU_HEADS):
                cs = slice(h * SGU_BLOCK, (h + 1) * SGU_BLOCK)
                vnb = vn_ref[rs, cs]
                wh = ws_ref[h]
                mixed = jnp.dot(wh, vnb, preferred_element_type=F32) + b_ref[:, cs]
                dys = dy_ref[rs, D_POOL + h * SGU_BLOCK:D_POOL + (h + 1) * SGU_BLOCK]
                dproj_ref[rs, D_POOL + h * SGU_BLOCK:D_POOL + (h + 1) * SGU_BLOCK] = (dys * mixed).astype(dproj_ref.dtype)
                dmix = dys * u_ref[rs, cs]
                acc_b[:, cs] += dmix
                dmb = dmix.astype(BF16)
                dws_ref[h] += lax.dot_general(dmb, vnb, _DN["nt"], preferred_element_type=F32)
                dvn_ref[rs, cs] = lax.dot_general(wh, dmb, _DN["tn"], preferred_element_type=F32)
        v = v_ref[...]
        r = lax.rsqrt(jnp.mean(v * v, axis=-1, keepdims=True) + EPS)
        vh = v * r
        dvn = dvn_ref[...]
        acc_g[...] += _rowsum8(dvn * vh)
        dxh = dvn * g_ref[...]
        m = jnp.mean(dxh * vh, axis=-1, keepdims=True)
        dproj_ref[:, D_POOL + D_SGU:D_POOL + 2 * D_SGU] = (r * (dxh - vh * m)).astype(dproj_ref.dtype)

        @pl.when(i == nsteps - 1)
        def _():
            dsc_ref[...] = jnp.sum(acc_sc[...], axis=0, keepdims=True)
            dg_ref[...] = jnp.sum(acc_g[...], axis=0, keepdims=True)
            t_idx = lax.broadcasted_iota(jnp.int32, (SGU_BLOCK, SGU_BLOCK), 0) // SGU_CHUNK
            s_idx = lax.broadcasted_iota(jnp.int32, (SGU_BLOCK, SGU_BLOCK), 1) // SGU_CHUNK
            mask = s_idx <= t_idx
            for h in range(N_SGU_HEADS):
                cs = slice(h * SGU_BLOCK, (h + 1) * SGU_BLOCK)
                dws_ref[h] = jnp.where(mask, dws_ref[h], 0.0)
                col = jnp.sum(acc_b[:, cs], axis=1, keepdims=True)
                db_ref[h] = jnp.broadcast_to(col, (SGU_BLOCK, SGU_BLOCK))

    const2 = lambda i: (0, 0)
    const3 = lambda i: (0, 0, 0)
    last_halo = S // POOL_HALO - 1
    return _tc_call(
        body, name=name, grid=(nsteps,),
        in_specs=[
            pl.BlockSpec((ts, D_MODEL), lambda i: (i, 0)),
            pl.BlockSpec((POOL_HALO, D_POOL), lambda i: (jnp.minimum((i + 1) * halo_blocks, last_halo), 0)),
            pl.BlockSpec((ts, D_SGU), lambda i: (i, 1)),
            pl.BlockSpec((ts, D_SGU), lambda i: (i, 2)),
            pl.BlockSpec((ts, D_POOL), lambda i: (i, 0)),
            pl.BlockSpec((ts, D_SGU), lambda i: (i, 0)),
            pl.BlockSpec((4, POOL_GROUP, POOL_GROUP), const3),
            pl.BlockSpec((1, D_POOL), const2),
            pl.BlockSpec((1, D_SGU), const2),
            pl.BlockSpec((N_SGU_HEADS, SGU_BLOCK, SGU_BLOCK), const3),
            pl.BlockSpec((SGU_BLOCK, D_SGU), const2),
        ],
        out_specs=[
            pl.BlockSpec((ts, D_POOL + 2 * D_SGU), lambda i: (i, 0)),
            pl.BlockSpec((4, POOL_GROUP, POOL_GROUP), const3),
            pl.BlockSpec((1, D_POOL), const2),
            pl.BlockSpec((1, D_SGU), const2),
            pl.BlockSpec((N_SGU_HEADS, SGU_BLOCK, SGU_BLOCK), const3),
            pl.BlockSpec((N_SGU_HEADS, SGU_BLOCK, SGU_BLOCK), const3),
        ],
        out_shape=[
            _sds((S, D_POOL + 2 * D_SGU), BF16),
            _sds((4, POOL_GROUP, POOL_GROUP), F32),
            _sds((1, D_POOL), F32),
            _sds((1, D_SGU), F32),
            _sds((N_SGU_HEADS, SGU_BLOCK, SGU_BLOCK), F32),
            _sds((N_SGU_HEADS, SGU_BLOCK, SGU_BLOCK), F32),
        ],
        scratch_shapes=[
            pltpu.VMEM((ts + POOL_HALO, D_POOL), F32),
            pltpu.VMEM((ts, D_SGU), F32),
            pltpu.VMEM((8, D_POOL), F32),
            pltpu.VMEM((8, D_SGU), F32),
            pltpu.VMEM((SGU_BLOCK, D_SGU), F32),
        ],
        compiler_params=_cparams(("arbitrary",)),
    )(dymix, dymix, proj, proj, p, vn, pool_w, pool_scale, sgu_g, ws, bias_full)


def _silu_mul(accs, extras):
    (up,) = accs
    gt = extras[0]
    sig = 1.0 / (1.0 + jnp.exp(-gt))
    return gt, up, gt * sig * up


def _silu_mul_bwd(accs, extras):
    (dact,) = accs
    gt = extras[0].astype(F32)
    up = extras[1].astype(F32)
    sig = 1.0 / (1.0 + jnp.exp(-gt))
    silu = gt * sig
    dgt = dact * up * (sig * (1.0 + gt * (1.0 - sig)))
    dup = dact * silu
    return dgt, dup


def _add_residual(accs, extras):
    return (extras[0] + accs[0],)


def _add_residual_and_cast(accs, extras):
    y = extras[0] + accs[0]
    return y, y


def _local_step(x, mem, target, W, sm, rs):
    S = x.shape[0]
    tm = min(1024, S)
    th = min(512, S)
    tq = min(256, S)
    ts = min(512, S)
    tr = min(512, S)
    tk_s = min(2048, S)
    M = mem.shape[0]

    h1 = _rms_fwd("rms_mix", x, sm["norm_mix_g"], tr)
    proj = _mm1("proj_in", h1, W["w_in_t"], "nt", tm, 1024, 2048, F32)
    ymix, p, vn = _mixer_fwd("mixer_fwd", proj, W["pool_w"], sm["pool_scale"], sm["sgu_norm_g"],
                             sm["ws_masked"], sm["bias_full"], ts)
    x1, h2 = _mm_rows("proj_out", [ymix], [W["w_out"]], [(0, 0, 0)], "nn", tq, 2048, [x], [sm["norm_xattn_g"]],
                      [F32, BF16], 0, _ep_residual_norm)

    mb = _rms_fwd("rms_mem", mem, sm["norm_mem_g"], M)
    q = _mm1("proj_q", h2, W["w_q"], "nn", tm, 1024, 2048, BF16)
    kk = _mm1("proj_k", mb, W["w_k"], "nn", M, 1024, 2048, BF16)
    vv = _mm1("proj_v", mb, W["w_v"], "nn", M, 1024, 2048, BF16)
    o = _attn_fwd("attn_fwd", q, kk, vv, ts)
    x2, h3 = _mm_rows("proj_o", [o], [W["w_o"]], [(0, 0, 0)], "nn", tq, 2048, [x1], [sm["norm_ffn_g"]],
                      [F32, BF16], 0, _ep_residual_norm)

    gt32 = _mm1("ffn_gate", h3, W["w_gate_t"], "nt", tm, 512, 2048, F32)
    gt, up, act = _mm("ffn_up", [h3], [W["w_up_t"]], [(0, 0, 0)], "nt", tm, 512, 2048, [BF16, BF16, BF16],
                      epilogue=_silu_mul, extras=(gt32,))
    x3 = _mm1("ffn_down", act, W["w_down"], "nn", th, 512, 5632, F32, epilogue=_add_residual, extras=(x2,))
    dx3, dx3b, d_final_g, loss = _final_loss("final_loss", x3, sm["final_norm_g"], target, tr)

    dgt, dup = _mm("ffn_down_dgrad", [dx3b], [W["w_down"]], [(0, 0, 0)], "nt", tm, 512, 2048, [BF16, BF16],
                   epilogue=_silu_mul_bwd, extras=(gt, up))
    rs.push("w_down", _mm1("ffn_down_wgrad", act, dx3b, "tn", 1408, 1024, tk_s, BF16))
    rs.push("w_gate_t", _mm1("ffn_gate_wgrad", dgt, h3, "tn", 1408, 1024, tk_s, BF16))
    rs.reduce("w_down")
    rs.push("w_up_t", _mm1("ffn_up_wgrad", dup, h3, "tn", 1408, 1024, tk_s, BF16))
    rs.reduce("w_gate_t")
    dh3 = _mm("ffn_gate_up_dgrad", [dgt, dup], [W["w_gate_t"], W["w_up_t"]], [(0, 0, 0), (1, 1, 0)], "nn",
              th, 256, 5632, [F32])[0]
    rs.reduce("w_up_t")
    dx2, dx2b, d_ffn_g = _rms_bwd("rms_ffn_bwd", dh3, x2, sm["norm_ffn_g"], dx3, tr)
    rs.finish("w_down")

    rs.push("w_o", _mm1("proj_o_wgrad", o, dx2b, "tn", 1024, 1024, tk_s, BF16))
    rs.finish("w_gate_t")
    do = _mm1("proj_o_dgrad", dx2b, W["w_o"], "nt", tm, 1024, 2048, BF16)
    rs.reduce("w_o")
    dq, dk, dv = _attn_bwd("attn_bwd", q, kk, vv, do, ts)
    rs.push("w_q", _mm1("proj_q_wgrad", h2, dq, "tn", 1024, 1024, tk_s, BF16))
    rs.push("w_k", _mm1("proj_k_wgrad", mb, dk, "tn", 1024, 1024, M, BF16))
    rs.push("w_v", _mm1("proj_v_wgrad", mb, dv, "tn", 1024, 1024, M, BF16))
    rs.finish("w_up_t")
    dx1, dx1b, d_xattn_g = _mm_rows(
        "proj_q_dgrad", [dq], [W["w_q"]], [(0, 0, 0)], "nt", tq, 2048, [x1, dx2], [sm["norm_xattn_g"]],
        [F32, BF16], 1, _ep_norm_bwd)
    rs.reduce("w_q")
    rs.reduce("w_k")
    rs.reduce("w_v")
    dmb = _mm("proj_kv_dgrad", [dk, dv], [W["w_k"], W["w_v"]], [(0, 0, 0), (1, 1, 0)], "nt",
              M, 1024, 2048, [F32])[0]
    (d_mem_g,) = _rms_bwd("rms_mem_bwd", dmb, mem, sm["norm_mem_g"], None, M, want_dx=False)
    rs.finish("w_o")

    rs.push("w_out", _mm1("proj_out_wgrad", ymix, dx1b, "tn", 1024, 1024, tk_s, BF16))
    dymix = _mm1("proj_out_dgrad", dx1b, W["w_out"], "nt", tm, 1024, 2048, F32)
    rs.reduce("w_out")
    dproj, d_pool_w, d_pool_scale, d_sgu_g, d_ws, d_b = _mixer_bwd(
        "mixer_bwd", dymix, proj, p, vn, W["pool_w"], sm["pool_scale"], sm["sgu_norm_g"],
        sm["ws_masked"], sm["bias_full"], ts)
    rs.small("early", dict(
        pool_w=d_pool_w, pool_scale=d_pool_scale, sgu_norm_g=d_sgu_g, w_spatial=d_ws, b_spatial=d_b[:, :, 0],
        norm_xattn_g=d_xattn_g, norm_mem_g=d_mem_g, norm_ffn_g=d_ffn_g, final_norm_g=d_final_g))
    rs.push("w_in_t", _mm1("proj_in_wgrad", dproj, h1, "tn", 1024, 1024, tk_s, BF16))
    rs.finish("w_q")
    rs.finish("w_k")
    rs.finish("w_v")
    rs.reduce("w_in_t")
    grad_x, d_mix_g = _mm_rows(
        "proj_in_dgrad", [dproj], [W["w_in_t"]], [(0, 0, 0)], "nn", tq, 3072, [x, dx1], [sm["norm_mix_g"]],
        [F32], 1, _ep_norm_bwd)
    rs.small("late", dict(norm_mix_g=d_mix_g, loss=jnp.pad(loss, ((0, 0), (0, _LANES - 1)))))
    rs.finish("w_out")
    rs.finish_small("early")
    rs.finish("w_in_t")
    rs.finish_small("late")
    return loss, grad_x


def _mesh_pos():
    return lax.axis_index("x"), lax.axis_index("y"), lax.axis_index("c")


def _handshake(peers):
    barrier = pltpu.get_barrier_semaphore()
    for peer in peers:
        pl.semaphore_signal(barrier, inc=1, device_id=peer, device_id_type=MESH)
    pl.semaphore_wait(barrier, len(peers))


def _seq_all_gather(name, shards, collective_id):
    n = len(shards)

    def body(*refs):
        ins = refs[:n]
        outs = refs[n:2 * n]
        send_sems, recv_sems, local_sems = refs[2 * n:]
        x, y, c = _mesh_pos()
        me, sibling = (x, y, c), (x, y, 1 - c)
        xn, yn, dg = (1 - x, y), (x, 1 - y), (1 - x, 1 - y)
        north = c == 1
        via = (jnp.where(north, xn[0], yn[0]), jnp.where(north, xn[1], yn[1]))
        to = (jnp.where(north, yn[0], xn[0]), jnp.where(north, yn[1], xn[1]))
        _handshake([sibling, (*xn, c), (*yn, c)])

        def copy(a, k, block, target, src=None):
            bx, by, bc = block
            dst = outs[a].at[4 * bx + 2 * by + bc]
            return pltpu.make_async_remote_copy(
                src_ref=dst if src is None else src, dst_ref=dst,
                send_sem=send_sems.at[a, k], recv_sem=recv_sems.at[a, k],
                device_id=target, device_id_type=MESH)

        mine = [pltpu.make_async_copy(ins[a], outs[a].at[4 * x + 2 * y + c], local_sems.at[a]) for a in range(n)]
        for cp in mine:
            cp.start()
        started = []
        for a in range(n):
            first = [copy(a, 0, me, sibling, src=ins[a]), copy(a, 1, me, (*xn, c), src=ins[a]),
                     copy(a, 2, me, (*yn, c), src=ins[a])]
            for cp in first:
                cp.start()
            started += first
        for a in range(n):
            copy(a, 1, (*xn, c), me).wait_recv()
            copy(a, 2, (*yn, c), me).wait_recv()
            second = [copy(a, 3, (*via, c), (*to, c)), copy(a, 4, (*xn, c), sibling), copy(a, 5, (*yn, c), sibling)]
            for cp in second:
                cp.start()
            started += second
        for a in range(n):
            copy(a, 3, (*dg, c), me).wait_recv()
            last = copy(a, 6, (*dg, c), sibling)
            last.start()
            started.append(last)
        for a in range(n):
            copy(a, 0, sibling, me).wait_recv()
            for k, chip in ((4, xn), (5, yn), (6, dg)):
                copy(a, k, (*chip, 1 - c), me).wait_recv()
        for cp in started:
            cp.wait_send()
        for cp in mine:
            cp.wait()

    return _sc_call(
        body, name=name,
        out_type=[_sds((N_DEV,) + s.shape, s.dtype) for s in shards],
        scratch_types=[pltpu.SemaphoreType.DMA((n, 7)), pltpu.SemaphoreType.DMA((n, 7)),
                       pltpu.SemaphoreType.DMA((n,))],
        compiler_params=pltpu.CompilerParams(collective_id=collective_id),
    )(*shards)


def _seq_pair_exchange(name, gview, collective_id):
    def body(g_ref, theirs_ref, send_sems, recv_sems):
        x, y, c = _mesh_pos()
        sibling = (x, y, 1 - c)
        _handshake([sibling])
        copies = [pltpu.make_async_remote_copy(
            src_ref=g_ref.at[k, 1 - c], dst_ref=theirs_ref.at[k],
            send_sem=send_sems.at[k], recv_sem=recv_sems.at[k],
            device_id=sibling, device_id_type=MESH) for k in range(4)]
        for cp in copies:
            cp.start()
        for cp in copies:
            cp.wait()

    return _sc_call(
        body, name=name, out_type=_sds((4,) + gview.shape[2:], gview.dtype),
        scratch_types=[pltpu.SemaphoreType.DMA((4,)), pltpu.SemaphoreType.DMA((4,))],
        compiler_params=pltpu.CompilerParams(collective_id=collective_id),
    )(gview)


def _pair_sum(name, gview, theirs, pos, tr):
    _, _, r, C = gview.shape

    def body(pos_ref, a_ref, b_ref, o_ref):
        o_ref[...] = (a_ref[...].astype(F32) + b_ref[...].astype(F32)).astype(o_ref.dtype)

    grid_spec = pltpu.PrefetchScalarGridSpec(
        num_scalar_prefetch=1, grid=(4, r // tr),
        in_specs=[pl.BlockSpec((None, None, tr, C), lambda k, t, pos_ref: (k, pos_ref[0], t, 0)),
                  pl.BlockSpec((None, tr, C), lambda k, t, pos_ref: (k, t, 0))],
        out_specs=pl.BlockSpec((None, tr, C), lambda k, t, pos_ref: (k, t, 0)))
    return _tc_call(
        body, name=name, grid_spec=grid_spec, out_shape=_sds(theirs.shape, theirs.dtype),
        compiler_params=_cparams(("parallel", "parallel")),
    )(pos, gview, theirs)


def _seq_chip_exchange(name, pair, collective_id):
    def body(p_ref, land_ref, send_sems, recv_sems):
        x, y, c = _mesh_pos()
        my_chip = 2 * x + y
        chips = [(1 - x, y), (x, 1 - y), (1 - x, 1 - y)]
        _handshake([(cx, cy, c) for cx, cy in chips])
        copies = [pltpu.make_async_remote_copy(
            src_ref=p_ref.at[2 * cx + cy], dst_ref=land_ref.at[my_chip],
            send_sem=send_sems.at[j], recv_sem=recv_sems.at[j],
            device_id=(cx, cy, c), device_id_type=MESH) for j, (cx, cy) in enumerate(chips)]
        for cp in copies:
            cp.start()
        for cp in copies:
            cp.wait_send()
        for j, (cx, cy) in enumerate(chips):
            pltpu.make_async_remote_copy(
                src_ref=p_ref.at[my_chip], dst_ref=land_ref.at[2 * cx + cy],
                send_sem=send_sems.at[j], recv_sem=recv_sems.at[j],
                device_id=(cx, cy, c), device_id_type=MESH).wait_recv()

    return _sc_call(
        body, name=name, out_type=_sds(pair.shape, pair.dtype),
        scratch_types=[pltpu.SemaphoreType.DMA((3,)), pltpu.SemaphoreType.DMA((3,))],
        compiler_params=pltpu.CompilerParams(collective_id=collective_id),
    )(pair)


def _sum_leading(name, parts, tr, out_dtype=F32):
    n, r, C = parts.shape

    def body(p_ref, o_ref):
        acc = p_ref[0].astype(F32)
        for k in range(1, n):
            acc = acc + p_ref[k].astype(F32)
        o_ref[...] = acc.astype(o_ref.dtype)

    return _tc_call(
        body, name=name, grid=(r // tr,),
        in_specs=[pl.BlockSpec((n, tr, C), lambda t: (0, t, 0))],
        out_specs=pl.BlockSpec((tr, C), lambda t: (t, 0)),
        out_shape=_sds((r, C), out_dtype), compiler_params=_cparams(("parallel",)),
    )(parts)


def _row_tile(r):
    for t in (512, 384, 352, 256, 128, 64, 32, 16, 8):
        if r % t == 0:
            return t
    return r


def _adamw_math(w, g, m, v):
    c1 = 1.0 - ADAM_B1 ** ADAM_STEP
    c2 = 1.0 - ADAM_B2 ** ADAM_STEP
    nm = ADAM_B1 * m + (1.0 - ADAM_B1) * g
    nv = ADAM_B2 * v + (1.0 - ADAM_B2) * (g * g)
    m_hat = nm / c1
    v_hat = nv / c2
    return -ADAM_LR * (m_hat / (jnp.sqrt(v_hat) + ADAM_EPS) + ADAM_WD * w), nm, nv


def _chip_sum_adamw(name, pair, landed, pos, w, m, v, transposed):
    _, r, C = pair.shape
    if transposed:
        tr, tc = r, 512
        r_pad = -r % _LANES
        wspec = pl.BlockSpec((tc, r), lambda t, k, pos_ref: (t, 0))
        shape = (C, r)
        scratch = [pltpu.VMEM((tr, tc), F32), pltpu.VMEM((tc, r + r_pad), F32)]
    else:
        tr, tc = _row_tile(r), C
        wspec = pl.BlockSpec((tr, C), lambda t, k, pos_ref: (t, 0))
        shape = (r, C)
        scratch = [pltpu.VMEM((tr, tc), F32)]
    n_t = (C // tc) if transposed else (r // tr)

    def block(chip, t):
        return (chip, 0, t) if transposed else (chip, t, 0)

    def body(pos_ref, own_ref, land_ref, w_ref, m_ref, v_ref, g_ref, d_ref, nm_ref, nv_ref, acc_ref, *turn):
        k = pl.program_id(1)
        val = jnp.where(k == pos_ref[1], own_ref[...], land_ref[...]).astype(F32)

        @pl.when(k == 0)
        def _():
            acc_ref[...] = val

        @pl.when(k > 0)
        def _():
            acc_ref[...] += val

        @pl.when(k == 3)
        def _():
            if transposed:
                g_t = acc_ref[...]
                if r_pad:
                    g_t = jnp.concatenate([g_t, jnp.zeros((r_pad, tc), F32)], axis=0)
                turn[0][...] = g_t.T
                g = turn[0][:, 0:r]
            else:
                g = acc_ref[...]
            d, nm, nv = _adamw_math(w_ref[...], g, m_ref[...], v_ref[...])
            g_ref[...] = g
            d_ref[...] = d
            nm_ref[...] = nm
            nv_ref[...] = nv

    def land_index(t, k, pos_ref):
        return block(jnp.where(k == pos_ref[1], (k + 1) % 4, k), t)

    grid_spec = pltpu.PrefetchScalarGridSpec(
        num_scalar_prefetch=1, grid=(n_t, 4),
        in_specs=[pl.BlockSpec((None, tr, tc), lambda t, k, pos_ref: block(pos_ref[1], t)),
                  pl.BlockSpec((None, tr, tc), land_index), wspec, wspec, wspec],
        out_specs=[wspec] * 4, scratch_shapes=scratch)
    return _tc_call(
        body, name=name, grid_spec=grid_spec, out_shape=[_sds(shape, F32)] * 4,
        compiler_params=_cparams(("parallel", "arbitrary")),
    )(pos, pair, landed, w, m, v)


def _adamw(name, w, g, m, v):
    R, C = w.shape
    tr = _row_tile(R)

    def body(w_ref, g_ref, m_ref, v_ref, d_ref, nm_ref, nv_ref):
        d_ref[...], nm_ref[...], nv_ref[...] = _adamw_math(w_ref[...], g_ref[...], m_ref[...], v_ref[...])

    spec = pl.BlockSpec((tr, C), lambda i: (i, 0))
    return _tc_call(
        body, name=name, grid=(R // tr,), in_specs=[spec] * 4, out_specs=[spec] * 3,
        out_shape=[_sds((R, C), F32)] * 3, compiler_params=_cparams(("parallel",)),
    )(w, g, m, v)


_BIG = ("w_in_t", "w_out", "w_q", "w_k", "w_v", "w_o", "w_gate_t", "w_up_t", "w_down")
_SMALL = ("norm_mix_g", "pool_scale", "sgu_norm_g", "w_spatial", "b_spatial", "norm_xattn_g",
          "norm_mem_g", "norm_ffn_g", "final_norm_g")
_LANES = 128
_GATHER_GROUPS = (("w_in_t", "pool_w"), ("w_out",), ("w_q",), ("w_k", "w_v"), ("w_o",), ("w_gate_t",),
                  ("w_up_t",), ("w_down",))
_RS_ORDER = ("w_down", "w_gate_t", "w_up_t", "w_o", "w_q", "w_k", "w_v", "w_out", "w_in_t")
_SMALL_GROUPS = dict(
    early=("pool_w", "pool_scale", "sgu_norm_g", "w_spatial", "b_spatial", "norm_xattn_g", "norm_mem_g",
           "norm_ffn_g", "final_norm_g"),
    late=("norm_mix_g", "loss"))
_ID_GATHER, _ID_PAIR, _ID_CHIP = 0, 1, 2


_PACK_ROWS = 512


def _pack(parts):
    rows = [p.reshape(-1, _LANES) for p in parts]
    n = sum(r.shape[0] for r in rows)
    pad = -n % (_PACK_ROWS if n > _PACK_ROWS else 8)
    if pad:
        rows.append(jnp.zeros((pad, _LANES), rows[0].dtype))
    return jnp.concatenate(rows, axis=0)


class _GradReducer:
    def __init__(self, pos, apply, apply_small):
        self.pos, self.apply, self.apply_small = pos, apply, apply_small
        self.view, self.theirs, self.pair, self.landed = {}, {}, {}, {}
        self.small_gathered = {}

    def push(self, k, g):
        r = g.shape[0] // N_DEV
        self.view[k] = g.reshape(4, 2, r, g.shape[1])
        self.theirs[k] = _seq_pair_exchange("grad_pair_exchange_" + k, self.view[k], _ID_PAIR)

    def reduce(self, k):
        r = self.view[k].shape[2]
        self.pair[k] = _pair_sum("grad_pair_sum_" + k, self.view[k], self.theirs[k], self.pos, _row_tile(r))
        self.landed[k] = _seq_chip_exchange("grad_chip_exchange_" + k, self.pair[k], _ID_CHIP)

    def finish(self, k):
        self.apply(k, self.pair[k], self.landed[k])

    def small(self, tag, parts):
        packed = _pack([parts[k] for k in _SMALL_GROUPS[tag]])
        (self.small_gathered[tag],) = _seq_all_gather("gather_small_grads_" + tag, [packed], _ID_GATHER)

    def finish_small(self, tag):
        allp = self.small_gathered[tag]
        self.apply_small(tag, _sum_leading("sum_small_grads_" + tag, allp, min(_PACK_ROWS, allp.shape[1])))


def _unpack(packed, like):
    out, row = [], 0
    for ref in like:
        rows = ref.size // _LANES
        out.append(packed[row:row + rows].reshape(ref.shape))
        row += rows
    return out


def kernel(x, mem, norm_mix_g, w_in, pool_w, pool_scale, sgu_norm_g, w_spatial, b_spatial, w_out, norm_xattn_g, norm_mem_g, w_q, w_k, w_v, w_o, norm_ffn_g, w_gate, w_up, w_down, final_norm_g, loss_target, m_norm_mix_g, m_w_in, m_pool_w, m_pool_scale, m_sgu_norm_g, m_w_spatial, m_b_spatial, m_w_out, m_norm_xattn_g, m_norm_mem_g, m_w_q, m_w_k, m_w_v, m_w_o, m_norm_ffn_g, m_w_gate, m_w_up, m_w_down, m_final_norm_g, v_norm_mix_g, v_w_in, v_pool_w, v_pool_scale, v_sgu_norm_g, v_w_spatial, v_b_spatial, v_w_out, v_norm_xattn_g, v_norm_mem_g, v_w_q, v_w_k, v_w_v, v_w_o, v_norm_ffn_g, v_w_gate, v_w_up, v_w_down, v_final_norm_g):
    args = dict(locals())
    names = ("norm_mix_g", "w_in", "pool_w", "pool_scale", "sgu_norm_g", "w_spatial", "b_spatial", "w_out",
             "norm_xattn_g", "norm_mem_g", "w_q", "w_k", "w_v", "w_o", "norm_ffn_g", "w_gate", "w_up",
             "w_down", "final_norm_g")
    w = {k: args[k] for k in names}
    m = {k: args["m_" + k] for k in names}
    v = {k: args["v_" + k] for k in names}
    my_dev = 4 * lax.axis_index("x") + 2 * lax.axis_index("y") + lax.axis_index("c")
    _CHAIN.__init__()

    shards = dict(
        w_in_t=w["w_in"][0].T, w_out=w["w_out"][0], w_q=w["w_q"][0], w_k=w["w_k"][0], w_v=w["w_v"][0],
        w_o=w["w_o"][0], w_gate_t=w["w_gate"][0].T, w_up_t=w["w_up"][0].T, w_down=w["w_down"][0])
    send = {k: shards[k].astype(BF16) for k in _BIG}
    send["pool_w"] = w["pool_w"][0].reshape(4 * 32, POOL_GROUP).astype(BF16)
    W = {}
    for gi, group in enumerate(_GATHER_GROUPS):
        gathered = _seq_all_gather("gather_weights_%d" % gi, [send[k] for k in group], _ID_GATHER)
        for k, g in zip(group, gathered):
            W[k] = g.reshape(-1, g.shape[-1])
    W["pool_w"] = W["pool_w"].reshape(N_DEV, 4, 32, POOL_GROUP).transpose(1, 0, 2, 3).reshape(4, POOL_GROUP, POOL_GROUP)

    t = jnp.arange(SGU_BLOCK)
    mask = (t[None, :] // SGU_CHUNK) <= (t[:, None] // SGU_CHUNK)
    sm = dict(
        norm_mix_g=w["norm_mix_g"], pool_scale=w["pool_scale"], sgu_norm_g=w["sgu_norm_g"],
        norm_xattn_g=w["norm_xattn_g"], norm_mem_g=w["norm_mem_g"], norm_ffn_g=w["norm_ffn_g"],
        final_norm_g=w["final_norm_g"].reshape(1, D_MODEL),
        ws_masked=jnp.where(mask[None], w["w_spatial"][0], 0.0).astype(BF16),
        bias_full=jnp.repeat(w["b_spatial"][0].T, SGU_BLOCK, axis=1))

    natural = dict(w_in_t="w_in", w_gate_t="w_gate", w_up_t="w_up")
    grads, delta, new_m, new_v = {}, {}, {}, {}

    def adamw(k):
        shp = w[k].shape
        two_d = (-1, shp[-1])
        d_, m_, v_ = _adamw("adamw_" + k, w[k].reshape(two_d), grads[k].reshape(two_d), m[k].reshape(two_d),
                            v[k].reshape(two_d))
        delta[k], new_m[k], new_v[k] = d_.reshape(shp), m_.reshape(shp), v_.reshape(shp)

    def apply(k, pair, landed):
        name = natural.get(k, k)
        res = _chip_sum_adamw("grad_finish_" + k, pair, landed, pos, w[name][0], m[name][0], v[name][0],
                              k in natural)
        grads[name], delta[name], new_m[name], new_v[name] = (a[None] for a in res)

    like = dict(w)
    like["pool_w"] = _sds((4, POOL_GROUP, POOL_GROUP), F32)
    like["loss"] = _sds((1, _LANES), F32)

    def apply_small(tag, total):
        group = _SMALL_GROUPS[tag]
        grads.update(zip(group, _unpack(total, [like[k] for k in group])))
        if tag == "early":
            grads["pool_w"] = lax.dynamic_slice_in_dim(grads["pool_w"], my_dev * 32, 32, axis=1)[None]
            adamw("pool_w")
        else:
            d_, m_, v_ = _adamw("adamw_small", _pack([w[k] for k in _SMALL]), _pack([grads[k] for k in _SMALL]),
                                _pack([m[k] for k in _SMALL]), _pack([v[k] for k in _SMALL]))
            shapes = [w[k] for k in _SMALL]
            for k, a, b, c_ in zip(_SMALL, _unpack(d_, shapes), _unpack(m_, shapes), _unpack(v_, shapes)):
                delta[k], new_m[k], new_v[k] = a, b, c_

    pos = jnp.stack([lax.axis_index("c"), 2 * lax.axis_index("x") + lax.axis_index("y")]).astype(jnp.int32)
    rs = _GradReducer(pos, apply, apply_small)
    _, grad_x = _local_step(x[0], mem[0], loss_target[0], W, sm, rs)

    outs = [grads["loss"][0, 0], grad_x[None]]
    outs += [grads[k].reshape(w[k].shape) for k in names]
    outs += [delta[k] for k in names]
    outs += [new_m[k] for k in names]
    outs += [new_v[k] for k in names]
    return tuple(outs)
```

```python
import functools

import jax
import jax.numpy as jnp
from jax import lax
from jax.experimental import pallas as pl
from jax.experimental.pallas import tpu as pltpu
from jax.experimental.pallas import tpu_sc as plsc

F32 = jnp.float32
BF16 = jnp.bfloat16
MESH = pl.DeviceIdType.MESH

EPS = 1e-6
D_MODEL = 2048
D_POOL = 1024
D_SGU = 1024
POOL_WINDOWS = (2, 4, 8, 16)
POOL_GROUP = 256
POOL_HALO = 16
SGU_BLOCK = 128
SGU_CHUNK = 64
N_SGU_HEADS = 8
N_HEADS = 4
HEAD_DIM = 512
N_DEV = 8

ADAM_LR = 0.001
ADAM_B1 = 0.9
ADAM_B2 = 0.999
ADAM_EPS = 1e-08
ADAM_WD = 0.01
ADAM_STEP = 10

VMEM_LIMIT = 56 * 1024 * 1024


def _cparams(sem=None):
    return pltpu.CompilerParams(dimension_semantics=sem, vmem_limit_bytes=VMEM_LIMIT)


def _sds(shape, dtype):
    return jax.ShapeDtypeStruct(shape, dtype)


_ANY = pl.BlockSpec(memory_space=pl.ANY)


class _Chain:
    def __init__(self):
        self.tc = None
        self.sc = None


_CHAIN = _Chain()


def _first(out):
    return out[0] if isinstance(out, (list, tuple)) else out


def _tc_call(body, *, in_specs=None, grid_spec=None, **kw):
    def run(*args):
        prev, n = _CHAIN.tc, len(args)
        fn, specs, spec, operands = body, in_specs, grid_spec, args
        if prev is not None:
            def fn(*refs):
                return body(*refs[:n], *refs[n + 1:])
            operands = args + (prev,)
            if grid_spec is None:
                specs = list(in_specs) + [_ANY]
            else:
                spec = pltpu.PrefetchScalarGridSpec(
                    num_scalar_prefetch=grid_spec.num_scalar_prefetch, grid=grid_spec.grid,
                    in_specs=list(grid_spec.in_specs) + [_ANY], out_specs=grid_spec.out_specs,
                    scratch_shapes=grid_spec.scratch_shapes)
        if spec is None:
            out = pl.pallas_call(fn, in_specs=specs, **kw)(*operands)
        else:
            out = pl.pallas_call(fn, grid_spec=spec, **kw)(*operands)
        _CHAIN.tc = _first(out)
        return out
    return run


def _sc_call(body, **kw):
    return pl.kernel(body, mesh=plsc.ScalarSubcoreMesh(axis_name="seq", num_cores=1), **kw)


def _rowsum8(v):
    r, c = v.shape
    return v.reshape(r // 8, 8, c).sum(axis=0)


_EPILOGUE_COLS = 256
_DN = {
    "nn": (((1,), (0,)), ((), ())),
    "nt": (((1,), (1,)), ((), ())),
    "tn": (((0,), (0,)), ((), ())),
}


def _mm(name, a_list, b_list, terms, mode, tm, tn, tk, out_dtypes, epilogue=None, extras=(), n_acc=1):
    a0, b0 = a_list[0], b_list[0]
    if mode == "tn":
        K, M = a0.shape
    else:
        M, K = a0.shape
    N = b0.shape[0] if mode == "nt" else b0.shape[1]
    assert M % tm == 0 and N % tn == 0 and K % tk == 0, (name, M, N, K, tm, tn, tk)
    nk = K // tk
    na, nb, ne, no = len(a_list), len(b_list), len(extras), len(out_dtypes)
    dn = _DN[mode]

    if mode == "tn":
        a_spec = pl.BlockSpec((tk, tm), lambda i, j, k: (k, i))
    else:
        a_spec = pl.BlockSpec((tm, tk), lambda i, j, k: (i, k))
    if mode == "nt":
        b_spec = pl.BlockSpec((tn, tk), lambda i, j, k: (j, k))
    else:
        b_spec = pl.BlockSpec((tk, tn), lambda i, j, k: (k, j))
    o_spec = pl.BlockSpec((tm, tn), lambda i, j, k: (i, j))

    def body(*refs):
        a_refs = refs[:na]
        b_refs = refs[na:na + nb]
        e_refs = refs[na + nb:na + nb + ne]
        o_refs = refs[na + nb + ne:na + nb + ne + no]
        acc_refs = refs[na + nb + ne + no:]

        def products(cols):
            parts = [None] * n_acc
            for ai, bi, ci in terms:
                b = b_refs[bi][cols, :] if mode == "nt" else b_refs[bi][:, cols]
                d = lax.dot_general(a_refs[ai][...].astype(BF16), b.astype(BF16), dn, preferred_element_type=F32)
                parts[ci] = d if parts[ci] is None else parts[ci] + d
            return parts

        def finish(accs, cols=slice(None)):
            outs = epilogue(accs, [e[:, cols] for e in e_refs]) if epilogue is not None else accs
            for o_ref, v in zip(o_refs, outs):
                o_ref[:, cols] = v.astype(o_ref.dtype)

        if nk == 1 and epilogue is not None and tn > _EPILOGUE_COLS:
            for c0 in range(0, tn, _EPILOGUE_COLS):
                cols = slice(c0, c0 + _EPILOGUE_COLS)
                finish(products(cols), cols)
            return
        parts = products(slice(None))
        if nk == 1:
            finish(parts)
        else:
            k = pl.program_id(2)

            @pl.when(k == 0)
            def _():
                for c in range(n_acc):
                    acc_refs[c][...] = parts[c]

            @pl.when(k > 0)
            def _():
                for c in range(n_acc):
                    acc_refs[c][...] += parts[c]

            @pl.when(k == nk - 1)
            def _():
                finish([acc_refs[c][...] for c in range(n_acc)])

    scratch = [pltpu.VMEM((tm, tn), F32) for _ in range(n_acc)] if nk > 1 else []
    res = _tc_call(
        body, name=name, grid=(M // tm, N // tn, nk),
        in_specs=[a_spec] * na + [b_spec] * nb + [o_spec] * ne,
        out_specs=[o_spec] * no,
        out_shape=[_sds((M, N), dt) for dt in out_dtypes],
        scratch_shapes=scratch,
        compiler_params=_cparams(("parallel", "parallel", "arbitrary")),
    )(*a_list, *b_list, *extras)
    return res


def _mm_rows(name, a_list, b_list, terms, mode, tm, tk, rows, vecs, row_dtypes, n_vec_out, epilogue,
             n_scalar_out=0):
    a0, b0 = a_list[0], b_list[0]
    M, K = a0.shape
    N = b0.shape[0] if mode == "nt" else b0.shape[1]
    assert mode in ("nn", "nt") and M % tm == 0 and K % tk == 0, (name, M, N, K, tm, tk)
    nm, nk = M // tm, K // tk
    slab = min(128, tm)
    na, nb, nr, nv, no = len(a_list), len(b_list), len(rows), len(vecs), len(row_dtypes)
    dn = _DN[mode]
    a_spec = pl.BlockSpec((tm, tk), lambda i, k: (i, k))
    b_spec = pl.BlockSpec((N, tk), lambda i, k: (0, k)) if mode == "nt" else pl.BlockSpec((tk, N), lambda i, k: (k, 0))
    row_spec = pl.BlockSpec((tm, N), lambda i, k: (i, 0))
    vec_spec = pl.BlockSpec((1, N), lambda i, k: (0, 0))
    one_spec = pl.BlockSpec((1, 1), lambda i, k: (0, 0))

    def body(*refs):
        pos = 0
        a_refs = refs[pos:pos + na]; pos += na
        b_refs = refs[pos:pos + nb]; pos += nb
        r_refs = refs[pos:pos + nr]; pos += nr
        v_refs = refs[pos:pos + nv]; pos += nv
        o_refs = refs[pos:pos + no]; pos += no
        s_refs = refs[pos:pos + n_vec_out]; pos += n_vec_out
        vacc_refs = refs[pos:pos + n_vec_out]; pos += n_vec_out
        acc_ref = refs[pos] if nk > 1 else None
        i, k = pl.program_id(0), pl.program_id(1)
        part = None
        for ai, bi, _ in terms:
            d = lax.dot_general(a_refs[ai][...].astype(BF16), b_refs[bi][...].astype(BF16), dn,
                                preferred_element_type=F32)
            part = d if part is None else part + d

        def finish(acc):
            vecs_now = [v[...] for v in v_refs]
            vparts = None
            for r0 in range(0, tm, slab):
                rs_ = slice(r0, r0 + slab)
                outs, vp = epilogue(acc[rs_, :], [r[rs_, :] for r in r_refs], vecs_now)
                for o_ref, val in zip(o_refs, outs):
                    o_ref[rs_, :] = val.astype(o_ref.dtype)
                vparts = vp if vparts is None else [a + b for a, b in zip(vparts, vp)]

            @pl.when(i == 0)
            def _():
                for vacc, vp in zip(vacc_refs, vparts):
                    vacc[...] = vp

            @pl.when(i > 0)
            def _():
                for vacc, vp in zip(vacc_refs, vparts):
                    vacc[...] += vp

            @pl.when(i == nm - 1)
            def _():
                for j, (s_ref, vacc) in enumerate(zip(s_refs, vacc_refs)):
                    col = jnp.sum(vacc[...], axis=0, keepdims=True)
                    s_ref[...] = jnp.sum(col, axis=1, keepdims=True) if j >= n_vec_out - n_scalar_out else col

        if nk == 1:
            finish(part)
        else:
            @pl.when(k == 0)
            def _():
                acc_ref[...] = part

            @pl.when(k > 0)
            def _():
                acc_ref[...] += part

            @pl.when(k == nk - 1)
            def _():
                finish(acc_ref)

    n_plain = n_vec_out - n_scalar_out
    return _tc_call(
        body, name=name, grid=(nm, nk),
        in_specs=[a_spec] * na + [b_spec] * nb + [row_spec] * nr + [vec_spec] * nv,
        out_specs=[row_spec] * no + [vec_spec] * n_plain + [one_spec] * n_scalar_out,
        out_shape=[_sds((M, N), dt) for dt in row_dtypes] + [_sds((1, N), F32)] * n_plain
        + [_sds((1, 1), F32)] * n_scalar_out,
        scratch_shapes=[pltpu.VMEM((8, N), F32)] * n_vec_out + ([pltpu.VMEM((tm, N), F32)] if nk > 1 else []),
        compiler_params=_cparams(("arbitrary", "arbitrary")),
    )(*a_list, *b_list, *rows, *vecs)


def _ep_residual_norm(acc, rows, vecs):
    x_new = rows[0] + acc
    r = lax.rsqrt(jnp.mean(x_new * x_new, axis=-1, keepdims=True) + EPS)
    return [x_new, x_new * r * vecs[0]], []


def _ep_norm_bwd(acc, rows, vecs):
    xv, dres = rows
    r = lax.rsqrt(jnp.mean(xv * xv, axis=-1, keepdims=True) + EPS)
    xh = xv * r
    dxh = acc * vecs[0]
    m = jnp.mean(dxh * xh, axis=-1, keepdims=True)
    dx = dres + r * (dxh - xh * m)
    return [dx, dx], [_rowsum8(acc * xh)]


def _ep_final_loss(acc, rows, vecs):
    x2, target = rows
    gv = vecs[0]
    xv = x2 + acc
    inv_d = 1.0 / xv.shape[-1]
    r = lax.rsqrt(jnp.mean(xv * xv, axis=-1, keepdims=True) + EPS)
    xh = xv * r
    e = xh * gv - target
    dy = e * inv_d
    dxh = dy * gv
    m = jnp.mean(dxh * xh, axis=-1, keepdims=True)
    dx = r * (dxh - xh * m)
    return [dx, dx], [_rowsum8(dy * xh), _rowsum8(e * e) * (0.5 * inv_d)]


def _mm1(name, a, b, mode, tm, tn, tk, out_dtype, **kw):
    return _mm(name, [a], [b], [(0, 0, 0)], mode, tm, tn, tk, [out_dtype], **kw)[0]


def _rms_fwd(name, x, g, tr):
    S, Dm = x.shape

    def body(x_ref, g_ref, h_ref):
        xv = x_ref[...]
        r = lax.rsqrt(jnp.mean(xv * xv, axis=-1, keepdims=True) + EPS)
        h_ref[...] = (xv * r * g_ref[...]).astype(h_ref.dtype)

    return _tc_call(
        body, name=name, grid=(S // tr,),
        in_specs=[pl.BlockSpec((tr, Dm), lambda i: (i, 0)), pl.BlockSpec((1, Dm), lambda i: (0, 0))],
        out_specs=pl.BlockSpec((tr, Dm), lambda i: (i, 0)),
        out_shape=_sds((S, Dm), BF16),
        compiler_params=_cparams(("parallel",)),
    )(x, g)


def _rms_bwd(name, dh, x, g, dres, tr, want_dx=True):
    S, Dm = x.shape
    nsteps = S // tr

    def body(*refs):
        if want_dx:
            dh_ref, x_ref, g_ref, dres_ref, dx_ref, dxb_ref, dg_ref, acc_ref = refs
        else:
            dh_ref, x_ref, g_ref, dg_ref, acc_ref = refs
        i = pl.program_id(0)
        xv = x_ref[...]
        r = lax.rsqrt(jnp.mean(xv * xv, axis=-1, keepdims=True) + EPS)
        xh = xv * r
        dhv = dh_ref[...]
        part = _rowsum8(dhv * xh)

        @pl.when(i == 0)
        def _():
            acc_ref[...] = part

        @pl.when(i > 0)
        def _():
            acc_ref[...] += part

        @pl.when(i == nsteps - 1)
        def _():
            dg_ref[...] = jnp.sum(acc_ref[...], axis=0, keepdims=True)

        if want_dx:
            dxh = dhv * g_ref[...]
            m = jnp.mean(dxh * xh, axis=-1, keepdims=True)
            dx = dres_ref[...] + r * (dxh - xh * m)
            dx_ref[...] = dx
            dxb_ref[...] = dx.astype(BF16)

    row = pl.BlockSpec((tr, Dm), lambda i: (i, 0))
    vec = pl.BlockSpec((1, Dm), lambda i: (0, 0))
    if want_dx:
        in_specs = [row, row, vec, row]
        out_specs = [row, row, vec]
        out_shape = [_sds((S, Dm), F32), _sds((S, Dm), BF16), _sds((1, Dm), F32)]
        args = (dh, x, g, dres)
    else:
        in_specs = [row, row, vec]
        out_specs = [vec]
        out_shape = [_sds((1, Dm), F32)]
        args = (dh, x, g)
    return _tc_call(
        body, name=name, grid=(nsteps,), in_specs=in_specs, out_specs=out_specs, out_shape=out_shape,
        scratch_shapes=[pltpu.VMEM((8, Dm), F32)],
        compiler_params=_cparams(("arbitrary",)),
    )(*args)


def _final_loss(name, x3, g, target, tr):
    S, Dm = x3.shape
    nsteps = S // tr

    def body(x_ref, g_ref, t_ref, dx_ref, dxb_ref, dg_ref, loss_ref, acc_g, acc_l):
        i = pl.program_id(0)
        xv = x_ref[...]
        gv = g_ref[...]
        r = lax.rsqrt(jnp.mean(xv * xv, axis=-1, keepdims=True) + EPS)
        xh = xv * r
        e = xh * gv - t_ref[...]
        dy = e * (1.0 / Dm)
        lpart = _rowsum8(e * e)
        gpart = _rowsum8(dy * xh)

        @pl.when(i == 0)
        def _():
            acc_g[...] = gpart
            acc_l[...] = lpart

        @pl.when(i > 0)
        def _():
            acc_g[...] += gpart
            acc_l[...] += lpart

        @pl.when(i == nsteps - 1)
        def _():
            dg_ref[...] = jnp.sum(acc_g[...], axis=0, keepdims=True)
            tot = jnp.sum(jnp.sum(acc_l[...], axis=1, keepdims=True), axis=0, keepdims=True)
            loss_ref[...] = tot * (0.5 / Dm)

        dxh = dy * gv
        m = jnp.mean(dxh * xh, axis=-1, keepdims=True)
        dx = r * (dxh - xh * m)
        dx_ref[...] = dx
        dxb_ref[...] = dx.astype(BF16)

    row = pl.BlockSpec((tr, Dm), lambda i: (i, 0))
    vec = pl.BlockSpec((1, Dm), lambda i: (0, 0))
    return _tc_call(
        body, name=name, grid=(nsteps,),
        in_specs=[row, vec, row],
        out_specs=[row, row, vec, pl.BlockSpec((1, 1), lambda i: (0, 0))],
        out_shape=[_sds((S, Dm), F32), _sds((S, Dm), BF16), _sds((1, Dm), F32), _sds((1, 1), F32)],
        scratch_shapes=[pltpu.VMEM((8, Dm), F32), pltpu.VMEM((8, Dm), F32)],
        compiler_params=_cparams(("arbitrary",)),
    )(x3, g, target)


def _softmax_rows(s):
    e = jnp.exp(s - jnp.max(s, axis=-1, keepdims=True))
    return e / jnp.sum(e, axis=-1, keepdims=True)


def _attn_fwd(name, q, k, v, ts):
    S, Dm = q.shape
    M = k.shape[0]
    scale = HEAD_DIM ** -0.5

    def body(q_ref, k_ref, v_ref, o_ref):
        for h in range(N_HEADS):
            sl = slice(h * HEAD_DIM, (h + 1) * HEAD_DIM)
            s = lax.dot_general(q_ref[:, sl], k_ref[:, sl], _DN["nt"], preferred_element_type=F32) * scale
            p = _softmax_rows(s)
            o_ref[:, sl] = jnp.dot(p.astype(BF16), v_ref[:, sl], preferred_element_type=F32).astype(o_ref.dtype)

    row = pl.BlockSpec((ts, Dm), lambda i: (i, 0))
    mem = pl.BlockSpec((M, Dm), lambda i: (0, 0))
    return _tc_call(
        body, name=name, grid=(S // ts,), in_specs=[row, mem, mem], out_specs=row,
        out_shape=_sds((S, Dm), BF16), compiler_params=_cparams(("parallel",)),
    )(q, k, v)


def _attn_bwd(name, q, k, v, do, ts):
    S, Dm = q.shape
    M = k.shape[0]
    scale = HEAD_DIM ** -0.5

    def body(q_ref, k_ref, v_ref, do_ref, dq_ref, dk_ref, dv_ref):
        i = pl.program_id(0)

        @pl.when(i == 0)
        def _():
            dk_ref[...] = jnp.zeros_like(dk_ref)
            dv_ref[...] = jnp.zeros_like(dv_ref)

        for h in range(N_HEADS):
            sl = slice(h * HEAD_DIM, (h + 1) * HEAD_DIM)
            qh = q_ref[:, sl]
            kh = k_ref[:, sl]
            doh = do_ref[:, sl]
            s = lax.dot_general(qh, kh, _DN["nt"], preferred_element_type=F32) * scale
            p = _softmax_rows(s)
            dp = lax.dot_general(doh, v_ref[:, sl], _DN["nt"], preferred_element_type=F32)
            ds = p * (dp - jnp.sum(dp * p, axis=-1, keepdims=True)) * scale
            dsb = ds.astype(BF16)
            dq_ref[:, sl] = jnp.dot(dsb, kh, preferred_element_type=F32).astype(dq_ref.dtype)
            dk_ref[:, sl] += lax.dot_general(dsb, qh, _DN["tn"], preferred_element_type=F32)
            dv_ref[:, sl] += lax.dot_general(p.astype(BF16), doh, _DN["tn"], preferred_element_type=F32)

    row = pl.BlockSpec((ts, Dm), lambda i: (i, 0))
    mem = pl.BlockSpec((M, Dm), lambda i: (0, 0))
    return _tc_call(
        body, name=name, grid=(S // ts,), in_specs=[row, mem, mem, row], out_specs=[row, mem, mem],
        out_shape=[_sds((S, Dm), BF16), _sds((M, Dm), F32), _sds((M, Dm), F32)],
        compiler_params=_cparams(("arbitrary",)),
    )(q, k, v, do)


def _pool_denominators(row0, ts):
    return (row0 + lax.broadcasted_iota(jnp.int32, (ts, 1), 0) + 1).astype(F32)


def _mixer_fwd(name, proj, pool_w, pool_scale, sgu_g, ws, bias_full, ts):
    S = proj.shape[0]
    nblk = ts // SGU_BLOCK
    halo_blocks = ts // POOL_HALO

    def body(proj_ref, halo_ref, pw_ref, sc_ref, g_ref, ws_ref, b_ref, y_ref, p_ref, vn_ref, ext_ref):
        i = pl.program_id(0)
        a = proj_ref[:, 0:D_POOL]
        ext_ref[0:POOL_HALO, :] = jnp.where(i > 0, halo_ref[...], 0.0)
        ext_ref[POOL_HALO:POOL_HALO + ts, :] = a
        pos = _pool_denominators(i * ts, ts)
        for gi, w in enumerate(POOL_WINDOWS):
            cs = slice(gi * POOL_GROUP, (gi + 1) * POOL_GROUP)
            acc = a[:, cs]
            for j in range(1, w):
                acc = acc + ext_ref[POOL_HALO - j:POOL_HALO - j + ts, cs]
            pg = (acc / jnp.minimum(pos, float(w)) - a[:, cs]).astype(BF16)
            p_ref[:, cs] = pg
            ypre = jnp.dot(pg, pw_ref[gi], preferred_element_type=F32)
            y_ref[:, cs] = (ypre * sc_ref[:, cs]).astype(y_ref.dtype)

        v = proj_ref[:, D_POOL + D_SGU:D_POOL + 2 * D_SGU]
        r = lax.rsqrt(jnp.mean(v * v, axis=-1, keepdims=True) + EPS)
        vn_ref[...] = (v * r * g_ref[...]).astype(BF16)
        for n in range(nblk):
            rs = slice(n * SGU_BLOCK, (n + 1) * SGU_BLOCK)
            for h in range(N_SGU_HEADS):
                cs = slice(h * SGU_BLOCK, (h + 1) * SGU_BLOCK)
                mixed = jnp.dot(ws_ref[h], vn_ref[rs, cs], preferred_element_type=F32) + b_ref[:, cs]
                u = proj_ref[rs, D_POOL + h * SGU_BLOCK:D_POOL + (h + 1) * SGU_BLOCK]
                y_ref[rs, D_POOL + h * SGU_BLOCK:D_POOL + (h + 1) * SGU_BLOCK] = (u * mixed).astype(y_ref.dtype)

    return _tc_call(
        body, name=name, grid=(S // ts,),
        in_specs=[
            pl.BlockSpec((ts, D_POOL + 2 * D_SGU), lambda i: (i, 0)),
            pl.BlockSpec((POOL_HALO, D_POOL), lambda i: (jnp.maximum(i * halo_blocks - 1, 0), 0)),
            pl.BlockSpec((4, POOL_GROUP, POOL_GROUP), lambda i: (0, 0, 0)),
            pl.BlockSpec((1, D_POOL), lambda i: (0, 0)),
            pl.BlockSpec((1, D_SGU), lambda i: (0, 0)),
            pl.BlockSpec((N_SGU_HEADS, SGU_BLOCK, SGU_BLOCK), lambda i: (0, 0, 0)),
            pl.BlockSpec((SGU_BLOCK, D_SGU), lambda i: (0, 0)),
        ],
        out_specs=[
            pl.BlockSpec((ts, D_MODEL), lambda i: (i, 0)),
            pl.BlockSpec((ts, D_POOL), lambda i: (i, 0)),
            pl.BlockSpec((ts, D_SGU), lambda i: (i, 0)),
        ],
        out_shape=[_sds((S, D_MODEL), BF16), _sds((S, D_POOL), BF16), _sds((S, D_SGU), BF16)],
        scratch_shapes=[pltpu.VMEM((ts + POOL_HALO, D_POOL), F32)],
        compiler_params=_cparams(("parallel",)),
    )(proj, proj, pool_w, pool_scale, sgu_g, ws, bias_full)


def _mixer_bwd(name, dymix, proj, p, vn, pool_w, pool_scale, sgu_g, ws, bias_full, ts):
    S = proj.shape[0]
    nsteps = S // ts
    nblk = ts // SGU_BLOCK
    halo_blocks = ts // POOL_HALO

    def body(dy_ref, dyh_ref, u_ref, v_ref, p_ref, vn_ref, pw_ref, sc_ref, g_ref, ws_ref, b_ref,
             dproj_ref, dpw_ref, dsc_ref, dg_ref, dws_ref, db_ref,
             ext_ref, dvn_ref, acc_sc, acc_g, acc_b):
        i = pl.program_id(0)

        @pl.when(i == 0)
        def _():
            dpw_ref[...] = jnp.zeros_like(dpw_ref)
            dws_ref[...] = jnp.zeros_like(dws_ref)
            acc_sc[...] = jnp.zeros_like(acc_sc)
            acc_g[...] = jnp.zeros_like(acc_g)
            acc_b[...] = jnp.zeros_like(acc_b)

        pos = _pool_denominators(i * ts, ts)
        pos_h = _pool_denominators((i + 1) * ts, POOL_HALO)
        for gi, w in enumerate(POOL_WINDOWS):
            cs = slice(gi * POOL_GROUP, (gi + 1) * POOL_GROUP)
            pg = p_ref[:, cs]
            wg = pw_ref[gi]
            dyp = dy_ref[:, cs]
            ypre = jnp.dot(pg, wg, preferred_element_type=F32)
            acc_sc[:, cs] += _rowsum8(dyp * ypre)
            dz = (dyp * sc_ref[:, cs]).astype(BF16)
            dpw_ref[gi] += lax.dot_general(pg, dz, _DN["tn"], preferred_element_type=F32)
            dp = lax.dot_general(dz, wg, _DN["nt"], preferred_element_type=F32)
            dzh = (dyh_ref[:, cs] * sc_ref[:, cs]).astype(BF16)
            dph = lax.dot_general(dzh, wg, _DN["nt"], preferred_element_type=F32)
            ext_ref[0:ts, cs] = dp / jnp.minimum(pos, float(w))
            ext_ref[ts:ts + POOL_HALO, cs] = jnp.where(i < nsteps - 1, dph / jnp.minimum(pos_h, float(w)), 0.0)
            acc = ext_ref[0:ts, cs]
            for j in range(1, w):
                acc = acc + ext_ref[j:j + ts, cs]
            dproj_ref[:, cs] = (acc - dp).astype(dproj_ref.dtype)

        for n in range(nblk):
            rs = slice(n * SGU_BLOCK, (n + 1) * SGU_BLOCK)
            for h in range(N_SGU_HEADS):
                cs = slice(h * SGU_BLOCK, (h + 1) * SGU_BLOCK)
                vnb = vn_ref[rs, cs]
                wh = ws_ref[h]
                mixed = jnp.dot(wh, vnb, preferred_element_type=F32) + b_ref[:, cs]
                dys = dy_ref[rs, D_POOL + h * SGU_BLOCK:D_POOL + (h + 1) * SGU_BLOCK]
                dproj_ref[rs, D_POOL + h * SGU_BLOCK:D_POOL + (h + 1) * SGU_BLOCK] = (dys * mixed).astype(dproj_ref.dtype)
                dmix = dys * u_ref[rs, cs]
                acc_b[:, cs] += dmix
                dmb = dmix.astype(BF16)
                dws_ref[h] += lax.dot_general(dmb, vnb, _DN["nt"], preferred_element_type=F32)
                dvn_ref[rs, cs] = lax.dot_general(wh, dmb, _DN["tn"], preferred_element_type=F32)
        v = v_ref[...]
        r = lax.rsqrt(jnp.mean(v * v, axis=-1, keepdims=True) + EPS)
        vh = v * r
        dvn = dvn_ref[...]
        acc_g[...] += _rowsum8(dvn * vh)
        dxh = dvn * g_ref[...]
        m = jnp.mean(dxh * vh, axis=-1, keepdims=True)
        dproj_ref[:, D_POOL + D_SGU:D_POOL + 2 * D_SGU] = (r * (dxh - vh * m)).astype(dproj_ref.dtype)

        @pl.when(i == nsteps - 1)
        def _():
            dsc_ref[...] = jnp.sum(acc_sc[...], axis=0, keepdims=True)
            dg_ref[...] = jnp.sum(acc_g[...], axis=0, keepdims=True)
            t_idx = lax.broadcasted_iota(jnp.int32, (SGU_BLOCK, SGU_BLOCK), 0) // SGU_CHUNK
            s_idx = lax.broadcasted_iota(jnp.int32, (SGU_BLOCK, SGU_BLOCK), 1) // SGU_CHUNK
            mask = s_idx <= t_idx
            for h in range(N_SGU_HEADS):
                cs = slice(h * SGU_BLOCK, (h + 1) * SGU_BLOCK)
                dws_ref[h] = jnp.where(mask, dws_ref[h], 0.0)
                col = jnp.sum(acc_b[:, cs], axis=1, keepdims=True)
                db_ref[h] = jnp.broadcast_to(col, (SGU_BLOCK, SGU_BLOCK))

    const2 = lambda i: (0, 0)
    const3 = lambda i: (0, 0, 0)
    last_halo = S // POOL_HALO - 1
    return _tc_call(
        body, name=name, grid=(nsteps,),
        in_specs=[
            pl.BlockSpec((ts, D_MODEL), lambda i: (i, 0)),
            pl.BlockSpec((POOL_HALO, D_POOL), lambda i: (jnp.minimum((i + 1) * halo_blocks, last_halo), 0)),
            pl.BlockSpec((ts, D_SGU), lambda i: (i, 1)),
            pl.BlockSpec((ts, D_SGU), lambda i: (i, 2)),
            pl.BlockSpec((ts, D_POOL), lambda i: (i, 0)),
            pl.BlockSpec((ts, D_SGU), lambda i: (i, 0)),
            pl.BlockSpec((4, POOL_GROUP, POOL_GROUP), const3),
            pl.BlockSpec((1, D_POOL), const2),
            pl.BlockSpec((1, D_SGU), const2),
            pl.BlockSpec((N_SGU_HEADS, SGU_BLOCK, SGU_BLOCK), const3),
            pl.BlockSpec((SGU_BLOCK, D_SGU), const2),
        ],
        out_specs=[
            pl.BlockSpec((ts, D_POOL + 2 * D_SGU), lambda i: (i, 0)),
            pl.BlockSpec((4, POOL_GROUP, POOL_GROUP), const3),
            pl.BlockSpec((1, D_POOL), const2),
            pl.BlockSpec((1, D_SGU), const2),
            pl.BlockSpec((N_SGU_HEADS, SGU_BLOCK, SGU_BLOCK), const3),
            pl.BlockSpec((N_SGU_HEADS, SGU_BLOCK, SGU_BLOCK), const3),
        ],
        out_shape=[
            _sds((S, D_POOL + 2 * D_SGU), BF16),
            _sds((4, POOL_GROUP, POOL_GROUP), F32),
            _sds((1, D_POOL), F32),
            _sds((1, D_SGU), F32),
            _sds((N_SGU_HEADS, SGU_BLOCK, SGU_BLOCK), F32),
            _sds((N_SGU_HEADS, SGU_BLOCK, SGU_BLOCK), F32),
        ],
        scratch_shapes=[
            pltpu.VMEM((ts + POOL_HALO, D_POOL), F32),
            pltpu.VMEM((ts, D_SGU), F32),
            pltpu.VMEM((8, D_POOL), F32),
            pltpu.VMEM((8, D_SGU), F32),
            pltpu.VMEM((SGU_BLOCK, D_SGU), F32),
        ],
        compiler_params=_cparams(("arbitrary",)),
    )(dymix, dymix, proj, proj, p, vn, pool_w, pool_scale, sgu_g, ws, bias_full)


def _silu_mul(accs, extras):
    (up,) = accs
    gt = extras[0]
    sig = 1.0 / (1.0 + jnp.exp(-gt))
    return gt, up, gt * sig * up


def _silu_mul_bwd(accs, extras):
    (dact,) = accs
    gt = extras[0].astype(F32)
    up = extras[1].astype(F32)
    sig = 1.0 / (1.0 + jnp.exp(-gt))
    silu = gt * sig
    dgt = dact * up * (sig * (1.0 + gt * (1.0 - sig)))
    dup = dact * silu
    return dgt, dup


def _add_residual(accs, extras):
    return (extras[0] + accs[0],)


def _add_residual_and_cast(accs, extras):
    y = extras[0] + accs[0]
    return y, y


def _local_step(x, mem, target, W, sm, rs):
    S = x.shape[0]
    tm = min(1024, S)
    th = min(512, S)
    tq = min(256, S)
    ts = min(512, S)
    tr = min(512, S)
    tk_s = min(2048, S)
    M = mem.shape[0]

    h1 = _rms_fwd("rms_mix", x, sm["norm_mix_g"], tr)
    proj = _mm1("proj_in", h1, W["w_in_t"], "nt", tm, 1024, 2048, F32)
    ymix, p, vn = _mixer_fwd("mixer_fwd", proj, W["pool_w"], sm["pool_scale"], sm["sgu_norm_g"],
                             sm["ws_masked"], sm["bias_full"], ts)
    x1, h2 = _mm_rows("proj_out", [ymix], [W["w_out"]], [(0, 0, 0)], "nn", tq, 2048, [x], [sm["norm_xattn_g"]],
                      [F32, BF16], 0, _ep_residual_norm)

    mb = _rms_fwd("rms_mem", mem, sm["norm_mem_g"], M)
    q = _mm1("proj_q", h2, W["w_q"], "nn", tm, 1024, 2048, BF16)
    kk = _mm1("proj_k", mb, W["w_k"], "nn", M, 1024, 2048, BF16)
    vv = _mm1("proj_v", mb, W["w_v"], "nn", M, 1024, 2048, BF16)
    o = _attn_fwd("attn_fwd", q, kk, vv, ts)
    x2, h3 = _mm_rows("proj_o", [o], [W["w_o"]], [(0, 0, 0)], "nn", tq, 2048, [x1], [sm["norm_ffn_g"]],
                      [F32, BF16], 0, _ep_residual_norm)

    gt32 = _mm1("ffn_gate", h3, W["w_gate_t"], "nt", tm, 512, 2048, F32)
    gt, up, act = _mm("ffn_up", [h3], [W["w_up_t"]], [(0, 0, 0)], "nt", tm, 512, 2048, [BF16, BF16, BF16],
                      epilogue=_silu_mul, extras=(gt32,))
    x3 = _mm1("ffn_down", act, W["w_down"], "nn", th, 512, 5632, F32, epilogue=_add_residual, extras=(x2,))
    dx3, dx3b, d_final_g, loss = _final_loss("final_loss", x3, sm["final_norm_g"], target, tr)

    dgt, dup = _mm("ffn_down_dgrad", [dx3b], [W["w_down"]], [(0, 0, 0)], "nt", tm, 512, 2048, [BF16, BF16],
                   epilogue=_silu_mul_bwd, extras=(gt, up))
    rs.push("w_down", _mm1("ffn_down_wgrad", act, dx3b, "tn", 1408, 1024, tk_s, BF16))
    rs.push("w_gate_t", _mm1("ffn_gate_wgrad", dgt, h3, "tn", 1408, 1024, tk_s, BF16))
    rs.reduce("w_down")
    rs.push("w_up_t", _mm1("ffn_up_wgrad", dup, h3, "tn", 1408, 1024, tk_s, BF16))
    rs.reduce("w_gate_t")
    dh3 = _mm("ffn_gate_up_dgrad", [dgt, dup], [W["w_gate_t"], W["w_up_t"]], [(0, 0, 0), (1, 1, 0)], "nn",
              th, 256, 5632, [F32])[0]
    rs.reduce("w_up_t")
    dx2, dx2b, d_ffn_g = _rms_bwd("rms_ffn_bwd", dh3, x2, sm["norm_ffn_g"], dx3, tr)
    rs.finish("w_down")

    rs.push("w_o", _mm1("proj_o_wgrad", o, dx2b, "tn", 1024, 1024, tk_s, BF16))
    rs.finish("w_gate_t")
    do = _mm1("proj_o_dgrad", dx2b, W["w_o"], "nt", tm, 1024, 2048, BF16)
    rs.reduce("w_o")
    dq, dk, dv = _attn_bwd("attn_bwd", q, kk, vv, do, ts)
    rs.push("w_q", _mm1("proj_q_wgrad", h2, dq, "tn", 1024, 1024, tk_s, BF16))
    rs.push("w_k", _mm1("proj_k_wgrad", mb, dk, "tn", 1024, 1024, M, BF16))
    rs.push("w_v", _mm1("proj_v_wgrad", mb, dv, "tn", 1024, 1024, M, BF16))
    rs.finish("w_up_t")
    dx1, dx1b, d_xattn_g = _mm_rows(
        "proj_q_dgrad", [dq], [W["w_q"]], [(0, 0, 0)], "nt", tq, 2048, [x1, dx2], [sm["norm_xattn_g"]],
        [F32, BF16], 1, _ep_norm_bwd)
    rs.reduce("w_q")
    rs.reduce("w_k")
    rs.reduce("w_v")
    dmb = _mm("proj_kv_dgrad", [dk, dv], [W["w_k"], W["w_v"]], [(0, 0, 0), (1, 1, 0)], "nt",
              M, 1024, 2048, [F32])[0]
    (d_mem_g,) = _rms_bwd("rms_mem_bwd", dmb, mem, sm["norm_mem_g"], None, M, want_dx=False)
    rs.finish("w_o")

    rs.push("w_out", _mm1("proj_out_wgrad", ymix, dx1b, "tn", 1024, 1024, tk_s, BF16))
    dymix = _mm1("proj_out_dgrad", dx1b, W["w_out"], "nt", tm, 1024, 2048, F32)
    rs.reduce("w_out")
    dproj, d_pool_w, d_pool_scale, d_sgu_g, d_ws, d_b = _mixer_bwd(
        "mixer_bwd", dymix, proj, p, vn, W["pool_w"], sm["pool_scale"], sm["sgu_norm_g"],
        sm["ws_masked"], sm["bias_full"], ts)
    rs.finish("w_q")
    rs.finish("w_k")
    rs.finish("w_v")
    rs.push("pool_w", d_pool_w.reshape(4, N_DEV, POOL_GROUP // N_DEV, POOL_GROUP).transpose(1, 0, 2, 3)
            .reshape(4 * POOL_GROUP, POOL_GROUP).astype(BF16))
    rs.small("early", dict(
        pool_scale=d_pool_scale, sgu_norm_g=d_sgu_g, w_spatial=d_ws, b_spatial=d_b[:, :, 0],
        norm_xattn_g=d_xattn_g, norm_mem_g=d_mem_g, norm_ffn_g=d_ffn_g, final_norm_g=d_final_g))
    rs.push("w_in_t", _mm1("proj_in_wgrad", dproj, h1, "tn", 1024, 1024, tk_s, BF16))
    rs.finish("w_out")
    rs.reduce("pool_w")
    rs.reduce("w_in_t")
    grad_x, d_mix_g = _mm_rows(
        "proj_in_dgrad", [dproj], [W["w_in_t"]], [(0, 0, 0)], "nn", tq, 3072, [x, dx1], [sm["norm_mix_g"]],
        [F32], 1, _ep_norm_bwd)
    rs.small("late", dict(norm_mix_g=d_mix_g, loss=jnp.pad(loss, ((0, 0), (0, _LANES - 1)))))
    rs.finish_small("early")
    rs.finish("pool_w")
    rs.finish("w_in_t")
    rs.finish_small("late")
    return loss, grad_x


def _mesh_pos():
    return lax.axis_index("x"), lax.axis_index("y"), lax.axis_index("c")


def _handshake(peers):
    barrier = pltpu.get_barrier_semaphore()
    for peer in peers:
        pl.semaphore_signal(barrier, inc=1, device_id=peer, device_id_type=MESH)
    pl.semaphore_wait(barrier, len(peers))


def _seq_all_gather(name, shards, collective_id):
    n = len(shards)

    def body(*refs):
        ins = refs[:n]
        outs = refs[n:2 * n]
        send_sems, recv_sems, local_sems = refs[2 * n:]
        x, y, c = _mesh_pos()
        me, sibling = (x, y, c), (x, y, 1 - c)
        xn, yn, dg = (1 - x, y), (x, 1 - y), (1 - x, 1 - y)
        north = c == 1
        via = (jnp.where(north, xn[0], yn[0]), jnp.where(north, xn[1], yn[1]))
        to = (jnp.where(north, yn[0], xn[0]), jnp.where(north, yn[1], xn[1]))
        _handshake([sibling, (*xn, c), (*yn, c)])

        def copy(a, k, block, target, src=None):
            bx, by, bc = block
            dst = outs[a].at[4 * bx + 2 * by + bc]
            return pltpu.make_async_remote_copy(
                src_ref=dst if src is None else src, dst_ref=dst,
                send_sem=send_sems.at[a, k], recv_sem=recv_sems.at[a, k],
                device_id=target, device_id_type=MESH)

        mine = [pltpu.make_async_copy(ins[a], outs[a].at[4 * x + 2 * y + c], local_sems.at[a]) for a in range(n)]
        for cp in mine:
            cp.start()
        started = []
        for a in range(n):
            first = [copy(a, 0, me, sibling, src=ins[a]), copy(a, 1, me, (*xn, c), src=ins[a]),
                     copy(a, 2, me, (*yn, c), src=ins[a])]
            for cp in first:
                cp.start()
            started += first
        for a in range(n):
            copy(a, 1, (*xn, c), me).wait_recv()
            copy(a, 2, (*yn, c), me).wait_recv()
            second = [copy(a, 3, (*via, c), (*to, c)), copy(a, 4, (*xn, c), sibling), copy(a, 5, (*yn, c), sibling)]
            for cp in second:
                cp.start()
            started += second
        for a in range(n):
            copy(a, 3, (*dg, c), me).wait_recv()
            last = copy(a, 6, (*dg, c), sibling)
            last.start()
            started.append(last)
        for a in range(n):
            copy(a, 0, sibling, me).wait_recv()
            for k, chip in ((4, xn), (5, yn), (6, dg)):
                copy(a, k, (*chip, 1 - c), me).wait_recv()
        for cp in started:
            cp.wait_send()
        for cp in mine:
            cp.wait()

    return _sc_call(
        body, name=name,
        out_type=[_sds((N_DEV,) + s.shape, s.dtype) for s in shards],
        scratch_types=[pltpu.SemaphoreType.DMA((n, 7)), pltpu.SemaphoreType.DMA((n, 7)),
                       pltpu.SemaphoreType.DMA((n,))],
        compiler_params=pltpu.CompilerParams(collective_id=collective_id),
    )(*shards)


def _seq_pair_exchange(name, gview, collective_id):
    def body(g_ref, theirs_ref, send_sems, recv_sems):
        x, y, c = _mesh_pos()
        sibling = (x, y, 1 - c)
        _handshake([sibling])
        copies = [pltpu.make_async_remote_copy(
            src_ref=g_ref.at[k, 1 - c], dst_ref=theirs_ref.at[k],
            send_sem=send_sems.at[k], recv_sem=recv_sems.at[k],
            device_id=sibling, device_id_type=MESH) for k in range(4)]
        for cp in copies:
            cp.start()
        for cp in copies:
            cp.wait()

    return _sc_call(
        body, name=name, out_type=_sds((4,) + gview.shape[2:], gview.dtype),
        scratch_types=[pltpu.SemaphoreType.DMA((4,)), pltpu.SemaphoreType.DMA((4,))],
        compiler_params=pltpu.CompilerParams(collective_id=collective_id),
    )(gview)


def _pair_sum(name, gview, theirs, pos, tr):
    _, _, r, C = gview.shape

    def body(pos_ref, a_ref, b_ref, o_ref):
        o_ref[...] = (a_ref[...].astype(F32) + b_ref[...].astype(F32)).astype(o_ref.dtype)

    grid_spec = pltpu.PrefetchScalarGridSpec(
        num_scalar_prefetch=1, grid=(4, r // tr),
        in_specs=[pl.BlockSpec((None, None, tr, C), lambda k, t, pos_ref: (k, pos_ref[0], t, 0)),
                  pl.BlockSpec((None, tr, C), lambda k, t, pos_ref: (k, t, 0))],
        out_specs=pl.BlockSpec((None, tr, C), lambda k, t, pos_ref: (k, t, 0)))
    return _tc_call(
        body, name=name, grid_spec=grid_spec, out_shape=_sds(theirs.shape, theirs.dtype),
        compiler_params=_cparams(("parallel", "parallel")),
    )(pos, gview, theirs)


def _seq_chip_exchange(name, pair, collective_id):
    def body(p_ref, land_ref, send_sems, recv_sems):
        x, y, c = _mesh_pos()
        my_chip = 2 * x + y
        chips = [(1 - x, y), (x, 1 - y), (1 - x, 1 - y)]
        _handshake([(cx, cy, c) for cx, cy in chips])
        copies = [pltpu.make_async_remote_copy(
            src_ref=p_ref.at[2 * cx + cy], dst_ref=land_ref.at[my_chip],
            send_sem=send_sems.at[j], recv_sem=recv_sems.at[j],
            device_id=(cx, cy, c), device_id_type=MESH) for j, (cx, cy) in enumerate(chips)]
        for cp in copies:
            cp.start()
        for cp in copies:
            cp.wait_send()
        for j, (cx, cy) in enumerate(chips):
            pltpu.make_async_remote_copy(
                src_ref=p_ref.at[my_chip], dst_ref=land_ref.at[2 * cx + cy],
                send_sem=send_sems.at[j], recv_sem=recv_sems.at[j],
                device_id=(cx, cy, c), device_id_type=MESH).wait_recv()

    return _sc_call(
        body, name=name, out_type=_sds(pair.shape, pair.dtype),
        scratch_types=[pltpu.SemaphoreType.DMA((3,)), pltpu.SemaphoreType.DMA((3,))],
        compiler_params=pltpu.CompilerParams(collective_id=collective_id),
    )(pair)


def _sum_leading(name, parts, tr, out_dtype=F32):
    n, r, C = parts.shape

    def body(p_ref, o_ref):
        acc = p_ref[0].astype(F32)
        for k in range(1, n):
            acc = acc + p_ref[k].astype(F32)
        o_ref[...] = acc.astype(o_ref.dtype)

    return _tc_call(
        body, name=name, grid=(r // tr,),
        in_specs=[pl.BlockSpec((n, tr, C), lambda t: (0, t, 0))],
        out_specs=pl.BlockSpec((tr, C), lambda t: (t, 0)),
        out_shape=_sds((r, C), out_dtype), compiler_params=_cparams(("parallel",)),
    )(parts)


def _row_tile(r):
    for t in (512, 384, 352, 256, 128, 64, 32, 16, 8):
        if r % t == 0:
            return t
    return r


def _adamw_math(w, g, m, v):
    c1 = 1.0 - ADAM_B1 ** ADAM_STEP
    c2 = 1.0 - ADAM_B2 ** ADAM_STEP
    nm = ADAM_B1 * m + (1.0 - ADAM_B1) * g
    nv = ADAM_B2 * v + (1.0 - ADAM_B2) * (g * g)
    m_hat = nm / c1
    v_hat = nv / c2
    return -ADAM_LR * (m_hat / (jnp.sqrt(v_hat) + ADAM_EPS) + ADAM_WD * w), nm, nv


def _chip_sum_adamw(name, pair, landed, pos, w, m, v, transposed):
    _, r, C = pair.shape
    if transposed:
        tr, tc = r, 512
        r_pad = -r % _LANES
        wspec = pl.BlockSpec((tc, r), lambda t, k, pos_ref: (t, 0))
        shape = (C, r)
        scratch = [pltpu.VMEM((tr, tc), F32), pltpu.VMEM((tc, r + r_pad), F32)]
    else:
        tr, tc = _row_tile(r), C
        wspec = pl.BlockSpec((tr, C), lambda t, k, pos_ref: (t, 0))
        shape = (r, C)
        scratch = [pltpu.VMEM((tr, tc), F32)]
    n_t = (C // tc) if transposed else (r // tr)

    def block(chip, t):
        return (chip, 0, t) if transposed else (chip, t, 0)

    def body(pos_ref, own_ref, land_ref, w_ref, m_ref, v_ref, g_ref, d_ref, nm_ref, nv_ref, acc_ref, *turn):
        k = pl.program_id(1)
        val = jnp.where(k == pos_ref[1], own_ref[...], land_ref[...]).astype(F32)

        @pl.when(k == 0)
        def _():
            acc_ref[...] = val

        @pl.when(k > 0)
        def _():
            acc_ref[...] += val

        @pl.when(k == 3)
        def _():
            if transposed:
                g_t = acc_ref[...]
                if r_pad:
                    g_t = jnp.concatenate([g_t, jnp.zeros((r_pad, tc), F32)], axis=0)
                turn[0][...] = g_t.T
                g = turn[0][:, 0:r]
            else:
                g = acc_ref[...]
            d, nm, nv = _adamw_math(w_ref[...], g, m_ref[...], v_ref[...])
            g_ref[...] = g
            d_ref[...] = d
            nm_ref[...] = nm
            nv_ref[...] = nv

    def land_index(t, k, pos_ref):
        return block(jnp.where(k == pos_ref[1], (k + 1) % 4, k), t)

    grid_spec = pltpu.PrefetchScalarGridSpec(
        num_scalar_prefetch=1, grid=(n_t, 4),
        in_specs=[pl.BlockSpec((None, tr, tc), lambda t, k, pos_ref: block(pos_ref[1], t)),
                  pl.BlockSpec((None, tr, tc), land_index), wspec, wspec, wspec],
        out_specs=[wspec] * 4, scratch_shapes=scratch)
    return _tc_call(
        body, name=name, grid_spec=grid_spec, out_shape=[_sds(shape, F32)] * 4,
        compiler_params=_cparams(("parallel", "arbitrary")),
    )(pos, pair, landed, w, m, v)


def _adamw(name, w, g, m, v):
    R, C = w.shape
    tr = _row_tile(R)

    def body(w_ref, g_ref, m_ref, v_ref, d_ref, nm_ref, nv_ref):
        d_ref[...], nm_ref[...], nv_ref[...] = _adamw_math(w_ref[...], g_ref[...], m_ref[...], v_ref[...])

    spec = pl.BlockSpec((tr, C), lambda i: (i, 0))
    return _tc_call(
        body, name=name, grid=(R // tr,), in_specs=[spec] * 4, out_specs=[spec] * 3,
        out_shape=[_sds((R, C), F32)] * 3, compiler_params=_cparams(("parallel",)),
    )(w, g, m, v)


_BIG = ("w_in_t", "w_out", "w_q", "w_k", "w_v", "w_o", "w_gate_t", "w_up_t", "w_down")
_SMALL = ("norm_mix_g", "pool_scale", "sgu_norm_g", "w_spatial", "b_spatial", "norm_xattn_g",
          "norm_mem_g", "norm_ffn_g", "final_norm_g")
_LANES = 128
_GATHER_GROUPS = (("w_in_t", "pool_w"), ("w_out",), ("w_q",), ("w_k", "w_v"), ("w_o",), ("w_gate_t",),
                  ("w_up_t",), ("w_down",))
_RS_ORDER = ("w_down", "w_gate_t", "w_up_t", "w_o", "w_q", "w_k", "w_v", "w_out", "w_in_t")
_SMALL_GROUPS = dict(
    early=("pool_scale", "sgu_norm_g", "w_spatial", "b_spatial", "norm_xattn_g", "norm_mem_g",
           "norm_ffn_g", "final_norm_g"),
    late=("norm_mix_g", "loss"))
_TURN_OUTSIDE = ("w_gate_t", "w_up_t")
_ID_GATHER, _ID_PAIR, _ID_CHIP = 0, 1, 2


_PACK_ROWS = 512


def _pack(parts):
    rows = [p.reshape(-1, _LANES) for p in parts]
    n = sum(r.shape[0] for r in rows)
    pad = -n % (_PACK_ROWS if n > _PACK_ROWS else 8)
    if pad:
        rows.append(jnp.zeros((pad, _LANES), rows[0].dtype))
    return jnp.concatenate(rows, axis=0)


class _GradReducer:
    def __init__(self, pos, apply, apply_small):
        self.pos, self.apply, self.apply_small = pos, apply, apply_small
        self.view, self.theirs, self.pair, self.landed = {}, {}, {}, {}
        self.small_gathered = {}

    def push(self, k, g):
        r = g.shape[0] // N_DEV
        self.view[k] = g.reshape(4, 2, r, g.shape[1])
        self.theirs[k] = _seq_pair_exchange("grad_pair_exchange_" + k, self.view[k], _ID_PAIR)

    def reduce(self, k):
        r = self.view[k].shape[2]
        self.pair[k] = _pair_sum("grad_pair_sum_" + k, self.view[k], self.theirs[k], self.pos, r)
        self.landed[k] = _seq_chip_exchange("grad_chip_exchange_" + k, self.pair[k], _ID_CHIP)

    def finish(self, k):
        self.apply(k, self.pair[k], self.landed[k])

    def small(self, tag, parts):
        packed = _pack([parts[k] for k in _SMALL_GROUPS[tag]])
        (self.small_gathered[tag],) = _seq_all_gather("gather_small_grads_" + tag, [packed], _ID_GATHER)

    def finish_small(self, tag):
        allp = self.small_gathered[tag]
        self.apply_small(tag, _sum_leading("sum_small_grads_" + tag, allp, min(_PACK_ROWS, allp.shape[1])))


def _unpack(packed, like):
    out, row = [], 0
    for ref in like:
        rows = ref.size // _LANES
        out.append(packed[row:row + rows].reshape(ref.shape))
        row += rows
    return out


def kernel(x, mem, norm_mix_g, w_in, pool_w, pool_scale, sgu_norm_g, w_spatial, b_spatial, w_out, norm_xattn_g, norm_mem_g, w_q, w_k, w_v, w_o, norm_ffn_g, w_gate, w_up, w_down, final_norm_g, loss_target, m_norm_mix_g, m_w_in, m_pool_w, m_pool_scale, m_sgu_norm_g, m_w_spatial, m_b_spatial, m_w_out, m_norm_xattn_g, m_norm_mem_g, m_w_q, m_w_k, m_w_v, m_w_o, m_norm_ffn_g, m_w_gate, m_w_up, m_w_down, m_final_norm_g, v_norm_mix_g, v_w_in, v_pool_w, v_pool_scale, v_sgu_norm_g, v_w_spatial, v_b_spatial, v_w_out, v_norm_xattn_g, v_norm_mem_g, v_w_q, v_w_k, v_w_v, v_w_o, v_norm_ffn_g, v_w_gate, v_w_up, v_w_down, v_final_norm_g):
    args = dict(locals())
    names = ("norm_mix_g", "w_in", "pool_w", "pool_scale", "sgu_norm_g", "w_spatial", "b_spatial", "w_out",
             "norm_xattn_g", "norm_mem_g", "w_q", "w_k", "w_v", "w_o", "norm_ffn_g", "w_gate", "w_up",
             "w_down", "final_norm_g")
    w = {k: args[k] for k in names}
    m = {k: args["m_" + k] for k in names}
    v = {k: args["v_" + k] for k in names}
    _CHAIN.__init__()

    shards = dict(
        w_in_t=w["w_in"][0].T, w_out=w["w_out"][0], w_q=w["w_q"][0], w_k=w["w_k"][0], w_v=w["w_v"][0],
        w_o=w["w_o"][0], w_gate_t=w["w_gate"][0].T, w_up_t=w["w_up"][0].T, w_down=w["w_down"][0])
    send = {k: shards[k].astype(BF16) for k in _BIG}
    send["pool_w"] = w["pool_w"][0].reshape(4 * 32, POOL_GROUP).astype(BF16)
    W = {}
    for gi, group in enumerate(_GATHER_GROUPS):
        gathered = _seq_all_gather("gather_weights_%d" % gi, [send[k] for k in group], _ID_GATHER)
        for k, g in zip(group, gathered):
            W[k] = g.reshape(-1, g.shape[-1])
    W["pool_w"] = W["pool_w"].reshape(N_DEV, 4, 32, POOL_GROUP).transpose(1, 0, 2, 3).reshape(4, POOL_GROUP, POOL_GROUP)

    t = jnp.arange(SGU_BLOCK)
    mask = (t[None, :] // SGU_CHUNK) <= (t[:, None] // SGU_CHUNK)
    sm = dict(
        norm_mix_g=w["norm_mix_g"], pool_scale=w["pool_scale"], sgu_norm_g=w["sgu_norm_g"],
        norm_xattn_g=w["norm_xattn_g"], norm_mem_g=w["norm_mem_g"], norm_ffn_g=w["norm_ffn_g"],
        final_norm_g=w["final_norm_g"].reshape(1, D_MODEL),
        ws_masked=jnp.where(mask[None], w["w_spatial"][0], 0.0).astype(BF16),
        bias_full=jnp.repeat(w["b_spatial"][0].T, SGU_BLOCK, axis=1))

    natural = dict(w_in_t="w_in", w_gate_t="w_gate", w_up_t="w_up")
    grads, delta, new_m, new_v = {}, {}, {}, {}

    def apply(k, pair, landed):
        name = natural.get(k, k)
        if k == "pool_w":
            flat = (4 * POOL_GROUP // N_DEV, POOL_GROUP)
            res = _chip_sum_adamw("grad_finish_" + k, pair, landed, pos, w[k].reshape(flat), m[k].reshape(flat),
                                  v[k].reshape(flat), False)
            grads[k], delta[k], new_m[k], new_v[k] = (a.reshape(w[k].shape) for a in res)
            return
        if k in _TURN_OUTSIDE:
            res = _chip_sum_adamw("grad_finish_" + k, pair, landed, pos, w[name][0].T, m[name][0].T, v[name][0].T,
                                  False)
            res = [a.T for a in res]
        else:
            res = _chip_sum_adamw("grad_finish_" + k, pair, landed, pos, w[name][0], m[name][0], v[name][0],
                                  k in natural)
        grads[name], delta[name], new_m[name], new_v[name] = (a[None] for a in res)

    like = dict(w)
    like["loss"] = _sds((1, _LANES), F32)

    def apply_small(tag, total):
        group = _SMALL_GROUPS[tag]
        grads.update(zip(group, _unpack(total, [like[k] for k in group])))
        if tag == "late":
            d_, m_, v_ = _adamw("adamw_small", _pack([w[k] for k in _SMALL]), _pack([grads[k] for k in _SMALL]),
                                _pack([m[k] for k in _SMALL]), _pack([v[k] for k in _SMALL]))
            shapes = [w[k] for k in _SMALL]
            for k, a, b, c_ in zip(_SMALL, _unpack(d_, shapes), _unpack(m_, shapes), _unpack(v_, shapes)):
                delta[k], new_m[k], new_v[k] = a, b, c_

    pos = jnp.stack([lax.axis_index("c"), 2 * lax.axis_index("x") + lax.axis_index("y")]).astype(jnp.int32)
    rs = _GradReducer(pos, apply, apply_small)
    _, grad_x = _local_step(x[0], mem[0], loss_target[0], W, sm, rs)

    outs = [grads["loss"][0, 0], grad_x[None]]
    outs += [grads[k].reshape(w[k].shape) for k in names]
    outs += [delta[k] for k in names]
    outs += [new_m[k] for k in names]
    outs += [new_v[k] for k in names]
    return tuple(outs)
```

```python
import functools

import jax
import jax.numpy as jnp
from jax import lax
from jax.experimental import pallas as pl
from jax.experimental.pallas import tpu as pltpu
from jax.experimental.pallas import tpu_sc as plsc

F32 = jnp.float32
BF16 = jnp.bfloat16
MESH = pl.DeviceIdType.MESH

EPS = 1e-6
D_MODEL = 2048
D_POOL = 1024
D_SGU = 1024
POOL_WINDOWS = (2, 4, 8, 16)
POOL_GROUP = 256
POOL_HALO = 16
SGU_BLOCK = 128
SGU_CHUNK = 64
N_SGU_HEADS = 8
N_HEADS = 4
HEAD_DIM = 512
N_DEV = 8

ADAM_LR = 0.001
ADAM_B1 = 0.9
ADAM_B2 = 0.999
ADAM_EPS = 1e-08
ADAM_WD = 0.01
ADAM_STEP = 10

VMEM_LIMIT = 56 * 1024 * 1024


def _cparams(sem=None):
    return pltpu.CompilerParams(dimension_semantics=sem, vmem_limit_bytes=VMEM_LIMIT)


def _sds(shape, dtype):
    return jax.ShapeDtypeStruct(shape, dtype)


_ANY = pl.BlockSpec(memory_space=pl.ANY)


class _Chain:
    def __init__(self):
        self.tc = None
        self.sc = None


_CHAIN = _Chain()


def _first(out):
    return out[0] if isinstance(out, (list, tuple)) else out


def _tc_call(body, *, in_specs=None, grid_spec=None, **kw):
    def run(*args):
        prev, n = _CHAIN.tc, len(args)
        fn, specs, spec, operands = body, in_specs, grid_spec, args
        if prev is not None:
            def fn(*refs):
                return body(*refs[:n], *refs[n + 1:])
            operands = args + (prev,)
            if grid_spec is None:
                specs = list(in_specs) + [_ANY]
            else:
                spec = pltpu.PrefetchScalarGridSpec(
                    num_scalar_prefetch=grid_spec.num_scalar_prefetch, grid=grid_spec.grid,
                    in_specs=list(grid_spec.in_specs) + [_ANY], out_specs=grid_spec.out_specs,
                    scratch_shapes=grid_spec.scratch_shapes)
        if spec is None:
            out = pl.pallas_call(fn, in_specs=specs, **kw)(*operands)
        else:
            out = pl.pallas_call(fn, grid_spec=spec, **kw)(*operands)
        _CHAIN.tc = _first(out)
        return out
    return run


def _sc_call(body, **kw):
    return pl.kernel(body, mesh=plsc.ScalarSubcoreMesh(axis_name="seq", num_cores=1), **kw)


def _rowsum8(v):
    r, c = v.shape
    return v.reshape(r // 8, 8, c).sum(axis=0)


_EPILOGUE_COLS = 256
_DN = {
    "nn": (((1,), (0,)), ((), ())),
    "nt": (((1,), (1,)), ((), ())),
    "tn": (((0,), (0,)), ((), ())),
}


def _mm(name, a_list, b_list, terms, mode, tm, tn, tk, out_dtypes, epilogue=None, extras=(), n_acc=1):
    a0, b0 = a_list[0], b_list[0]
    if mode == "tn":
        K, M = a0.shape
    else:
        M, K = a0.shape
    N = b0.shape[0] if mode == "nt" else b0.shape[1]
    assert M % tm == 0 and N % tn == 0 and K % tk == 0, (name, M, N, K, tm, tn, tk)
    nk = K // tk
    na, nb, ne, no = len(a_list), len(b_list), len(extras), len(out_dtypes)
    dn = _DN[mode]

    if mode == "tn":
        a_spec = pl.BlockSpec((tk, tm), lambda i, j, k: (k, i))
    else:
        a_spec = pl.BlockSpec((tm, tk), lambda i, j, k: (i, k))
    if mode == "nt":
        b_spec = pl.BlockSpec((tn, tk), lambda i, j, k: (j, k))
    else:
        b_spec = pl.BlockSpec((tk, tn), lambda i, j, k: (k, j))
    o_spec = pl.BlockSpec((tm, tn), lambda i, j, k: (i, j))

    def body(*refs):
        a_refs = refs[:na]
        b_refs = refs[na:na + nb]
        e_refs = refs[na + nb:na + nb + ne]
        o_refs = refs[na + nb + ne:na + nb + ne + no]
        acc_refs = refs[na + nb + ne + no:]

        def products(cols):
            parts = [None] * n_acc
            for ai, bi, ci in terms:
                b = b_refs[bi][cols, :] if mode == "nt" else b_refs[bi][:, cols]
                d = lax.dot_general(a_refs[ai][...].astype(BF16), b.astype(BF16), dn, preferred_element_type=F32)
                parts[ci] = d if parts[ci] is None else parts[ci] + d
            return parts

        def finish(accs, cols=slice(None)):
            outs = epilogue(accs, [e[:, cols] for e in e_refs]) if epilogue is not None else accs
            for o_ref, v in zip(o_refs, outs):
                o_ref[:, cols] = v.astype(o_ref.dtype)

        if nk == 1 and epilogue is not None and tn > _EPILOGUE_COLS:
            for c0 in range(0, tn, _EPILOGUE_COLS):
                cols = slice(c0, c0 + _EPILOGUE_COLS)
                finish(products(cols), cols)
            return
        parts = products(slice(None))
        if nk == 1:
            finish(parts)
        else:
            k = pl.program_id(2)

            @pl.when(k == 0)
            def _():
                for c in range(n_acc):
                    acc_refs[c][...] = parts[c]

            @pl.when(k > 0)
            def _():
                for c in range(n_acc):
                    acc_refs[c][...] += parts[c]

            @pl.when(k == nk - 1)
            def _():
                finish([acc_refs[c][...] for c in range(n_acc)])

    scratch = [pltpu.VMEM((tm, tn), F32) for _ in range(n_acc)] if nk > 1 else []
    res = _tc_call(
        body, name=name, grid=(M // tm, N // tn, nk),
        in_specs=[a_spec] * na + [b_spec] * nb + [o_spec] * ne,
        out_specs=[o_spec] * no,
        out_shape=[_sds((M, N), dt) for dt in out_dtypes],
        scratch_shapes=scratch,
        compiler_params=_cparams(("parallel", "parallel", "arbitrary")),
    )(*a_list, *b_list, *extras)
    return res


def _mm_rows(name, a_list, b_list, terms, mode, tm, tk, rows, vecs, row_dtypes, n_vec_out, epilogue,
             n_scalar_out=0):
    a0, b0 = a_list[0], b_list[0]
    M, K = a0.shape
    N = b0.shape[0] if mode == "nt" else b0.shape[1]
    assert mode in ("nn", "nt") and M % tm == 0 and K % tk == 0, (name, M, N, K, tm, tk)
    nm, nk = M // tm, K // tk
    slab = min(128, tm)
    na, nb, nr, nv, no = len(a_list), len(b_list), len(rows), len(vecs), len(row_dtypes)
    dn = _DN[mode]
    a_spec = pl.BlockSpec((tm, tk), lambda i, k: (i, k))
    b_spec = pl.BlockSpec((N, tk), lambda i, k: (0, k)) if mode == "nt" else pl.BlockSpec((tk, N), lambda i, k: (k, 0))
    row_spec = pl.BlockSpec((tm, N), lambda i, k: (i, 0))
    vec_spec = pl.BlockSpec((1, N), lambda i, k: (0, 0))
    one_spec = pl.BlockSpec((1, 1), lambda i, k: (0, 0))

    def body(*refs):
        pos = 0
        a_refs = refs[pos:pos + na]; pos += na
        b_refs = refs[pos:pos + nb]; pos += nb
        r_refs = refs[pos:pos + nr]; pos += nr
        v_refs = refs[pos:pos + nv]; pos += nv
        o_refs = refs[pos:pos + no]; pos += no
        s_refs = refs[pos:pos + n_vec_out]; pos += n_vec_out
        vacc_refs = refs[pos:pos + n_vec_out]; pos += n_vec_out
        acc_ref = refs[pos] if nk > 1 else None
        i, k = pl.program_id(0), pl.program_id(1)
        part = None
        for ai, bi, _ in terms:
            d = lax.dot_general(a_refs[ai][...].astype(BF16), b_refs[bi][...].astype(BF16), dn,
                                preferred_element_type=F32)
            part = d if part is None else part + d

        def finish(acc):
            vecs_now = [v[...] for v in v_refs]
            vparts = None
            for r0 in range(0, tm, slab):
                rs_ = slice(r0, r0 + slab)
                outs, vp = epilogue(acc[rs_, :], [r[rs_, :] for r in r_refs], vecs_now)
                for o_ref, val in zip(o_refs, outs):
                    o_ref[rs_, :] = val.astype(o_ref.dtype)
                vparts = vp if vparts is None else [a + b for a, b in zip(vparts, vp)]

            @pl.when(i == 0)
            def _():
                for vacc, vp in zip(vacc_refs, vparts):
                    vacc[...] = vp

            @pl.when(i > 0)
            def _():
                for vacc, vp in zip(vacc_refs, vparts):
                    vacc[...] += vp

            @pl.when(i == nm - 1)
            def _():
                for j, (s_ref, vacc) in enumerate(zip(s_refs, vacc_refs)):
                    col = jnp.sum(vacc[...], axis=0, keepdims=True)
                    s_ref[...] = jnp.sum(col, axis=1, keepdims=True) if j >= n_vec_out - n_scalar_out else col

        if nk == 1:
            finish(part)
        else:
            @pl.when(k == 0)
            def _():
                acc_ref[...] = part

            @pl.when(k > 0)
            def _():
                acc_ref[...] += part

            @pl.when(k == nk - 1)
            def _():
                finish(acc_ref)

    n_plain = n_vec_out - n_scalar_out
    return _tc_call(
        body, name=name, grid=(nm, nk),
        in_specs=[a_spec] * na + [b_spec] * nb + [row_spec] * nr + [vec_spec] * nv,
        out_specs=[row_spec] * no + [vec_spec] * n_plain + [one_spec] * n_scalar_out,
        out_shape=[_sds((M, N), dt) for dt in row_dtypes] + [_sds((1, N), F32)] * n_plain
        + [_sds((1, 1), F32)] * n_scalar_out,
        scratch_shapes=[pltpu.VMEM((8, N), F32)] * n_vec_out + ([pltpu.VMEM((tm, N), F32)] if nk > 1 else []),
        compiler_params=_cparams(("arbitrary", "arbitrary")),
    )(*a_list, *b_list, *rows, *vecs)


def _ep_residual_norm(acc, rows, vecs):
    x_new = rows[0] + acc
    r = lax.rsqrt(jnp.mean(x_new * x_new, axis=-1, keepdims=True) + EPS)
    return [x_new, x_new * r * vecs[0]], []


def _ep_norm_bwd(acc, rows, vecs):
    xv, dres = rows
    r = lax.rsqrt(jnp.mean(xv * xv, axis=-1, keepdims=True) + EPS)
    xh = xv * r
    dxh = acc * vecs[0]
    m = jnp.mean(dxh * xh, axis=-1, keepdims=True)
    dx = dres + r * (dxh - xh * m)
    return [dx, dx], [_rowsum8(acc * xh)]


def _ep_final_loss(acc, rows, vecs):
    x2, target = rows
    gv = vecs[0]
    xv = x2 + acc
    inv_d = 1.0 / xv.shape[-1]
    r = lax.rsqrt(jnp.mean(xv * xv, axis=-1, keepdims=True) + EPS)
    xh = xv * r
    e = xh * gv - target
    dy = e * inv_d
    dxh = dy * gv
    m = jnp.mean(dxh * xh, axis=-1, keepdims=True)
    dx = r * (dxh - xh * m)
    return [dx, dx], [_rowsum8(dy * xh), _rowsum8(e * e) * (0.5 * inv_d)]


def _mm1(name, a, b, mode, tm, tn, tk, out_dtype, **kw):
    return _mm(name, [a], [b], [(0, 0, 0)], mode, tm, tn, tk, [out_dtype], **kw)[0]


def _rms_fwd(name, x, g, tr):
    S, Dm = x.shape

    def body(x_ref, g_ref, h_ref):
        xv = x_ref[...]
        r = lax.rsqrt(jnp.mean(xv * xv, axis=-1, keepdims=True) + EPS)
        h_ref[...] = (xv * r * g_ref[...]).astype(h_ref.dtype)

    return _tc_call(
        body, name=name, grid=(S // tr,),
        in_specs=[pl.BlockSpec((tr, Dm), lambda i: (i, 0)), pl.BlockSpec((1, Dm), lambda i: (0, 0))],
        out_specs=pl.BlockSpec((tr, Dm), lambda i: (i, 0)),
        out_shape=_sds((S, Dm), BF16),
        compiler_params=_cparams(("parallel",)),
    )(x, g)


def _rms_bwd(name, dh, x, g, dres, tr, want_dx=True):
    S, Dm = x.shape
    nsteps = S // tr

    def body(*refs):
        if want_dx:
            dh_ref, x_ref, g_ref, dres_ref, dx_ref, dxb_ref, dg_ref, acc_ref = refs
        else:
            dh_ref, x_ref, g_ref, dg_ref, acc_ref = refs
        i = pl.program_id(0)
        xv = x_ref[...]
        r = lax.rsqrt(jnp.mean(xv * xv, axis=-1, keepdims=True) + EPS)
        xh = xv * r
        dhv = dh_ref[...]
        part = _rowsum8(dhv * xh)

        @pl.when(i == 0)
        def _():
            acc_ref[...] = part

        @pl.when(i > 0)
        def _():
            acc_ref[...] += part

        @pl.when(i == nsteps - 1)
        def _():
            dg_ref[...] = jnp.sum(acc_ref[...], axis=0, keepdims=True)

        if want_dx:
            dxh = dhv * g_ref[...]
            m = jnp.mean(dxh * xh, axis=-1, keepdims=True)
            dx = dres_ref[...] + r * (dxh - xh * m)
            dx_ref[...] = dx
            dxb_ref[...] = dx.astype(BF16)

    row = pl.BlockSpec((tr, Dm), lambda i: (i, 0))
    vec = pl.BlockSpec((1, Dm), lambda i: (0, 0))
    if want_dx:
        in_specs = [row, row, vec, row]
        out_specs = [row, row, vec]
        out_shape = [_sds((S, Dm), F32), _sds((S, Dm), BF16), _sds((1, Dm), F32)]
        args = (dh, x, g, dres)
    else:
        in_specs = [row, row, vec]
        out_specs = [vec]
        out_shape = [_sds((1, Dm), F32)]
        args = (dh, x, g)
    return _tc_call(
        body, name=name, grid=(nsteps,), in_specs=in_specs, out_specs=out_specs, out_shape=out_shape,
        scratch_shapes=[pltpu.VMEM((8, Dm), F32)],
        compiler_params=_cparams(("arbitrary",)),
    )(*args)


def _final_loss(name, x3, g, target, tr):
    S, Dm = x3.shape
    nsteps = S // tr

    def body(x_ref, g_ref, t_ref, dx_ref, dxb_ref, dg_ref, loss_ref, acc_g, acc_l):
        i = pl.program_id(0)
        xv = x_ref[...]
        gv = g_ref[...]
        r = lax.rsqrt(jnp.mean(xv * xv, axis=-1, keepdims=True) + EPS)
        xh = xv * r
        e = xh * gv - t_ref[...]
        dy = e * (1.0 / Dm)
        lpart = _rowsum8(e * e)
        gpart = _rowsum8(dy * xh)

        @pl.when(i == 0)
        def _():
            acc_g[...] = gpart
            acc_l[...] = lpart

        @pl.when(i > 0)
        def _():
            acc_g[...] += gpart
            acc_l[...] += lpart

        @pl.when(i == nsteps - 1)
        def _():
            dg_ref[...] = jnp.sum(acc_g[...], axis=0, keepdims=True)
            tot = jnp.sum(jnp.sum(acc_l[...], axis=1, keepdims=True), axis=0, keepdims=True)
            loss_ref[...] = tot * (0.5 / Dm)

        dxh = dy * gv
        m = jnp.mean(dxh * xh, axis=-1, keepdims=True)
        dx = r * (dxh - xh * m)
        dx_ref[...] = dx
        dxb_ref[...] = dx.astype(BF16)

    row = pl.BlockSpec((tr, Dm), lambda i: (i, 0))
    vec = pl.BlockSpec((1, Dm), lambda i: (0, 0))
    return _tc_call(
        body, name=name, grid=(nsteps,),
        in_specs=[row, vec, row],
        out_specs=[row, row, vec, pl.BlockSpec((1, 1), lambda i: (0, 0))],
        out_shape=[_sds((S, Dm), F32), _sds((S, Dm), BF16), _sds((1, Dm), F32), _sds((1, 1), F32)],
        scratch_shapes=[pltpu.VMEM((8, Dm), F32), pltpu.VMEM((8, Dm), F32)],
        compiler_params=_cparams(("arbitrary",)),
    )(x3, g, target)


def _softmax_rows(s):
    e = jnp.exp(s - jnp.max(s, axis=-1, keepdims=True))
    return e / jnp.sum(e, axis=-1, keepdims=True)


def _attn_fwd(name, q, k, v, ts):
    S, Dm = q.shape
    M = k.shape[0]
    scale = HEAD_DIM ** -0.5

    def body(q_ref, k_ref, v_ref, o_ref):
        for h in range(N_HEADS):
            sl = slice(h * HEAD_DIM, (h + 1) * HEAD_DIM)
            s = lax.dot_general(q_ref[:, sl], k_ref[:, sl], _DN["nt"], preferred_element_type=F32) * scale
            p = _softmax_rows(s)
            o_ref[:, sl] = jnp.dot(p.astype(BF16), v_ref[:, sl], preferred_element_type=F32).astype(o_ref.dtype)

    row = pl.BlockSpec((ts, Dm), lambda i: (i, 0))
    mem = pl.BlockSpec((M, Dm), lambda i: (0, 0))
    return _tc_call(
        body, name=name, grid=(S // ts,), in_specs=[row, mem, mem], out_specs=row,
        out_shape=_sds((S, Dm), BF16), compiler_params=_cparams(("parallel",)),
    )(q, k, v)


def _attn_bwd(name, q, k, v, do, ts):
    S, Dm = q.shape
    M = k.shape[0]
    scale = HEAD_DIM ** -0.5

    def body(q_ref, k_ref, v_ref, do_ref, dq_ref, dk_ref, dv_ref):
        i = pl.program_id(0)

        @pl.when(i == 0)
        def _():
            dk_ref[...] = jnp.zeros_like(dk_ref)
            dv_ref[...] = jnp.zeros_like(dv_ref)

        for h in range(N_HEADS):
            sl = slice(h * HEAD_DIM, (h + 1) * HEAD_DIM)
            qh = q_ref[:, sl]
            kh = k_ref[:, sl]
            doh = do_ref[:, sl]
            s = lax.dot_general(qh, kh, _DN["nt"], preferred_element_type=F32) * scale
            p = _softmax_rows(s)
            dp = lax.dot_general(doh, v_ref[:, sl], _DN["nt"], preferred_element_type=F32)
            ds = p * (dp - jnp.sum(dp * p, axis=-1, keepdims=True)) * scale
            dsb = ds.astype(BF16)
            dq_ref[:, sl] = jnp.dot(dsb, kh, preferred_element_type=F32).astype(dq_ref.dtype)
            dk_ref[:, sl] += lax.dot_general(dsb, qh, _DN["tn"], preferred_element_type=F32)
            dv_ref[:, sl] += lax.dot_general(p.astype(BF16), doh, _DN["tn"], preferred_element_type=F32)

    row = pl.BlockSpec((ts, Dm), lambda i: (i, 0))
    mem = pl.BlockSpec((M, Dm), lambda i: (0, 0))
    return _tc_call(
        body, name=name, grid=(S // ts,), in_specs=[row, mem, mem, row], out_specs=[row, mem, mem],
        out_shape=[_sds((S, Dm), BF16), _sds((M, Dm), F32), _sds((M, Dm), F32)],
        compiler_params=_cparams(("arbitrary",)),
    )(q, k, v, do)


def _pool_denominators(row0, ts):
    return (row0 + lax.broadcasted_iota(jnp.int32, (ts, 1), 0) + 1).astype(F32)


def _mixer_fwd(name, proj, pool_w, pool_scale, sgu_g, ws, bias_full, ts):
    S = proj.shape[0]
    nblk = ts // SGU_BLOCK
    halo_blocks = ts // POOL_HALO

    def body(proj_ref, halo_ref, pw_ref, sc_ref, g_ref, ws_ref, b_ref, y_ref, p_ref, vn_ref, ext_ref):
        i = pl.program_id(0)
        a = proj_ref[:, 0:D_POOL]
        ext_ref[0:POOL_HALO, :] = jnp.where(i > 0, halo_ref[...], 0.0)
        ext_ref[POOL_HALO:POOL_HALO + ts, :] = a
        pos = _pool_denominators(i * ts, ts)
        for gi, w in enumerate(POOL_WINDOWS):
            cs = slice(gi * POOL_GROUP, (gi + 1) * POOL_GROUP)
            acc = a[:, cs]
            for j in range(1, w):
                acc = acc + ext_ref[POOL_HALO - j:POOL_HALO - j + ts, cs]
            pg = (acc / jnp.minimum(pos, float(w)) - a[:, cs]).astype(BF16)
            p_ref[:, cs] = pg
            ypre = jnp.dot(pg, pw_ref[gi], preferred_element_type=F32)
            y_ref[:, cs] = (ypre * sc_ref[:, cs]).astype(y_ref.dtype)

        v = proj_ref[:, D_POOL + D_SGU:D_POOL + 2 * D_SGU]
        r = lax.rsqrt(jnp.mean(v * v, axis=-1, keepdims=True) + EPS)
        vn_ref[...] = (v * r * g_ref[...]).astype(BF16)
        for n in range(nblk):
            rs = slice(n * SGU_BLOCK, (n + 1) * SGU_BLOCK)
            for h in range(N_SGU_HEADS):
                cs = slice(h * SGU_BLOCK, (h + 1) * SGU_BLOCK)
                mixed = jnp.dot(ws_ref[h], vn_ref[rs, cs], preferred_element_type=F32) + b_ref[:, cs]
                u = proj_ref[rs, D_POOL + h * SGU_BLOCK:D_POOL + (h + 1) * SGU_BLOCK]
                y_ref[rs, D_POOL + h * SGU_BLOCK:D_POOL + (h + 1) * SGU_BLOCK] = (u * mixed).astype(y_ref.dtype)

    return _tc_call(
        body, name=name, grid=(S // ts,),
        in_specs=[
            pl.BlockSpec((ts, D_POOL + 2 * D_SGU), lambda i: (i, 0)),
            pl.BlockSpec((POOL_HALO, D_POOL), lambda i: (jnp.maximum(i * halo_blocks - 1, 0), 0)),
            pl.BlockSpec((4, POOL_GROUP, POOL_GROUP), lambda i: (0, 0, 0)),
            pl.BlockSpec((1, D_POOL), lambda i: (0, 0)),
            pl.BlockSpec((1, D_SGU), lambda i: (0, 0)),
            pl.BlockSpec((N_SGU_HEADS, SGU_BLOCK, SGU_BLOCK), lambda i: (0, 0, 0)),
            pl.BlockSpec((SGU_BLOCK, D_SGU), lambda i: (0, 0)),
        ],
        out_specs=[
            pl.BlockSpec((ts, D_MODEL), lambda i: (i, 0)),
            pl.BlockSpec((ts, D_POOL), lambda i: (i, 0)),
            pl.BlockSpec((ts, D_SGU), lambda i: (i, 0)),
        ],
        out_shape=[_sds((S, D_MODEL), BF16), _sds((S, D_POOL), BF16), _sds((S, D_SGU), BF16)],
        scratch_shapes=[pltpu.VMEM((ts + POOL_HALO, D_POOL), F32)],
        compiler_params=_cparams(("parallel",)),
    )(proj, proj, pool_w, pool_scale, sgu_g, ws, bias_full)


def _mixer_bwd(name, dymix, proj, p, vn, pool_w, pool_scale, sgu_g, ws, bias_full, ts):
    S = proj.shape[0]
    nsteps = S // ts
    nblk = ts // SGU_BLOCK
    halo_blocks = ts // POOL_HALO

    def body(dy_ref, dyh_ref, u_ref, v_ref, p_ref, vn_ref, pw_ref, sc_ref, g_ref, ws_ref, b_ref,
             dproj_ref, dpw_ref, dsc_ref, dg_ref, dws_ref, db_ref,
             ext_ref, dvn_ref, acc_sc, acc_g, acc_b):
        i = pl.program_id(0)

        @pl.when(i == 0)
        def _():
            dpw_ref[...] = jnp.zeros_like(dpw_ref)
            dws_ref[...] = jnp.zeros_like(dws_ref)
            acc_sc[...] = jnp.zeros_like(acc_sc)
            acc_g[...] = jnp.zeros_like(acc_g)
            acc_b[...] = jnp.zeros_like(acc_b)

        pos = _pool_denominators(i * ts, ts)
        pos_h = _pool_denominators((i + 1) * ts, POOL_HALO)
        for gi, w in enumerate(POOL_WINDOWS):
            cs = slice(gi * POOL_GROUP, (gi + 1) * POOL_GROUP)
            pg = p_ref[:, cs]
            wg = pw_ref[gi]
            dyp = dy_ref[:, cs]
            ypre = jnp.dot(pg, wg, preferred_element_type=F32)
            acc_sc[:, cs] += _rowsum8(dyp * ypre)
            dz = (dyp * sc_ref[:, cs]).astype(BF16)
            dpw_ref[gi] += lax.dot_general(pg, dz, _DN["tn"], preferred_element_type=F32)
            dp = lax.dot_general(dz, wg, _DN["nt"], preferred_element_type=F32)
            dzh = (dyh_ref[:, cs] * sc_ref[:, cs]).astype(BF16)
            dph = lax.dot_general(dzh, wg, _DN["nt"], preferred_element_type=F32)
            ext_ref[0:ts, cs] = dp / jnp.minimum(pos, float(w))
            ext_ref[ts:ts + POOL_HALO, cs] = jnp.where(i < nsteps - 1, dph / jnp.minimum(pos_h, float(w)), 0.0)
            acc = ext_ref[0:ts, cs]
            for j in range(1, w):
                acc = acc + ext_ref[j:j + ts, cs]
            dproj_ref[:, cs] = (acc - dp).astype(dproj_ref.dtype)

        for n in range(nblk):
            rs = slice(n * SGU_BLOCK, (n + 1) * SGU_BLOCK)
            for h in range(N_SGU_HEADS):
                cs = slice(h * SGU_BLOCK, (h + 1) * SGU_BLOCK)
                vnb = vn_ref[rs, cs]
                wh = ws_ref[h]
                mixed = jnp.dot(wh, vnb, preferred_element_type=F32) + b_ref[:, cs]
                dys = dy_ref[rs, D_POOL + h * SGU_BLOCK:D_POOL + (h + 1) * SGU_BLOCK]
                dproj_ref[rs, D_POOL + h * SGU_BLOCK:D_POOL + (h + 1) * SGU_BLOCK] = (dys * mixed).astype(dproj_ref.dtype)
                dmix = dys * u_ref[rs, cs]
                acc_b[:, cs] += dmix
                dmb = dmix.astype(BF16)
                dws_ref[h] += lax.dot_general(dmb, vnb, _DN["nt"], preferred_element_type=F32)
                dvn_ref[rs, cs] = lax.dot_general(wh, dmb, _DN["tn"], preferred_element_type=F32)
        v = v_ref[...]
        r = lax.rsqrt(jnp.mean(v * v, axis=-1, keepdims=True) + EPS)
        vh = v * r
        dvn = dvn_ref[...]
        acc_g[...] += _rowsum8(dvn * vh)
        dxh = dvn * g_ref[...]
        m = jnp.mean(dxh * vh, axis=-1, keepdims=True)
        dproj_ref[:, D_POOL + D_SGU:D_POOL + 2 * D_SGU] = (r * (dxh - vh * m)).astype(dproj_ref.dtype)

        @pl.when(i == nsteps - 1)
        def _():
            dsc_ref[...] = jnp.sum(acc_sc[...], axis=0, keepdims=True)
            dg_ref[...] = jnp.sum(acc_g[...], axis=0, keepdims=True)
            t_idx = lax.broadcasted_iota(jnp.int32, (SGU_BLOCK, SGU_BLOCK), 0) // SGU_CHUNK
            s_idx = lax.broadcasted_iota(jnp.int32, (SGU_BLOCK, SGU_BLOCK), 1) // SGU_CHUNK
            mask = s_idx <= t_idx
            for h in range(N_SGU_HEADS):
                cs = slice(h * SGU_BLOCK, (h + 1) * SGU_BLOCK)
                dws_ref[h] = jnp.where(mask, dws_ref[h], 0.0)
                col = jnp.sum(acc_b[:, cs], axis=1, keepdims=True)
                db_ref[h] = jnp.broadcast_to(col, (SGU_BLOCK, SGU_BLOCK))

    const2 = lambda i: (0, 0)
    const3 = lambda i: (0, 0, 0)
    last_halo = S // POOL_HALO - 1
    return _tc_call(
        body, name=name, grid=(nsteps,),
        in_specs=[
            pl.BlockSpec((ts, D_MODEL), lambda i: (i, 0)),
            pl.BlockSpec((POOL_HALO, D_POOL), lambda i: (jnp.minimum((i + 1) * halo_blocks, last_halo), 0)),
            pl.BlockSpec((ts, D_SGU), lambda i: (i, 1)),
            pl.BlockSpec((ts, D_SGU), lambda i: (i, 2)),
            pl.BlockSpec((ts, D_POOL), lambda i: (i, 0)),
            pl.BlockSpec((ts, D_SGU), lambda i: (i, 0)),
            pl.BlockSpec((4, POOL_GROUP, POOL_GROUP), const3),
            pl.BlockSpec((1, D_POOL), const2),
            pl.BlockSpec((1, D_SGU), const2),
            pl.BlockSpec((N_SGU_HEADS, SGU_BLOCK, SGU_BLOCK), const3),
            pl.BlockSpec((SGU_BLOCK, D_SGU), const2),
        ],
        out_specs=[
            pl.BlockSpec((ts, D_POOL + 2 * D_SGU), lambda i: (i, 0)),
            pl.BlockSpec((4, POOL_GROUP, POOL_GROUP), const3),
            pl.BlockSpec((1, D_POOL), const2),
            pl.BlockSpec((1, D_SGU), const2),
            pl.BlockSpec((N_SGU_HEADS, SGU_BLOCK, SGU_BLOCK), const3),
            pl.BlockSpec((N_SGU_HEADS, SGU_BLOCK, SGU_BLOCK), const3),
        ],
        out_shape=[
            _sds((S, D_POOL + 2 * D_SGU), BF16),
            _sds((4, POOL_GROUP, POOL_GROUP), F32),
            _sds((1, D_POOL), F32),
            _sds((1, D_SGU), F32),
            _sds((N_SGU_HEADS, SGU_BLOCK, SGU_BLOCK), F32),
            _sds((N_SGU_HEADS, SGU_BLOCK, SGU_BLOCK), F32),
        ],
        scratch_shapes=[
            pltpu.VMEM((ts + POOL_HALO, D_POOL), F32),
            pltpu.VMEM((ts, D_SGU), F32),
            pltpu.VMEM((8, D_POOL), F32),
            pltpu.VMEM((8, D_SGU), F32),
            pltpu.VMEM((SGU_BLOCK, D_SGU), F32),
        ],
        compiler_params=_cparams(("arbitrary",)),
    )(dymix, dymix, proj, proj, p, vn, pool_w, pool_scale, sgu_g, ws, bias_full)


def _silu_mul(accs, extras):
    (up,) = accs
    gt = extras[0]
    sig = 1.0 / (1.0 + jnp.exp(-gt))
    return gt, up, gt * sig * up


def _silu_mul_bwd(accs, extras):
    (dact,) = accs
    gt = extras[0].astype(F32)
    up = extras[1].astype(F32)
    sig = 1.0 / (1.0 + jnp.exp(-gt))
    silu = gt * sig
    dgt = dact * up * (sig * (1.0 + gt * (1.0 - sig)))
    dup = dact * silu
    return dgt, dup


def _add_residual(accs, extras):
    return (extras[0] + accs[0],)


def _add_residual_and_cast(accs, extras):
    y = extras[0] + accs[0]
    return y, y


def _local_step(x, mem, target, W, sm, rs):
    S = x.shape[0]
    tm = min(1024, S)
    th = min(512, S)
    tq = min(256, S)
    ts = min(512, S)
    tr = min(512, S)
    tk_s = min(2048, S)
    M = mem.shape[0]

    h1 = _rms_fwd("rms_mix", x, sm["norm_mix_g"], tr)
    proj = _mm1("proj_in", h1, W["w_in_t"], "nt", tm, 1024, 2048, F32)
    ymix, p, vn = _mixer_fwd("mixer_fwd", proj, W["pool_w"], sm["pool_scale"], sm["sgu_norm_g"],
                             sm["ws_masked"], sm["bias_full"], ts)
    x1, h2 = _mm_rows("proj_out", [ymix], [W["w_out"]], [(0, 0, 0)], "nn", tq, 2048, [x], [sm["norm_xattn_g"]],
                      [F32, BF16], 0, _ep_residual_norm)

    mb = _rms_fwd("rms_mem", mem, sm["norm_mem_g"], M)
    q = _mm1("proj_q", h2, W["w_q"], "nn", tm, 1024, 2048, BF16)
    kk = _mm1("proj_k", mb, W["w_k"], "nn", M, 1024, 2048, BF16)
    vv = _mm1("proj_v", mb, W["w_v"], "nn", M, 1024, 2048, BF16)
    o = _attn_fwd("attn_fwd", q, kk, vv, ts)
    x2, h3 = _mm_rows("proj_o", [o], [W["w_o"]], [(0, 0, 0)], "nn", tq, 2048, [x1], [sm["norm_ffn_g"]],
                      [F32, BF16], 0, _ep_residual_norm)

    gt32 = _mm1("ffn_gate", h3, W["w_gate_t"], "nt", tm, 512, 2048, F32)
    gt, up, act = _mm("ffn_up", [h3], [W["w_up_t"]], [(0, 0, 0)], "nt", tm, 512, 2048, [BF16, BF16, BF16],
                      epilogue=_silu_mul, extras=(gt32,))
    x3 = _mm1("ffn_down", act, W["w_down"], "nn", th, 1024, 5632, F32, epilogue=_add_residual, extras=(x2,))
    dx3, dx3b, d_final_g, loss = _final_loss("final_loss", x3, sm["final_norm_g"], target, tr)

    dgt, dup = _mm("ffn_down_dgrad", [dx3b], [W["w_down"]], [(0, 0, 0)], "nt", tm, 512, 2048, [BF16, BF16],
                   epilogue=_silu_mul_bwd, extras=(gt, up))
    rs.push("w_down", _mm1("ffn_down_wgrad", act, dx3b, "tn", 1408, 1024, tk_s, BF16))
    rs.push("w_gate_t", _mm1("ffn_gate_wgrad", dgt, h3, "tn", 1408, 1024, tk_s, BF16))
    rs.reduce("w_down")
    rs.push("w_up_t", _mm1("ffn_up_wgrad", dup, h3, "tn", 1408, 1024, tk_s, BF16))
    rs.reduce("w_gate_t")
    dh3 = _mm("ffn_gate_up_dgrad", [dgt, dup], [W["w_gate_t"], W["w_up_t"]], [(0, 0, 0), (1, 1, 0)], "nn",
              th, 512, 5632, [F32])[0]
    rs.reduce("w_up_t")
    dx2, dx2b, d_ffn_g = _rms_bwd("rms_ffn_bwd", dh3, x2, sm["norm_ffn_g"], dx3, tr)
    rs.finish("w_down")

    rs.push("w_o", _mm1("proj_o_wgrad", o, dx2b, "tn", 1024, 1024, tk_s, BF16))
    rs.finish("w_gate_t")
    do = _mm1("proj_o_dgrad", dx2b, W["w_o"], "nt", tm, 1024, 2048, BF16)
    rs.reduce("w_o")
    dq, dk, dv = _attn_bwd("attn_bwd", q, kk, vv, do, ts)
    rs.push("w_q", _mm1("proj_q_wgrad", h2, dq, "tn", 1024, 1024, tk_s, BF16))
    rs.push("w_k", _mm1("proj_k_wgrad", mb, dk, "tn", 1024, 1024, M, BF16))
    rs.push("w_v", _mm1("proj_v_wgrad", mb, dv, "tn", 1024, 1024, M, BF16))
    rs.finish("w_up_t")
    dx1, dx1b, d_xattn_g = _mm_rows(
        "proj_q_dgrad", [dq], [W["w_q"]], [(0, 0, 0)], "nt", tq, 2048, [x1, dx2], [sm["norm_xattn_g"]],
        [F32, BF16], 1, _ep_norm_bwd)
    rs.reduce("w_q")
    rs.reduce("w_k")
    rs.reduce("w_v")
    dmb = _mm("proj_kv_dgrad", [dk, dv], [W["w_k"], W["w_v"]], [(0, 0, 0), (1, 1, 0)], "nt",
              M, 1024, 2048, [F32])[0]
    (d_mem_g,) = _rms_bwd("rms_mem_bwd", dmb, mem, sm["norm_mem_g"], None, M, want_dx=False)

    rs.push("w_out", _mm1("proj_out_wgrad", ymix, dx1b, "tn", 1024, 1024, tk_s, BF16))
    dymix = _mm1("proj_out_dgrad", dx1b, W["w_out"], "nt", tm, 1024, 2048, F32)
    rs.finish("w_o")
    dproj, d_pool_w, d_pool_scale, d_sgu_g, d_ws, d_b = _mixer_bwd(
        "mixer_bwd", dymix, proj, p, vn, W["pool_w"], sm["pool_scale"], sm["sgu_norm_g"],
        sm["ws_masked"], sm["bias_full"], ts)
    rs.finish("w_q")
    rs.finish("w_k")
    rs.finish("w_v")
    rs.reduce("w_out")
    rs.push("pool_w", d_pool_w.reshape(4, N_DEV, POOL_GROUP // N_DEV, POOL_GROUP).transpose(1, 0, 2, 3)
            .reshape(4 * POOL_GROUP, POOL_GROUP).astype(BF16))
    rs.small("early", dict(
        pool_scale=d_pool_scale, sgu_norm_g=d_sgu_g, w_spatial=d_ws, b_spatial=d_b[:, :, 0],
        norm_xattn_g=d_xattn_g, norm_mem_g=d_mem_g, norm_ffn_g=d_ffn_g, final_norm_g=d_final_g))
    rs.push("w_in_t", _mm1("proj_in_wgrad", dproj, h1, "tn", 1024, 1024, tk_s, BF16))
    rs.finish("w_out")
    rs.reduce("pool_w")
    rs.reduce("w_in_t")
    grad_x, d_mix_g = _mm_rows(
        "proj_in_dgrad", [dproj], [W["w_in_t"]], [(0, 0, 0)], "nn", tq, 3072, [x, dx1], [sm["norm_mix_g"]],
        [F32], 1, _ep_norm_bwd)
    rs.small("late", dict(norm_mix_g=d_mix_g, loss=jnp.pad(loss, ((0, 0), (0, _LANES - 1)))))
    rs.finish_small("early")
    rs.finish("pool_w")
    rs.finish("w_in_t")
    rs.finish_small("late")
    return loss, grad_x


def _mesh_pos():
    return lax.axis_index("x"), lax.axis_index("y"), lax.axis_index("c")


def _handshake(peers):
    barrier = pltpu.get_barrier_semaphore()
    for peer in peers:
        pl.semaphore_signal(barrier, inc=1, device_id=peer, device_id_type=MESH)
    pl.semaphore_wait(barrier, len(peers))


def _seq_all_gather(name, shards, collective_id):
    n = len(shards)

    def body(*refs):
        ins = refs[:n]
        outs = refs[n:2 * n]
        send_sems, recv_sems, local_sems = refs[2 * n:]
        x, y, c = _mesh_pos()
        me, sibling = (x, y, c), (x, y, 1 - c)
        xn, yn, dg = (1 - x, y), (x, 1 - y), (1 - x, 1 - y)
        north = c == 1
        via = (jnp.where(north, xn[0], yn[0]), jnp.where(north, xn[1], yn[1]))
        to = (jnp.where(north, yn[0], xn[0]), jnp.where(north, yn[1], xn[1]))
        _handshake([sibling, (*xn, c), (*yn, c)])

        def copy(a, k, block, target, src=None):
            bx, by, bc = block
            dst = outs[a].at[4 * bx + 2 * by + bc]
            return pltpu.make_async_remote_copy(
                src_ref=dst if src is None else src, dst_ref=dst,
                send_sem=send_sems.at[a, k], recv_sem=recv_sems.at[a, k],
                device_id=target, device_id_type=MESH)

        mine = [pltpu.make_async_copy(ins[a], outs[a].at[4 * x + 2 * y + c], local_sems.at[a]) for a in range(n)]
        for cp in mine:
            cp.start()
        started = []
        for a in range(n):
            first = [copy(a, 0, me, sibling, src=ins[a]), copy(a, 1, me, (*xn, c), src=ins[a]),
                     copy(a, 2, me, (*yn, c), src=ins[a])]
            for cp in first:
                cp.start()
            started += first
        for a in range(n):
            copy(a, 1, (*xn, c), me).wait_recv()
            copy(a, 2, (*yn, c), me).wait_recv()
            second = [copy(a, 3, (*via, c), (*to, c)), copy(a, 4, (*xn, c), sibling), copy(a, 5, (*yn, c), sibling)]
            for cp in second:
                cp.start()
            started += second
        for a in range(n):
            copy(a, 3, (*dg, c), me).wait_recv()
            last = copy(a, 6, (*dg, c), sibling)
            last.start()
            started.append(last)
        for a in range(n):
            copy(a, 0, sibling, me).wait_recv()
            for k, chip in ((4, xn), (5, yn), (6, dg)):
                copy(a, k, (*chip, 1 - c), me).wait_recv()
        for cp in started:
            cp.wait_send()
        for cp in mine:
            cp.wait()

    return _sc_call(
        body, name=name,
        out_type=[_sds((N_DEV,) + s.shape, s.dtype) for s in shards],
        scratch_types=[pltpu.SemaphoreType.DMA((n, 7)), pltpu.SemaphoreType.DMA((n, 7)),
                       pltpu.SemaphoreType.DMA((n,))],
        compiler_params=pltpu.CompilerParams(collective_id=collective_id),
    )(*shards)


def _seq_pair_exchange(name, gview, collective_id):
    def body(g_ref, theirs_ref, send_sems, recv_sems):
        x, y, c = _mesh_pos()
        sibling = (x, y, 1 - c)
        _handshake([sibling])
        copies = [pltpu.make_async_remote_copy(
            src_ref=g_ref.at[k, 1 - c], dst_ref=theirs_ref.at[k],
            send_sem=send_sems.at[k], recv_sem=recv_sems.at[k],
            device_id=sibling, device_id_type=MESH) for k in range(4)]
        for cp in copies:
            cp.start()
        for cp in copies:
            cp.wait()

    return _sc_call(
        body, name=name, out_type=_sds((4,) + gview.shape[2:], gview.dtype),
        scratch_types=[pltpu.SemaphoreType.DMA((4,)), pltpu.SemaphoreType.DMA((4,))],
        compiler_params=pltpu.CompilerParams(collective_id=collective_id),
    )(gview)


def _pair_sum(name, gview, theirs, pos, tr):
    _, _, r, C = gview.shape

    def body(pos_ref, a_ref, b_ref, o_ref):
        o_ref[...] = (a_ref[...].astype(F32) + b_ref[...].astype(F32)).astype(o_ref.dtype)

    grid_spec = pltpu.PrefetchScalarGridSpec(
        num_scalar_prefetch=1, grid=(4, r // tr),
        in_specs=[pl.BlockSpec((None, None, tr, C), lambda k, t, pos_ref: (k, pos_ref[0], t, 0)),
                  pl.BlockSpec((None, tr, C), lambda k, t, pos_ref: (k, t, 0))],
        out_specs=pl.BlockSpec((None, tr, C), lambda k, t, pos_ref: (k, t, 0)))
    return _tc_call(
        body, name=name, grid_spec=grid_spec, out_shape=_sds(theirs.shape, theirs.dtype),
        compiler_params=_cparams(("parallel", "parallel")),
    )(pos, gview, theirs)


def _seq_chip_exchange(name, pair, collective_id):
    def body(p_ref, land_ref, send_sems, recv_sems):
        x, y, c = _mesh_pos()
        my_chip = 2 * x + y
        chips = [(1 - x, y), (x, 1 - y), (1 - x, 1 - y)]
        _handshake([(cx, cy, c) for cx, cy in chips])
        copies = [pltpu.make_async_remote_copy(
            src_ref=p_ref.at[2 * cx + cy], dst_ref=land_ref.at[my_chip],
            send_sem=send_sems.at[j], recv_sem=recv_sems.at[j],
            device_id=(cx, cy, c), device_id_type=MESH) for j, (cx, cy) in enumerate(chips)]
        for cp in copies:
            cp.start()
        for cp in copies:
            cp.wait_send()
        for j, (cx, cy) in enumerate(chips):
            pltpu.make_async_remote_copy(
                src_ref=p_ref.at[my_chip], dst_ref=land_ref.at[2 * cx + cy],
                send_sem=send_sems.at[j], recv_sem=recv_sems.at[j],
                device_id=(cx, cy, c), device_id_type=MESH).wait_recv()

    return _sc_call(
        body, name=name, out_type=_sds(pair.shape, pair.dtype),
        scratch_types=[pltpu.SemaphoreType.DMA((3,)), pltpu.SemaphoreType.DMA((3,))],
        compiler_params=pltpu.CompilerParams(collective_id=collective_id),
    )(pair)


def _sum_leading(name, parts, tr, out_dtype=F32):
    n, r, C = parts.shape

    def body(p_ref, o_ref):
        acc = p_ref[0].astype(F32)
        for k in range(1, n):
            acc = acc + p_ref[k].astype(F32)
        o_ref[...] = acc.astype(o_ref.dtype)

    return _tc_call(
        body, name=name, grid=(r // tr,),
        in_specs=[pl.BlockSpec((n, tr, C), lambda t: (0, t, 0))],
        out_specs=pl.BlockSpec((tr, C), lambda t: (t, 0)),
        out_shape=_sds((r, C), out_dtype), compiler_params=_cparams(("parallel",)),
    )(parts)


def _row_tile(r):
    for t in (512, 384, 352, 256, 128, 64, 32, 16, 8):
        if r % t == 0:
            return t
    return r


def _adamw_math(w, g, m, v):
    c1 = 1.0 - ADAM_B1 ** ADAM_STEP
    c2 = 1.0 - ADAM_B2 ** ADAM_STEP
    nm = ADAM_B1 * m + (1.0 - ADAM_B1) * g
    nv = ADAM_B2 * v + (1.0 - ADAM_B2) * (g * g)
    m_hat = nm / c1
    v_hat = nv / c2
    return -ADAM_LR * (m_hat / (jnp.sqrt(v_hat) + ADAM_EPS) + ADAM_WD * w), nm, nv


def _chip_sum_adamw(name, pair, landed, pos, w, m, v, transposed):
    _, r, C = pair.shape
    if transposed:
        tr, tc = r, 512
        r_pad = -r % _LANES
        wspec = pl.BlockSpec((tc, r), lambda t, k, pos_ref: (t, 0))
        shape = (C, r)
        scratch = [pltpu.VMEM((tr, tc), F32), pltpu.VMEM((tc, r + r_pad), F32)]
    else:
        tr, tc = _row_tile(r), C
        wspec = pl.BlockSpec((tr, C), lambda t, k, pos_ref: (t, 0))
        shape = (r, C)
        scratch = [pltpu.VMEM((tr, tc), F32)]
    n_t = (C // tc) if transposed else (r // tr)

    def block(chip, t):
        return (chip, 0, t) if transposed else (chip, t, 0)

    def body(pos_ref, own_ref, land_ref, w_ref, m_ref, v_ref, g_ref, d_ref, nm_ref, nv_ref, acc_ref, *turn):
        k = pl.program_id(1)
        val = jnp.where(k == pos_ref[1], own_ref[...], land_ref[...]).astype(F32)

        @pl.when(k == 0)
        def _():
            acc_ref[...] = val

        @pl.when(k > 0)
        def _():
            acc_ref[...] += val

        @pl.when(k == 3)
        def _():
            if transposed:
                g_t = acc_ref[...]
                if r_pad:
                    g_t = jnp.concatenate([g_t, jnp.zeros((r_pad, tc), F32)], axis=0)
                turn[0][...] = g_t.T
                g = turn[0][:, 0:r]
            else:
                g = acc_ref[...]
            d, nm, nv = _adamw_math(w_ref[...], g, m_ref[...], v_ref[...])
            g_ref[...] = g
            d_ref[...] = d
            nm_ref[...] = nm
            nv_ref[...] = nv

    def land_index(t, k, pos_ref):
        return block(jnp.where(k == pos_ref[1], (k + 1) % 4, k), t)

    grid_spec = pltpu.PrefetchScalarGridSpec(
        num_scalar_prefetch=1, grid=(n_t, 4),
        in_specs=[pl.BlockSpec((None, tr, tc), lambda t, k, pos_ref: block(pos_ref[1], t)),
                  pl.BlockSpec((None, tr, tc), land_index), wspec, wspec, wspec],
        out_specs=[wspec] * 4, scratch_shapes=scratch)
    return _tc_call(
        body, name=name, grid_spec=grid_spec, out_shape=[_sds(shape, F32)] * 4,
        compiler_params=_cparams(("parallel", "arbitrary")),
    )(pos, pair, landed, w, m, v)


def _adamw(name, w, g, m, v):
    R, C = w.shape
    tr = _row_tile(R)

    def body(w_ref, g_ref, m_ref, v_ref, d_ref, nm_ref, nv_ref):
        d_ref[...], nm_ref[...], nv_ref[...] = _adamw_math(w_ref[...], g_ref[...], m_ref[...], v_ref[...])

    spec = pl.BlockSpec((tr, C), lambda i: (i, 0))
    return _tc_call(
        body, name=name, grid=(R // tr,), in_specs=[spec] * 4, out_specs=[spec] * 3,
        out_shape=[_sds((R, C), F32)] * 3, compiler_params=_cparams(("parallel",)),
    )(w, g, m, v)


_BIG = ("w_in_t", "w_out", "w_q", "w_k", "w_v", "w_o", "w_gate_t", "w_up_t", "w_down")
_SMALL = ("norm_mix_g", "pool_scale", "sgu_norm_g", "w_spatial", "b_spatial", "norm_xattn_g",
          "norm_mem_g", "norm_ffn_g", "final_norm_g")
_LANES = 128
_GATHER_GROUPS = (("w_in_t", "pool_w"), ("w_out",), ("w_q",), ("w_k", "w_v"), ("w_o",), ("w_gate_t",),
                  ("w_up_t",), ("w_down",))
_RS_ORDER = ("w_down", "w_gate_t", "w_up_t", "w_o", "w_q", "w_k", "w_v", "w_out", "w_in_t")
_SMALL_GROUPS = dict(
    early=("pool_scale", "sgu_norm_g", "w_spatial", "b_spatial", "norm_xattn_g", "norm_mem_g",
           "norm_ffn_g", "final_norm_g"),
    late=("norm_mix_g", "loss"))
_TURN_OUTSIDE = ("w_gate_t", "w_up_t")
_ID_GATHER, _ID_PAIR, _ID_CHIP = 0, 1, 2


_PACK_ROWS = 512


def _pack(parts):
    rows = [p.reshape(-1, _LANES) for p in parts]
    n = sum(r.shape[0] for r in rows)
    pad = -n % (_PACK_ROWS if n > _PACK_ROWS else 8)
    if pad:
        rows.append(jnp.zeros((pad, _LANES), rows[0].dtype))
    return jnp.concatenate(rows, axis=0)


class _GradReducer:
    def __init__(self, pos, apply, apply_small):
        self.pos, self.apply, self.apply_small = pos, apply, apply_small
        self.view, self.theirs, self.pair, self.landed = {}, {}, {}, {}
        self.small_gathered = {}

    def push(self, k, g):
        r = g.shape[0] // N_DEV
        self.view[k] = g.reshape(4, 2, r, g.shape[1])
        self.theirs[k] = _seq_pair_exchange("grad_pair_exchange_" + k, self.view[k], _ID_PAIR)

    def reduce(self, k):
        r = self.view[k].shape[2]
        self.pair[k] = _pair_sum("grad_pair_sum_" + k, self.view[k], self.theirs[k], self.pos, r)
        self.landed[k] = _seq_chip_exchange("grad_chip_exchange_" + k, self.pair[k], _ID_CHIP)

    def finish(self, k):
        self.apply(k, self.pair[k], self.landed[k])

    def small(self, tag, parts):
        packed = _pack([parts[k] for k in _SMALL_GROUPS[tag]])
        (self.small_gathered[tag],) = _seq_all_gather("gather_small_grads_" + tag, [packed], _ID_GATHER)

    def finish_small(self, tag):
        allp = self.small_gathered[tag]
        self.apply_small(tag, _sum_leading("sum_small_grads_" + tag, allp, min(_PACK_ROWS, allp.shape[1])))


def _unpack(packed, like):
    out, row = [], 0
    for ref in like:
        rows = ref.size // _LANES
        out.append(packed[row:row + rows].reshape(ref.shape))
        row += rows
    return out


def kernel(x, mem, norm_mix_g, w_in, pool_w, pool_scale, sgu_norm_g, w_spatial, b_spatial, w_out, norm_xattn_g, norm_mem_g, w_q, w_k, w_v, w_o, norm_ffn_g, w_gate, w_up, w_down, final_norm_g, loss_target, m_norm_mix_g, m_w_in, m_pool_w, m_pool_scale, m_sgu_norm_g, m_w_spatial, m_b_spatial, m_w_out, m_norm_xattn_g, m_norm_mem_g, m_w_q, m_w_k, m_w_v, m_w_o, m_norm_ffn_g, m_w_gate, m_w_up, m_w_down, m_final_norm_g, v_norm_mix_g, v_w_in, v_pool_w, v_pool_scale, v_sgu_norm_g, v_w_spatial, v_b_spatial, v_w_out, v_norm_xattn_g, v_norm_mem_g, v_w_q, v_w_k, v_w_v, v_w_o, v_norm_ffn_g, v_w_gate, v_w_up, v_w_down, v_final_norm_g):
    args = dict(locals())
    names = ("norm_mix_g", "w_in", "pool_w", "pool_scale", "sgu_norm_g", "w_spatial", "b_spatial", "w_out",
             "norm_xattn_g", "norm_mem_g", "w_q", "w_k", "w_v", "w_o", "norm_ffn_g", "w_gate", "w_up",
             "w_down", "final_norm_g")
    w = {k: args[k] for k in names}
    m = {k: args["m_" + k] for k in names}
    v = {k: args["v_" + k] for k in names}
    _CHAIN.__init__()

    shards = dict(
        w_in_t=w["w_in"][0].T, w_out=w["w_out"][0], w_q=w["w_q"][0], w_k=w["w_k"][0], w_v=w["w_v"][0],
        w_o=w["w_o"][0], w_gate_t=w["w_gate"][0].T, w_up_t=w["w_up"][0].T, w_down=w["w_down"][0])
    send = {k: shards[k].astype(BF16) for k in _BIG}
    send["pool_w"] = w["pool_w"][0].reshape(4 * 32, POOL_GROUP).astype(BF16)
    W = {}
    for gi, group in enumerate(_GATHER_GROUPS):
        gathered = _seq_all_gather("gather_weights_%d" % gi, [send[k] for k in group], _ID_GATHER)
        for k, g in zip(group, gathered):
            W[k] = g.reshape(-1, g.shape[-1])
    W["pool_w"] = W["pool_w"].reshape(N_DEV, 4, 32, POOL_GROUP).transpose(1, 0, 2, 3).reshape(4, POOL_GROUP, POOL_GROUP)

    t = jnp.arange(SGU_BLOCK)
    mask = (t[None, :] // SGU_CHUNK) <= (t[:, None] // SGU_CHUNK)
    sm = dict(
        norm_mix_g=w["norm_mix_g"], pool_scale=w["pool_scale"], sgu_norm_g=w["sgu_norm_g"],
        norm_xattn_g=w["norm_xattn_g"], norm_mem_g=w["norm_mem_g"], norm_ffn_g=w["norm_ffn_g"],
        final_norm_g=w["final_norm_g"].reshape(1, D_MODEL),
        ws_masked=jnp.where(mask[None], w["w_spatial"][0], 0.0).astype(BF16),
        bias_full=jnp.repeat(w["b_spatial"][0].T, SGU_BLOCK, axis=1))

    natural = dict(w_in_t="w_in", w_gate_t="w_gate", w_up_t="w_up")
    grads, delta, new_m, new_v = {}, {}, {}, {}

    def apply(k, pair, landed):
        name = natural.get(k, k)
        if k == "pool_w":
            flat = (4 * POOL_GROUP // N_DEV, POOL_GROUP)
            res = _chip_sum_adamw("grad_finish_" + k, pair, landed, pos, w[k].reshape(flat), m[k].reshape(flat),
                                  v[k].reshape(flat), False)
            grads[k], delta[k], new_m[k], new_v[k] = (a.reshape(w[k].shape) for a in res)
            return
        if k in _TURN_OUTSIDE:
            res = _chip_sum_adamw("grad_finish_" + k, pair, landed, pos, w[name][0].T, m[name][0].T, v[name][0].T,
                                  False)
            res = [a.T for a in res]
        else:
            res = _chip_sum_adamw("grad_finish_" + k, pair, landed, pos, w[name][0], m[name][0], v[name][0],
                                  k in natural)
        grads[name], delta[name], new_m[name], new_v[name] = (a[None] for a in res)

    like = dict(w)
    like["loss"] = _sds((1, _LANES), F32)

    def apply_small(tag, total):
        group = _SMALL_GROUPS[tag]
        grads.update(zip(group, _unpack(total, [like[k] for k in group])))
        if tag == "late":
            d_, m_, v_ = _adamw("adamw_small", _pack([w[k] for k in _SMALL]), _pack([grads[k] for k in _SMALL]),
                                _pack([m[k] for k in _SMALL]), _pack([v[k] for k in _SMALL]))
            shapes = [w[k] for k in _SMALL]
            for k, a, b, c_ in zip(_SMALL, _unpack(d_, shapes), _unpack(m_, shapes), _unpack(v_, shapes)):
                delta[k], new_m[k], new_v[k] = a, b, c_

    pos = jnp.stack([lax.axis_index("c"), 2 * lax.axis_index("x") + lax.axis_index("y")]).astype(jnp.int32)
    rs = _GradReducer(pos, apply, apply_small)
    _, grad_x = _local_step(x[0], mem[0], loss_target[0], W, sm, rs)

    outs = [grads["loss"][0, 0], grad_x[None]]
    outs += [grads[k].reshape(w[k].shape) for k in names]
    outs += [delta[k] for k in names]
    outs += [new_m[k] for k in names]
    outs += [new_v[k] for k in names]
    return tuple(outs)
```

```python
import functools

import jax
import jax.numpy as jnp
from jax import lax
from jax.experimental import pallas as pl
from jax.experimental.pallas import tpu as pltpu
from jax.experimental.pallas import tpu_sc as plsc

F32 = jnp.float32
BF16 = jnp.bfloat16
MESH = pl.DeviceIdType.MESH

EPS = 1e-6
D_MODEL = 2048
D_POOL = 1024
D_SGU = 1024
POOL_WINDOWS = (2, 4, 8, 16)
POOL_GROUP = 256
POOL_HALO = 16
SGU_BLOCK = 128
SGU_CHUNK = 64
N_SGU_HEADS = 8
N_HEADS = 4
HEAD_DIM = 512
N_DEV = 8

ADAM_LR = 0.001
ADAM_B1 = 0.9
ADAM_B2 = 0.999
ADAM_EPS = 1e-08
ADAM_WD = 0.01
ADAM_STEP = 10

VMEM_LIMIT = 56 * 1024 * 1024


def _cparams(sem=None):
    return pltpu.CompilerParams(dimension_semantics=sem, vmem_limit_bytes=VMEM_LIMIT)


def _sds(shape, dtype):
    return jax.ShapeDtypeStruct(shape, dtype)


_ANY = pl.BlockSpec(memory_space=pl.ANY)


class _Chain:
    def __init__(self):
        self.tc = None
        self.sc = None


_CHAIN = _Chain()


def _first(out):
    return out[0] if isinstance(out, (list, tuple)) else out


def _tc_call(body, *, in_specs=None, grid_spec=None, **kw):
    def run(*args):
        prev, n = _CHAIN.tc, len(args)
        fn, specs, spec, operands = body, in_specs, grid_spec, args
        if prev is not None:
            def fn(*refs):
                return body(*refs[:n], *refs[n + 1:])
            operands = args + (prev,)
            if grid_spec is None:
                specs = list(in_specs) + [_ANY]
            else:
                spec = pltpu.PrefetchScalarGridSpec(
                    num_scalar_prefetch=grid_spec.num_scalar_prefetch, grid=grid_spec.grid,
                    in_specs=list(grid_spec.in_specs) + [_ANY], out_specs=grid_spec.out_specs,
                    scratch_shapes=grid_spec.scratch_shapes)
        if spec is None:
            out = pl.pallas_call(fn, in_specs=specs, **kw)(*operands)
        else:
            out = pl.pallas_call(fn, grid_spec=spec, **kw)(*operands)
        _CHAIN.tc = _first(out)
        return out
    return run


def _sc_call(body, **kw):
    return pl.kernel(body, mesh=plsc.ScalarSubcoreMesh(axis_name="seq", num_cores=1), **kw)


def _rowsum8(v):
    r, c = v.shape
    return v.reshape(r // 8, 8, c).sum(axis=0)


_EPILOGUE_COLS = 256
_DN = {
    "nn": (((1,), (0,)), ((), ())),
    "nt": (((1,), (1,)), ((), ())),
    "tn": (((0,), (0,)), ((), ())),
}


def _mm(name, a_list, b_list, terms, mode, tm, tn, tk, out_dtypes, epilogue=None, extras=(), n_acc=1):
    a0, b0 = a_list[0], b_list[0]
    if mode == "tn":
        K, M = a0.shape
    else:
        M, K = a0.shape
    N = b0.shape[0] if mode == "nt" else b0.shape[1]
    assert M % tm == 0 and N % tn == 0 and K % tk == 0, (name, M, N, K, tm, tn, tk)
    nk = K // tk
    na, nb, ne, no = len(a_list), len(b_list), len(extras), len(out_dtypes)
    dn = _DN[mode]

    if mode == "tn":
        a_spec = pl.BlockSpec((tk, tm), lambda i, j, k: (k, i))
    else:
        a_spec = pl.BlockSpec((tm, tk), lambda i, j, k: (i, k))
    if mode == "nt":
        b_spec = pl.BlockSpec((tn, tk), lambda i, j, k: (j, k))
    else:
        b_spec = pl.BlockSpec((tk, tn), lambda i, j, k: (k, j))
    o_spec = pl.BlockSpec((tm, tn), lambda i, j, k: (i, j))

    def body(*refs):
        a_refs = refs[:na]
        b_refs = refs[na:na + nb]
        e_refs = refs[na + nb:na + nb + ne]
        o_refs = refs[na + nb + ne:na + nb + ne + no]
        acc_refs = refs[na + nb + ne + no:]

        def products(cols):
            parts = [None] * n_acc
            for ai, bi, ci in terms:
                b = b_refs[bi][cols, :] if mode == "nt" else b_refs[bi][:, cols]
                d = lax.dot_general(a_refs[ai][...].astype(BF16), b.astype(BF16), dn, preferred_element_type=F32)
                parts[ci] = d if parts[ci] is None else parts[ci] + d
            return parts

        def finish(accs, cols=slice(None)):
            outs = epilogue(accs, [e[:, cols] for e in e_refs]) if epilogue is not None else accs
            for o_ref, v in zip(o_refs, outs):
                o_ref[:, cols] = v.astype(o_ref.dtype)

        if nk == 1 and epilogue is not None and tn > _EPILOGUE_COLS:
            for c0 in range(0, tn, _EPILOGUE_COLS):
                cols = slice(c0, c0 + _EPILOGUE_COLS)
                finish(products(cols), cols)
            return
        parts = products(slice(None))
        if nk == 1:
            finish(parts)
        else:
            k = pl.program_id(2)

            @pl.when(k == 0)
            def _():
                for c in range(n_acc):
                    acc_refs[c][...] = parts[c]

            @pl.when(k > 0)
            def _():
                for c in range(n_acc):
                    acc_refs[c][...] += parts[c]

            @pl.when(k == nk - 1)
            def _():
                finish([acc_refs[c][...] for c in range(n_acc)])

    scratch = [pltpu.VMEM((tm, tn), F32) for _ in range(n_acc)] if nk > 1 else []
    return _tc_call(
        body, name=name, grid=(M // tm, N // tn, nk),
        in_specs=[a_spec] * na + [b_spec] * nb + [o_spec] * ne,
        out_specs=[o_spec] * no,
        out_shape=[_sds((M, N), dt) for dt in out_dtypes],
        scratch_shapes=scratch,
        compiler_params=_cparams(("parallel", "parallel", "arbitrary")),
    )(*a_list, *b_list, *extras)


def _mm_rows(name, a_list, b_list, terms, mode, tm, tk, rows, vecs, row_dtypes, n_vec_out, epilogue,
             n_scalar_out=0):
    a0, b0 = a_list[0], b_list[0]
    M, K = a0.shape
    N = b0.shape[0] if mode == "nt" else b0.shape[1]
    assert mode in ("nn", "nt") and M % tm == 0 and K % tk == 0, (name, M, N, K, tm, tk)
    nm, nk = M // tm, K // tk
    slab = min(128, tm)
    na, nb, nr, nv, no = len(a_list), len(b_list), len(rows), len(vecs), len(row_dtypes)
    dn = _DN[mode]
    a_spec = pl.BlockSpec((tm, tk), lambda i, k: (i, k))
    b_spec = pl.BlockSpec((N, tk), lambda i, k: (0, k)) if mode == "nt" else pl.BlockSpec((tk, N), lambda i, k: (k, 0))
    row_spec = pl.BlockSpec((tm, N), lambda i, k: (i, 0))
    vec_spec = pl.BlockSpec((1, N), lambda i, k: (0, 0))
    one_spec = pl.BlockSpec((1, 1), lambda i, k: (0, 0))

    def body(*refs):
        pos = 0
        a_refs = refs[pos:pos + na]; pos += na
        b_refs = refs[pos:pos + nb]; pos += nb
        r_refs = refs[pos:pos + nr]; pos += nr
        v_refs = refs[pos:pos + nv]; pos += nv
        o_refs = refs[pos:pos + no]; pos += no
        s_refs = refs[pos:pos + n_vec_out]; pos += n_vec_out
        vacc_refs = refs[pos:pos + n_vec_out]; pos += n_vec_out
        acc_ref = refs[pos] if nk > 1 else None
        i, k = pl.program_id(0), pl.program_id(1)
        part = None
        for ai, bi, _ in terms:
            d = lax.dot_general(a_refs[ai][...].astype(BF16), b_refs[bi][...].astype(BF16), dn,
                                preferred_element_type=F32)
            part = d if part is None else part + d

        def finish(acc):
            vecs_now = [v[...] for v in v_refs]
            vparts = None
            for r0 in range(0, tm, slab):
                rs_ = slice(r0, r0 + slab)
                outs, vp = epilogue(acc[rs_, :], [r[rs_, :] for r in r_refs], vecs_now)
                for o_ref, val in zip(o_refs, outs):
                    o_ref[rs_, :] = val.astype(o_ref.dtype)
                vparts = vp if vparts is None else [a + b for a, b in zip(vparts, vp)]

            @pl.when(i == 0)
            def _():
                for vacc, vp in zip(vacc_refs, vparts):
                    vacc[...] = vp

            @pl.when(i > 0)
            def _():
                for vacc, vp in zip(vacc_refs, vparts):
                    vacc[...] += vp

            @pl.when(i == nm - 1)
            def _():
                for j, (s_ref, vacc) in enumerate(zip(s_refs, vacc_refs)):
                    col = jnp.sum(vacc[...], axis=0, keepdims=True)
                    s_ref[...] = jnp.sum(col, axis=1, keepdims=True) if j >= n_vec_out - n_scalar_out else col

        if nk == 1:
            finish(part)
        else:
            @pl.when(k == 0)
            def _():
                acc_ref[...] = part

            @pl.when(k > 0)
            def _():
                acc_ref[...] += part

            @pl.when(k == nk - 1)
            def _():
                finish(acc_ref)

    n_plain = n_vec_out - n_scalar_out
    return _tc_call(
        body, name=name, grid=(nm, nk),
        in_specs=[a_spec] * na + [b_spec] * nb + [row_spec] * nr + [vec_spec] * nv,
        out_specs=[row_spec] * no + [vec_spec] * n_plain + [one_spec] * n_scalar_out,
        out_shape=[_sds((M, N), dt) for dt in row_dtypes] + [_sds((1, N), F32)] * n_plain
        + [_sds((1, 1), F32)] * n_scalar_out,
        scratch_shapes=[pltpu.VMEM((8, N), F32)] * n_vec_out + ([pltpu.VMEM((tm, N), F32)] if nk > 1 else []),
        compiler_params=_cparams(("arbitrary", "arbitrary")),
    )(*a_list, *b_list, *rows, *vecs)


def _ep_residual_norm(acc, rows, vecs):
    x_new = rows[0] + acc
    r = lax.rsqrt(jnp.mean(x_new * x_new, axis=-1, keepdims=True) + EPS)
    return [x_new, x_new * r * vecs[0]], []


def _ep_norm_bwd(acc, rows, vecs):
    xv, dres = rows
    r = lax.rsqrt(jnp.mean(xv * xv, axis=-1, keepdims=True) + EPS)
    xh = xv * r
    dxh = acc * vecs[0]
    m = jnp.mean(dxh * xh, axis=-1, keepdims=True)
    dx = dres + r * (dxh - xh * m)
    return [dx, dx], [_rowsum8(acc * xh)]


def _ep_final_loss(acc, rows, vecs):
    x2, target = rows
    gv = vecs[0]
    xv = x2 + acc
    inv_d = 1.0 / xv.shape[-1]
    r = lax.rsqrt(jnp.mean(xv * xv, axis=-1, keepdims=True) + EPS)
    xh = xv * r
    e = xh * gv - target
    dy = e * inv_d
    dxh = dy * gv
    m = jnp.mean(dxh * xh, axis=-1, keepdims=True)
    dx = r * (dxh - xh * m)
    return [dx, dx], [_rowsum8(dy * xh), _rowsum8(e * e) * (0.5 * inv_d)]


def _mm1(name, a, b, mode, tm, tn, tk, out_dtype, **kw):
    return _mm(name, [a], [b], [(0, 0, 0)], mode, tm, tn, tk, [out_dtype], **kw)[0]


def _rms_fwd(name, x, g, tr):
    S, Dm = x.shape

    def body(x_ref, g_ref, h_ref):
        xv = x_ref[...]
        r = lax.rsqrt(jnp.mean(xv * xv, axis=-1, keepdims=True) + EPS)
        h_ref[...] = (xv * r * g_ref[...]).astype(h_ref.dtype)

    return _tc_call(
        body, name=name, grid=(S // tr,),
        in_specs=[pl.BlockSpec((tr, Dm), lambda i: (i, 0)), pl.BlockSpec((1, Dm), lambda i: (0, 0))],
        out_specs=pl.BlockSpec((tr, Dm), lambda i: (i, 0)),
        out_shape=_sds((S, Dm), BF16),
        compiler_params=_cparams(("parallel",)),
    )(x, g)


def _rms_bwd(name, dh, x, g, dres, tr, want_dx=True):
    S, Dm = x.shape
    nsteps = S // tr

    def body(*refs):
        if want_dx:
            dh_ref, x_ref, g_ref, dres_ref, dx_ref, dxb_ref, dg_ref, acc_ref = refs
        else:
            dh_ref, x_ref, g_ref, dg_ref, acc_ref = refs
        i = pl.program_id(0)
        xv = x_ref[...]
        r = lax.rsqrt(jnp.mean(xv * xv, axis=-1, keepdims=True) + EPS)
        xh = xv * r
        dhv = dh_ref[...]
        part = _rowsum8(dhv * xh)

        @pl.when(i == 0)
        def _():
            acc_ref[...] = part

        @pl.when(i > 0)
        def _():
            acc_ref[...] += part

        @pl.when(i == nsteps - 1)
        def _():
            dg_ref[...] = jnp.sum(acc_ref[...], axis=0, keepdims=True)

        if want_dx:
            dxh = dhv * g_ref[...]
            m = jnp.mean(dxh * xh, axis=-1, keepdims=True)
            dx = dres_ref[...] + r * (dxh - xh * m)
            dx_ref[...] = dx
            dxb_ref[...] = dx.astype(BF16)

    row = pl.BlockSpec((tr, Dm), lambda i: (i, 0))
    vec = pl.BlockSpec((1, Dm), lambda i: (0, 0))
    if want_dx:
        in_specs = [row, row, vec, row]
        out_specs = [row, row, vec]
        out_shape = [_sds((S, Dm), F32), _sds((S, Dm), BF16), _sds((1, Dm), F32)]
        args = (dh, x, g, dres)
    else:
        in_specs = [row, row, vec]
        out_specs = [vec]
        out_shape = [_sds((1, Dm), F32)]
        args = (dh, x, g)
    return _tc_call(
        body, name=name, grid=(nsteps,), in_specs=in_specs, out_specs=out_specs, out_shape=out_shape,
        scratch_shapes=[pltpu.VMEM((8, Dm), F32)],
        compiler_params=_cparams(("arbitrary",)),
    )(*args)


def _final_loss(name, x3, g, target, tr):
    S, Dm = x3.shape
    nsteps = S // tr

    def body(x_ref, g_ref, t_ref, dx_ref, dxb_ref, dg_ref, loss_ref, acc_g, acc_l):
        i = pl.program_id(0)
        xv = x_ref[...]
        gv = g_ref[...]
        r = lax.rsqrt(jnp.mean(xv * xv, axis=-1, keepdims=True) + EPS)
        xh = xv * r
        e = xh * gv - t_ref[...]
        dy = e * (1.0 / Dm)
        lpart = _rowsum8(e * e)
        gpart = _rowsum8(dy * xh)

        @pl.when(i == 0)
        def _():
            acc_g[...] = gpart
            acc_l[...] = lpart

        @pl.when(i > 0)
        def _():
            acc_g[...] += gpart
            acc_l[...] += lpart

        @pl.when(i == nsteps - 1)
        def _():
            dg_ref[...] = jnp.sum(acc_g[...], axis=0, keepdims=True)
            tot = jnp.sum(jnp.sum(acc_l[...], axis=1, keepdims=True), axis=0, keepdims=True)
            loss_ref[...] = tot * (0.5 / Dm)

        dxh = dy * gv
        m = jnp.mean(dxh * xh, axis=-1, keepdims=True)
        dx = r * (dxh - xh * m)
        dx_ref[...] = dx
        dxb_ref[...] = dx.astype(BF16)

    row = pl.BlockSpec((tr, Dm), lambda i: (i, 0))
    vec = pl.BlockSpec((1, Dm), lambda i: (0, 0))
    return _tc_call(
        body, name=name, grid=(nsteps,),
        in_specs=[row, vec, row],
        out_specs=[row, row, vec, pl.BlockSpec((1, 1), lambda i: (0, 0))],
        out_shape=[_sds((S, Dm), F32), _sds((S, Dm), BF16), _sds((1, Dm), F32), _sds((1, 1), F32)],
        scratch_shapes=[pltpu.VMEM((8, Dm), F32), pltpu.VMEM((8, Dm), F32)],
        compiler_params=_cparams(("arbitrary",)),
    )(x3, g, target)


def _softmax_rows(s):
    e = jnp.exp(s - jnp.max(s, axis=-1, keepdims=True))
    return e / jnp.sum(e, axis=-1, keepdims=True)


def _attn_fwd(name, q, k, v, ts):
    S, Dm = q.shape
    M = k.shape[0]
    scale = HEAD_DIM ** -0.5

    def body(q_ref, k_ref, v_ref, o_ref):
        for h in range(N_HEADS):
            sl = slice(h * HEAD_DIM, (h + 1) * HEAD_DIM)
            s = lax.dot_general(q_ref[:, sl], k_ref[:, sl], _DN["nt"], preferred_element_type=F32) * scale
            p = _softmax_rows(s)
            o_ref[:, sl] = jnp.dot(p.astype(BF16), v_ref[:, sl], preferred_element_type=F32).astype(o_ref.dtype)

    row = pl.BlockSpec((ts, Dm), lambda i: (i, 0))
    mem = pl.BlockSpec((M, Dm), lambda i: (0, 0))
    return _tc_call(
        body, name=name, grid=(S // ts,), in_specs=[row, mem, mem], out_specs=row,
        out_shape=_sds((S, Dm), BF16), compiler_params=_cparams(("parallel",)),
    )(q, k, v)


def _attn_bwd(name, q, k, v, do, ts):
    S, Dm = q.shape
    M = k.shape[0]
    scale = HEAD_DIM ** -0.5

    def body(q_ref, k_ref, v_ref, do_ref, dq_ref, dk_ref, dv_ref):
        i = pl.program_id(0)

        @pl.when(i == 0)
        def _():
            dk_ref[...] = jnp.zeros_like(dk_ref)
            dv_ref[...] = jnp.zeros_like(dv_ref)

        for h in range(N_HEADS):
            sl = slice(h * HEAD_DIM, (h + 1) * HEAD_DIM)
            qh = q_ref[:, sl]
            kh = k_ref[:, sl]
            doh = do_ref[:, sl]
            s = lax.dot_general(qh, kh, _DN["nt"], preferred_element_type=F32) * scale
            p = _softmax_rows(s)
            dp = lax.dot_general(doh, v_ref[:, sl], _DN["nt"], preferred_element_type=F32)
            ds = p * (dp - jnp.sum(dp * p, axis=-1, keepdims=True)) * scale
            dsb = ds.astype(BF16)
            dq_ref[:, sl] = jnp.dot(dsb, kh, preferred_element_type=F32).astype(dq_ref.dtype)
            dk_ref[:, sl] += lax.dot_general(dsb, qh, _DN["tn"], preferred_element_type=F32)
            dv_ref[:, sl] += lax.dot_general(p.astype(BF16), doh, _DN["tn"], preferred_element_type=F32)

    row = pl.BlockSpec((ts, Dm), lambda i: (i, 0))
    mem = pl.BlockSpec((M, Dm), lambda i: (0, 0))
    return _tc_call(
        body, name=name, grid=(S // ts,), in_specs=[row, mem, mem, row], out_specs=[row, mem, mem],
        out_shape=[_sds((S, Dm), BF16), _sds((M, Dm), F32), _sds((M, Dm), F32)],
        compiler_params=_cparams(("arbitrary",)),
    )(q, k, v, do)


def _pool_denominators(row0, ts):
    return (row0 + lax.broadcasted_iota(jnp.int32, (ts, 1), 0) + 1).astype(F32)


def _mixer_fwd(name, proj, pool_w, pool_scale, sgu_g, ws, bias_full, ts):
    S = proj.shape[0]
    nblk = ts // SGU_BLOCK
    halo_blocks = ts // POOL_HALO

    def body(proj_ref, halo_ref, pw_ref, sc_ref, g_ref, ws_ref, b_ref, y_ref, p_ref, vn_ref, ext_ref):
        i = pl.program_id(0)
        a = proj_ref[:, 0:D_POOL]
        ext_ref[0:POOL_HALO, :] = jnp.where(i > 0, halo_ref[...], 0.0)
        ext_ref[POOL_HALO:POOL_HALO + ts, :] = a
        pos = _pool_denominators(i * ts, ts)
        for gi, w in enumerate(POOL_WINDOWS):
            cs = slice(gi * POOL_GROUP, (gi + 1) * POOL_GROUP)
            acc = a[:, cs]
            for j in range(1, w):
                acc = acc + ext_ref[POOL_HALO - j:POOL_HALO - j + ts, cs]
            pg = (acc / jnp.minimum(pos, float(w)) - a[:, cs]).astype(BF16)
            p_ref[:, cs] = pg
            ypre = jnp.dot(pg, pw_ref[gi], preferred_element_type=F32)
            y_ref[:, cs] = (ypre * sc_ref[:, cs]).astype(y_ref.dtype)

        v = proj_ref[:, D_POOL + D_SGU:D_POOL + 2 * D_SGU]
        r = lax.rsqrt(jnp.mean(v * v, axis=-1, keepdims=True) + EPS)
        vn_ref[...] = (v * r * g_ref[...]).astype(BF16)
        for n in range(nblk):
            rs = slice(n * SGU_BLOCK, (n + 1) * SGU_BLOCK)
            for h in range(N_SGU_HEADS):
                cs = slice(h * SGU_BLOCK, (h + 1) * SGU_BLOCK)
                mixed = jnp.dot(ws_ref[h], vn_ref[rs, cs], preferred_element_type=F32) + b_ref[:, cs]
                u = proj_ref[rs, D_POOL + h * SGU_BLOCK:D_POOL + (h + 1) * SGU_BLOCK]
                y_ref[rs, D_POOL + h * SGU_BLOCK:D_POOL + (h + 1) * SGU_BLOCK] = (u * mixed).astype(y_ref.dtype)

    return _tc_call(
        body, name=name, grid=(S // ts,),
        in_specs=[
            pl.BlockSpec((ts, D_POOL + 2 * D_SGU), lambda i: (i, 0)),
            pl.BlockSpec((POOL_HALO, D_POOL), lambda i: (jnp.maximum(i * halo_blocks - 1, 0), 0)),
            pl.BlockSpec((4, POOL_GROUP, POOL_GROUP), lambda i: (0, 0, 0)),
            pl.BlockSpec((1, D_POOL), lambda i: (0, 0)),
            pl.BlockSpec((1, D_SGU), lambda i: (0, 0)),
            pl.BlockSpec((N_SGU_HEADS, SGU_BLOCK, SGU_BLOCK), lambda i: (0, 0, 0)),
            pl.BlockSpec((SGU_BLOCK, D_SGU), lambda i: (0, 0)),
        ],
        out_specs=[
            pl.BlockSpec((ts, D_MODEL), lambda i: (i, 0)),
            pl.BlockSpec((ts, D_POOL), lambda i: (i, 0)),
            pl.BlockSpec((ts, D_SGU), lambda i: (i, 0)),
        ],
        out_shape=[_sds((S, D_MODEL), BF16), _sds((S, D_POOL), BF16), _sds((S, D_SGU), BF16)],
        scratch_shapes=[pltpu.VMEM((ts + POOL_HALO, D_POOL), F32)],
        compiler_params=_cparams(("parallel",)),
    )(proj, proj, pool_w, pool_scale, sgu_g, ws, bias_full)


def _mixer_bwd(name, dymix, proj, p, vn, pool_w, pool_scale, sgu_g, ws, bias_full, ts):
    S = proj.shape[0]
    nsteps = S // ts
    nblk = ts // SGU_BLOCK
    halo_blocks = ts // POOL_HALO

    def body(dy_ref, dyh_ref, u_ref, v_ref, p_ref, vn_ref, pw_ref, sc_ref, g_ref, ws_ref, b_ref,
             dproj_ref, dpw_ref, dsc_ref, dg_ref, dws_ref, db_ref,
             ext_ref, dvn_ref, acc_sc, acc_g, acc_b):
        i = pl.program_id(0)

        @pl.when(i == 0)
        def _():
            dpw_ref[...] = jnp.zeros_like(dpw_ref)
            dws_ref[...] = jnp.zeros_like(dws_ref)
            acc_sc[...] = jnp.zeros_like(acc_sc)
            acc_g[...] = jnp.zeros_like(acc_g)
            acc_b[...] = jnp.zeros_like(acc_b)

        pos = _pool_denominators(i * ts, ts)
        pos_h = _pool_denominators((i + 1) * ts, POOL_HALO)
        for gi, w in enumerate(POOL_WINDOWS):
            cs = slice(gi * POOL_GROUP, (gi + 1) * POOL_GROUP)
            pg = p_ref[:, cs]
            wg = pw_ref[gi]
            dyp = dy_ref[:, cs]
            ypre = jnp.dot(pg, wg, preferred_element_type=F32)
            acc_sc[:, cs] += _rowsum8(dyp * ypre)
            dz = (dyp * sc_ref[:, cs]).astype(BF16)
            dpw_ref[gi] += lax.dot_general(pg, dz, _DN["tn"], preferred_element_type=F32)
            dp = lax.dot_general(dz, wg, _DN["nt"], preferred_element_type=F32)
            dzh = (dyh_ref[:, cs] * sc_ref[:, cs]).astype(BF16)
            dph = lax.dot_general(dzh, wg, _DN["nt"], preferred_element_type=F32)
            ext_ref[0:ts, cs] = dp / jnp.minimum(pos, float(w))
            ext_ref[ts:ts + POOL_HALO, cs] = jnp.where(i < nsteps - 1, dph / jnp.minimum(pos_h, float(w)), 0.0)
            acc = ext_ref[0:ts, cs]
            for j in range(1, w):
                acc = acc + ext_ref[j:j + ts, cs]
            dproj_ref[:, cs] = (acc - dp).astype(dproj_ref.dtype)

        for n in range(nblk):
            rs = slice(n * SGU_BLOCK, (n + 1) * SGU_BLOCK)
            for h in range(N_SGU_HEADS):
                cs = slice(h * SGU_BLOCK, (h + 1) * SGU_BLOCK)
                vnb = vn_ref[rs, cs]
                wh = ws_ref[h]
                mixed = jnp.dot(wh, vnb, preferred_element_type=F32) + b_ref[:, cs]
                dys = dy_ref[rs, D_POOL + h * SGU_BLOCK:D_POOL + (h + 1) * SGU_BLOCK]
                dproj_ref[rs, D_POOL + h * SGU_BLOCK:D_POOL + (h + 1) * SGU_BLOCK] = (dys * mixed).astype(dproj_ref.dtype)
                dmix = dys * u_ref[rs, cs]
                acc_b[:, cs] += dmix
                dmb = dmix.astype(BF16)
                dws_ref[h] += lax.dot_general(dmb, vnb, _DN["nt"], preferred_element_type=F32)
                dvn_ref[rs, cs] = lax.dot_general(wh, dmb, _DN["tn"], preferred_element_type=F32)
        v = v_ref[...]
        r = lax.rsqrt(jnp.mean(v * v, axis=-1, keepdims=True) + EPS)
        vh = v * r
        dvn = dvn_ref[...]
        acc_g[...] += _rowsum8(dvn * vh)
        dxh = dvn * g_ref[...]
        m = jnp.mean(dxh * vh, axis=-1, keepdims=True)
        dproj_ref[:, D_POOL + D_SGU:D_POOL + 2 * D_SGU] = (r * (dxh - vh * m)).astype(dproj_ref.dtype)

        @pl.when(i == nsteps - 1)
        def _():
            dsc_ref[...] = jnp.sum(acc_sc[...], axis=0, keepdims=True)
            dg_ref[...] = jnp.sum(acc_g[...], axis=0, keepdims=True)
            t_idx = lax.broadcasted_iota(jnp.int32, (SGU_BLOCK, SGU_BLOCK), 0) // SGU_CHUNK
            s_idx = lax.broadcasted_iota(jnp.int32, (SGU_BLOCK, SGU_BLOCK), 1) // SGU_CHUNK
            mask = s_idx <= t_idx
            for h in range(N_SGU_HEADS):
                cs = slice(h * SGU_BLOCK, (h + 1) * SGU_BLOCK)
                dws_ref[h] = jnp.where(mask, dws_ref[h], 0.0)
                col = jnp.sum(acc_b[:, cs], axis=1, keepdims=True)
                db_ref[h] = jnp.broadcast_to(col, (SGU_BLOCK, SGU_BLOCK))

    const2 = lambda i: (0, 0)
    const3 = lambda i: (0, 0, 0)
    last_halo = S // POOL_HALO - 1
    return _tc_call(
        body, name=name, grid=(nsteps,),
        in_specs=[
            pl.BlockSpec((ts, D_MODEL), lambda i: (i, 0)),
            pl.BlockSpec((POOL_HALO, D_POOL), lambda i: (jnp.minimum((i + 1) * halo_blocks, last_halo), 0)),
            pl.BlockSpec((ts, D_SGU), lambda i: (i, 1)),
            pl.BlockSpec((ts, D_SGU), lambda i: (i, 2)),
            pl.BlockSpec((ts, D_POOL), lambda i: (i, 0)),
            pl.BlockSpec((ts, D_SGU), lambda i: (i, 0)),
            pl.BlockSpec((4, POOL_GROUP, POOL_GROUP), const3),
            pl.BlockSpec((1, D_POOL), const2),
            pl.BlockSpec((1, D_SGU), const2),
            pl.BlockSpec((N_SGU_HEADS, SGU_BLOCK, SGU_BLOCK), const3),
            pl.BlockSpec((SGU_BLOCK, D_SGU), const2),
        ],
        out_specs=[
            pl.BlockSpec((ts, D_POOL + 2 * D_SGU), lambda i: (i, 0)),
            pl.BlockSpec((4, POOL_GROUP, POOL_GROUP), const3),
            pl.BlockSpec((1, D_POOL), const2),
            pl.BlockSpec((1, D_SGU), const2),
            pl.BlockSpec((N_SGU_HEADS, SGU_BLOCK, SGU_BLOCK), const3),
            pl.BlockSpec((N_SGU_HEADS, SGU_BLOCK, SGU_BLOCK), const3),
        ],
        out_shape=[
            _sds((S, D_POOL + 2 * D_SGU), BF16),
            _sds((4, POOL_GROUP, POOL_GROUP), F32),
            _sds((1, D_POOL), F32),
            _sds((1, D_SGU), F32),
            _sds((N_SGU_HEADS, SGU_BLOCK, SGU_BLOCK), F32),
            _sds((N_SGU_HEADS, SGU_BLOCK, SGU_BLOCK), F32),
        ],
        scratch_shapes=[
            pltpu.VMEM((ts + POOL_HALO, D_POOL), F32),
            pltpu.VMEM((ts, D_SGU), F32),
            pltpu.VMEM((8, D_POOL), F32),
            pltpu.VMEM((8, D_SGU), F32),
            pltpu.VMEM((SGU_BLOCK, D_SGU), F32),
        ],
        compiler_params=_cparams(("arbitrary",)),
    )(dymix, dymix, proj, proj, p, vn, pool_w, pool_scale, sgu_g, ws, bias_full)


def _silu_mul(accs, extras):
    (up,) = accs
    gt = extras[0]
    sig = 1.0 / (1.0 + jnp.exp(-gt))
    return gt, up, gt * sig * up


def _silu_mul_bwd(accs, extras):
    (dact,) = accs
    gt = extras[0].astype(F32)
    up = extras[1].astype(F32)
    sig = 1.0 / (1.0 + jnp.exp(-gt))
    silu = gt * sig
    dgt = dact * up * (sig * (1.0 + gt * (1.0 - sig)))
    dup = dact * silu
    return dgt, dup


def _add_residual(accs, extras):
    return (extras[0] + accs[0],)


def _add_residual_and_cast(accs, extras):
    y = extras[0] + accs[0]
    return y, y


def _local_step(x, mem, target, W, sm, rs):
    S = x.shape[0]
    tm = min(1024, S)
    th = min(512, S)
    tq = min(256, S)
    ts = min(512, S)
    tr = min(512, S)
    tk_s = min(2048, S)
    M = mem.shape[0]

    h1 = _rms_fwd("rms_mix", x, sm["norm_mix_g"], tr)
    proj = _mm1("proj_in", h1, W["w_in_t"], "nt", tm, 1024, 2048, F32)
    ymix, p, vn = _mixer_fwd("mixer_fwd", proj, W["pool_w"], sm["pool_scale"], sm["sgu_norm_g"],
                             sm["ws_masked"], sm["bias_full"], ts)
    x1, h2 = _mm_rows("proj_out", [ymix], [W["w_out"]], [(0, 0, 0)], "nn", tq, 2048, [x], [sm["norm_xattn_g"]],
                      [F32, BF16], 0, _ep_residual_norm)

    mb = _rms_fwd("rms_mem", mem, sm["norm_mem_g"], M)
    q = _mm1("proj_q", h2, W["w_q"], "nn", tm, 1024, 2048, BF16)
    kk, vv = _mm("proj_kv", [mb], [W["w_k"], W["w_v"]], [(0, 0, 0), (0, 1, 1)], "nn", M, 1024, 2048, [BF16, BF16],
                 n_acc=2)
    o = _attn_fwd("attn_fwd", q, kk, vv, tm)
    x2, h3 = _mm_rows("proj_o", [o], [W["w_o"]], [(0, 0, 0)], "nn", tq, 2048, [x1], [sm["norm_ffn_g"]],
                      [F32, BF16], 0, _ep_residual_norm)

    gt32 = _mm1("ffn_gate", h3, W["w_gate_t"], "nt", tm, 1408, 2048, F32)
    gt, up, act = _mm("ffn_up", [h3], [W["w_up_t"]], [(0, 0, 0)], "nt", tm, 512, 2048, [BF16, BF16, BF16],
                      epilogue=_silu_mul, extras=(gt32,))
    x3 = _mm1("ffn_down", act, W["w_down"], "nn", th, 1024, 5632, F32, epilogue=_add_residual, extras=(x2,))
    dx3, dx3b, d_final_g, loss = _final_loss("final_loss", x3, sm["final_norm_g"], target, tr)

    dgt, dup = _mm("ffn_down_dgrad", [dx3b], [W["w_down"]], [(0, 0, 0)], "nt", tm, 512, 2048, [BF16, BF16],
                   epilogue=_silu_mul_bwd, extras=(gt, up))
    rs.push("w_down", _mm1("ffn_down_wgrad", act, dx3b, "tn", 1408, 1024, tk_s, BF16))
    rs.push("w_gate_t", _mm1("ffn_gate_wgrad", dgt, h3, "tn", 1408, 1024, tk_s, BF16))
    rs.reduce("w_down")
    rs.push("w_up_t", _mm1("ffn_up_wgrad", dup, h3, "tn", 1408, 1024, tk_s, BF16))
    rs.reduce("w_gate_t")
    dh3 = _mm("ffn_gate_up_dgrad", [dgt, dup], [W["w_gate_t"], W["w_up_t"]], [(0, 0, 0), (1, 1, 0)], "nn",
              th, 512, 5632, [F32])[0]
    rs.reduce("w_up_t")
    dx2, dx2b, d_ffn_g = _rms_bwd("rms_ffn_bwd", dh3, x2, sm["norm_ffn_g"], dx3, tr)
    rs.finish("w_down")

    rs.push("w_o", _mm1("proj_o_wgrad", o, dx2b, "tn", 1024, 1024, tk_s, BF16))
    rs.finish("w_gate_t")
    do = _mm1("proj_o_dgrad", dx2b, W["w_o"], "nt", tm, 1024, 2048, BF16)
    rs.reduce("w_o")
    dq, dk, dv = _attn_bwd("attn_bwd", q, kk, vv, do, tm)
    rs.push("w_q", _mm1("proj_q_wgrad", h2, dq, "tn", 1024, 1024, tk_s, BF16))
    rs.push("w_k", _mm1("proj_k_wgrad", mb, dk, "tn", 1024, 1024, M, BF16))
    rs.push("w_v", _mm1("proj_v_wgrad", mb, dv, "tn", 1024, 1024, M, BF16))
    rs.finish("w_up_t")
    dx1, dx1b, d_xattn_g = _mm_rows(
        "proj_q_dgrad", [dq], [W["w_q"]], [(0, 0, 0)], "nt", tq, 2048, [x1, dx2], [sm["norm_xattn_g"]],
        [F32, BF16], 1, _ep_norm_bwd)
    rs.reduce("w_q")
    rs.reduce("w_k")
    rs.reduce("w_v")
    dmb = _mm("proj_kv_dgrad", [dk, dv], [W["w_k"], W["w_v"]], [(0, 0, 0), (1, 1, 0)], "nt",
              M, 1024, 2048, [F32])[0]
    (d_mem_g,) = _rms_bwd("rms_mem_bwd", dmb, mem, sm["norm_mem_g"], None, M, want_dx=False)

    rs.push("w_out", _mm1("proj_out_wgrad", ymix, dx1b, "tn", 1024, 1024, tk_s, BF16))
    dymix = _mm1("proj_out_dgrad", dx1b, W["w_out"], "nt", tm, 1024, 2048, F32)
    rs.finish("w_o")
    dproj, d_pool_w, d_pool_scale, d_sgu_g, d_ws, d_b = _mixer_bwd(
        "mixer_bwd", dymix, proj, p, vn, W["pool_w"], sm["pool_scale"], sm["sgu_norm_g"],
        sm["ws_masked"], sm["bias_full"], ts)
    rs.finish("w_q")
    rs.finish("w_k")
    rs.finish("w_v")
    rs.reduce("w_out")
    rs.push("pool_w", d_pool_w.reshape(4, N_DEV, POOL_GROUP // N_DEV, POOL_GROUP).transpose(1, 0, 2, 3)
            .reshape(4 * POOL_GROUP, POOL_GROUP).astype(BF16))
    rs.small("early", dict(
        pool_scale=d_pool_scale, sgu_norm_g=d_sgu_g, w_spatial=d_ws, b_spatial=d_b[:, :, 0],
        norm_xattn_g=d_xattn_g, norm_mem_g=d_mem_g, norm_ffn_g=d_ffn_g, final_norm_g=d_final_g))
    rs.push("w_in_t", _mm1("proj_in_wgrad", dproj, h1, "tn", 1024, 1024, tk_s, BF16))
    rs.finish("w_out")
    rs.reduce("pool_w")
    rs.reduce("w_in_t")
    grad_x, d_mix_g = _mm_rows(
        "proj_in_dgrad", [dproj], [W["w_in_t"]], [(0, 0, 0)], "nn", tq, 3072, [x, dx1], [sm["norm_mix_g"]],
        [F32], 1, _ep_norm_bwd)
    rs.small("late", dict(norm_mix_g=d_mix_g, loss=jnp.pad(loss, ((0, 0), (0, _LANES - 1)))))
    rs.finish_small("early")
    rs.finish("pool_w")
    rs.finish("w_in_t")
    rs.finish_small("late")
    return loss, grad_x


def _mesh_pos():
    return lax.axis_index("x"), lax.axis_index("y"), lax.axis_index("c")


def _handshake(peers):
    barrier = pltpu.get_barrier_semaphore()
    for peer in peers:
        pl.semaphore_signal(barrier, inc=1, device_id=peer, device_id_type=MESH)
    pl.semaphore_wait(barrier, len(peers))


def _seq_all_gather(name, shards, collective_id):
    n = len(shards)

    def body(*refs):
        ins = refs[:n]
        outs = refs[n:2 * n]
        send_sems, recv_sems, local_sems = refs[2 * n:]
        x, y, c = _mesh_pos()
        me, sibling = (x, y, c), (x, y, 1 - c)
        xn, yn, dg = (1 - x, y), (x, 1 - y), (1 - x, 1 - y)
        north = c == 1
        via = (jnp.where(north, xn[0], yn[0]), jnp.where(north, xn[1], yn[1]))
        to = (jnp.where(north, yn[0], xn[0]), jnp.where(north, yn[1], xn[1]))
        _handshake([sibling, (*xn, c), (*yn, c)])

        def copy(a, k, block, target, src=None):
            bx, by, bc = block
            dst = outs[a].at[4 * bx + 2 * by + bc]
            return pltpu.make_async_remote_copy(
                src_ref=dst if src is None else src, dst_ref=dst,
                send_sem=send_sems.at[a, k], recv_sem=recv_sems.at[a, k],
                device_id=target, device_id_type=MESH)

        mine = [pltpu.make_async_copy(ins[a], outs[a].at[4 * x + 2 * y + c], local_sems.at[a]) for a in range(n)]
        for cp in mine:
            cp.start()
        started = []
        for a in range(n):
            first = [copy(a, 0, me, sibling, src=ins[a]), copy(a, 1, me, (*xn, c), src=ins[a]),
                     copy(a, 2, me, (*yn, c), src=ins[a])]
            for cp in first:
                cp.start()
            started += first
        for a in range(n):
            copy(a, 1, (*xn, c), me).wait_recv()
            copy(a, 2, (*yn, c), me).wait_recv()
            second = [copy(a, 3, (*via, c), (*to, c)), copy(a, 4, (*xn, c), sibling), copy(a, 5, (*yn, c), sibling)]
            for cp in second:
                cp.start()
            started += second
        for a in range(n):
            copy(a, 3, (*dg, c), me).wait_recv()
            last = copy(a, 6, (*dg, c), sibling)
            last.start()
            started.append(last)
        for a in range(n):
            copy(a, 0, sibling, me).wait_recv()
            for k, chip in ((4, xn), (5, yn), (6, dg)):
                copy(a, k, (*chip, 1 - c), me).wait_recv()
        for cp in started:
            cp.wait_send()
        for cp in mine:
            cp.wait()

    return _sc_call(
        body, name=name,
        out_type=[_sds((N_DEV,) + s.shape, s.dtype) for s in shards],
        scratch_types=[pltpu.SemaphoreType.DMA((n, 7)), pltpu.SemaphoreType.DMA((n, 7)),
                       pltpu.SemaphoreType.DMA((n,))],
        compiler_params=pltpu.CompilerParams(collective_id=collective_id),
    )(*shards)


def _seq_pair_exchange(name, gview, collective_id):
    def body(g_ref, theirs_ref, send_sems, recv_sems):
        x, y, c = _mesh_pos()
        sibling = (x, y, 1 - c)
        _handshake([sibling])
        copies = [pltpu.make_async_remote_copy(
            src_ref=g_ref.at[k, 1 - c], dst_ref=theirs_ref.at[k],
            send_sem=send_sems.at[k], recv_sem=recv_sems.at[k],
            device_id=sibling, device_id_type=MESH) for k in range(4)]
        for cp in copies:
            cp.start()
        for cp in copies:
            cp.wait()

    return _sc_call(
        body, name=name, out_type=_sds((4,) + gview.shape[2:], gview.dtype),
        scratch_types=[pltpu.SemaphoreType.DMA((4,)), pltpu.SemaphoreType.DMA((4,))],
        compiler_params=pltpu.CompilerParams(collective_id=collective_id),
    )(gview)


def _pair_sum(name, gview, theirs, pos, tr):
    _, _, r, C = gview.shape

    def body(pos_ref, a_ref, b_ref, o_ref):
        o_ref[...] = (a_ref[...].astype(F32) + b_ref[...].astype(F32)).astype(o_ref.dtype)

    grid_spec = pltpu.PrefetchScalarGridSpec(
        num_scalar_prefetch=1, grid=(4, r // tr),
        in_specs=[pl.BlockSpec((None, None, tr, C), lambda k, t, pos_ref: (k, pos_ref[0], t, 0)),
                  pl.BlockSpec((None, tr, C), lambda k, t, pos_ref: (k, t, 0))],
        out_specs=pl.BlockSpec((None, tr, C), lambda k, t, pos_ref: (k, t, 0)))
    return _tc_call(
        body, name=name, grid_spec=grid_spec, out_shape=_sds(theirs.shape, theirs.dtype),
        compiler_params=_cparams(("parallel", "parallel")),
    )(pos, gview, theirs)


def _seq_chip_exchange(name, pair, collective_id):
    def body(p_ref, land_ref, send_sems, recv_sems):
        x, y, c = _mesh_pos()
        my_chip = 2 * x + y
        chips = [(1 - x, y), (x, 1 - y), (1 - x, 1 - y)]
        _handshake([(cx, cy, c) for cx, cy in chips])
        copies = [pltpu.make_async_remote_copy(
            src_ref=p_ref.at[2 * cx + cy], dst_ref=land_ref.at[my_chip],
            send_sem=send_sems.at[j], recv_sem=recv_sems.at[j],
            device_id=(cx, cy, c), device_id_type=MESH) for j, (cx, cy) in enumerate(chips)]
        for cp in copies:
            cp.start()
        for cp in copies:
            cp.wait_send()
        for j, (cx, cy) in enumerate(chips):
            pltpu.make_async_remote_copy(
                src_ref=p_ref.at[my_chip], dst_ref=land_ref.at[2 * cx + cy],
                send_sem=send_sems.at[j], recv_sem=recv_sems.at[j],
                device_id=(cx, cy, c), device_id_type=MESH).wait_recv()

    return _sc_call(
        body, name=name, out_type=_sds(pair.shape, pair.dtype),
        scratch_types=[pltpu.SemaphoreType.DMA((3,)), pltpu.SemaphoreType.DMA((3,))],
        compiler_params=pltpu.CompilerParams(collective_id=collective_id),
    )(pair)


def _sum_leading(name, parts, tr, out_dtype=F32):
    n, r, C = parts.shape

    def body(p_ref, o_ref):
        acc = p_ref[0].astype(F32)
        for k in range(1, n):
            acc = acc + p_ref[k].astype(F32)
        o_ref[...] = acc.astype(o_ref.dtype)

    return _tc_call(
        body, name=name, grid=(r // tr,),
        in_specs=[pl.BlockSpec((n, tr, C), lambda t: (0, t, 0))],
        out_specs=pl.BlockSpec((tr, C), lambda t: (t, 0)),
        out_shape=_sds((r, C), out_dtype), compiler_params=_cparams(("parallel",)),
    )(parts)


def _row_tile(r):
    for t in (512, 384, 352, 256, 128, 64, 32, 16, 8):
        if r % t == 0:
            return t
    return r


def _adamw_math(w, g, m, v):
    c1 = 1.0 - ADAM_B1 ** ADAM_STEP
    c2 = 1.0 - ADAM_B2 ** ADAM_STEP
    nm = ADAM_B1 * m + (1.0 - ADAM_B1) * g
    nv = ADAM_B2 * v + (1.0 - ADAM_B2) * (g * g)
    m_hat = nm / c1
    v_hat = nv / c2
    return -ADAM_LR * (m_hat / (jnp.sqrt(v_hat) + ADAM_EPS) + ADAM_WD * w), nm, nv


def _chip_sum_adamw(name, pair, landed, pos, w, m, v, transposed):
    _, r, C = pair.shape
    if transposed:
        tr, tc = r, 512
        r_pad = -r % _LANES
        wspec = pl.BlockSpec((tc, r), lambda t, k, pos_ref: (t, 0))
        shape = (C, r)
        scratch = [pltpu.VMEM((tr, tc), F32), pltpu.VMEM((tc, r + r_pad), F32)]
    else:
        tr, tc = _row_tile(r), C
        wspec = pl.BlockSpec((tr, C), lambda t, k, pos_ref: (t, 0))
        shape = (r, C)
        scratch = [pltpu.VMEM((tr, tc), F32)]
    n_t = (C // tc) if transposed else (r // tr)

    def block(chip, t):
        return (chip, 0, t) if transposed else (chip, t, 0)

    def body(pos_ref, own_ref, land_ref, w_ref, m_ref, v_ref, g_ref, d_ref, nm_ref, nv_ref, acc_ref, *turn):
        k = pl.program_id(1)
        val = jnp.where(k == pos_ref[1], own_ref[...], land_ref[...]).astype(F32)

        @pl.when(k == 0)
        def _():
            acc_ref[...] = val

        @pl.when(k > 0)
        def _():
            acc_ref[...] += val

        @pl.when(k == 3)
        def _():
            if transposed:
                g_t = acc_ref[...]
                if r_pad:
                    g_t = jnp.concatenate([g_t, jnp.zeros((r_pad, tc), F32)], axis=0)
                turn[0][...] = g_t.T
                g = turn[0][:, 0:r]
            else:
                g = acc_ref[...]
            d, nm, nv = _adamw_math(w_ref[...], g, m_ref[...], v_ref[...])
            g_ref[...] = g
            d_ref[...] = d
            nm_ref[...] = nm
            nv_ref[...] = nv

    def land_index(t, k, pos_ref):
        return block(jnp.where(k == pos_ref[1], (k + 1) % 4, k), t)

    grid_spec = pltpu.PrefetchScalarGridSpec(
        num_scalar_prefetch=1, grid=(n_t, 4),
        in_specs=[pl.BlockSpec((None, tr, tc), lambda t, k, pos_ref: block(pos_ref[1], t)),
                  pl.BlockSpec((None, tr, tc), land_index), wspec, wspec, wspec],
        out_specs=[wspec] * 4, scratch_shapes=scratch)
    return _tc_call(
        body, name=name, grid_spec=grid_spec, out_shape=[_sds(shape, F32)] * 4,
        compiler_params=_cparams(("parallel", "arbitrary")),
    )(pos, pair, landed, w, m, v)


def _adamw(name, w, g, m, v):
    R, C = w.shape
    tr = _row_tile(R)

    def body(w_ref, g_ref, m_ref, v_ref, d_ref, nm_ref, nv_ref):
        d_ref[...], nm_ref[...], nv_ref[...] = _adamw_math(w_ref[...], g_ref[...], m_ref[...], v_ref[...])

    spec = pl.BlockSpec((tr, C), lambda i: (i, 0))
    return _tc_call(
        body, name=name, grid=(R // tr,), in_specs=[spec] * 4, out_specs=[spec] * 3,
        out_shape=[_sds((R, C), F32)] * 3, compiler_params=_cparams(("parallel",)),
    )(w, g, m, v)


_BIG = ("w_in_t", "w_out", "w_q", "w_k", "w_v", "w_o", "w_gate_t", "w_up_t", "w_down")
_SMALL = ("norm_mix_g", "pool_scale", "sgu_norm_g", "w_spatial", "b_spatial", "norm_xattn_g",
          "norm_mem_g", "norm_ffn_g", "final_norm_g")
_LANES = 128
_GATHER_GROUPS = (("w_in_t", "pool_w"), ("w_out",), ("w_q",), ("w_k", "w_v"), ("w_o",), ("w_gate_t",),
                  ("w_up_t",), ("w_down",))
_RS_ORDER = ("w_down", "w_gate_t", "w_up_t", "w_o", "w_q", "w_k", "w_v", "w_out", "w_in_t")
_SMALL_GROUPS = dict(
    early=("pool_scale", "sgu_norm_g", "w_spatial", "b_spatial", "norm_xattn_g", "norm_mem_g",
           "norm_ffn_g", "final_norm_g"),
    late=("norm_mix_g", "loss"))
_TURN_OUTSIDE = ("w_gate_t", "w_up_t")
_ID_GATHER, _ID_PAIR, _ID_CHIP = 0, 1, 2


_PACK_ROWS = 512


def _pack(parts):
    rows = [p.reshape(-1, _LANES) for p in parts]
    n = sum(r.shape[0] for r in rows)
    pad = -n % (_PACK_ROWS if n > _PACK_ROWS else 8)
    if pad:
        rows.append(jnp.zeros((pad, _LANES), rows[0].dtype))
    return jnp.concatenate(rows, axis=0)


class _GradReducer:
    def __init__(self, pos, apply, apply_small):
        self.pos, self.apply, self.apply_small = pos, apply, apply_small
        self.view, self.theirs, self.pair, self.landed = {}, {}, {}, {}
        self.small_gathered = {}

    def push(self, k, g):
        r = g.shape[0] // N_DEV
        self.view[k] = g.reshape(4, 2, r, g.shape[1])
        self.theirs[k] = _seq_pair_exchange("grad_pair_exchange_" + k, self.view[k], _ID_PAIR)

    def reduce(self, k):
        r = self.view[k].shape[2]
        self.pair[k] = _pair_sum("grad_pair_sum_" + k, self.view[k], self.theirs[k], self.pos, r)
        self.landed[k] = _seq_chip_exchange("grad_chip_exchange_" + k, self.pair[k], _ID_CHIP)

    def finish(self, k):
        self.apply(k, self.pair[k], self.landed[k])

    def small(self, tag, parts):
        packed = _pack([parts[k] for k in _SMALL_GROUPS[tag]])
        (self.small_gathered[tag],) = _seq_all_gather("gather_small_grads_" + tag, [packed], _ID_GATHER)

    def finish_small(self, tag):
        allp = self.small_gathered[tag]
        self.apply_small(tag, _sum_leading("sum_small_grads_" + tag, allp, min(_PACK_ROWS, allp.shape[1])))


def _unpack(packed, like):
    out, row = [], 0
    for ref in like:
        rows = ref.size // _LANES
        out.append(packed[row:row + rows].reshape(ref.shape))
        row += rows
    return out


def kernel(x, mem, norm_mix_g, w_in, pool_w, pool_scale, sgu_norm_g, w_spatial, b_spatial, w_out, norm_xattn_g, norm_mem_g, w_q, w_k, w_v, w_o, norm_ffn_g, w_gate, w_up, w_down, final_norm_g, loss_target, m_norm_mix_g, m_w_in, m_pool_w, m_pool_scale, m_sgu_norm_g, m_w_spatial, m_b_spatial, m_w_out, m_norm_xattn_g, m_norm_mem_g, m_w_q, m_w_k, m_w_v, m_w_o, m_norm_ffn_g, m_w_gate, m_w_up, m_w_down, m_final_norm_g, v_norm_mix_g, v_w_in, v_pool_w, v_pool_scale, v_sgu_norm_g, v_w_spatial, v_b_spatial, v_w_out, v_norm_xattn_g, v_norm_mem_g, v_w_q, v_w_k, v_w_v, v_w_o, v_norm_ffn_g, v_w_gate, v_w_up, v_w_down, v_final_norm_g):
    args = dict(locals())
    names = ("norm_mix_g", "w_in", "pool_w", "pool_scale", "sgu_norm_g", "w_spatial", "b_spatial", "w_out",
             "norm_xattn_g", "norm_mem_g", "w_q", "w_k", "w_v", "w_o", "norm_ffn_g", "w_gate", "w_up",
             "w_down", "final_norm_g")
    w = {k: args[k] for k in names}
    m = {k: args["m_" + k] for k in names}
    v = {k: args["v_" + k] for k in names}
    _CHAIN.__init__()

    shards = dict(
        w_in_t=w["w_in"][0].T, w_out=w["w_out"][0], w_q=w["w_q"][0], w_k=w["w_k"][0], w_v=w["w_v"][0],
        w_o=w["w_o"][0], w_gate_t=w["w_gate"][0].T, w_up_t=w["w_up"][0].T, w_down=w["w_down"][0])
    send = {k: shards[k].astype(BF16) for k in _BIG}
    send["pool_w"] = w["pool_w"][0].reshape(4 * 32, POOL_GROUP).astype(BF16)
    W = {}
    for gi, group in enumerate(_GATHER_GROUPS):
        gathered = _seq_all_gather("gather_weights_%d" % gi, [send[k] for k in group], _ID_GATHER)
        for k, g in zip(group, gathered):
            W[k] = g.reshape(-1, g.shape[-1])
    W["pool_w"] = W["pool_w"].reshape(N_DEV, 4, 32, POOL_GROUP).transpose(1, 0, 2, 3).reshape(4, POOL_GROUP, POOL_GROUP)

    t = jnp.arange(SGU_BLOCK)
    mask = (t[None, :] // SGU_CHUNK) <= (t[:, None] // SGU_CHUNK)
    sm = dict(
        norm_mix_g=w["norm_mix_g"], pool_scale=w["pool_scale"], sgu_norm_g=w["sgu_norm_g"],
        norm_xattn_g=w["norm_xattn_g"], norm_mem_g=w["norm_mem_g"], norm_ffn_g=w["norm_ffn_g"],
        final_norm_g=w["final_norm_g"].reshape(1, D_MODEL),
        ws_masked=jnp.where(mask[None], w["w_spatial"][0], 0.0).astype(BF16),
        bias_full=jnp.repeat(w["b_spatial"][0].T, SGU_BLOCK, axis=1))

    natural = dict(w_in_t="w_in", w_gate_t="w_gate", w_up_t="w_up")
    grads, delta, new_m, new_v = {}, {}, {}, {}

    def apply(k, pair, landed):
        name = natural.get(k, k)
        if k == "pool_w":
            flat = (4 * POOL_GROUP // N_DEV, POOL_GROUP)
            res = _chip_sum_adamw("grad_finish_" + k, pair, landed, pos, w[k].reshape(flat), m[k].reshape(flat),
                                  v[k].reshape(flat), False)
            grads[k], delta[k], new_m[k], new_v[k] = (a.reshape(w[k].shape) for a in res)
            return
        if k in _TURN_OUTSIDE:
            res = _chip_sum_adamw("grad_finish_" + k, pair, landed, pos, w[name][0].T, m[name][0].T, v[name][0].T,
                                  False)
            res = [a.T for a in res]
        else:
            res = _chip_sum_adamw("grad_finish_" + k, pair, landed, pos, w[name][0], m[name][0], v[name][0],
                                  k in natural)
        grads[name], delta[name], new_m[name], new_v[name] = (a[None] for a in res)

    like = dict(w)
    like["loss"] = _sds((1, _LANES), F32)

    def apply_small(tag, total):
        group = _SMALL_GROUPS[tag]
        grads.update(zip(group, _unpack(total, [like[k] for k in group])))
        if tag == "late":
            d_, m_, v_ = _adamw("adamw_small", _pack([w[k] for k in _SMALL]), _pack([grads[k] for k in _SMALL]),
                                _pack([m[k] for k in _SMALL]), _pack([v[k] for k in _SMALL]))
            shapes = [w[k] for k in _SMALL]
            for k, a, b, c_ in zip(_SMALL, _unpack(d_, shapes), _unpack(m_, shapes), _unpack(v_, shapes)):
                delta[k], new_m[k], new_v[k] = a, b, c_

    pos = jnp.stack([lax.axis_index("c"), 2 * lax.axis_index("x") + lax.axis_index("y")]).astype(jnp.int32)
    rs = _GradReducer(pos, apply, apply_small)
    _, grad_x = _local_step(x[0], mem[0], loss_target[0], W, sm, rs)

    outs = [grads["loss"][0, 0], grad_x[None]]
    outs += [grads[k].reshape(w[k].shape) for k in names]
    outs += [delta[k] for k in names]
    outs += [new_m[k] for k in names]
    outs += [new_v[k] for k in names]
    return tuple(outs)
```

```python
import functools

import jax
import jax.numpy as jnp
from jax import lax
from jax.experimental import pallas as pl
from jax.experimental.pallas import tpu as pltpu
from jax.experimental.pallas import tpu_sc as plsc

F32 = jnp.float32
BF16 = jnp.bfloat16
MESH = pl.DeviceIdType.MESH

EPS = 1e-6
D_MODEL = 2048
D_POOL = 1024
D_SGU = 1024
POOL_WINDOWS = (2, 4, 8, 16)
POOL_GROUP = 256
POOL_HALO = 16
SGU_BLOCK = 128
SGU_CHUNK = 64
N_SGU_HEADS = 8
N_HEADS = 4
HEAD_DIM = 512
N_DEV = 8

ADAM_LR = 0.001
ADAM_B1 = 0.9
ADAM_B2 = 0.999
ADAM_EPS = 1e-08
ADAM_WD = 0.01
ADAM_STEP = 10

VMEM_LIMIT = 56 * 1024 * 1024


def _cparams(sem=None):
    return pltpu.CompilerParams(dimension_semantics=sem, vmem_limit_bytes=VMEM_LIMIT)


def _sds(shape, dtype):
    return jax.ShapeDtypeStruct(shape, dtype)


_ANY = pl.BlockSpec(memory_space=pl.ANY)


class _Chain:
    def __init__(self):
        self.tc = None
        self.sc = None


_CHAIN = _Chain()


def _first(out):
    return out[0] if isinstance(out, (list, tuple)) else out


def _tc_call(body, *, in_specs=None, grid_spec=None, **kw):
    def run(*args):
        prev, n = _CHAIN.tc, len(args)
        fn, specs, spec, operands = body, in_specs, grid_spec, args
        if prev is not None:
            def fn(*refs):
                return body(*refs[:n], *refs[n + 1:])
            operands = args + (prev,)
            if grid_spec is None:
                specs = list(in_specs) + [_ANY]
            else:
                spec = pltpu.PrefetchScalarGridSpec(
                    num_scalar_prefetch=grid_spec.num_scalar_prefetch, grid=grid_spec.grid,
                    in_specs=list(grid_spec.in_specs) + [_ANY], out_specs=grid_spec.out_specs,
                    scratch_shapes=grid_spec.scratch_shapes)
        if spec is None:
            out = pl.pallas_call(fn, in_specs=specs, **kw)(*operands)
        else:
            out = pl.pallas_call(fn, grid_spec=spec, **kw)(*operands)
        _CHAIN.tc = _first(out)
        return out
    return run


def _sc_call(body, **kw):
    return pl.kernel(body, mesh=plsc.ScalarSubcoreMesh(axis_name="seq", num_cores=1), **kw)


def _rowsum8(v):
    r, c = v.shape
    return v.reshape(r // 8, 8, c).sum(axis=0)


_EPILOGUE_COLS = 256
_DN = {
    "nn": (((1,), (0,)), ((), ())),
    "nt": (((1,), (1,)), ((), ())),
    "tn": (((0,), (0,)), ((), ())),
}


def _mm(name, a_list, b_list, terms, mode, tm, tn, tk, out_dtypes, epilogue=None, extras=(), n_acc=1):
    a0, b0 = a_list[0], b_list[0]
    if mode == "tn":
        K, M = a0.shape
    else:
        M, K = a0.shape
    N = b0.shape[0] if mode == "nt" else b0.shape[1]
    assert M % tm == 0 and N % tn == 0 and K % tk == 0, (name, M, N, K, tm, tn, tk)
    nk = K // tk
    na, nb, ne, no = len(a_list), len(b_list), len(extras), len(out_dtypes)
    dn = _DN[mode]

    if mode == "tn":
        a_spec = pl.BlockSpec((tk, tm), lambda i, j, k: (k, i))
    else:
        a_spec = pl.BlockSpec((tm, tk), lambda i, j, k: (i, k))
    if mode == "nt":
        b_spec = pl.BlockSpec((tn, tk), lambda i, j, k: (j, k))
    else:
        b_spec = pl.BlockSpec((tk, tn), lambda i, j, k: (k, j))
    o_spec = pl.BlockSpec((tm, tn), lambda i, j, k: (i, j))

    def body(*refs):
        a_refs = refs[:na]
        b_refs = refs[na:na + nb]
        e_refs = refs[na + nb:na + nb + ne]
        o_refs = refs[na + nb + ne:na + nb + ne + no]
        acc_refs = refs[na + nb + ne + no:]

        def products(cols):
            parts = [None] * n_acc
            for ai, bi, ci in terms:
                b = b_refs[bi][cols, :] if mode == "nt" else b_refs[bi][:, cols]
                d = lax.dot_general(a_refs[ai][...].astype(BF16), b.astype(BF16), dn, preferred_element_type=F32)
                parts[ci] = d if parts[ci] is None else parts[ci] + d
            return parts

        def finish(accs, cols=slice(None)):
            outs = epilogue(accs, [e[:, cols] for e in e_refs]) if epilogue is not None else accs
            for o_ref, v in zip(o_refs, outs):
                o_ref[:, cols] = v.astype(o_ref.dtype)

        if nk == 1 and epilogue is not None and tn > _EPILOGUE_COLS:
            for c0 in range(0, tn, _EPILOGUE_COLS):
                cols = slice(c0, c0 + _EPILOGUE_COLS)
                finish(products(cols), cols)
            return
        parts = products(slice(None))
        if nk == 1:
            finish(parts)
        else:
            k = pl.program_id(2)

            @pl.when(k == 0)
            def _():
                for c in range(n_acc):
                    acc_refs[c][...] = parts[c]

            @pl.when(k > 0)
            def _():
                for c in range(n_acc):
                    acc_refs[c][...] += parts[c]

            @pl.when(k == nk - 1)
            def _():
                finish([acc_refs[c][...] for c in range(n_acc)])

    scratch = [pltpu.VMEM((tm, tn), F32) for _ in range(n_acc)] if nk > 1 else []
    return _tc_call(
        body, name=name, grid=(M // tm, N // tn, nk),
        in_specs=[a_spec] * na + [b_spec] * nb + [o_spec] * ne,
        out_specs=[o_spec] * no,
        out_shape=[_sds((M, N), dt) for dt in out_dtypes],
        scratch_shapes=scratch,
        compiler_params=_cparams(("parallel", "parallel", "arbitrary")),
    )(*a_list, *b_list, *extras)


def _mm_rows(name, a_list, b_list, terms, mode, tm, tk, rows, vecs, row_dtypes, n_vec_out, epilogue,
             n_scalar_out=0):
    a0, b0 = a_list[0], b_list[0]
    M, K = a0.shape
    N = b0.shape[0] if mode == "nt" else b0.shape[1]
    assert mode in ("nn", "nt") and M % tm == 0 and K % tk == 0, (name, M, N, K, tm, tk)
    nm, nk = M // tm, K // tk
    slab = min(128, tm)
    na, nb, nr, nv, no = len(a_list), len(b_list), len(rows), len(vecs), len(row_dtypes)
    dn = _DN[mode]
    a_spec = pl.BlockSpec((tm, tk), lambda i, k: (i, k))
    b_mode = dict(pipeline_mode=pl.Buffered(1)) if nk == 1 else {}
    b_spec = (pl.BlockSpec((N, tk), lambda i, k: (0, k), **b_mode) if mode == "nt"
              else pl.BlockSpec((tk, N), lambda i, k: (k, 0), **b_mode))
    row_spec = pl.BlockSpec((tm, N), lambda i, k: (i, 0))
    vec_spec = pl.BlockSpec((1, N), lambda i, k: (0, 0))
    one_spec = pl.BlockSpec((1, 1), lambda i, k: (0, 0))

    def body(*refs):
        pos = 0
        a_refs = refs[pos:pos + na]; pos += na
        b_refs = refs[pos:pos + nb]; pos += nb
        r_refs = refs[pos:pos + nr]; pos += nr
        v_refs = refs[pos:pos + nv]; pos += nv
        o_refs = refs[pos:pos + no]; pos += no
        s_refs = refs[pos:pos + n_vec_out]; pos += n_vec_out
        vacc_refs = refs[pos:pos + n_vec_out]; pos += n_vec_out
        acc_ref = refs[pos] if nk > 1 else None
        i, k = pl.program_id(0), pl.program_id(1)
        part = None
        for ai, bi, _ in terms:
            d = lax.dot_general(a_refs[ai][...].astype(BF16), b_refs[bi][...].astype(BF16), dn,
                                preferred_element_type=F32)
            part = d if part is None else part + d

        def finish(acc):
            vecs_now = [v[...] for v in v_refs]
            vparts = None
            for r0 in range(0, tm, slab):
                rs_ = slice(r0, r0 + slab)
                outs, vp = epilogue(acc[rs_, :], [r[rs_, :] for r in r_refs], vecs_now)
                for o_ref, val in zip(o_refs, outs):
                    o_ref[rs_, :] = val.astype(o_ref.dtype)
                vparts = vp if vparts is None else [a + b for a, b in zip(vparts, vp)]

            @pl.when(i == 0)
            def _():
                for vacc, vp in zip(vacc_refs, vparts):
                    vacc[...] = vp

            @pl.when(i > 0)
            def _():
                for vacc, vp in zip(vacc_refs, vparts):
                    vacc[...] += vp

            @pl.when(i == nm - 1)
            def _():
                for j, (s_ref, vacc) in enumerate(zip(s_refs, vacc_refs)):
                    col = jnp.sum(vacc[...], axis=0, keepdims=True)
                    s_ref[...] = jnp.sum(col, axis=1, keepdims=True) if j >= n_vec_out - n_scalar_out else col

        if nk == 1:
            finish(part)
        else:
            @pl.when(k == 0)
            def _():
                acc_ref[...] = part

            @pl.when(k > 0)
            def _():
                acc_ref[...] += part

            @pl.when(k == nk - 1)
            def _():
                finish(acc_ref)

    n_plain = n_vec_out - n_scalar_out
    return _tc_call(
        body, name=name, grid=(nm, nk),
        in_specs=[a_spec] * na + [b_spec] * nb + [row_spec] * nr + [vec_spec] * nv,
        out_specs=[row_spec] * no + [vec_spec] * n_plain + [one_spec] * n_scalar_out,
        out_shape=[_sds((M, N), dt) for dt in row_dtypes] + [_sds((1, N), F32)] * n_plain
        + [_sds((1, 1), F32)] * n_scalar_out,
        scratch_shapes=[pltpu.VMEM((8, N), F32)] * n_vec_out + ([pltpu.VMEM((tm, N), F32)] if nk > 1 else []),
        compiler_params=_cparams(("arbitrary", "arbitrary")),
    )(*a_list, *b_list, *rows, *vecs)


def _ep_residual_norm(acc, rows, vecs):
    x_new = rows[0] + acc
    r = lax.rsqrt(jnp.mean(x_new * x_new, axis=-1, keepdims=True) + EPS)
    return [x_new, x_new * r * vecs[0]], []


def _ep_norm_bwd(acc, rows, vecs):
    xv, dres = rows
    r = lax.rsqrt(jnp.mean(xv * xv, axis=-1, keepdims=True) + EPS)
    xh = xv * r
    dxh = acc * vecs[0]
    m = jnp.mean(dxh * xh, axis=-1, keepdims=True)
    dx = dres + r * (dxh - xh * m)
    return [dx, dx], [_rowsum8(acc * xh)]


def _ep_final_loss(acc, rows, vecs):
    x2, target = rows
    gv = vecs[0]
    xv = x2 + acc
    inv_d = 1.0 / xv.shape[-1]
    r = lax.rsqrt(jnp.mean(xv * xv, axis=-1, keepdims=True) + EPS)
    xh = xv * r
    e = xh * gv - target
    dy = e * inv_d
    dxh = dy * gv
    m = jnp.mean(dxh * xh, axis=-1, keepdims=True)
    dx = r * (dxh - xh * m)
    return [dx, dx], [_rowsum8(dy * xh), _rowsum8(e * e) * (0.5 * inv_d)]


def _mm1(name, a, b, mode, tm, tn, tk, out_dtype, **kw):
    return _mm(name, [a], [b], [(0, 0, 0)], mode, tm, tn, tk, [out_dtype], **kw)[0]


def _rms_fwd(name, x, g, tr):
    S, Dm = x.shape

    def body(x_ref, g_ref, h_ref):
        xv = x_ref[...]
        r = lax.rsqrt(jnp.mean(xv * xv, axis=-1, keepdims=True) + EPS)
        h_ref[...] = (xv * r * g_ref[...]).astype(h_ref.dtype)

    return _tc_call(
        body, name=name, grid=(S // tr,),
        in_specs=[pl.BlockSpec((tr, Dm), lambda i: (i, 0)), pl.BlockSpec((1, Dm), lambda i: (0, 0))],
        out_specs=pl.BlockSpec((tr, Dm), lambda i: (i, 0)),
        out_shape=_sds((S, Dm), BF16),
        compiler_params=_cparams(("parallel",)),
    )(x, g)


def _rms_bwd(name, dh, x, g, dres, tr, want_dx=True):
    S, Dm = x.shape
    nsteps = S // tr

    def body(*refs):
        if want_dx:
            dh_ref, x_ref, g_ref, dres_ref, dx_ref, dxb_ref, dg_ref, acc_ref = refs
        else:
            dh_ref, x_ref, g_ref, dg_ref, acc_ref = refs
        i = pl.program_id(0)
        xv = x_ref[...]
        r = lax.rsqrt(jnp.mean(xv * xv, axis=-1, keepdims=True) + EPS)
        xh = xv * r
        dhv = dh_ref[...]
        part = _rowsum8(dhv * xh)

        @pl.when(i == 0)
        def _():
            acc_ref[...] = part

        @pl.when(i > 0)
        def _():
            acc_ref[...] += part

        @pl.when(i == nsteps - 1)
        def _():
            dg_ref[...] = jnp.sum(acc_ref[...], axis=0, keepdims=True)

        if want_dx:
            dxh = dhv * g_ref[...]
            m = jnp.mean(dxh * xh, axis=-1, keepdims=True)
            dx = dres_ref[...] + r * (dxh - xh * m)
            dx_ref[...] = dx
            dxb_ref[...] = dx.astype(BF16)

    row = pl.BlockSpec((tr, Dm), lambda i: (i, 0))
    vec = pl.BlockSpec((1, Dm), lambda i: (0, 0))
    if want_dx:
        in_specs = [row, row, vec, row]
        out_specs = [row, row, vec]
        out_shape = [_sds((S, Dm), F32), _sds((S, Dm), BF16), _sds((1, Dm), F32)]
        args = (dh, x, g, dres)
    else:
        in_specs = [row, row, vec]
        out_specs = [vec]
        out_shape = [_sds((1, Dm), F32)]
        args = (dh, x, g)
    return _tc_call(
        body, name=name, grid=(nsteps,), in_specs=in_specs, out_specs=out_specs, out_shape=out_shape,
        scratch_shapes=[pltpu.VMEM((8, Dm), F32)],
        compiler_params=_cparams(("arbitrary",)),
    )(*args)


def _final_loss(name, x3, g, target, tr):
    S, Dm = x3.shape
    nsteps = S // tr

    def body(x_ref, g_ref, t_ref, dx_ref, dxb_ref, dg_ref, loss_ref, acc_g, acc_l):
        i = pl.program_id(0)
        xv = x_ref[...]
        gv = g_ref[...]
        r = lax.rsqrt(jnp.mean(xv * xv, axis=-1, keepdims=True) + EPS)
        xh = xv * r
        e = xh * gv - t_ref[...]
        dy = e * (1.0 / Dm)
        lpart = _rowsum8(e * e)
        gpart = _rowsum8(dy * xh)

        @pl.when(i == 0)
        def _():
            acc_g[...] = gpart
            acc_l[...] = lpart

        @pl.when(i > 0)
        def _():
            acc_g[...] += gpart
            acc_l[...] += lpart

        @pl.when(i == nsteps - 1)
        def _():
            dg_ref[...] = jnp.sum(acc_g[...], axis=0, keepdims=True)
            tot = jnp.sum(jnp.sum(acc_l[...], axis=1, keepdims=True), axis=0, keepdims=True)
            loss_ref[...] = tot * (0.5 / Dm)

        dxh = dy * gv
        m = jnp.mean(dxh * xh, axis=-1, keepdims=True)
        dx = r * (dxh - xh * m)
        dx_ref[...] = dx
        dxb_ref[...] = dx.astype(BF16)

    row = pl.BlockSpec((tr, Dm), lambda i: (i, 0))
    vec = pl.BlockSpec((1, Dm), lambda i: (0, 0))
    return _tc_call(
        body, name=name, grid=(nsteps,),
        in_specs=[row, vec, row],
        out_specs=[row, row, vec, pl.BlockSpec((1, 1), lambda i: (0, 0))],
        out_shape=[_sds((S, Dm), F32), _sds((S, Dm), BF16), _sds((1, Dm), F32), _sds((1, 1), F32)],
        scratch_shapes=[pltpu.VMEM((8, Dm), F32), pltpu.VMEM((8, Dm), F32)],
        compiler_params=_cparams(("arbitrary",)),
    )(x3, g, target)


def _softmax_rows(s):
    e = jnp.exp(s - jnp.max(s, axis=-1, keepdims=True))
    return e / jnp.sum(e, axis=-1, keepdims=True)


def _attn_fwd(name, q, k, v, ts):
    S, Dm = q.shape
    M = k.shape[0]
    scale = HEAD_DIM ** -0.5

    def body(q_ref, k_ref, v_ref, o_ref):
        for h in range(N_HEADS):
            sl = slice(h * HEAD_DIM, (h + 1) * HEAD_DIM)
            s = lax.dot_general(q_ref[:, sl], k_ref[:, sl], _DN["nt"], preferred_element_type=F32) * scale
            p = _softmax_rows(s)
            o_ref[:, sl] = jnp.dot(p.astype(BF16), v_ref[:, sl], preferred_element_type=F32).astype(o_ref.dtype)

    row = pl.BlockSpec((ts, Dm), lambda i: (i, 0))
    mem = pl.BlockSpec((M, Dm), lambda i: (0, 0))
    return _tc_call(
        body, name=name, grid=(S // ts,), in_specs=[row, mem, mem], out_specs=row,
        out_shape=_sds((S, Dm), BF16), compiler_params=_cparams(("parallel",)),
    )(q, k, v)


def _attn_bwd(name, q, k, v, do, ts):
    S, Dm = q.shape
    M = k.shape[0]
    scale = HEAD_DIM ** -0.5

    def body(q_ref, k_ref, v_ref, do_ref, dq_ref, dk_ref, dv_ref):
        i = pl.program_id(0)

        @pl.when(i == 0)
        def _():
            dk_ref[...] = jnp.zeros_like(dk_ref)
            dv_ref[...] = jnp.zeros_like(dv_ref)

        for h in range(N_HEADS):
            sl = slice(h * HEAD_DIM, (h + 1) * HEAD_DIM)
            qh = q_ref[:, sl]
            kh = k_ref[:, sl]
            doh = do_ref[:, sl]
            s = lax.dot_general(qh, kh, _DN["nt"], preferred_element_type=F32) * scale
            p = _softmax_rows(s)
            dp = lax.dot_general(doh, v_ref[:, sl], _DN["nt"], preferred_element_type=F32)
            ds = p * (dp - jnp.sum(dp * p, axis=-1, keepdims=True)) * scale
            dsb = ds.astype(BF16)
            dq_ref[:, sl] = jnp.dot(dsb, kh, preferred_element_type=F32).astype(dq_ref.dtype)
            dk_ref[:, sl] += lax.dot_general(dsb, qh, _DN["tn"], preferred_element_type=F32)
            dv_ref[:, sl] += lax.dot_general(p.astype(BF16), doh, _DN["tn"], preferred_element_type=F32)

    row = pl.BlockSpec((ts, Dm), lambda i: (i, 0))
    mem = pl.BlockSpec((M, Dm), lambda i: (0, 0))
    return _tc_call(
        body, name=name, grid=(S // ts,), in_specs=[row, mem, mem, row], out_specs=[row, mem, mem],
        out_shape=[_sds((S, Dm), BF16), _sds((M, Dm), F32), _sds((M, Dm), F32)],
        compiler_params=_cparams(("arbitrary",)),
    )(q, k, v, do)


def _pool_denominators(row0, ts):
    return (row0 + lax.broadcasted_iota(jnp.int32, (ts, 1), 0) + 1).astype(F32)


def _mixer_fwd(name, proj, pool_w, pool_scale, sgu_g, ws, bias_full, ts):
    S = proj.shape[0]
    nblk = ts // SGU_BLOCK
    halo_blocks = ts // POOL_HALO

    def body(proj_ref, halo_ref, pw_ref, sc_ref, g_ref, ws_ref, b_ref, y_ref, p_ref, vn_ref, ext_ref):
        i = pl.program_id(0)
        a = proj_ref[:, 0:D_POOL]
        ext_ref[0:POOL_HALO, :] = jnp.where(i > 0, halo_ref[...], 0.0)
        ext_ref[POOL_HALO:POOL_HALO + ts, :] = a
        pos = _pool_denominators(i * ts, ts)
        for gi, w in enumerate(POOL_WINDOWS):
            cs = slice(gi * POOL_GROUP, (gi + 1) * POOL_GROUP)
            acc = a[:, cs]
            for j in range(1, w):
                acc = acc + ext_ref[POOL_HALO - j:POOL_HALO - j + ts, cs]
            pg = (acc / jnp.minimum(pos, float(w)) - a[:, cs]).astype(BF16)
            p_ref[:, cs] = pg
            ypre = jnp.dot(pg, pw_ref[gi], preferred_element_type=F32)
            y_ref[:, cs] = (ypre * sc_ref[:, cs]).astype(y_ref.dtype)

        v = proj_ref[:, D_POOL + D_SGU:D_POOL + 2 * D_SGU]
        r = lax.rsqrt(jnp.mean(v * v, axis=-1, keepdims=True) + EPS)
        vn_ref[...] = (v * r * g_ref[...]).astype(BF16)
        for n in range(nblk):
            rs = slice(n * SGU_BLOCK, (n + 1) * SGU_BLOCK)
            for h in range(N_SGU_HEADS):
                cs = slice(h * SGU_BLOCK, (h + 1) * SGU_BLOCK)
                mixed = jnp.dot(ws_ref[h], vn_ref[rs, cs], preferred_element_type=F32) + b_ref[:, cs]
                u = proj_ref[rs, D_POOL + h * SGU_BLOCK:D_POOL + (h + 1) * SGU_BLOCK]
                y_ref[rs, D_POOL + h * SGU_BLOCK:D_POOL + (h + 1) * SGU_BLOCK] = (u * mixed).astype(y_ref.dtype)

    return _tc_call(
        body, name=name, grid=(S // ts,),
        in_specs=[
            pl.BlockSpec((ts, D_POOL + 2 * D_SGU), lambda i: (i, 0)),
            pl.BlockSpec((POOL_HALO, D_POOL), lambda i: (jnp.maximum(i * halo_blocks - 1, 0), 0)),
            pl.BlockSpec((4, POOL_GROUP, POOL_GROUP), lambda i: (0, 0, 0)),
            pl.BlockSpec((1, D_POOL), lambda i: (0, 0)),
            pl.BlockSpec((1, D_SGU), lambda i: (0, 0)),
            pl.BlockSpec((N_SGU_HEADS, SGU_BLOCK, SGU_BLOCK), lambda i: (0, 0, 0)),
            pl.BlockSpec((SGU_BLOCK, D_SGU), lambda i: (0, 0)),
        ],
        out_specs=[
            pl.BlockSpec((ts, D_MODEL), lambda i: (i, 0)),
            pl.BlockSpec((ts, D_POOL), lambda i: (i, 0)),
            pl.BlockSpec((ts, D_SGU), lambda i: (i, 0)),
        ],
        out_shape=[_sds((S, D_MODEL), BF16), _sds((S, D_POOL), BF16), _sds((S, D_SGU), BF16)],
        scratch_shapes=[pltpu.VMEM((ts + POOL_HALO, D_POOL), F32)],
        compiler_params=_cparams(("parallel",)),
    )(proj, proj, pool_w, pool_scale, sgu_g, ws, bias_full)


def _mixer_bwd(name, dymix, proj, p, vn, pool_w, pool_scale, sgu_g, ws, bias_full, ts):
    S = proj.shape[0]
    nsteps = S // ts
    nblk = ts // SGU_BLOCK
    halo_blocks = ts // POOL_HALO

    def body(dy_ref, dyh_ref, u_ref, v_ref, p_ref, vn_ref, pw_ref, sc_ref, g_ref, ws_ref, b_ref,
             dproj_ref, dpw_ref, dsc_ref, dg_ref, dws_ref, db_ref,
             ext_ref, dvn_ref, acc_sc, acc_g, acc_b):
        i = pl.program_id(0)

        @pl.when(i == 0)
        def _():
            dpw_ref[...] = jnp.zeros_like(dpw_ref)
            dws_ref[...] = jnp.zeros_like(dws_ref)
            acc_sc[...] = jnp.zeros_like(acc_sc)
            acc_g[...] = jnp.zeros_like(acc_g)
            acc_b[...] = jnp.zeros_like(acc_b)

        pos = _pool_denominators(i * ts, ts)
        pos_h = _pool_denominators((i + 1) * ts, POOL_HALO)
        for gi, w in enumerate(POOL_WINDOWS):
            cs = slice(gi * POOL_GROUP, (gi + 1) * POOL_GROUP)
            pg = p_ref[:, cs]
            wg = pw_ref[gi]
            dyp = dy_ref[:, cs]
            ypre = jnp.dot(pg, wg, preferred_element_type=F32)
            acc_sc[:, cs] += _rowsum8(dyp * ypre)
            dz = (dyp * sc_ref[:, cs]).astype(BF16)
            dpw_ref[gi] += lax.dot_general(pg, dz, _DN["tn"], preferred_element_type=F32)
            dp = lax.dot_general(dz, wg, _DN["nt"], preferred_element_type=F32)
            dzh = (dyh_ref[:, cs] * sc_ref[:, cs]).astype(BF16)
            dph = lax.dot_general(dzh, wg, _DN["nt"], preferred_element_type=F32)
            ext_ref[0:ts, cs] = dp / jnp.minimum(pos, float(w))
            ext_ref[ts:ts + POOL_HALO, cs] = jnp.where(i < nsteps - 1, dph / jnp.minimum(pos_h, float(w)), 0.0)
            acc = ext_ref[0:ts, cs]
            for j in range(1, w):
                acc = acc + ext_ref[j:j + ts, cs]
            dproj_ref[:, cs] = (acc - dp).astype(dproj_ref.dtype)

        for n in range(nblk):
            rs = slice(n * SGU_BLOCK, (n + 1) * SGU_BLOCK)
            for h in range(N_SGU_HEADS):
                cs = slice(h * SGU_BLOCK, (h + 1) * SGU_BLOCK)
                vnb = vn_ref[rs, cs]
                wh = ws_ref[h]
                mixed = jnp.dot(wh, vnb, preferred_element_type=F32) + b_ref[:, cs]
                dys = dy_ref[rs, D_POOL + h * SGU_BLOCK:D_POOL + (h + 1) * SGU_BLOCK]
                dproj_ref[rs, D_POOL + h * SGU_BLOCK:D_POOL + (h + 1) * SGU_BLOCK] = (dys * mixed).astype(dproj_ref.dtype)
                dmix = dys * u_ref[rs, cs]
                acc_b[:, cs] += dmix
                dmb = dmix.astype(BF16)
                dws_ref[h] += lax.dot_general(dmb, vnb, _DN["nt"], preferred_element_type=F32)
                dvn_ref[rs, cs] = lax.dot_general(wh, dmb, _DN["tn"], preferred_element_type=F32)
        v = v_ref[...]
        r = lax.rsqrt(jnp.mean(v * v, axis=-1, keepdims=True) + EPS)
        vh = v * r
        dvn = dvn_ref[...]
        acc_g[...] += _rowsum8(dvn * vh)
        dxh = dvn * g_ref[...]
        m = jnp.mean(dxh * vh, axis=-1, keepdims=True)
        dproj_ref[:, D_POOL + D_SGU:D_POOL + 2 * D_SGU] = (r * (dxh - vh * m)).astype(dproj_ref.dtype)

        @pl.when(i == nsteps - 1)
        def _():
            dsc_ref[...] = jnp.sum(acc_sc[...], axis=0, keepdims=True)
            dg_ref[...] = jnp.sum(acc_g[...], axis=0, keepdims=True)
            t_idx = lax.broadcasted_iota(jnp.int32, (SGU_BLOCK, SGU_BLOCK), 0) // SGU_CHUNK
            s_idx = lax.broadcasted_iota(jnp.int32, (SGU_BLOCK, SGU_BLOCK), 1) // SGU_CHUNK
            mask = s_idx <= t_idx
            for h in range(N_SGU_HEADS):
                cs = slice(h * SGU_BLOCK, (h + 1) * SGU_BLOCK)
                dws_ref[h] = jnp.where(mask, dws_ref[h], 0.0)
                col = jnp.sum(acc_b[:, cs], axis=1, keepdims=True)
                db_ref[h] = jnp.broadcast_to(col, (SGU_BLOCK, SGU_BLOCK))

    const2 = lambda i: (0, 0)
    const3 = lambda i: (0, 0, 0)
    last_halo = S // POOL_HALO - 1
    return _tc_call(
        body, name=name, grid=(nsteps,),
        in_specs=[
            pl.BlockSpec((ts, D_MODEL), lambda i: (i, 0)),
            pl.BlockSpec((POOL_HALO, D_POOL), lambda i: (jnp.minimum((i + 1) * halo_blocks, last_halo), 0)),
            pl.BlockSpec((ts, D_SGU), lambda i: (i, 1)),
            pl.BlockSpec((ts, D_SGU), lambda i: (i, 2)),
            pl.BlockSpec((ts, D_POOL), lambda i: (i, 0)),
            pl.BlockSpec((ts, D_SGU), lambda i: (i, 0)),
            pl.BlockSpec((4, POOL_GROUP, POOL_GROUP), const3),
            pl.BlockSpec((1, D_POOL), const2),
            pl.BlockSpec((1, D_SGU), const2),
            pl.BlockSpec((N_SGU_HEADS, SGU_BLOCK, SGU_BLOCK), const3),
            pl.BlockSpec((SGU_BLOCK, D_SGU), const2),
        ],
        out_specs=[
            pl.BlockSpec((ts, D_POOL + 2 * D_SGU), lambda i: (i, 0)),
            pl.BlockSpec((4, POOL_GROUP, POOL_GROUP), const3),
            pl.BlockSpec((1, D_POOL), const2),
            pl.BlockSpec((1, D_SGU), const2),
            pl.BlockSpec((N_SGU_HEADS, SGU_BLOCK, SGU_BLOCK), const3),
            pl.BlockSpec((N_SGU_HEADS, SGU_BLOCK, SGU_BLOCK), const3),
        ],
        out_shape=[
            _sds((S, D_POOL + 2 * D_SGU), BF16),
            _sds((4, POOL_GROUP, POOL_GROUP), F32),
            _sds((1, D_POOL), F32),
            _sds((1, D_SGU), F32),
            _sds((N_SGU_HEADS, SGU_BLOCK, SGU_BLOCK), F32),
            _sds((N_SGU_HEADS, SGU_BLOCK, SGU_BLOCK), F32),
        ],
        scratch_shapes=[
            pltpu.VMEM((ts + POOL_HALO, D_POOL), F32),
            pltpu.VMEM((ts, D_SGU), F32),
            pltpu.VMEM((8, D_POOL), F32),
            pltpu.VMEM((8, D_SGU), F32),
            pltpu.VMEM((SGU_BLOCK, D_SGU), F32),
        ],
        compiler_params=_cparams(("arbitrary",)),
    )(dymix, dymix, proj, proj, p, vn, pool_w, pool_scale, sgu_g, ws, bias_full)


def _silu_mul(accs, extras):
    (up,) = accs
    gt = extras[0]
    sig = 1.0 / (1.0 + jnp.exp(-gt))
    return gt, up, gt * sig * up


def _silu_mul_bwd(accs, extras):
    (dact,) = accs
    gt = extras[0].astype(F32)
    up = extras[1].astype(F32)
    sig = 1.0 / (1.0 + jnp.exp(-gt))
    silu = gt * sig
    dgt = dact * up * (sig * (1.0 + gt * (1.0 - sig)))
    dup = dact * silu
    return dgt, dup


def _add_residual(accs, extras):
    return (extras[0] + accs[0],)


def _add_residual_and_cast(accs, extras):
    y = extras[0] + accs[0]
    return y, y


def _local_step(x, mem, target, W, sm, rs):
    S = x.shape[0]
    tm = min(1024, S)
    th = min(512, S)
    tq = min(256, S)
    ts = min(512, S)
    tr = min(512, S)
    tk_s = min(2048, S)
    M = mem.shape[0]

    h1 = _rms_fwd("rms_mix", x, sm["norm_mix_g"], tr)
    proj = _mm1("proj_in", h1, W["w_in_t"], "nt", tm, 1024, 2048, F32)
    ymix, p, vn = _mixer_fwd("mixer_fwd", proj, W["pool_w"], sm["pool_scale"], sm["sgu_norm_g"],
                             sm["ws_masked"], sm["bias_full"], ts)
    x1, h2 = _mm_rows("proj_out", [ymix], [W["w_out"]], [(0, 0, 0)], "nn", th, 2048, [x], [sm["norm_xattn_g"]],
                      [F32, BF16], 0, _ep_residual_norm)

    mb = _rms_fwd("rms_mem", mem, sm["norm_mem_g"], M)
    q = _mm1("proj_q", h2, W["w_q"], "nn", tm, 1024, 2048, BF16)
    kk, vv = _mm("proj_kv", [mb], [W["w_k"], W["w_v"]], [(0, 0, 0), (0, 1, 1)], "nn", M, 1024, 2048, [BF16, BF16],
                 n_acc=2)
    o = _attn_fwd("attn_fwd", q, kk, vv, tm)
    x2, h3 = _mm_rows("proj_o", [o], [W["w_o"]], [(0, 0, 0)], "nn", th, 2048, [x1], [sm["norm_ffn_g"]],
                      [F32, BF16], 0, _ep_residual_norm)

    gt32 = _mm1("ffn_gate", h3, W["w_gate_t"], "nt", tm, 1408, 2048, F32)
    gt, up, act = _mm("ffn_up", [h3], [W["w_up_t"]], [(0, 0, 0)], "nt", tm, 512, 2048, [BF16, BF16, BF16],
                      epilogue=_silu_mul, extras=(gt32,))
    dx3, dx3b, d_final_g, loss = _mm_rows(
        "ffn_down", [act], [W["w_down"]], [(0, 0, 0)], "nn", tq, 5632, [x2, target], [sm["final_norm_g"]],
        [F32, BF16], 2, _ep_final_loss, n_scalar_out=1)

    dgt, dup = _mm("ffn_down_dgrad", [dx3b], [W["w_down"]], [(0, 0, 0)], "nt", tm, 512, 2048, [BF16, BF16],
                   epilogue=_silu_mul_bwd, extras=(gt, up))
    rs.push("w_down", _mm1("ffn_down_wgrad", act, dx3b, "tn", 1408, 1024, tk_s, BF16))
    rs.push("w_gate_t", _mm1("ffn_gate_wgrad", dgt, h3, "tn", 1408, 1024, tk_s, BF16))
    rs.reduce("w_down")
    rs.push("w_up_t", _mm1("ffn_up_wgrad", dup, h3, "tn", 1408, 1024, tk_s, BF16))
    rs.reduce("w_gate_t")
    dh3 = _mm("ffn_gate_up_dgrad", [dgt, dup], [W["w_gate_t"], W["w_up_t"]], [(0, 0, 0), (1, 1, 0)], "nn",
              th, 512, 5632, [F32])[0]
    rs.reduce("w_up_t")
    dx2, dx2b, d_ffn_g = _rms_bwd("rms_ffn_bwd", dh3, x2, sm["norm_ffn_g"], dx3, tr)
    rs.finish("w_down")

    rs.push("w_o", _mm1("proj_o_wgrad", o, dx2b, "tn", 1024, 1024, tk_s, BF16))
    rs.finish("w_gate_t")
    do = _mm1("proj_o_dgrad", dx2b, W["w_o"], "nt", tm, 1024, 2048, BF16)
    rs.reduce("w_o")
    dq, dk, dv = _attn_bwd("attn_bwd", q, kk, vv, do, tm)
    rs.push("w_q", _mm1("proj_q_wgrad", h2, dq, "tn", 1024, 1024, tk_s, BF16))
    rs.push("w_k", _mm1("proj_k_wgrad", mb, dk, "tn", 1024, 1024, M, BF16))
    rs.push("w_v", _mm1("proj_v_wgrad", mb, dv, "tn", 1024, 1024, M, BF16))
    rs.finish("w_up_t")
    dx1, dx1b, d_xattn_g = _mm_rows(
        "proj_q_dgrad", [dq], [W["w_q"]], [(0, 0, 0)], "nt", tq, 2048, [x1, dx2], [sm["norm_xattn_g"]],
        [F32, BF16], 1, _ep_norm_bwd)
    rs.reduce("w_q")
    rs.reduce("w_k")
    rs.reduce("w_v")
    dmb = _mm("proj_kv_dgrad", [dk, dv], [W["w_k"], W["w_v"]], [(0, 0, 0), (1, 1, 0)], "nt",
              M, 1024, 2048, [F32])[0]
    (d_mem_g,) = _rms_bwd("rms_mem_bwd", dmb, mem, sm["norm_mem_g"], None, M, want_dx=False)

    rs.push("w_out", _mm1("proj_out_wgrad", ymix, dx1b, "tn", 1024, 1024, tk_s, BF16))
    dymix = _mm1("proj_out_dgrad", dx1b, W["w_out"], "nt", tm, 1024, 2048, F32)
    rs.finish("w_o")
    dproj, d_pool_w, d_pool_scale, d_sgu_g, d_ws, d_b = _mixer_bwd(
        "mixer_bwd", dymix, proj, p, vn, W["pool_w"], sm["pool_scale"], sm["sgu_norm_g"],
        sm["ws_masked"], sm["bias_full"], ts)
    rs.finish("w_q")
    rs.finish("w_k")
    rs.finish("w_v")
    rs.reduce("w_out")
    rs.push("pool_w", d_pool_w.reshape(4, N_DEV, POOL_GROUP // N_DEV, POOL_GROUP).transpose(1, 0, 2, 3)
            .reshape(4 * POOL_GROUP, POOL_GROUP).astype(BF16))
    rs.small("early", dict(
        pool_scale=d_pool_scale, sgu_norm_g=d_sgu_g, w_spatial=d_ws, b_spatial=d_b[:, :, 0],
        norm_xattn_g=d_xattn_g, norm_mem_g=d_mem_g, norm_ffn_g=d_ffn_g, final_norm_g=d_final_g))
    rs.push("w_in_t", _mm1("proj_in_wgrad", dproj, h1, "tn", 1024, 1024, tk_s, BF16))
    rs.finish("w_out")
    rs.reduce("pool_w")
    rs.reduce("w_in_t")
    grad_x, d_mix_g = _mm_rows(
        "proj_in_dgrad", [dproj], [W["w_in_t"]], [(0, 0, 0)], "nn", tq, 3072, [x, dx1], [sm["norm_mix_g"]],
        [F32], 1, _ep_norm_bwd)
    rs.small("late", dict(norm_mix_g=d_mix_g, loss=jnp.pad(loss, ((0, 0), (0, _LANES - 1)))))
    rs.finish_small("early")
    rs.finish("pool_w")
    rs.finish("w_in_t")
    rs.finish_small("late")
    return loss, grad_x


def _mesh_pos():
    return lax.axis_index("x"), lax.axis_index("y"), lax.axis_index("c")


def _handshake(peers):
    barrier = pltpu.get_barrier_semaphore()
    for peer in peers:
        pl.semaphore_signal(barrier, inc=1, device_id=peer, device_id_type=MESH)
    pl.semaphore_wait(barrier, len(peers))


def _seq_all_gather(name, shards, collective_id):
    n = len(shards)

    def body(*refs):
        ins = refs[:n]
        outs = refs[n:2 * n]
        send_sems, recv_sems, local_sems = refs[2 * n:]
        x, y, c = _mesh_pos()
        me, sibling = (x, y, c), (x, y, 1 - c)
        xn, yn, dg = (1 - x, y), (x, 1 - y), (1 - x, 1 - y)
        north = c == 1
        via = (jnp.where(north, xn[0], yn[0]), jnp.where(north, xn[1], yn[1]))
        to = (jnp.where(north, yn[0], xn[0]), jnp.where(north, yn[1], xn[1]))
        _handshake([sibling, (*xn, c), (*yn, c)])

        def copy(a, k, block, target, src=None):
            bx, by, bc = block
            dst = outs[a].at[4 * bx + 2 * by + bc]
            return pltpu.make_async_remote_copy(
                src_ref=dst if src is None else src, dst_ref=dst,
                send_sem=send_sems.at[a, k], recv_sem=recv_sems.at[a, k],
                device_id=target, device_id_type=MESH)

        mine = [pltpu.make_async_copy(ins[a], outs[a].at[4 * x + 2 * y + c], local_sems.at[a]) for a in range(n)]
        for cp in mine:
            cp.start()
        started = []
        for a in range(n):
            first = [copy(a, 0, me, sibling, src=ins[a]), copy(a, 1, me, (*xn, c), src=ins[a]),
                     copy(a, 2, me, (*yn, c), src=ins[a])]
            for cp in first:
                cp.start()
            started += first
        for a in range(n):
            copy(a, 1, (*xn, c), me).wait_recv()
            copy(a, 2, (*yn, c), me).wait_recv()
            second = [copy(a, 3, (*via, c), (*to, c)), copy(a, 4, (*xn, c), sibling), copy(a, 5, (*yn, c), sibling)]
            for cp in second:
                cp.start()
            started += second
        for a in range(n):
            copy(a, 3, (*dg, c), me).wait_recv()
            last = copy(a, 6, (*dg, c), sibling)
            last.start()
            started.append(last)
        for a in range(n):
            copy(a, 0, sibling, me).wait_recv()
            for k, chip in ((4, xn), (5, yn), (6, dg)):
                copy(a, k, (*chip, 1 - c), me).wait_recv()
        for cp in started:
            cp.wait_send()
        for cp in mine:
            cp.wait()

    return _sc_call(
        body, name=name,
        out_type=[_sds((N_DEV,) + s.shape, s.dtype) for s in shards],
        scratch_types=[pltpu.SemaphoreType.DMA((n, 7)), pltpu.SemaphoreType.DMA((n, 7)),
                       pltpu.SemaphoreType.DMA((n,))],
        compiler_params=pltpu.CompilerParams(collective_id=collective_id),
    )(*shards)


def _seq_pair_exchange(name, gview, collective_id):
    def body(g_ref, theirs_ref, send_sems, recv_sems):
        x, y, c = _mesh_pos()
        sibling = (x, y, 1 - c)
        _handshake([sibling])
        copies = [pltpu.make_async_remote_copy(
            src_ref=g_ref.at[k, 1 - c], dst_ref=theirs_ref.at[k],
            send_sem=send_sems.at[k], recv_sem=recv_sems.at[k],
            device_id=sibling, device_id_type=MESH) for k in range(4)]
        for cp in copies:
            cp.start()
        for cp in copies:
            cp.wait()

    return _sc_call(
        body, name=name, out_type=_sds((4,) + gview.shape[2:], gview.dtype),
        scratch_types=[pltpu.SemaphoreType.DMA((4,)), pltpu.SemaphoreType.DMA((4,))],
        compiler_params=pltpu.CompilerParams(collective_id=collective_id),
    )(gview)


def _pair_sum(name, gview, theirs, pos, tr):
    _, _, r, C = gview.shape

    def body(pos_ref, a_ref, b_ref, o_ref):
        o_ref[...] = (a_ref[...].astype(F32) + b_ref[...].astype(F32)).astype(o_ref.dtype)

    grid_spec = pltpu.PrefetchScalarGridSpec(
        num_scalar_prefetch=1, grid=(4, r // tr),
        in_specs=[pl.BlockSpec((None, None, tr, C), lambda k, t, pos_ref: (k, pos_ref[0], t, 0)),
                  pl.BlockSpec((None, tr, C), lambda k, t, pos_ref: (k, t, 0))],
        out_specs=pl.BlockSpec((None, tr, C), lambda k, t, pos_ref: (k, t, 0)))
    return _tc_call(
        body, name=name, grid_spec=grid_spec, out_shape=_sds(theirs.shape, theirs.dtype),
        compiler_params=_cparams(("parallel", "parallel")),
    )(pos, gview, theirs)


def _seq_chip_exchange(name, pair, collective_id):
    def body(p_ref, land_ref, send_sems, recv_sems):
        x, y, c = _mesh_pos()
        my_chip = 2 * x + y
        chips = [(1 - x, y), (x, 1 - y), (1 - x, 1 - y)]
        _handshake([(cx, cy, c) for cx, cy in chips])
        copies = [pltpu.make_async_remote_copy(
            src_ref=p_ref.at[2 * cx + cy], dst_ref=land_ref.at[my_chip],
            send_sem=send_sems.at[j], recv_sem=recv_sems.at[j],
            device_id=(cx, cy, c), device_id_type=MESH) for j, (cx, cy) in enumerate(chips)]
        for cp in copies:
            cp.start()
        for cp in copies:
            cp.wait_send()
        for j, (cx, cy) in enumerate(chips):
            pltpu.make_async_remote_copy(
                src_ref=p_ref.at[my_chip], dst_ref=land_ref.at[2 * cx + cy],
                send_sem=send_sems.at[j], recv_sem=recv_sems.at[j],
                device_id=(cx, cy, c), device_id_type=MESH).wait_recv()

    return _sc_call(
        body, name=name, out_type=_sds(pair.shape, pair.dtype),
        scratch_types=[pltpu.SemaphoreType.DMA((3,)), pltpu.SemaphoreType.DMA((3,))],
        compiler_params=pltpu.CompilerParams(collective_id=collective_id),
    )(pair)


def _sum_leading(name, parts, tr, out_dtype=F32):
    n, r, C = parts.shape

    def body(p_ref, o_ref):
        acc = p_ref[0].astype(F32)
        for k in range(1, n):
            acc = acc + p_ref[k].astype(F32)
        o_ref[...] = acc.astype(o_ref.dtype)

    return _tc_call(
        body, name=name, grid=(r // tr,),
        in_specs=[pl.BlockSpec((n, tr, C), lambda t: (0, t, 0))],
        out_specs=pl.BlockSpec((tr, C), lambda t: (t, 0)),
        out_shape=_sds((r, C), out_dtype), compiler_params=_cparams(("parallel",)),
    )(parts)


def _row_tile(r):
    for t in (512, 384, 352, 256, 128, 64, 32, 16, 8):
        if r % t == 0:
            return t
    return r


def _adamw_math(w, g, m, v):
    c1 = 1.0 - ADAM_B1 ** ADAM_STEP
    c2 = 1.0 - ADAM_B2 ** ADAM_STEP
    nm = ADAM_B1 * m + (1.0 - ADAM_B1) * g
    nv = ADAM_B2 * v + (1.0 - ADAM_B2) * (g * g)
    m_hat = nm / c1
    v_hat = nv / c2
    return -ADAM_LR * (m_hat / (jnp.sqrt(v_hat) + ADAM_EPS) + ADAM_WD * w), nm, nv


def _chip_sum_adamw(name, pair, landed, pos, w, m, v, transposed):
    _, r, C = pair.shape
    if transposed:
        tr, tc = r, 512
        r_pad = -r % _LANES
        wspec = pl.BlockSpec((tc, r), lambda t, k, pos_ref: (t, 0))
        shape = (C, r)
        scratch = [pltpu.VMEM((tr, tc), F32), pltpu.VMEM((tc, r + r_pad), F32)]
    else:
        tr, tc = _row_tile(r), C
        wspec = pl.BlockSpec((tr, C), lambda t, k, pos_ref: (t, 0))
        shape = (r, C)
        scratch = [pltpu.VMEM((tr, tc), F32)]
    n_t = (C // tc) if transposed else (r // tr)

    def block(chip, t):
        return (chip, 0, t) if transposed else (chip, t, 0)

    def body(pos_ref, own_ref, land_ref, w_ref, m_ref, v_ref, g_ref, d_ref, nm_ref, nv_ref, acc_ref, *turn):
        k = pl.program_id(1)
        val = jnp.where(k == pos_ref[1], own_ref[...], land_ref[...]).astype(F32)

        @pl.when(k == 0)
        def _():
            acc_ref[...] = val

        @pl.when(k > 0)
        def _():
            acc_ref[...] += val

        @pl.when(k == 3)
        def _():
            if transposed:
                g_t = acc_ref[...]
                if r_pad:
                    g_t = jnp.concatenate([g_t, jnp.zeros((r_pad, tc), F32)], axis=0)
                turn[0][...] = g_t.T
                g = turn[0][:, 0:r]
            else:
                g = acc_ref[...]
            d, nm, nv = _adamw_math(w_ref[...], g, m_ref[...], v_ref[...])
            g_ref[...] = g
            d_ref[...] = d
            nm_ref[...] = nm
            nv_ref[...] = nv

    def land_index(t, k, pos_ref):
        return block(jnp.where(k == pos_ref[1], (k + 1) % 4, k), t)

    grid_spec = pltpu.PrefetchScalarGridSpec(
        num_scalar_prefetch=1, grid=(n_t, 4),
        in_specs=[pl.BlockSpec((None, tr, tc), lambda t, k, pos_ref: block(pos_ref[1], t)),
                  pl.BlockSpec((None, tr, tc), land_index), wspec, wspec, wspec],
        out_specs=[wspec] * 4, scratch_shapes=scratch)
    return _tc_call(
        body, name=name, grid_spec=grid_spec, out_shape=[_sds(shape, F32)] * 4,
        compiler_params=_cparams(("parallel", "arbitrary")),
    )(pos, pair, landed, w, m, v)


def _adamw(name, w, g, m, v):
    R, C = w.shape
    tr = _row_tile(R)

    def body(w_ref, g_ref, m_ref, v_ref, d_ref, nm_ref, nv_ref):
        d_ref[...], nm_ref[...], nv_ref[...] = _adamw_math(w_ref[...], g_ref[...], m_ref[...], v_ref[...])

    spec = pl.BlockSpec((tr, C), lambda i: (i, 0))
    return _tc_call(
        body, name=name, grid=(R // tr,), in_specs=[spec] * 4, out_specs=[spec] * 3,
        out_shape=[_sds((R, C), F32)] * 3, compiler_params=_cparams(("parallel",)),
    )(w, g, m, v)


_BIG = ("w_in_t", "w_out", "w_q", "w_k", "w_v", "w_o", "w_gate_t", "w_up_t", "w_down")
_SMALL = ("norm_mix_g", "pool_scale", "sgu_norm_g", "w_spatial", "b_spatial", "norm_xattn_g",
          "norm_mem_g", "norm_ffn_g", "final_norm_g")
_LANES = 128
_GATHER_GROUPS = (("w_in_t", "pool_w"), ("w_out",), ("w_q",), ("w_k", "w_v"), ("w_o",), ("w_gate_t",),
                  ("w_up_t",), ("w_down",))
_RS_ORDER = ("w_down", "w_gate_t", "w_up_t", "w_o", "w_q", "w_k", "w_v", "w_out", "w_in_t")
_SMALL_GROUPS = dict(
    early=("pool_scale", "sgu_norm_g", "w_spatial", "b_spatial", "norm_xattn_g", "norm_mem_g",
           "norm_ffn_g", "final_norm_g"),
    late=("norm_mix_g", "loss"))
_TURN_OUTSIDE = ("w_gate_t", "w_up_t")
_ID_GATHER, _ID_PAIR, _ID_CHIP = 0, 1, 2


_PACK_ROWS = 512


def _pack(parts):
    rows = [p.reshape(-1, _LANES) for p in parts]
    n = sum(r.shape[0] for r in rows)
    pad = -n % (_PACK_ROWS if n > _PACK_ROWS else 8)
    if pad:
        rows.append(jnp.zeros((pad, _LANES), rows[0].dtype))
    return jnp.concatenate(rows, axis=0)


class _GradReducer:
    def __init__(self, pos, apply, apply_small):
        self.pos, self.apply, self.apply_small = pos, apply, apply_small
        self.view, self.theirs, self.pair, self.landed = {}, {}, {}, {}
        self.small_gathered = {}

    def push(self, k, g):
        r = g.shape[0] // N_DEV
        self.view[k] = g.reshape(4, 2, r, g.shape[1])
        self.theirs[k] = _seq_pair_exchange("grad_pair_exchange_" + k, self.view[k], _ID_PAIR)

    def reduce(self, k):
        r = self.view[k].shape[2]
        self.pair[k] = _pair_sum("grad_pair_sum_" + k, self.view[k], self.theirs[k], self.pos, r)
        self.landed[k] = _seq_chip_exchange("grad_chip_exchange_" + k, self.pair[k], _ID_CHIP)

    def finish(self, k):
        self.apply(k, self.pair[k], self.landed[k])

    def small(self, tag, parts):
        packed = _pack([parts[k] for k in _SMALL_GROUPS[tag]])
        (self.small_gathered[tag],) = _seq_all_gather("gather_small_grads_" + tag, [packed], _ID_GATHER)

    def finish_small(self, tag):
        allp = self.small_gathered[tag]
        self.apply_small(tag, _sum_leading("sum_small_grads_" + tag, allp, min(_PACK_ROWS, allp.shape[1])))


def _unpack(packed, like):
    out, row = [], 0
    for ref in like:
        rows = ref.size // _LANES
        out.append(packed[row:row + rows].reshape(ref.shape))
        row += rows
    return out


def kernel(x, mem, norm_mix_g, w_in, pool_w, pool_scale, sgu_norm_g, w_spatial, b_spatial, w_out, norm_xattn_g, norm_mem_g, w_q, w_k, w_v, w_o, norm_ffn_g, w_gate, w_up, w_down, final_norm_g, loss_target, m_norm_mix_g, m_w_in, m_pool_w, m_pool_scale, m_sgu_norm_g, m_w_spatial, m_b_spatial, m_w_out, m_norm_xattn_g, m_norm_mem_g, m_w_q, m_w_k, m_w_v, m_w_o, m_norm_ffn_g, m_w_gate, m_w_up, m_w_down, m_final_norm_g, v_norm_mix_g, v_w_in, v_pool_w, v_pool_scale, v_sgu_norm_g, v_w_spatial, v_b_spatial, v_w_out, v_norm_xattn_g, v_norm_mem_g, v_w_q, v_w_k, v_w_v, v_w_o, v_norm_ffn_g, v_w_gate, v_w_up, v_w_down, v_final_norm_g):
    args = dict(locals())
    names = ("norm_mix_g", "w_in", "pool_w", "pool_scale", "sgu_norm_g", "w_spatial", "b_spatial", "w_out",
             "norm_xattn_g", "norm_mem_g", "w_q", "w_k", "w_v", "w_o", "norm_ffn_g", "w_gate", "w_up",
             "w_down", "final_norm_g")
    w = {k: args[k] for k in names}
    m = {k: args["m_" + k] for k in names}
    v = {k: args["v_" + k] for k in names}
    _CHAIN.__init__()

    shards = dict(
        w_in_t=w["w_in"][0].T, w_out=w["w_out"][0], w_q=w["w_q"][0], w_k=w["w_k"][0], w_v=w["w_v"][0],
        w_o=w["w_o"][0], w_gate_t=w["w_gate"][0].T, w_up_t=w["w_up"][0].T, w_down=w["w_down"][0])
    send = {k: shards[k].astype(BF16) for k in _BIG}
    send["pool_w"] = w["pool_w"][0].reshape(4 * 32, POOL_GROUP).astype(BF16)
    W = {}
    for gi, group in enumerate(_GATHER_GROUPS):
        gathered = _seq_all_gather("gather_weights_%d" % gi, [send[k] for k in group], _ID_GATHER)
        for k, g in zip(group, gathered):
            W[k] = g.reshape(-1, g.shape[-1])
    W["pool_w"] = W["pool_w"].reshape(N_DEV, 4, 32, POOL_GROUP).transpose(1, 0, 2, 3).reshape(4, POOL_GROUP, POOL_GROUP)

    t = jnp.arange(SGU_BLOCK)
    mask = (t[None, :] // SGU_CHUNK) <= (t[:, None] // SGU_CHUNK)
    sm = dict(
        norm_mix_g=w["norm_mix_g"], pool_scale=w["pool_scale"], sgu_norm_g=w["sgu_norm_g"],
        norm_xattn_g=w["norm_xattn_g"], norm_mem_g=w["norm_mem_g"], norm_ffn_g=w["norm_ffn_g"],
        final_norm_g=w["final_norm_g"].reshape(1, D_MODEL),
        ws_masked=jnp.where(mask[None], w["w_spatial"][0], 0.0).astype(BF16),
        bias_full=jnp.repeat(w["b_spatial"][0].T, SGU_BLOCK, axis=1))

    natural = dict(w_in_t="w_in", w_gate_t="w_gate", w_up_t="w_up")
    grads, delta, new_m, new_v = {}, {}, {}, {}

    def apply(k, pair, landed):
        name = natural.get(k, k)
        if k == "pool_w":
            flat = (4 * POOL_GROUP // N_DEV, POOL_GROUP)
            res = _chip_sum_adamw("grad_finish_" + k, pair, landed, pos, w[k].reshape(flat), m[k].reshape(flat),
                                  v[k].reshape(flat), False)
            grads[k], delta[k], new_m[k], new_v[k] = (a.reshape(w[k].shape) for a in res)
            return
        if k in _TURN_OUTSIDE:
            res = _chip_sum_adamw("grad_finish_" + k, pair, landed, pos, w[name][0].T, m[name][0].T, v[name][0].T,
                                  False)
            res = [a.T for a in res]
        else:
            res = _chip_sum_adamw("grad_finish_" + k, pair, landed, pos, w[name][0], m[name][0], v[name][0],
                                  k in natural)
        grads[name], delta[name], new_m[name], new_v[name] = (a[None] for a in res)

    like = dict(w)
    like["loss"] = _sds((1, _LANES), F32)

    def apply_small(tag, total):
        group = _SMALL_GROUPS[tag]
        grads.update(zip(group, _unpack(total, [like[k] for k in group])))
        if tag == "late":
            d_, m_, v_ = _adamw("adamw_small", _pack([w[k] for k in _SMALL]), _pack([grads[k] for k in _SMALL]),
                                _pack([m[k] for k in _SMALL]), _pack([v[k] for k in _SMALL]))
            shapes = [w[k] for k in _SMALL]
            for k, a, b, c_ in zip(_SMALL, _unpack(d_, shapes), _unpack(m_, shapes), _unpack(v_, shapes)):
                delta[k], new_m[k], new_v[k] = a, b, c_

    pos = jnp.stack([lax.axis_index("c"), 2 * lax.axis_index("x") + lax.axis_index("y")]).astype(jnp.int32)
    rs = _GradReducer(pos, apply, apply_small)
    _, grad_x = _local_step(x[0], mem[0], loss_target[0], W, sm, rs)

    outs = [grads["loss"][0, 0], grad_x[None]]
    outs += [grads[k].reshape(w[k].shape) for k in names]
    outs += [delta[k] for k in names]
    outs += [new_m[k] for k in names]
    outs += [new_v[k] for k in names]
    return tuple(outs)
```

```python
import functools

import jax
import jax.numpy as jnp
from jax import lax
from jax.experimental import pallas as pl
from jax.experimental.pallas import tpu as pltpu
from jax.experimental.pallas import tpu_sc as plsc

F32 = jnp.float32
BF16 = jnp.bfloat16
MESH = pl.DeviceIdType.MESH

EPS = 1e-6
D_MODEL = 2048
D_POOL = 1024
D_SGU = 1024
POOL_WINDOWS = (2, 4, 8, 16)
POOL_GROUP = 256
POOL_HALO = 16
SGU_BLOCK = 128
SGU_CHUNK = 64
N_SGU_HEADS = 8
N_HEADS = 4
HEAD_DIM = 512
N_DEV = 8

ADAM_LR = 0.001
ADAM_B1 = 0.9
ADAM_B2 = 0.999
ADAM_EPS = 1e-08
ADAM_WD = 0.01
ADAM_STEP = 10

VMEM_LIMIT = 56 * 1024 * 1024


def _cparams(sem=None):
    return pltpu.CompilerParams(dimension_semantics=sem, vmem_limit_bytes=VMEM_LIMIT)


def _sds(shape, dtype):
    return jax.ShapeDtypeStruct(shape, dtype)


_ANY = pl.BlockSpec(memory_space=pl.ANY)


class _Chain:
    def __init__(self):
        self.tc = None
        self.sc = None


_CHAIN = _Chain()


def _first(out):
    return out[0] if isinstance(out, (list, tuple)) else out


def _tc_call(body, *, in_specs=None, grid_spec=None, **kw):
    def run(*args):
        prev, n = _CHAIN.tc, len(args)
        fn, specs, spec, operands = body, in_specs, grid_spec, args
        if prev is not None:
            def fn(*refs):
                return body(*refs[:n], *refs[n + 1:])
            operands = args + (prev,)
            if grid_spec is None:
                specs = list(in_specs) + [_ANY]
            else:
                spec = pltpu.PrefetchScalarGridSpec(
                    num_scalar_prefetch=grid_spec.num_scalar_prefetch, grid=grid_spec.grid,
                    in_specs=list(grid_spec.in_specs) + [_ANY], out_specs=grid_spec.out_specs,
                    scratch_shapes=grid_spec.scratch_shapes)
        if spec is None:
            out = pl.pallas_call(fn, in_specs=specs, **kw)(*operands)
        else:
            out = pl.pallas_call(fn, grid_spec=spec, **kw)(*operands)
        _CHAIN.tc = _first(out)
        return out
    return run


def _sc_call(body, **kw):
    return pl.kernel(body, mesh=plsc.ScalarSubcoreMesh(axis_name="seq", num_cores=1), **kw)


def _rowsum8(v):
    r, c = v.shape
    return v.reshape(r // 8, 8, c).sum(axis=0)


_DN = {
    "nn": (((1,), (0,)), ((), ())),
    "nt": (((1,), (1,)), ((), ())),
    "tn": (((0,), (0,)), ((), ())),
}


def _mm(name, a_list, b_list, terms, mode, tm, tn, tk, out_dtypes, epilogue=None, extras=(), n_acc=1):
    a0, b0 = a_list[0], b_list[0]
    if mode == "tn":
        K, M = a0.shape
    else:
        M, K = a0.shape
    N = b0.shape[0] if mode == "nt" else b0.shape[1]
    assert M % tm == 0 and N % tn == 0 and K % tk == 0, (name, M, N, K, tm, tn, tk)
    nk = K // tk
    na, nb, ne, no = len(a_list), len(b_list), len(extras), len(out_dtypes)
    dn = _DN[mode]

    if mode == "tn":
        a_spec = pl.BlockSpec((tk, tm), lambda i, j, k: (k, i))
    else:
        a_spec = pl.BlockSpec((tm, tk), lambda i, j, k: (i, k))
    if mode == "nt":
        b_spec = pl.BlockSpec((tn, tk), lambda i, j, k: (j, k))
    else:
        b_spec = pl.BlockSpec((tk, tn), lambda i, j, k: (k, j))
    o_spec = pl.BlockSpec((tm, tn), lambda i, j, k: (i, j))

    def body(*refs):
        a_refs = refs[:na]
        b_refs = refs[na:na + nb]
        e_refs = refs[na + nb:na + nb + ne]
        o_refs = refs[na + nb + ne:na + nb + ne + no]
        acc_refs = refs[na + nb + ne + no:]

        parts = [None] * n_acc
        for ai, bi, ci in terms:
            d = lax.dot_general(a_refs[ai][...].astype(BF16), b_refs[bi][...].astype(BF16), dn,
                                preferred_element_type=F32)
            parts[ci] = d if parts[ci] is None else parts[ci] + d

        def finish(accs):
            outs = epilogue(accs, [e[...] for e in e_refs]) if epilogue is not None else accs
            for o_ref, v in zip(o_refs, outs):
                o_ref[...] = v.astype(o_ref.dtype)

        if nk == 1:
            finish(parts)
        else:
            k = pl.program_id(2)

            @pl.when(k == 0)
            def _():
                for c in range(n_acc):
                    acc_refs[c][...] = parts[c]

            @pl.when(k > 0)
            def _():
                for c in range(n_acc):
                    acc_refs[c][...] += parts[c]

            @pl.when(k == nk - 1)
            def _():
                finish([acc_refs[c][...] for c in range(n_acc)])

    scratch = [pltpu.VMEM((tm, tn), F32) for _ in range(n_acc)] if nk > 1 else []
    return _tc_call(
        body, name=name, grid=(M // tm, N // tn, nk),
        in_specs=[a_spec] * na + [b_spec] * nb + [o_spec] * ne,
        out_specs=[o_spec] * no,
        out_shape=[_sds((M, N), dt) for dt in out_dtypes],
        scratch_shapes=scratch,
        compiler_params=_cparams(("parallel", "parallel", "arbitrary")),
    )(*a_list, *b_list, *extras)


def _mm_rows(name, a_list, b_list, terms, mode, tm, tk, rows, vecs, row_dtypes, n_vec_out, epilogue,
             n_scalar_out=0):
    a0, b0 = a_list[0], b_list[0]
    M, K = a0.shape
    N = b0.shape[0] if mode == "nt" else b0.shape[1]
    assert mode in ("nn", "nt") and M % tm == 0 and K % tk == 0, (name, M, N, K, tm, tk)
    nm, nk = M // tm, K // tk
    slab = min(128, tm)
    na, nb, nr, nv, no = len(a_list), len(b_list), len(rows), len(vecs), len(row_dtypes)
    dn = _DN[mode]
    a_spec = pl.BlockSpec((tm, tk), lambda i, k: (i, k))
    b_mode = dict(pipeline_mode=pl.Buffered(1)) if nk == 1 else {}
    b_spec = (pl.BlockSpec((N, tk), lambda i, k: (0, k), **b_mode) if mode == "nt"
              else pl.BlockSpec((tk, N), lambda i, k: (k, 0), **b_mode))
    row_spec = pl.BlockSpec((tm, N), lambda i, k: (i, 0))
    vec_spec = pl.BlockSpec((1, N), lambda i, k: (0, 0))
    one_spec = pl.BlockSpec((1, 1), lambda i, k: (0, 0))

    def body(*refs):
        pos = 0
        a_refs = refs[pos:pos + na]; pos += na
        b_refs = refs[pos:pos + nb]; pos += nb
        r_refs = refs[pos:pos + nr]; pos += nr
        v_refs = refs[pos:pos + nv]; pos += nv
        o_refs = refs[pos:pos + no]; pos += no
        s_refs = refs[pos:pos + n_vec_out]; pos += n_vec_out
        vacc_refs = refs[pos:pos + n_vec_out]; pos += n_vec_out
        acc_ref = refs[pos] if nk > 1 else None
        i, k = pl.program_id(0), pl.program_id(1)
        part = None
        for ai, bi, _ in terms:
            d = lax.dot_general(a_refs[ai][...].astype(BF16), b_refs[bi][...].astype(BF16), dn,
                                preferred_element_type=F32)
            part = d if part is None else part + d

        def finish(acc):
            vecs_now = [v[...] for v in v_refs]
            vparts = None
            for r0 in range(0, tm, slab):
                rs_ = slice(r0, r0 + slab)
                outs, vp = epilogue(acc[rs_, :], [r[rs_, :] for r in r_refs], vecs_now)
                for o_ref, val in zip(o_refs, outs):
                    o_ref[rs_, :] = val.astype(o_ref.dtype)
                vparts = vp if vparts is None else [a + b for a, b in zip(vparts, vp)]

            @pl.when(i == 0)
            def _():
                for vacc, vp in zip(vacc_refs, vparts):
                    vacc[...] = vp

            @pl.when(i > 0)
            def _():
                for vacc, vp in zip(vacc_refs, vparts):
                    vacc[...] += vp

            @pl.when(i == nm - 1)
            def _():
                for j, (s_ref, vacc) in enumerate(zip(s_refs, vacc_refs)):
                    col = jnp.sum(vacc[...], axis=0, keepdims=True)
                    s_ref[...] = jnp.sum(col, axis=1, keepdims=True) if j >= n_vec_out - n_scalar_out else col

        if nk == 1:
            finish(part)
        else:
            @pl.when(k == 0)
            def _():
                acc_ref[...] = part

            @pl.when(k > 0)
            def _():
                acc_ref[...] += part

            @pl.when(k == nk - 1)
            def _():
                finish(acc_ref)

    n_plain = n_vec_out - n_scalar_out
    return _tc_call(
        body, name=name, grid=(nm, nk),
        in_specs=[a_spec] * na + [b_spec] * nb + [row_spec] * nr + [vec_spec] * nv,
        out_specs=[row_spec] * no + [vec_spec] * n_plain + [one_spec] * n_scalar_out,
        out_shape=[_sds((M, N), dt) for dt in row_dtypes] + [_sds((1, N), F32)] * n_plain
        + [_sds((1, 1), F32)] * n_scalar_out,
        scratch_shapes=[pltpu.VMEM((8, N), F32)] * n_vec_out + ([pltpu.VMEM((tm, N), F32)] if nk > 1 else []),
        compiler_params=_cparams(("arbitrary", "arbitrary")),
    )(*a_list, *b_list, *rows, *vecs)


def _ep_residual_norm(acc, rows, vecs):
    x_new = rows[0] + acc
    r = lax.rsqrt(jnp.mean(x_new * x_new, axis=-1, keepdims=True) + EPS)
    return [x_new, x_new * r * vecs[0]], []


def _ep_norm_bwd(acc, rows, vecs):
    xv, dres = rows
    r = lax.rsqrt(jnp.mean(xv * xv, axis=-1, keepdims=True) + EPS)
    xh = xv * r
    dxh = acc * vecs[0]
    m = jnp.mean(dxh * xh, axis=-1, keepdims=True)
    dx = dres + r * (dxh - xh * m)
    return [dx, dx], [_rowsum8(acc * xh)]


def _ep_final_loss(acc, rows, vecs):
    x2, target = rows
    gv = vecs[0]
    xv = x2 + acc
    inv_d = 1.0 / xv.shape[-1]
    r = lax.rsqrt(jnp.mean(xv * xv, axis=-1, keepdims=True) + EPS)
    xh = xv * r
    e = xh * gv - target
    dy = e * inv_d
    dxh = dy * gv
    m = jnp.mean(dxh * xh, axis=-1, keepdims=True)
    dx = r * (dxh - xh * m)
    return [dx, dx], [_rowsum8(dy * xh), _rowsum8(e * e) * (0.5 * inv_d)]


def _mm1(name, a, b, mode, tm, tn, tk, out_dtype, **kw):
    return _mm(name, [a], [b], [(0, 0, 0)], mode, tm, tn, tk, [out_dtype], **kw)[0]


def _rms_fwd(name, x, g, tr):
    S, Dm = x.shape

    def body(x_ref, g_ref, h_ref):
        xv = x_ref[...]
        r = lax.rsqrt(jnp.mean(xv * xv, axis=-1, keepdims=True) + EPS)
        h_ref[...] = (xv * r * g_ref[...]).astype(h_ref.dtype)

    return _tc_call(
        body, name=name, grid=(S // tr,),
        in_specs=[pl.BlockSpec((tr, Dm), lambda i: (i, 0)), pl.BlockSpec((1, Dm), lambda i: (0, 0))],
        out_specs=pl.BlockSpec((tr, Dm), lambda i: (i, 0)),
        out_shape=_sds((S, Dm), BF16),
        compiler_params=_cparams(("parallel",)),
    )(x, g)


def _rms_bwd(name, dh, x, g, dres, tr, want_dx=True):
    S, Dm = x.shape
    nsteps = S // tr

    def body(*refs):
        if want_dx:
            dh_ref, x_ref, g_ref, dres_ref, dx_ref, dxb_ref, dg_ref, acc_ref = refs
        else:
            dh_ref, x_ref, g_ref, dg_ref, acc_ref = refs
        i = pl.program_id(0)
        xv = x_ref[...]
        r = lax.rsqrt(jnp.mean(xv * xv, axis=-1, keepdims=True) + EPS)
        xh = xv * r
        dhv = dh_ref[...]
        part = _rowsum8(dhv * xh)

        @pl.when(i == 0)
        def _():
            acc_ref[...] = part

        @pl.when(i > 0)
        def _():
            acc_ref[...] += part

        @pl.when(i == nsteps - 1)
        def _():
            dg_ref[...] = jnp.sum(acc_ref[...], axis=0, keepdims=True)

        if want_dx:
            dxh = dhv * g_ref[...]
            m = jnp.mean(dxh * xh, axis=-1, keepdims=True)
            dx = dres_ref[...] + r * (dxh - xh * m)
            dx_ref[...] = dx
            dxb_ref[...] = dx.astype(BF16)

    row = pl.BlockSpec((tr, Dm), lambda i: (i, 0))
    vec = pl.BlockSpec((1, Dm), lambda i: (0, 0))
    if want_dx:
        in_specs = [row, row, vec, row]
        out_specs = [row, row, vec]
        out_shape = [_sds((S, Dm), F32), _sds((S, Dm), BF16), _sds((1, Dm), F32)]
        args = (dh, x, g, dres)
    else:
        in_specs = [row, row, vec]
        out_specs = [vec]
        out_shape = [_sds((1, Dm), F32)]
        args = (dh, x, g)
    return _tc_call(
        body, name=name, grid=(nsteps,), in_specs=in_specs, out_specs=out_specs, out_shape=out_shape,
        scratch_shapes=[pltpu.VMEM((8, Dm), F32)],
        compiler_params=_cparams(("arbitrary",)),
    )(*args)


def _final_loss(name, x3, g, target, tr):
    S, Dm = x3.shape
    nsteps = S // tr

    def body(x_ref, g_ref, t_ref, dx_ref, dxb_ref, dg_ref, loss_ref, acc_g, acc_l):
        i = pl.program_id(0)
        xv = x_ref[...]
        gv = g_ref[...]
        r = lax.rsqrt(jnp.mean(xv * xv, axis=-1, keepdims=True) + EPS)
        xh = xv * r
        e = xh * gv - t_ref[...]
        dy = e * (1.0 / Dm)
        lpart = _rowsum8(e * e)
        gpart = _rowsum8(dy * xh)

        @pl.when(i == 0)
        def _():
            acc_g[...] = gpart
            acc_l[...] = lpart

        @pl.when(i > 0)
        def _():
            acc_g[...] += gpart
            acc_l[...] += lpart

        @pl.when(i == nsteps - 1)
        def _():
            dg_ref[...] = jnp.sum(acc_g[...], axis=0, keepdims=True)
            tot = jnp.sum(jnp.sum(acc_l[...], axis=1, keepdims=True), axis=0, keepdims=True)
            loss_ref[...] = tot * (0.5 / Dm)

        dxh = dy * gv
        m = jnp.mean(dxh * xh, axis=-1, keepdims=True)
        dx = r * (dxh - xh * m)
        dx_ref[...] = dx
        dxb_ref[...] = dx.astype(BF16)

    row = pl.BlockSpec((tr, Dm), lambda i: (i, 0))
    vec = pl.BlockSpec((1, Dm), lambda i: (0, 0))
    return _tc_call(
        body, name=name, grid=(nsteps,),
        in_specs=[row, vec, row],
        out_specs=[row, row, vec, pl.BlockSpec((1, 1), lambda i: (0, 0))],
        out_shape=[_sds((S, Dm), F32), _sds((S, Dm), BF16), _sds((1, Dm), F32), _sds((1, 1), F32)],
        scratch_shapes=[pltpu.VMEM((8, Dm), F32), pltpu.VMEM((8, Dm), F32)],
        compiler_params=_cparams(("arbitrary",)),
    )(x3, g, target)


def _softmax_rows(s):
    e = jnp.exp(s - jnp.max(s, axis=-1, keepdims=True))
    return e / jnp.sum(e, axis=-1, keepdims=True)


def _attn_fwd(name, q, k, v, ts):
    S, Dm = q.shape
    M = k.shape[0]
    scale = HEAD_DIM ** -0.5

    def body(q_ref, k_ref, v_ref, o_ref):
        for h in range(N_HEADS):
            sl = slice(h * HEAD_DIM, (h + 1) * HEAD_DIM)
            s = lax.dot_general(q_ref[:, sl], k_ref[:, sl], _DN["nt"], preferred_element_type=F32) * scale
            p = _softmax_rows(s)
            o_ref[:, sl] = jnp.dot(p.astype(BF16), v_ref[:, sl], preferred_element_type=F32).astype(o_ref.dtype)

    row = pl.BlockSpec((ts, Dm), lambda i: (i, 0))
    mem = pl.BlockSpec((M, Dm), lambda i: (0, 0))
    return _tc_call(
        body, name=name, grid=(S // ts,), in_specs=[row, mem, mem], out_specs=row,
        out_shape=_sds((S, Dm), BF16), compiler_params=_cparams(("parallel",)),
    )(q, k, v)


def _attn_bwd(name, q, k, v, do, ts):
    S, Dm = q.shape
    M = k.shape[0]
    scale = HEAD_DIM ** -0.5

    def body(q_ref, k_ref, v_ref, do_ref, dq_ref, dk_ref, dv_ref):
        i = pl.program_id(0)

        @pl.when(i == 0)
        def _():
            dk_ref[...] = jnp.zeros_like(dk_ref)
            dv_ref[...] = jnp.zeros_like(dv_ref)

        for h in range(N_HEADS):
            sl = slice(h * HEAD_DIM, (h + 1) * HEAD_DIM)
            qh = q_ref[:, sl]
            kh = k_ref[:, sl]
            doh = do_ref[:, sl]
            s = lax.dot_general(qh, kh, _DN["nt"], preferred_element_type=F32) * scale
            p = _softmax_rows(s)
            dp = lax.dot_general(doh, v_ref[:, sl], _DN["nt"], preferred_element_type=F32)
            ds = p * (dp - jnp.sum(dp * p, axis=-1, keepdims=True)) * scale
            dsb = ds.astype(BF16)
            dq_ref[:, sl] = jnp.dot(dsb, kh, preferred_element_type=F32).astype(dq_ref.dtype)
            dk_ref[:, sl] += lax.dot_general(dsb, qh, _DN["tn"], preferred_element_type=F32)
            dv_ref[:, sl] += lax.dot_general(p.astype(BF16), doh, _DN["tn"], preferred_element_type=F32)

    row = pl.BlockSpec((ts, Dm), lambda i: (i, 0))
    mem = pl.BlockSpec((M, Dm), lambda i: (0, 0))
    return _tc_call(
        body, name=name, grid=(S // ts,), in_specs=[row, mem, mem, row], out_specs=[row, mem, mem],
        out_shape=[_sds((S, Dm), BF16), _sds((M, Dm), F32), _sds((M, Dm), F32)],
        compiler_params=_cparams(("arbitrary",)),
    )(q, k, v, do)


def _pool_denominators(row0, ts):
    return (row0 + lax.broadcasted_iota(jnp.int32, (ts, 1), 0) + 1).astype(F32)


def _mixer_fwd(name, proj, pool_w, pool_scale, sgu_g, ws, bias_full, ts):
    S = proj.shape[0]
    nblk = ts // SGU_BLOCK
    halo_blocks = ts // POOL_HALO

    def body(proj_ref, halo_ref, pw_ref, sc_ref, g_ref, ws_ref, b_ref, y_ref, p_ref, vn_ref, ext_ref):
        i = pl.program_id(0)
        a = proj_ref[:, 0:D_POOL]
        ext_ref[0:POOL_HALO, :] = jnp.where(i > 0, halo_ref[...], 0.0)
        ext_ref[POOL_HALO:POOL_HALO + ts, :] = a
        pos = _pool_denominators(i * ts, ts)
        for gi, w in enumerate(POOL_WINDOWS):
            cs = slice(gi * POOL_GROUP, (gi + 1) * POOL_GROUP)
            acc = a[:, cs]
            for j in range(1, w):
                acc = acc + ext_ref[POOL_HALO - j:POOL_HALO - j + ts, cs]
            pg = (acc / jnp.minimum(pos, float(w)) - a[:, cs]).astype(BF16)
            p_ref[:, cs] = pg
            ypre = jnp.dot(pg, pw_ref[gi], preferred_element_type=F32)
            y_ref[:, cs] = (ypre * sc_ref[:, cs]).astype(y_ref.dtype)

        v = proj_ref[:, D_POOL + D_SGU:D_POOL + 2 * D_SGU]
        r = lax.rsqrt(jnp.mean(v * v, axis=-1, keepdims=True) + EPS)
        vn_ref[...] = (v * r * g_ref[...]).astype(BF16)
        for n in range(nblk):
            rs = slice(n * SGU_BLOCK, (n + 1) * SGU_BLOCK)
            for h in range(N_SGU_HEADS):
                cs = slice(h * SGU_BLOCK, (h + 1) * SGU_BLOCK)
                mixed = jnp.dot(ws_ref[h], vn_ref[rs, cs], preferred_element_type=F32) + b_ref[:, cs]
                u = proj_ref[rs, D_POOL + h * SGU_BLOCK:D_POOL + (h + 1) * SGU_BLOCK]
                y_ref[rs, D_POOL + h * SGU_BLOCK:D_POOL + (h + 1) * SGU_BLOCK] = (u * mixed).astype(y_ref.dtype)

    return _tc_call(
        body, name=name, grid=(S // ts,),
        in_specs=[
            pl.BlockSpec((ts, D_POOL + 2 * D_SGU), lambda i: (i, 0)),
            pl.BlockSpec((POOL_HALO, D_POOL), lambda i: (jnp.maximum(i * halo_blocks - 1, 0), 0)),
            pl.BlockSpec((4, POOL_GROUP, POOL_GROUP), lambda i: (0, 0, 0)),
            pl.BlockSpec((1, D_POOL), lambda i: (0, 0)),
            pl.BlockSpec((1, D_SGU), lambda i: (0, 0)),
            pl.BlockSpec((N_SGU_HEADS, SGU_BLOCK, SGU_BLOCK), lambda i: (0, 0, 0)),
            pl.BlockSpec((SGU_BLOCK, D_SGU), lambda i: (0, 0)),
        ],
        out_specs=[
            pl.BlockSpec((ts, D_MODEL), lambda i: (i, 0)),
            pl.BlockSpec((ts, D_POOL), lambda i: (i, 0)),
            pl.BlockSpec((ts, D_SGU), lambda i: (i, 0)),
        ],
        out_shape=[_sds((S, D_MODEL), BF16), _sds((S, D_POOL), BF16), _sds((S, D_SGU), BF16)],
        scratch_shapes=[pltpu.VMEM((ts + POOL_HALO, D_POOL), F32)],
        compiler_params=_cparams(("parallel",)),
    )(proj, proj, pool_w, pool_scale, sgu_g, ws, bias_full)


def _mixer_bwd(name, dymix, proj, p, vn, pool_w, pool_scale, sgu_g, ws, bias_full, ts):
    S = proj.shape[0]
    nsteps = S // ts
    nblk = ts // SGU_BLOCK
    halo_blocks = ts // POOL_HALO

    def body(dy_ref, dyh_ref, u_ref, v_ref, p_ref, vn_ref, pw_ref, sc_ref, g_ref, ws_ref, b_ref,
             dproj_ref, dpw_ref, dsc_ref, dg_ref, dws_ref, db_ref,
             ext_ref, dvn_ref, acc_sc, acc_g, acc_b):
        i = pl.program_id(0)

        @pl.when(i == 0)
        def _():
            dpw_ref[...] = jnp.zeros_like(dpw_ref)
            dws_ref[...] = jnp.zeros_like(dws_ref)
            acc_sc[...] = jnp.zeros_like(acc_sc)
            acc_g[...] = jnp.zeros_like(acc_g)
            acc_b[...] = jnp.zeros_like(acc_b)

        pos = _pool_denominators(i * ts, ts)
        pos_h = _pool_denominators((i + 1) * ts, POOL_HALO)
        for gi, w in enumerate(POOL_WINDOWS):
            cs = slice(gi * POOL_GROUP, (gi + 1) * POOL_GROUP)
            pg = p_ref[:, cs]
            wg = pw_ref[gi]
            dyp = dy_ref[:, cs]
            ypre = jnp.dot(pg, wg, preferred_element_type=F32)
            acc_sc[:, cs] += _rowsum8(dyp * ypre)
            dz = (dyp * sc_ref[:, cs]).astype(BF16)
            dpw_ref[gi] += lax.dot_general(pg, dz, _DN["tn"], preferred_element_type=F32)
            dp = lax.dot_general(dz, wg, _DN["nt"], preferred_element_type=F32)
            dzh = (dyh_ref[:, cs] * sc_ref[:, cs]).astype(BF16)
            dph = lax.dot_general(dzh, wg, _DN["nt"], preferred_element_type=F32)
            ext_ref[0:ts, cs] = dp / jnp.minimum(pos, float(w))
            ext_ref[ts:ts + POOL_HALO, cs] = jnp.where(i < nsteps - 1, dph / jnp.minimum(pos_h, float(w)), 0.0)
            acc = ext_ref[0:ts, cs]
            for j in range(1, w):
                acc = acc + ext_ref[j:j + ts, cs]
            dproj_ref[:, cs] = (acc - dp).astype(dproj_ref.dtype)

        for n in range(nblk):
            rs = slice(n * SGU_BLOCK, (n + 1) * SGU_BLOCK)
            for h in range(N_SGU_HEADS):
                cs = slice(h * SGU_BLOCK, (h + 1) * SGU_BLOCK)
                vnb = vn_ref[rs, cs]
                wh = ws_ref[h]
                mixed = jnp.dot(wh, vnb, preferred_element_type=F32) + b_ref[:, cs]
                dys = dy_ref[rs, D_POOL + h * SGU_BLOCK:D_POOL + (h + 1) * SGU_BLOCK]
                dproj_ref[rs, D_POOL + h * SGU_BLOCK:D_POOL + (h + 1) * SGU_BLOCK] = (dys * mixed).astype(dproj_ref.dtype)
                dmix = dys * u_ref[rs, cs]
                acc_b[:, cs] += dmix
                dmb = dmix.astype(BF16)
                dws_ref[h] += lax.dot_general(dmb, vnb, _DN["nt"], preferred_element_type=F32)
                dvn_ref[rs, cs] = lax.dot_general(wh, dmb, _DN["tn"], preferred_element_type=F32)
        v = v_ref[...]
        r = lax.rsqrt(jnp.mean(v * v, axis=-1, keepdims=True) + EPS)
        vh = v * r
        dvn = dvn_ref[...]
        acc_g[...] += _rowsum8(dvn * vh)
        dxh = dvn * g_ref[...]
        m = jnp.mean(dxh * vh, axis=-1, keepdims=True)
        dproj_ref[:, D_POOL + D_SGU:D_POOL + 2 * D_SGU] = (r * (dxh - vh * m)).astype(dproj_ref.dtype)

        @pl.when(i == nsteps - 1)
        def _():
            dsc_ref[...] = jnp.sum(acc_sc[...], axis=0, keepdims=True)
            dg_ref[...] = jnp.sum(acc_g[...], axis=0, keepdims=True)
            t_idx = lax.broadcasted_iota(jnp.int32, (SGU_BLOCK, SGU_BLOCK), 0) // SGU_CHUNK
            s_idx = lax.broadcasted_iota(jnp.int32, (SGU_BLOCK, SGU_BLOCK), 1) // SGU_CHUNK
            mask = s_idx <= t_idx
            for h in range(N_SGU_HEADS):
                cs = slice(h * SGU_BLOCK, (h + 1) * SGU_BLOCK)
                dws_ref[h] = jnp.where(mask, dws_ref[h], 0.0)
                col = jnp.sum(acc_b[:, cs], axis=1, keepdims=True)
                db_ref[h] = jnp.broadcast_to(col, (SGU_BLOCK, SGU_BLOCK))

    const2 = lambda i: (0, 0)
    const3 = lambda i: (0, 0, 0)
    last_halo = S // POOL_HALO - 1
    return _tc_call(
        body, name=name, grid=(nsteps,),
        in_specs=[
            pl.BlockSpec((ts, D_MODEL), lambda i: (i, 0)),
            pl.BlockSpec((POOL_HALO, D_POOL), lambda i: (jnp.minimum((i + 1) * halo_blocks, last_halo), 0)),
            pl.BlockSpec((ts, D_SGU), lambda i: (i, 1)),
            pl.BlockSpec((ts, D_SGU), lambda i: (i, 2)),
            pl.BlockSpec((ts, D_POOL), lambda i: (i, 0)),
            pl.BlockSpec((ts, D_SGU), lambda i: (i, 0)),
            pl.BlockSpec((4, POOL_GROUP, POOL_GROUP), const3),
            pl.BlockSpec((1, D_POOL), const2),
            pl.BlockSpec((1, D_SGU), const2),
            pl.BlockSpec((N_SGU_HEADS, SGU_BLOCK, SGU_BLOCK), const3),
            pl.BlockSpec((SGU_BLOCK, D_SGU), const2),
        ],
        out_specs=[
            pl.BlockSpec((ts, D_POOL + 2 * D_SGU), lambda i: (i, 0)),
            pl.BlockSpec((4, POOL_GROUP, POOL_GROUP), const3),
            pl.BlockSpec((1, D_POOL), const2),
            pl.BlockSpec((1, D_SGU), const2),
            pl.BlockSpec((N_SGU_HEADS, SGU_BLOCK, SGU_BLOCK), const3),
            pl.BlockSpec((N_SGU_HEADS, SGU_BLOCK, SGU_BLOCK), const3),
        ],
        out_shape=[
            _sds((S, D_POOL + 2 * D_SGU), BF16),
            _sds((4, POOL_GROUP, POOL_GROUP), F32),
            _sds((1, D_POOL), F32),
            _sds((1, D_SGU), F32),
            _sds((N_SGU_HEADS, SGU_BLOCK, SGU_BLOCK), F32),
            _sds((N_SGU_HEADS, SGU_BLOCK, SGU_BLOCK), F32),
        ],
        scratch_shapes=[
            pltpu.VMEM((ts + POOL_HALO, D_POOL), F32),
            pltpu.VMEM((ts, D_SGU), F32),
            pltpu.VMEM((8, D_POOL), F32),
            pltpu.VMEM((8, D_SGU), F32),
            pltpu.VMEM((SGU_BLOCK, D_SGU), F32),
        ],
        compiler_params=_cparams(("arbitrary",)),
    )(dymix, dymix, proj, proj, p, vn, pool_w, pool_scale, sgu_g, ws, bias_full)


def _silu_mul(accs, extras):
    (up,) = accs
    gt = extras[0]
    sig = 1.0 / (1.0 + jnp.exp(-gt))
    return gt, up, gt * sig * up


def _silu_mul_bwd(accs, extras):
    (dact,) = accs
    gt = extras[0].astype(F32)
    up = extras[1].astype(F32)
    sig = 1.0 / (1.0 + jnp.exp(-gt))
    silu = gt * sig
    dgt = dact * up * (sig * (1.0 + gt * (1.0 - sig)))
    dup = dact * silu
    return dgt, dup


def _add_residual(accs, extras):
    return (extras[0] + accs[0],)


def _add_residual_and_cast(accs, extras):
    y = extras[0] + accs[0]
    return y, y


def _local_step(x, mem, target, W, sm, rs):
    S = x.shape[0]
    tm = min(1024, S)
    th = min(512, S)
    tq = min(256, S)
    ts = min(512, S)
    tr = min(512, S)
    tk_s = min(2048, S)
    M = mem.shape[0]

    h1 = _rms_fwd("rms_mix", x, sm["norm_mix_g"], tr)
    proj = _mm1("proj_in", h1, W["w_in_t"], "nt", tm, 1024, 2048, F32)
    ymix, p, vn = _mixer_fwd("mixer_fwd", proj, W["pool_w"], sm["pool_scale"], sm["sgu_norm_g"],
                             sm["ws_masked"], sm["bias_full"], ts)
    x1, h2 = _mm_rows("proj_out", [ymix], [W["w_out"]], [(0, 0, 0)], "nn", th, 2048, [x], [sm["norm_xattn_g"]],
                      [F32, BF16], 0, _ep_residual_norm)

    mb = _rms_fwd("rms_mem", mem, sm["norm_mem_g"], M)
    q = _mm1("proj_q", h2, W["w_q"], "nn", tm, 1024, 2048, BF16)
    kk, vv = _mm("proj_kv", [mb], [W["w_k"], W["w_v"]], [(0, 0, 0), (0, 1, 1)], "nn", M, 1024, 2048, [BF16, BF16],
                 n_acc=2)
    o = _attn_fwd("attn_fwd", q, kk, vv, tm)
    x2, h3 = _mm_rows("proj_o", [o], [W["w_o"]], [(0, 0, 0)], "nn", th, 2048, [x1], [sm["norm_ffn_g"]],
                      [F32, BF16], 0, _ep_residual_norm)

    gt32 = _mm1("ffn_gate", h3, W["w_gate_t"], "nt", tm, 1408, 2048, F32)
    gt, up, act = _mm("ffn_up", [h3], [W["w_up_t"]], [(0, 0, 0)], "nt", tm, 1408, 2048, [BF16, BF16, BF16],
                      epilogue=_silu_mul, extras=(gt32,))
    dx3, dx3b, d_final_g, loss = _mm_rows(
        "ffn_down", [act], [W["w_down"]], [(0, 0, 0)], "nn", tq, 5632, [x2, target], [sm["final_norm_g"]],
        [F32, BF16], 2, _ep_final_loss, n_scalar_out=1)

    dgt, dup = _mm("ffn_down_dgrad", [dx3b], [W["w_down"]], [(0, 0, 0)], "nt", tm, 1408, 2048, [BF16, BF16],
                   epilogue=_silu_mul_bwd, extras=(gt, up))
    rs.push("w_down", _mm1("ffn_down_wgrad", act, dx3b, "tn", 1408, 1024, tk_s, BF16))
    rs.push("w_gate_t", _mm1("ffn_gate_wgrad", dgt, h3, "tn", 1408, 1024, tk_s, BF16))
    rs.reduce("w_down")
    rs.push("w_up_t", _mm1("ffn_up_wgrad", dup, h3, "tn", 1408, 1024, tk_s, BF16))
    rs.reduce("w_gate_t")
    dh3 = _mm("ffn_gate_up_dgrad", [dgt, dup], [W["w_gate_t"], W["w_up_t"]], [(0, 0, 0), (1, 1, 0)], "nn",
              th, 512, 5632, [F32])[0]
    rs.reduce("w_up_t")
    dx2, dx2b, d_ffn_g = _rms_bwd("rms_ffn_bwd", dh3, x2, sm["norm_ffn_g"], dx3, tr)
    rs.finish("w_down")

    rs.push("w_o", _mm1("proj_o_wgrad", o, dx2b, "tn", 1024, 1024, tk_s, BF16))
    rs.finish("w_gate_t")
    do = _mm1("proj_o_dgrad", dx2b, W["w_o"], "nt", tm, 1024, 2048, BF16)
    rs.reduce("w_o")
    dq, dk, dv = _attn_bwd("attn_bwd", q, kk, vv, do, tm)
    rs.push("w_q", _mm1("proj_q_wgrad", h2, dq, "tn", 1024, 1024, tk_s, BF16))
    rs.push("w_k", _mm1("proj_k_wgrad", mb, dk, "tn", 1024, 1024, M, BF16))
    rs.push("w_v", _mm1("proj_v_wgrad", mb, dv, "tn", 1024, 1024, M, BF16))
    rs.finish("w_up_t")
    dx1, dx1b, d_xattn_g = _mm_rows(
        "proj_q_dgrad", [dq], [W["w_q"]], [(0, 0, 0)], "nt", tq, 2048, [x1, dx2], [sm["norm_xattn_g"]],
        [F32, BF16], 1, _ep_norm_bwd)
    rs.reduce("w_q")
    rs.reduce("w_k")
    rs.reduce("w_v")
    dmb = _mm("proj_kv_dgrad", [dk, dv], [W["w_k"], W["w_v"]], [(0, 0, 0), (1, 1, 0)], "nt",
              M, 1024, 2048, [F32])[0]
    (d_mem_g,) = _rms_bwd("rms_mem_bwd", dmb, mem, sm["norm_mem_g"], None, M, want_dx=False)

    rs.push("w_out", _mm1("proj_out_wgrad", ymix, dx1b, "tn", 1024, 1024, tk_s, BF16))
    dymix = _mm1("proj_out_dgrad", dx1b, W["w_out"], "nt", tm, 1024, 2048, F32)
    rs.finish("w_o")
    dproj, d_pool_w, d_pool_scale, d_sgu_g, d_ws, d_b = _mixer_bwd(
        "mixer_bwd", dymix, proj, p, vn, W["pool_w"], sm["pool_scale"], sm["sgu_norm_g"],
        sm["ws_masked"], sm["bias_full"], ts)
    rs.finish("w_q")
    rs.finish("w_k")
    rs.finish("w_v")
    rs.reduce("w_out")
    rs.push("pool_w", d_pool_w.reshape(4, N_DEV, POOL_GROUP // N_DEV, POOL_GROUP).transpose(1, 0, 2, 3)
            .reshape(4 * POOL_GROUP, POOL_GROUP).astype(BF16))
    rs.small("early", dict(
        pool_scale=d_pool_scale, sgu_norm_g=d_sgu_g, w_spatial=d_ws, b_spatial=d_b[:, :, 0],
        norm_xattn_g=d_xattn_g, norm_mem_g=d_mem_g, norm_ffn_g=d_ffn_g, final_norm_g=d_final_g))
    rs.push("w_in_t", _mm1("proj_in_wgrad", dproj, h1, "tn", 1024, 1024, tk_s, BF16))
    rs.finish("w_out")
    rs.reduce("pool_w")
    rs.reduce("w_in_t")
    grad_x, d_mix_g = _mm_rows(
        "proj_in_dgrad", [dproj], [W["w_in_t"]], [(0, 0, 0)], "nn", tq, 3072, [x, dx1], [sm["norm_mix_g"]],
        [F32], 1, _ep_norm_bwd)
    rs.small("late", dict(norm_mix_g=d_mix_g, loss=jnp.pad(loss, ((0, 0), (0, _LANES - 1)))))
    rs.finish_small("early")
    rs.finish("pool_w")
    rs.finish("w_in_t")
    rs.finish_small("late")
    return loss, grad_x


def _mesh_pos():
    return lax.axis_index("x"), lax.axis_index("y"), lax.axis_index("c")


def _handshake(peers):
    barrier = pltpu.get_barrier_semaphore()
    for peer in peers:
        pl.semaphore_signal(barrier, inc=1, device_id=peer, device_id_type=MESH)
    pl.semaphore_wait(barrier, len(peers))


def _seq_all_gather(name, shards, collective_id):
    n = len(shards)

    def body(*refs):
        ins = refs[:n]
        outs = refs[n:2 * n]
        send_sems, recv_sems, local_sems = refs[2 * n:]
        x, y, c = _mesh_pos()
        me, sibling = (x, y, c), (x, y, 1 - c)
        xn, yn, dg = (1 - x, y), (x, 1 - y), (1 - x, 1 - y)
        north = c == 1
        via = (jnp.where(north, xn[0], yn[0]), jnp.where(north, xn[1], yn[1]))
        to = (jnp.where(north, yn[0], xn[0]), jnp.where(north, yn[1], xn[1]))
        _handshake([sibling, (*xn, c), (*yn, c)])

        def copy(a, k, block, target, src=None):
            bx, by, bc = block
            dst = outs[a].at[4 * bx + 2 * by + bc]
            return pltpu.make_async_remote_copy(
                src_ref=dst if src is None else src, dst_ref=dst,
                send_sem=send_sems.at[a, k], recv_sem=recv_sems.at[a, k],
                device_id=target, device_id_type=MESH)

        mine = [pltpu.make_async_copy(ins[a], outs[a].at[4 * x + 2 * y + c], local_sems.at[a]) for a in range(n)]
        for cp in mine:
            cp.start()
        started = []
        for a in range(n):
            first = [copy(a, 0, me, sibling, src=ins[a]), copy(a, 1, me, (*xn, c), src=ins[a]),
                     copy(a, 2, me, (*yn, c), src=ins[a])]
            for cp in first:
                cp.start()
            started += first
        for a in range(n):
            copy(a, 1, (*xn, c), me).wait_recv()
            copy(a, 2, (*yn, c), me).wait_recv()
            second = [copy(a, 3, (*via, c), (*to, c)), copy(a, 4, (*xn, c), sibling), copy(a, 5, (*yn, c), sibling)]
            for cp in second:
                cp.start()
            started += second
        for a in range(n):
            copy(a, 3, (*dg, c), me).wait_recv()
            last = copy(a, 6, (*dg, c), sibling)
            last.start()
            started.append(last)
        for a in range(n):
            copy(a, 0, sibling, me).wait_recv()
            for k, chip in ((4, xn), (5, yn), (6, dg)):
                copy(a, k, (*chip, 1 - c), me).wait_recv()
        for cp in started:
            cp.wait_send()
        for cp in mine:
            cp.wait()

    return _sc_call(
        body, name=name,
        out_type=[_sds((N_DEV,) + s.shape, s.dtype) for s in shards],
        scratch_types=[pltpu.SemaphoreType.DMA((n, 7)), pltpu.SemaphoreType.DMA((n, 7)),
                       pltpu.SemaphoreType.DMA((n,))],
        compiler_params=pltpu.CompilerParams(collective_id=collective_id),
    )(*shards)


def _seq_pair_exchange(name, gview, collective_id):
    def body(g_ref, theirs_ref, send_sems, recv_sems):
        x, y, c = _mesh_pos()
        sibling = (x, y, 1 - c)
        _handshake([sibling])
        copies = [pltpu.make_async_remote_copy(
            src_ref=g_ref.at[k, 1 - c], dst_ref=theirs_ref.at[k],
            send_sem=send_sems.at[k], recv_sem=recv_sems.at[k],
            device_id=sibling, device_id_type=MESH) for k in range(4)]
        for cp in copies:
            cp.start()
        for cp in copies:
            cp.wait()

    return _sc_call(
        body, name=name, out_type=_sds((4,) + gview.shape[2:], gview.dtype),
        scratch_types=[pltpu.SemaphoreType.DMA((4,)), pltpu.SemaphoreType.DMA((4,))],
        compiler_params=pltpu.CompilerParams(collective_id=collective_id),
    )(gview)


def _pair_sum(name, gview, theirs, pos, tr):
    _, _, r, C = gview.shape

    def body(pos_ref, a_ref, b_ref, o_ref):
        o_ref[...] = (a_ref[...].astype(F32) + b_ref[...].astype(F32)).astype(o_ref.dtype)

    grid_spec = pltpu.PrefetchScalarGridSpec(
        num_scalar_prefetch=1, grid=(4, r // tr),
        in_specs=[pl.BlockSpec((None, None, tr, C), lambda k, t, pos_ref: (k, pos_ref[0], t, 0)),
                  pl.BlockSpec((None, tr, C), lambda k, t, pos_ref: (k, t, 0))],
        out_specs=pl.BlockSpec((None, tr, C), lambda k, t, pos_ref: (k, t, 0)))
    return _tc_call(
        body, name=name, grid_spec=grid_spec, out_shape=_sds(theirs.shape, theirs.dtype),
        compiler_params=_cparams(("parallel", "parallel")),
    )(pos, gview, theirs)


def _seq_chip_exchange(name, pair, collective_id):
    def body(p_ref, land_ref, send_sems, recv_sems):
        x, y, c = _mesh_pos()
        my_chip = 2 * x + y
        chips = [(1 - x, y), (x, 1 - y), (1 - x, 1 - y)]
        _handshake([(cx, cy, c) for cx, cy in chips])
        copies = [pltpu.make_async_remote_copy(
            src_ref=p_ref.at[2 * cx + cy], dst_ref=land_ref.at[my_chip],
            send_sem=send_sems.at[j], recv_sem=recv_sems.at[j],
            device_id=(cx, cy, c), device_id_type=MESH) for j, (cx, cy) in enumerate(chips)]
        for cp in copies:
            cp.start()
        for cp in copies:
            cp.wait_send()
        for j, (cx, cy) in enumerate(chips):
            pltpu.make_async_remote_copy(
                src_ref=p_ref.at[my_chip], dst_ref=land_ref.at[2 * cx + cy],
                send_sem=send_sems.at[j], recv_sem=recv_sems.at[j],
                device_id=(cx, cy, c), device_id_type=MESH).wait_recv()

    return _sc_call(
        body, name=name, out_type=_sds(pair.shape, pair.dtype),
        scratch_types=[pltpu.SemaphoreType.DMA((3,)), pltpu.SemaphoreType.DMA((3,))],
        compiler_params=pltpu.CompilerParams(collective_id=collective_id),
    )(pair)


def _sum_leading(name, parts, tr, out_dtype=F32):
    n, r, C = parts.shape

    def body(p_ref, o_ref):
        acc = p_ref[0].astype(F32)
        for k in range(1, n):
            acc = acc + p_ref[k].astype(F32)
        o_ref[...] = acc.astype(o_ref.dtype)

    return _tc_call(
        body, name=name, grid=(r // tr,),
        in_specs=[pl.BlockSpec((n, tr, C), lambda t: (0, t, 0))],
        out_specs=pl.BlockSpec((tr, C), lambda t: (t, 0)),
        out_shape=_sds((r, C), out_dtype), compiler_params=_cparams(("parallel",)),
    )(parts)


def _row_tile(r):
    for t in (512, 384, 352, 256, 128, 64, 32, 16, 8):
        if r % t == 0:
            return t
    return r


def _adamw_math(w, g, m, v):
    c1 = 1.0 - ADAM_B1 ** ADAM_STEP
    c2 = 1.0 - ADAM_B2 ** ADAM_STEP
    nm = ADAM_B1 * m + (1.0 - ADAM_B1) * g
    nv = ADAM_B2 * v + (1.0 - ADAM_B2) * (g * g)
    m_hat = nm / c1
    v_hat = nv / c2
    return -ADAM_LR * (m_hat / (jnp.sqrt(v_hat) + ADAM_EPS) + ADAM_WD * w), nm, nv


def _chip_sum_adamw(name, pair, landed, pos, w, m, v, transposed):
    _, r, C = pair.shape
    if transposed:
        tr, tc = r, 512
        r_pad = -r % _LANES
        wspec = pl.BlockSpec((tc, r), lambda t, k, pos_ref: (t, 0))
        shape = (C, r)
        scratch = [pltpu.VMEM((tr, tc), F32), pltpu.VMEM((tc, r + r_pad), F32)]
    else:
        tr, tc = _row_tile(r), C
        wspec = pl.BlockSpec((tr, C), lambda t, k, pos_ref: (t, 0))
        shape = (r, C)
        scratch = [pltpu.VMEM((tr, tc), F32)]
    n_t = (C // tc) if transposed else (r // tr)

    def block(chip, t):
        return (chip, 0, t) if transposed else (chip, t, 0)

    def body(pos_ref, own_ref, land_ref, w_ref, m_ref, v_ref, g_ref, d_ref, nm_ref, nv_ref, acc_ref, *turn):
        k = pl.program_id(1)
        val = jnp.where(k == pos_ref[1], own_ref[...], land_ref[...]).astype(F32)

        @pl.when(k == 0)
        def _():
            acc_ref[...] = val

        @pl.when(k > 0)
        def _():
            acc_ref[...] += val

        @pl.when(k == 3)
        def _():
            if transposed:
                g_t = acc_ref[...]
                if r_pad:
                    g_t = jnp.concatenate([g_t, jnp.zeros((r_pad, tc), F32)], axis=0)
                turn[0][...] = g_t.T
                g = turn[0][:, 0:r]
            else:
                g = acc_ref[...]
            d, nm, nv = _adamw_math(w_ref[...], g, m_ref[...], v_ref[...])
            g_ref[...] = g
            d_ref[...] = d
            nm_ref[...] = nm
            nv_ref[...] = nv

    def land_index(t, k, pos_ref):
        return block(jnp.where(k == pos_ref[1], (k + 1) % 4, k), t)

    grid_spec = pltpu.PrefetchScalarGridSpec(
        num_scalar_prefetch=1, grid=(n_t, 4),
        in_specs=[pl.BlockSpec((None, tr, tc), lambda t, k, pos_ref: block(pos_ref[1], t)),
                  pl.BlockSpec((None, tr, tc), land_index), wspec, wspec, wspec],
        out_specs=[wspec] * 4, scratch_shapes=scratch)
    return _tc_call(
        body, name=name, grid_spec=grid_spec, out_shape=[_sds(shape, F32)] * 4,
        compiler_params=_cparams(("parallel", "arbitrary")),
    )(pos, pair, landed, w, m, v)


def _adamw(name, w, g, m, v):
    R, C = w.shape
    tr = _row_tile(R)

    def body(w_ref, g_ref, m_ref, v_ref, d_ref, nm_ref, nv_ref):
        d_ref[...], nm_ref[...], nv_ref[...] = _adamw_math(w_ref[...], g_ref[...], m_ref[...], v_ref[...])

    spec = pl.BlockSpec((tr, C), lambda i: (i, 0))
    return _tc_call(
        body, name=name, grid=(R // tr,), in_specs=[spec] * 4, out_specs=[spec] * 3,
        out_shape=[_sds((R, C), F32)] * 3, compiler_params=_cparams(("parallel",)),
    )(w, g, m, v)


_BIG = ("w_in_t", "w_out", "w_q", "w_k", "w_v", "w_o", "w_gate_t", "w_up_t", "w_down")
_SMALL = ("norm_mix_g", "pool_scale", "sgu_norm_g", "w_spatial", "b_spatial", "norm_xattn_g",
          "norm_mem_g", "norm_ffn_g", "final_norm_g")
_LANES = 128
_GATHER_GROUPS = (("w_in_t", "pool_w"), ("w_out",), ("w_q",), ("w_k", "w_v"), ("w_o",), ("w_gate_t",),
                  ("w_up_t",), ("w_down",))
_RS_ORDER = ("w_down", "w_gate_t", "w_up_t", "w_o", "w_q", "w_k", "w_v", "w_out", "w_in_t")
_SMALL_GROUPS = dict(
    early=("pool_scale", "sgu_norm_g", "w_spatial", "b_spatial", "norm_xattn_g", "norm_mem_g",
           "norm_ffn_g", "final_norm_g"),
    late=("norm_mix_g", "loss"))
_TURN_OUTSIDE = ("w_gate_t", "w_up_t")
_ID_GATHER, _ID_PAIR, _ID_CHIP = 0, 1, 2


_PACK_ROWS = 512


def _pack(parts):
    rows = [p.reshape(-1, _LANES) for p in parts]
    n = sum(r.shape[0] for r in rows)
    pad = -n % (_PACK_ROWS if n > _PACK_ROWS else 8)
    if pad:
        rows.append(jnp.zeros((pad, _LANES), rows[0].dtype))
    return jnp.concatenate(rows, axis=0)


class _GradReducer:
    def __init__(self, pos, apply, apply_small):
        self.pos, self.apply, self.apply_small = pos, apply, apply_small
        self.view, self.theirs, self.pair, self.landed = {}, {}, {}, {}
        self.small_gathered = {}

    def push(self, k, g):
        r = g.shape[0] // N_DEV
        self.view[k] = g.reshape(4, 2, r, g.shape[1])
        self.theirs[k] = _seq_pair_exchange("grad_pair_exchange_" + k, self.view[k], _ID_PAIR)

    def reduce(self, k):
        r = self.view[k].shape[2]
        self.pair[k] = _pair_sum("grad_pair_sum_" + k, self.view[k], self.theirs[k], self.pos, r)
        self.landed[k] = _seq_chip_exchange("grad_chip_exchange_" + k, self.pair[k], _ID_CHIP)

    def finish(self, k):
        self.apply(k, self.pair[k], self.landed[k])

    def small(self, tag, parts):
        packed = _pack([parts[k] for k in _SMALL_GROUPS[tag]])
        (self.small_gathered[tag],) = _seq_all_gather("gather_small_grads_" + tag, [packed], _ID_GATHER)

    def finish_small(self, tag):
        allp = self.small_gathered[tag]
        self.apply_small(tag, _sum_leading("sum_small_grads_" + tag, allp, min(_PACK_ROWS, allp.shape[1])))


def _unpack(packed, like):
    out, row = [], 0
    for ref in like:
        rows = ref.size // _LANES
        out.append(packed[row:row + rows].reshape(ref.shape))
        row += rows
    return out


def kernel(x, mem, norm_mix_g, w_in, pool_w, pool_scale, sgu_norm_g, w_spatial, b_spatial, w_out, norm_xattn_g, norm_mem_g, w_q, w_k, w_v, w_o, norm_ffn_g, w_gate, w_up, w_down, final_norm_g, loss_target, m_norm_mix_g, m_w_in, m_pool_w, m_pool_scale, m_sgu_norm_g, m_w_spatial, m_b_spatial, m_w_out, m_norm_xattn_g, m_norm_mem_g, m_w_q, m_w_k, m_w_v, m_w_o, m_norm_ffn_g, m_w_gate, m_w_up, m_w_down, m_final_norm_g, v_norm_mix_g, v_w_in, v_pool_w, v_pool_scale, v_sgu_norm_g, v_w_spatial, v_b_spatial, v_w_out, v_norm_xattn_g, v_norm_mem_g, v_w_q, v_w_k, v_w_v, v_w_o, v_norm_ffn_g, v_w_gate, v_w_up, v_w_down, v_final_norm_g):
    args = dict(locals())
    names = ("norm_mix_g", "w_in", "pool_w", "pool_scale", "sgu_norm_g", "w_spatial", "b_spatial", "w_out",
             "norm_xattn_g", "norm_mem_g", "w_q", "w_k", "w_v", "w_o", "norm_ffn_g", "w_gate", "w_up",
             "w_down", "final_norm_g")
    w = {k: args[k] for k in names}
    m = {k: args["m_" + k] for k in names}
    v = {k: args["v_" + k] for k in names}
    _CHAIN.__init__()

    shards = dict(
        w_in_t=w["w_in"][0].T, w_out=w["w_out"][0], w_q=w["w_q"][0], w_k=w["w_k"][0], w_v=w["w_v"][0],
        w_o=w["w_o"][0], w_gate_t=w["w_gate"][0].T, w_up_t=w["w_up"][0].T, w_down=w["w_down"][0])
    send = {k: shards[k].astype(BF16) for k in _BIG}
    send["pool_w"] = w["pool_w"][0].reshape(4 * 32, POOL_GROUP).astype(BF16)
    W = {}
    for gi, group in enumerate(_GATHER_GROUPS):
        gathered = _seq_all_gather("gather_weights_%d" % gi, [send[k] for k in group], _ID_GATHER)
        for k, g in zip(group, gathered):
            W[k] = g.reshape(-1, g.shape[-1])
    W["pool_w"] = W["pool_w"].reshape(N_DEV, 4, 32, POOL_GROUP).transpose(1, 0, 2, 3).reshape(4, POOL_GROUP, POOL_GROUP)

    t = jnp.arange(SGU_BLOCK)
    mask = (t[None, :] // SGU_CHUNK) <= (t[:, None] // SGU_CHUNK)
    sm = dict(
        norm_mix_g=w["norm_mix_g"], pool_scale=w["pool_scale"], sgu_norm_g=w["sgu_norm_g"],
        norm_xattn_g=w["norm_xattn_g"], norm_mem_g=w["norm_mem_g"], norm_ffn_g=w["norm_ffn_g"],
        final_norm_g=w["final_norm_g"].reshape(1, D_MODEL),
        ws_masked=jnp.where(mask[None], w["w_spatial"][0], 0.0).astype(BF16),
        bias_full=jnp.repeat(w["b_spatial"][0].T, SGU_BLOCK, axis=1))

    natural = dict(w_in_t="w_in", w_gate_t="w_gate", w_up_t="w_up")
    grads, delta, new_m, new_v = {}, {}, {}, {}

    def apply(k, pair, landed):
        name = natural.get(k, k)
        if k == "pool_w":
            flat = (4 * POOL_GROUP // N_DEV, POOL_GROUP)
            res = _chip_sum_adamw("grad_finish_" + k, pair, landed, pos, w[k].reshape(flat), m[k].reshape(flat),
                                  v[k].reshape(flat), False)
            grads[k], delta[k], new_m[k], new_v[k] = (a.reshape(w[k].shape) for a in res)
            return
        if k in _TURN_OUTSIDE:
            res = _chip_sum_adamw("grad_finish_" + k, pair, landed, pos, w[name][0].T, m[name][0].T, v[name][0].T,
                                  False)
            res = [a.T for a in res]
        else:
            res = _chip_sum_adamw("grad_finish_" + k, pair, landed, pos, w[name][0], m[name][0], v[name][0],
                                  k in natural)
        grads[name], delta[name], new_m[name], new_v[name] = (a[None] for a in res)

    like = dict(w)
    like["loss"] = _sds((1, _LANES), F32)

    def apply_small(tag, total):
        group = _SMALL_GROUPS[tag]
        grads.update(zip(group, _unpack(total, [like[k] for k in group])))
        if tag == "late":
            d_, m_, v_ = _adamw("adamw_small", _pack([w[k] for k in _SMALL]), _pack([grads[k] for k in _SMALL]),
                                _pack([m[k] for k in _SMALL]), _pack([v[k] for k in _SMALL]))
            shapes = [w[k] for k in _SMALL]
            for k, a, b, c_ in zip(_SMALL, _unpack(d_, shapes), _unpack(m_, shapes), _unpack(v_, shapes)):
                delta[k], new_m[k], new_v[k] = a, b, c_

    pos = jnp.stack([lax.axis_index("c"), 2 * lax.axis_index("x") + lax.axis_index("y")]).astype(jnp.int32)
    rs = _GradReducer(pos, apply, apply_small)
    _, grad_x = _local_step(x[0], mem[0], loss_target[0], W, sm, rs)

    outs = [grads["loss"][0, 0], grad_x[None]]
    outs += [grads[k].reshape(w[k].shape) for k in names]
    outs += [delta[k] for k in names]
    outs += [new_m[k] for k in names]
    outs += [new_v[k] for k in names]
    return tuple(outs)
```

```python
import jax
import jax.numpy as jnp
from jax import lax
from jax.experimental import pallas as pl
from jax.experimental.pallas import tpu as pltpu
from jax.experimental.pallas import tpu_sc as plsc

F32 = jnp.float32
BF16 = jnp.bfloat16
MESH = pl.DeviceIdType.MESH

EPS = 1e-6
D_MODEL = 2048
D_POOL = 1024
D_SGU = 1024
POOL_WINDOWS = (2, 4, 8, 16)
POOL_GROUP = 256
POOL_HALO = 16
SGU_BLOCK = 128
SGU_CHUNK = 64
N_SGU_HEADS = 8
N_HEADS = 4
HEAD_DIM = 512
N_DEV = 8

ADAM_LR = 0.001
ADAM_B1 = 0.9
ADAM_B2 = 0.999
ADAM_EPS = 1e-08
ADAM_WD = 0.01
ADAM_STEP = 10

VMEM_LIMIT = 56 * 1024 * 1024


def _cparams(sem=None):
    return pltpu.CompilerParams(dimension_semantics=sem, vmem_limit_bytes=VMEM_LIMIT)


def _sds(shape, dtype):
    return jax.ShapeDtypeStruct(shape, dtype)


_ANY = pl.BlockSpec(memory_space=pl.ANY)


class _Chain:
    def __init__(self):
        self.tc = None


_CHAIN = _Chain()


def _first(out):
    return out[0] if isinstance(out, (list, tuple)) else out


def _tc_call(body, *, in_specs=None, grid_spec=None, **kw):
    def run(*args):
        prev, n = _CHAIN.tc, len(args)
        fn, specs, spec, operands = body, in_specs, grid_spec, args
        if prev is not None:
            def fn(*refs):
                return body(*refs[:n], *refs[n + 1:])
            operands = args + (prev,)
            if grid_spec is None:
                specs = list(in_specs) + [_ANY]
            else:
                spec = pltpu.PrefetchScalarGridSpec(
                    num_scalar_prefetch=grid_spec.num_scalar_prefetch, grid=grid_spec.grid,
                    in_specs=list(grid_spec.in_specs) + [_ANY], out_specs=grid_spec.out_specs,
                    scratch_shapes=grid_spec.scratch_shapes)
        if spec is None:
            out = pl.pallas_call(fn, in_specs=specs, **kw)(*operands)
        else:
            out = pl.pallas_call(fn, grid_spec=spec, **kw)(*operands)
        _CHAIN.tc = _first(out)
        return out
    return run


def _sc_call(body, **kw):
    return pl.kernel(body, mesh=plsc.ScalarSubcoreMesh(axis_name="seq", num_cores=1), **kw)


def _rowsum8(v):
    r, c = v.shape
    return v.reshape(r // 8, 8, c).sum(axis=0)


_DN = {
    "nn": (((1,), (0,)), ((), ())),
    "nt": (((1,), (1,)), ((), ())),
    "tn": (((0,), (0,)), ((), ())),
}


def _mm(name, a_list, b_list, terms, mode, tm, tn, tk, out_dtypes, epilogue=None, extras=(), n_acc=1):
    a0, b0 = a_list[0], b_list[0]
    if mode == "tn":
        K, M = a0.shape
    else:
        M, K = a0.shape
    N = b0.shape[0] if mode == "nt" else b0.shape[1]
    assert M % tm == 0 and N % tn == 0 and K % tk == 0, (name, M, N, K, tm, tn, tk)
    nk = K // tk
    na, nb, ne, no = len(a_list), len(b_list), len(extras), len(out_dtypes)
    dn = _DN[mode]

    if mode == "tn":
        a_spec = pl.BlockSpec((tk, tm), lambda i, j, k: (k, i))
    else:
        a_spec = pl.BlockSpec((tm, tk), lambda i, j, k: (i, k))
    if mode == "nt":
        b_spec = pl.BlockSpec((tn, tk), lambda i, j, k: (j, k))
    else:
        b_spec = pl.BlockSpec((tk, tn), lambda i, j, k: (k, j))
    o_spec = pl.BlockSpec((tm, tn), lambda i, j, k: (i, j))

    def body(*refs):
        a_refs = refs[:na]
        b_refs = refs[na:na + nb]
        e_refs = refs[na + nb:na + nb + ne]
        o_refs = refs[na + nb + ne:na + nb + ne + no]
        acc_refs = refs[na + nb + ne + no:]

        parts = [None] * n_acc
        for ai, bi, ci in terms:
            d = lax.dot_general(a_refs[ai][...].astype(BF16), b_refs[bi][...].astype(BF16), dn,
                                preferred_element_type=F32)
            parts[ci] = d if parts[ci] is None else parts[ci] + d

        def finish(accs):
            outs = epilogue(accs, [e[...] for e in e_refs]) if epilogue is not None else accs
            for o_ref, v in zip(o_refs, outs):
                o_ref[...] = v.astype(o_ref.dtype)

        if nk == 1:
            finish(parts)
        else:
            k = pl.program_id(2)

            @pl.when(k == 0)
            def _():
                for c in range(n_acc):
                    acc_refs[c][...] = parts[c]

            @pl.when(k > 0)
            def _():
                for c in range(n_acc):
                    acc_refs[c][...] += parts[c]

            @pl.when(k == nk - 1)
            def _():
                finish([acc_refs[c][...] for c in range(n_acc)])

    scratch = [pltpu.VMEM((tm, tn), F32) for _ in range(n_acc)] if nk > 1 else []
    return _tc_call(
        body, name=name, grid=(M // tm, N // tn, nk),
        in_specs=[a_spec] * na + [b_spec] * nb + [o_spec] * ne,
        out_specs=[o_spec] * no,
        out_shape=[_sds((M, N), dt) for dt in out_dtypes],
        scratch_shapes=scratch,
        compiler_params=_cparams(("parallel", "parallel", "arbitrary")),
    )(*a_list, *b_list, *extras)


def _mm_rows(name, a_list, b_list, terms, mode, tm, tk, rows, vecs, row_dtypes, n_vec_out, epilogue,
             n_scalar_out=0):
    a0, b0 = a_list[0], b_list[0]
    M, K = a0.shape
    N = b0.shape[0] if mode == "nt" else b0.shape[1]
    assert mode in ("nn", "nt") and M % tm == 0 and K % tk == 0, (name, M, N, K, tm, tk)
    nm, nk = M // tm, K // tk
    slab = min(128, tm)
    na, nb, nr, nv, no = len(a_list), len(b_list), len(rows), len(vecs), len(row_dtypes)
    dn = _DN[mode]
    a_spec = pl.BlockSpec((tm, tk), lambda i, k: (i, k))
    b_mode = dict(pipeline_mode=pl.Buffered(1)) if nk == 1 else {}
    b_spec = (pl.BlockSpec((N, tk), lambda i, k: (0, k), **b_mode) if mode == "nt"
              else pl.BlockSpec((tk, N), lambda i, k: (k, 0), **b_mode))
    row_spec = pl.BlockSpec((tm, N), lambda i, k: (i, 0))
    vec_spec = pl.BlockSpec((1, N), lambda i, k: (0, 0))
    one_spec = pl.BlockSpec((1, 1), lambda i, k: (0, 0))

    def body(*refs):
        pos = 0
        a_refs = refs[pos:pos + na]; pos += na
        b_refs = refs[pos:pos + nb]; pos += nb
        r_refs = refs[pos:pos + nr]; pos += nr
        v_refs = refs[pos:pos + nv]; pos += nv
        o_refs = refs[pos:pos + no]; pos += no
        s_refs = refs[pos:pos + n_vec_out]; pos += n_vec_out
        vacc_refs = refs[pos:pos + n_vec_out]; pos += n_vec_out
        acc_ref = refs[pos] if nk > 1 else None
        i, k = pl.program_id(0), pl.program_id(1)
        part = None
        for ai, bi, _ in terms:
            d = lax.dot_general(a_refs[ai][...].astype(BF16), b_refs[bi][...].astype(BF16), dn,
                                preferred_element_type=F32)
            part = d if part is None else part + d

        def finish(acc):
            vecs_now = [v[...] for v in v_refs]
            vparts = None
            for r0 in range(0, tm, slab):
                rs_ = slice(r0, r0 + slab)
                outs, vp = epilogue(acc[rs_, :], [r[rs_, :] for r in r_refs], vecs_now)
                for o_ref, val in zip(o_refs, outs):
                    o_ref[rs_, :] = val.astype(o_ref.dtype)
                vparts = vp if vparts is None else [a + b for a, b in zip(vparts, vp)]

            @pl.when(i == 0)
            def _():
                for vacc, vp in zip(vacc_refs, vparts):
                    vacc[...] = vp

            @pl.when(i > 0)
            def _():
                for vacc, vp in zip(vacc_refs, vparts):
                    vacc[...] += vp

            @pl.when(i == nm - 1)
            def _():
                for j, (s_ref, vacc) in enumerate(zip(s_refs, vacc_refs)):
                    col = jnp.sum(vacc[...], axis=0, keepdims=True)
                    s_ref[...] = jnp.sum(col, axis=1, keepdims=True) if j >= n_vec_out - n_scalar_out else col

        if nk == 1:
            finish(part)
        else:
            @pl.when(k == 0)
            def _():
                acc_ref[...] = part

            @pl.when(k > 0)
            def _():
                acc_ref[...] += part

            @pl.when(k == nk - 1)
            def _():
                finish(acc_ref)

    n_plain = n_vec_out - n_scalar_out
    return _tc_call(
        body, name=name, grid=(nm, nk),
        in_specs=[a_spec] * na + [b_spec] * nb + [row_spec] * nr + [vec_spec] * nv,
        out_specs=[row_spec] * no + [vec_spec] * n_plain + [one_spec] * n_scalar_out,
        out_shape=[_sds((M, N), dt) for dt in row_dtypes] + [_sds((1, N), F32)] * n_plain
        + [_sds((1, 1), F32)] * n_scalar_out,
        scratch_shapes=[pltpu.VMEM((8, N), F32)] * n_vec_out + ([pltpu.VMEM((tm, N), F32)] if nk > 1 else []),
        compiler_params=_cparams(("arbitrary", "arbitrary")),
    )(*a_list, *b_list, *rows, *vecs)


def _ep_residual_norm(acc, rows, vecs):
    x_new = rows[0] + acc
    r = lax.rsqrt(jnp.mean(x_new * x_new, axis=-1, keepdims=True) + EPS)
    return [x_new, x_new * r * vecs[0]], []


def _ep_norm_bwd(acc, rows, vecs):
    xv, dres = rows
    r = lax.rsqrt(jnp.mean(xv * xv, axis=-1, keepdims=True) + EPS)
    xh = xv * r
    dxh = acc * vecs[0]
    m = jnp.mean(dxh * xh, axis=-1, keepdims=True)
    dx = dres + r * (dxh - xh * m)
    return [dx, dx], [_rowsum8(acc * xh)]


def _ep_final_loss(acc, rows, vecs):
    x2, target = rows
    gv = vecs[0]
    xv = x2 + acc
    inv_d = 1.0 / xv.shape[-1]
    r = lax.rsqrt(jnp.mean(xv * xv, axis=-1, keepdims=True) + EPS)
    xh = xv * r
    e = xh * gv - target
    dy = e * inv_d
    dxh = dy * gv
    m = jnp.mean(dxh * xh, axis=-1, keepdims=True)
    dx = r * (dxh - xh * m)
    return [dx, dx], [_rowsum8(dy * xh), _rowsum8(e * e) * (0.5 * inv_d)]


def _mm1(name, a, b, mode, tm, tn, tk, out_dtype, **kw):
    return _mm(name, [a], [b], [(0, 0, 0)], mode, tm, tn, tk, [out_dtype], **kw)[0]


def _rms_fwd(name, x, g, tr):
    S, Dm = x.shape

    def body(x_ref, g_ref, h_ref):
        xv = x_ref[...]
        r = lax.rsqrt(jnp.mean(xv * xv, axis=-1, keepdims=True) + EPS)
        h_ref[...] = (xv * r * g_ref[...]).astype(h_ref.dtype)

    return _tc_call(
        body, name=name, grid=(S // tr,),
        in_specs=[pl.BlockSpec((tr, Dm), lambda i: (i, 0)), pl.BlockSpec((1, Dm), lambda i: (0, 0))],
        out_specs=pl.BlockSpec((tr, Dm), lambda i: (i, 0)),
        out_shape=_sds((S, Dm), BF16),
        compiler_params=_cparams(("parallel",)),
    )(x, g)


def _rms_bwd(name, dh, x, g, dres, tr, want_dx=True):
    S, Dm = x.shape
    nsteps = S // tr

    def body(*refs):
        if want_dx:
            dh_ref, x_ref, g_ref, dres_ref, dx_ref, dxb_ref, dg_ref, acc_ref = refs
        else:
            dh_ref, x_ref, g_ref, dg_ref, acc_ref = refs
        i = pl.program_id(0)
        xv = x_ref[...]
        r = lax.rsqrt(jnp.mean(xv * xv, axis=-1, keepdims=True) + EPS)
        xh = xv * r
        dhv = dh_ref[...]
        part = _rowsum8(dhv * xh)

        @pl.when(i == 0)
        def _():
            acc_ref[...] = part

        @pl.when(i > 0)
        def _():
            acc_ref[...] += part

        @pl.when(i == nsteps - 1)
        def _():
            dg_ref[...] = jnp.sum(acc_ref[...], axis=0, keepdims=True)

        if want_dx:
            dxh = dhv * g_ref[...]
            m = jnp.mean(dxh * xh, axis=-1, keepdims=True)
            dx = dres_ref[...] + r * (dxh - xh * m)
            dx_ref[...] = dx
            dxb_ref[...] = dx.astype(BF16)

    row = pl.BlockSpec((tr, Dm), lambda i: (i, 0))
    vec = pl.BlockSpec((1, Dm), lambda i: (0, 0))
    if want_dx:
        in_specs = [row, row, vec, row]
        out_specs = [row, row, vec]
        out_shape = [_sds((S, Dm), F32), _sds((S, Dm), BF16), _sds((1, Dm), F32)]
        args = (dh, x, g, dres)
    else:
        in_specs = [row, row, vec]
        out_specs = [vec]
        out_shape = [_sds((1, Dm), F32)]
        args = (dh, x, g)
    return _tc_call(
        body, name=name, grid=(nsteps,), in_specs=in_specs, out_specs=out_specs, out_shape=out_shape,
        scratch_shapes=[pltpu.VMEM((8, Dm), F32)],
        compiler_params=_cparams(("arbitrary",)),
    )(*args)


def _softmax_rows(s):
    e = jnp.exp(s - jnp.max(s, axis=-1, keepdims=True))
    return e / jnp.sum(e, axis=-1, keepdims=True)


def _attn_fwd(name, q, k, v, ts):
    S, Dm = q.shape
    M = k.shape[0]
    scale = HEAD_DIM ** -0.5

    def body(q_ref, k_ref, v_ref, o_ref):
        for h in range(N_HEADS):
            sl = slice(h * HEAD_DIM, (h + 1) * HEAD_DIM)
            s = lax.dot_general(q_ref[:, sl], k_ref[:, sl], _DN["nt"], preferred_element_type=F32) * scale
            p = _softmax_rows(s)
            o_ref[:, sl] = jnp.dot(p.astype(BF16), v_ref[:, sl], preferred_element_type=F32).astype(o_ref.dtype)

    row = pl.BlockSpec((ts, Dm), lambda i: (i, 0))
    mem = pl.BlockSpec((M, Dm), lambda i: (0, 0))
    return _tc_call(
        body, name=name, grid=(S // ts,), in_specs=[row, mem, mem], out_specs=row,
        out_shape=_sds((S, Dm), BF16), compiler_params=_cparams(("parallel",)),
    )(q, k, v)


def _attn_bwd(name, q, k, v, do, ts):
    S, Dm = q.shape
    M = k.shape[0]
    scale = HEAD_DIM ** -0.5

    def body(q_ref, k_ref, v_ref, do_ref, dq_ref, dk_ref, dv_ref):
        i = pl.program_id(0)

        @pl.when(i == 0)
        def _():
            dk_ref[...] = jnp.zeros_like(dk_ref)
            dv_ref[...] = jnp.zeros_like(dv_ref)

        for h in range(N_HEADS):
            sl = slice(h * HEAD_DIM, (h + 1) * HEAD_DIM)
            qh = q_ref[:, sl]
            kh = k_ref[:, sl]
            doh = do_ref[:, sl]
            s = lax.dot_general(qh, kh, _DN["nt"], preferred_element_type=F32) * scale
            p = _softmax_rows(s)
            dp = lax.dot_general(doh, v_ref[:, sl], _DN["nt"], preferred_element_type=F32)
            ds = p * (dp - jnp.sum(dp * p, axis=-1, keepdims=True)) * scale
            dsb = ds.astype(BF16)
            dq_ref[:, sl] = jnp.dot(dsb, kh, preferred_element_type=F32).astype(dq_ref.dtype)
            dk_ref[:, sl] += lax.dot_general(dsb, qh, _DN["tn"], preferred_element_type=F32)
            dv_ref[:, sl] += lax.dot_general(p.astype(BF16), doh, _DN["tn"], preferred_element_type=F32)

    row = pl.BlockSpec((ts, Dm), lambda i: (i, 0))
    mem = pl.BlockSpec((M, Dm), lambda i: (0, 0))
    return _tc_call(
        body, name=name, grid=(S // ts,), in_specs=[row, mem, mem, row], out_specs=[row, mem, mem],
        out_shape=[_sds((S, Dm), BF16), _sds((M, Dm), F32), _sds((M, Dm), F32)],
        compiler_params=_cparams(("arbitrary",)),
    )(q, k, v, do)


def _pool_denominators(row0, ts):
    return (row0 + lax.broadcasted_iota(jnp.int32, (ts, 1), 0) + 1).astype(F32)


def _mixer_fwd(name, proj, pool_w, pool_scale, sgu_g, ws, bias_full, ts):
    S = proj.shape[0]
    nblk = ts // SGU_BLOCK
    halo_blocks = ts // POOL_HALO

    def body(proj_ref, halo_ref, pw_ref, sc_ref, g_ref, ws_ref, b_ref, y_ref, p_ref, vn_ref, ext_ref):
        i = pl.program_id(0)
        a = proj_ref[:, 0:D_POOL]
        ext_ref[0:POOL_HALO, :] = jnp.where(i > 0, halo_ref[...], 0.0)
        ext_ref[POOL_HALO:POOL_HALO + ts, :] = a
        pos = _pool_denominators(i * ts, ts)
        for gi, w in enumerate(POOL_WINDOWS):
            cs = slice(gi * POOL_GROUP, (gi + 1) * POOL_GROUP)
            acc = a[:, cs]
            for j in range(1, w):
                acc = acc + ext_ref[POOL_HALO - j:POOL_HALO - j + ts, cs]
            pg = (acc / jnp.minimum(pos, float(w)) - a[:, cs]).astype(BF16)
            p_ref[:, cs] = pg
            ypre = jnp.dot(pg, pw_ref[gi], preferred_element_type=F32)
            y_ref[:, cs] = (ypre * sc_ref[:, cs]).astype(y_ref.dtype)

        v = proj_ref[:, D_POOL + D_SGU:D_POOL + 2 * D_SGU]
        r = lax.rsqrt(jnp.mean(v * v, axis=-1, keepdims=True) + EPS)
        vn_ref[...] = (v * r * g_ref[...]).astype(BF16)
        for n in range(nblk):
            rs = slice(n * SGU_BLOCK, (n + 1) * SGU_BLOCK)
            for h in range(N_SGU_HEADS):
                cs = slice(h * SGU_BLOCK, (h + 1) * SGU_BLOCK)
                mixed = jnp.dot(ws_ref[h], vn_ref[rs, cs], preferred_element_type=F32) + b_ref[:, cs]
                u = proj_ref[rs, D_POOL + h * SGU_BLOCK:D_POOL + (h + 1) * SGU_BLOCK]
                y_ref[rs, D_POOL + h * SGU_BLOCK:D_POOL + (h + 1) * SGU_BLOCK] = (u * mixed).astype(y_ref.dtype)

    return _tc_call(
        body, name=name, grid=(S // ts,),
        in_specs=[
            pl.BlockSpec((ts, D_POOL + 2 * D_SGU), lambda i: (i, 0)),
            pl.BlockSpec((POOL_HALO, D_POOL), lambda i: (jnp.maximum(i * halo_blocks - 1, 0), 0)),
            pl.BlockSpec((4, POOL_GROUP, POOL_GROUP), lambda i: (0, 0, 0)),
            pl.BlockSpec((1, D_POOL), lambda i: (0, 0)),
            pl.BlockSpec((1, D_SGU), lambda i: (0, 0)),
            pl.BlockSpec((N_SGU_HEADS, SGU_BLOCK, SGU_BLOCK), lambda i: (0, 0, 0)),
            pl.BlockSpec((SGU_BLOCK, D_SGU), lambda i: (0, 0)),
        ],
        out_specs=[
            pl.BlockSpec((ts, D_MODEL), lambda i: (i, 0)),
            pl.BlockSpec((ts, D_POOL), lambda i: (i, 0)),
            pl.BlockSpec((ts, D_SGU), lambda i: (i, 0)),
        ],
        out_shape=[_sds((S, D_MODEL), BF16), _sds((S, D_POOL), BF16), _sds((S, D_SGU), BF16)],
        scratch_shapes=[pltpu.VMEM((ts + POOL_HALO, D_POOL), F32)],
        compiler_params=_cparams(("parallel",)),
    )(proj, proj, pool_w, pool_scale, sgu_g, ws, bias_full)


def _mixer_bwd(name, dymix, proj, p, vn, pool_w, pool_scale, sgu_g, ws, bias_full, ts):
    S = proj.shape[0]
    nsteps = S // ts
    nblk = ts // SGU_BLOCK
    halo_blocks = ts // POOL_HALO

    def body(dy_ref, dyh_ref, u_ref, v_ref, p_ref, vn_ref, pw_ref, sc_ref, g_ref, ws_ref, b_ref,
             dproj_ref, dpw_ref, dsc_ref, dg_ref, dws_ref, db_ref,
             ext_ref, dvn_ref, acc_sc, acc_g, acc_b):
        i = pl.program_id(0)

        @pl.when(i == 0)
        def _():
            dpw_ref[...] = jnp.zeros_like(dpw_ref)
            dws_ref[...] = jnp.zeros_like(dws_ref)
            acc_sc[...] = jnp.zeros_like(acc_sc)
            acc_g[...] = jnp.zeros_like(acc_g)
            acc_b[...] = jnp.zeros_like(acc_b)

        pos = _pool_denominators(i * ts, ts)
        pos_h = _pool_denominators((i + 1) * ts, POOL_HALO)
        for gi, w in enumerate(POOL_WINDOWS):
            cs = slice(gi * POOL_GROUP, (gi + 1) * POOL_GROUP)
            pg = p_ref[:, cs]
            wg = pw_ref[gi]
            dyp = dy_ref[:, cs]
            ypre = jnp.dot(pg, wg, preferred_element_type=F32)
            acc_sc[:, cs] += _rowsum8(dyp * ypre)
            dz = (dyp * sc_ref[:, cs]).astype(BF16)
            dpw_ref[gi] += lax.dot_general(pg, dz, _DN["tn"], preferred_element_type=F32)
            dp = lax.dot_general(dz, wg, _DN["nt"], preferred_element_type=F32)
            dzh = (dyh_ref[:, cs] * sc_ref[:, cs]).astype(BF16)
            dph = lax.dot_general(dzh, wg, _DN["nt"], preferred_element_type=F32)
            ext_ref[0:ts, cs] = dp / jnp.minimum(pos, float(w))
            ext_ref[ts:ts + POOL_HALO, cs] = jnp.where(i < nsteps - 1, dph / jnp.minimum(pos_h, float(w)), 0.0)
            acc = ext_ref[0:ts, cs]
            for j in range(1, w):
                acc = acc + ext_ref[j:j + ts, cs]
            dproj_ref[:, cs] = (acc - dp).astype(dproj_ref.dtype)

        for n in range(nblk):
            rs = slice(n * SGU_BLOCK, (n + 1) * SGU_BLOCK)
            for h in range(N_SGU_HEADS):
                cs = slice(h * SGU_BLOCK, (h + 1) * SGU_BLOCK)
                vnb = vn_ref[rs, cs]
                wh = ws_ref[h]
                mixed = jnp.dot(wh, vnb, preferred_element_type=F32) + b_ref[:, cs]
                dys = dy_ref[rs, D_POOL + h * SGU_BLOCK:D_POOL + (h + 1) * SGU_BLOCK]
                dproj_ref[rs, D_POOL + h * SGU_BLOCK:D_POOL + (h + 1) * SGU_BLOCK] = (dys * mixed).astype(dproj_ref.dtype)
                dmix = dys * u_ref[rs, cs]
                acc_b[:, cs] += dmix
                dmb = dmix.astype(BF16)
                dws_ref[h] += lax.dot_general(dmb, vnb, _DN["nt"], preferred_element_type=F32)
                dvn_ref[rs, cs] = lax.dot_general(wh, dmb, _DN["tn"], preferred_element_type=F32)
        v = v_ref[...]
        r = lax.rsqrt(jnp.mean(v * v, axis=-1, keepdims=True) + EPS)
        vh = v * r
        dvn = dvn_ref[...]
        acc_g[...] += _rowsum8(dvn * vh)
        dxh = dvn * g_ref[...]
        m = jnp.mean(dxh * vh, axis=-1, keepdims=True)
        dproj_ref[:, D_POOL + D_SGU:D_POOL + 2 * D_SGU] = (r * (dxh - vh * m)).astype(dproj_ref.dtype)

        @pl.when(i == nsteps - 1)
        def _():
            dsc_ref[...] = jnp.sum(acc_sc[...], axis=0, keepdims=True)
            dg_ref[...] = jnp.sum(acc_g[...], axis=0, keepdims=True)
            t_idx = lax.broadcasted_iota(jnp.int32, (SGU_BLOCK, SGU_BLOCK), 0) // SGU_CHUNK
            s_idx = lax.broadcasted_iota(jnp.int32, (SGU_BLOCK, SGU_BLOCK), 1) // SGU_CHUNK
            mask = s_idx <= t_idx
            for h in range(N_SGU_HEADS):
                cs = slice(h * SGU_BLOCK, (h + 1) * SGU_BLOCK)
                dws_ref[h] = jnp.where(mask, dws_ref[h], 0.0)
                col = jnp.sum(acc_b[:, cs], axis=1, keepdims=True)
                db_ref[h] = jnp.broadcast_to(col, (SGU_BLOCK, SGU_BLOCK))

    const2 = lambda i: (0, 0)
    const3 = lambda i: (0, 0, 0)
    last_halo = S // POOL_HALO - 1
    return _tc_call(
        body, name=name, grid=(nsteps,),
        in_specs=[
            pl.BlockSpec((ts, D_MODEL), lambda i: (i, 0)),
            pl.BlockSpec((POOL_HALO, D_POOL), lambda i: (jnp.minimum((i + 1) * halo_blocks, last_halo), 0)),
            pl.BlockSpec((ts, D_SGU), lambda i: (i, 1)),
            pl.BlockSpec((ts, D_SGU), lambda i: (i, 2)),
            pl.BlockSpec((ts, D_POOL), lambda i: (i, 0)),
            pl.BlockSpec((ts, D_SGU), lambda i: (i, 0)),
            pl.BlockSpec((4, POOL_GROUP, POOL_GROUP), const3),
            pl.BlockSpec((1, D_POOL), const2),
            pl.BlockSpec((1, D_SGU), const2),
            pl.BlockSpec((N_SGU_HEADS, SGU_BLOCK, SGU_BLOCK), const3),
            pl.BlockSpec((SGU_BLOCK, D_SGU), const2),
        ],
        out_specs=[
            pl.BlockSpec((ts, D_POOL + 2 * D_SGU), lambda i: (i, 0)),
            pl.BlockSpec((4, POOL_GROUP, POOL_GROUP), const3),
            pl.BlockSpec((1, D_POOL), const2),
            pl.BlockSpec((1, D_SGU), const2),
            pl.BlockSpec((N_SGU_HEADS, SGU_BLOCK, SGU_BLOCK), const3),
            pl.BlockSpec((N_SGU_HEADS, SGU_BLOCK, SGU_BLOCK), const3),
        ],
        out_shape=[
            _sds((S, D_POOL + 2 * D_SGU), BF16),
            _sds((4, POOL_GROUP, POOL_GROUP), F32),
            _sds((1, D_POOL), F32),
            _sds((1, D_SGU), F32),
            _sds((N_SGU_HEADS, SGU_BLOCK, SGU_BLOCK), F32),
            _sds((N_SGU_HEADS, SGU_BLOCK, SGU_BLOCK), F32),
        ],
        scratch_shapes=[
            pltpu.VMEM((ts + POOL_HALO, D_POOL), F32),
            pltpu.VMEM((ts, D_SGU), F32),
            pltpu.VMEM((8, D_POOL), F32),
            pltpu.VMEM((8, D_SGU), F32),
            pltpu.VMEM((SGU_BLOCK, D_SGU), F32),
        ],
        compiler_params=_cparams(("arbitrary",)),
    )(dymix, dymix, proj, proj, p, vn, pool_w, pool_scale, sgu_g, ws, bias_full)


def _silu_mul(accs, extras):
    (up,) = accs
    gt = extras[0]
    sig = 1.0 / (1.0 + jnp.exp(-gt))
    return gt, up, gt * sig * up


def _silu_mul_bwd(accs, extras):
    (dact,) = accs
    gt = extras[0].astype(F32)
    up = extras[1].astype(F32)
    sig = 1.0 / (1.0 + jnp.exp(-gt))
    silu = gt * sig
    dgt = dact * up * (sig * (1.0 + gt * (1.0 - sig)))
    dup = dact * silu
    return dgt, dup


def _local_step(x, mem, target, W, sm, rs):
    S = x.shape[0]
    tm = min(1024, S)
    th = min(512, S)
    tq = min(256, S)
    ts = min(512, S)
    tr = min(512, S)
    tk_s = min(2048, S)
    tk_p = min(4096, S)
    M = mem.shape[0]

    h1 = _rms_fwd("rms_mix", x, sm["norm_mix_g"], tr)
    proj = _mm1("proj_in", h1, W["w_in_t"], "nt", tm, 1024, 2048, F32)
    ymix, p, vn = _mixer_fwd("mixer_fwd", proj, W["pool_w"], sm["pool_scale"], sm["sgu_norm_g"],
                             sm["ws_masked"], sm["bias_full"], ts)
    x1, h2 = _mm_rows("proj_out", [ymix], [W["w_out"]], [(0, 0, 0)], "nn", th, 2048, [x], [sm["norm_xattn_g"]],
                      [F32, BF16], 0, _ep_residual_norm)

    mb = _rms_fwd("rms_mem", mem, sm["norm_mem_g"], M)
    q = _mm1("proj_q", h2, W["w_q"], "nn", tm, 1024, 2048, BF16)
    kk, vv = _mm("proj_kv", [mb], [W["w_k"], W["w_v"]], [(0, 0, 0), (0, 1, 1)], "nn", M, 1024, 2048, [BF16, BF16],
                 n_acc=2)
    o = _attn_fwd("attn_fwd", q, kk, vv, tm)
    x2, h3 = _mm_rows("proj_o", [o], [W["w_o"]], [(0, 0, 0)], "nn", th, 2048, [x1], [sm["norm_ffn_g"]],
                      [F32, BF16], 0, _ep_residual_norm)

    gt32 = _mm1("ffn_gate", h3, W["w_gate_t"], "nt", tm, 1408, 2048, F32)
    gt, up, act = _mm("ffn_up", [h3], [W["w_up_t"]], [(0, 0, 0)], "nt", tm, 1408, 2048, [BF16, BF16, BF16],
                      epilogue=_silu_mul, extras=(gt32,))
    dx3, dx3b, d_final_g, loss = _mm_rows(
        "ffn_down", [act], [W["w_down"]], [(0, 0, 0)], "nn", tq, 5632, [x2, target], [sm["final_norm_g"]],
        [F32, BF16], 2, _ep_final_loss, n_scalar_out=1)

    dgt, dup = _mm("ffn_down_dgrad", [dx3b], [W["w_down"]], [(0, 0, 0)], "nt", tm, 1408, 2048, [BF16, BF16],
                   epilogue=_silu_mul_bwd, extras=(gt, up))
    rs.push("w_down", _mm1("ffn_down_wgrad", act, dx3b, "tn", 1408, 1024, tk_s, BF16))
    rs.push("w_gate_t", _mm1("ffn_gate_wgrad", dgt, h3, "tn", 1408, 1024, tk_s, BF16))
    rs.reduce("w_down")
    rs.push("w_up_t", _mm1("ffn_up_wgrad", dup, h3, "tn", 1408, 1024, tk_s, BF16))
    rs.reduce("w_gate_t")
    dh3 = _mm("ffn_gate_up_dgrad", [dgt, dup], [W["w_gate_t"], W["w_up_t"]], [(0, 0, 0), (1, 1, 0)], "nn",
              th, 512, 5632, [F32])[0]
    rs.reduce("w_up_t")
    dx2, dx2b, d_ffn_g = _rms_bwd("rms_ffn_bwd", dh3, x2, sm["norm_ffn_g"], dx3, tr)
    rs.finish("w_down")

    rs.push("w_o", _mm1("proj_o_wgrad", o, dx2b, "tn", 1024, 1024, tk_p, BF16))
    rs.finish("w_gate_t")
    do = _mm1("proj_o_dgrad", dx2b, W["w_o"], "nt", tm, 1024, 2048, BF16)
    rs.reduce("w_o")
    dq, dk, dv = _attn_bwd("attn_bwd", q, kk, vv, do, tm)
    rs.push("w_q", _mm1("proj_q_wgrad", h2, dq, "tn", 1024, 1024, tk_p, BF16))
    rs.push("w_k", _mm1("proj_k_wgrad", mb, dk, "tn", 1024, 1024, M, BF16))
    rs.push("w_v", _mm1("proj_v_wgrad", mb, dv, "tn", 1024, 1024, M, BF16))
    rs.finish("w_up_t")
    dx1, dx1b, d_xattn_g = _mm_rows(
        "proj_q_dgrad", [dq], [W["w_q"]], [(0, 0, 0)], "nt", tq, 2048, [x1, dx2], [sm["norm_xattn_g"]],
        [F32, BF16], 1, _ep_norm_bwd)
    rs.reduce("w_q")
    rs.reduce("w_k")
    rs.reduce("w_v")
    dmb = _mm("proj_kv_dgrad", [dk, dv], [W["w_k"], W["w_v"]], [(0, 0, 0), (1, 1, 0)], "nt",
              M, 1024, 2048, [F32])[0]
    (d_mem_g,) = _rms_bwd("rms_mem_bwd", dmb, mem, sm["norm_mem_g"], None, M, want_dx=False)

    rs.push("w_out", _mm1("proj_out_wgrad", ymix, dx1b, "tn", 1024, 1024, tk_p, BF16))
    dymix = _mm1("proj_out_dgrad", dx1b, W["w_out"], "nt", tm, 1024, 2048, F32)
    rs.finish("w_o")
    dproj, d_pool_w, d_pool_scale, d_sgu_g, d_ws, d_b = _mixer_bwd(
        "mixer_bwd", dymix, proj, p, vn, W["pool_w"], sm["pool_scale"], sm["sgu_norm_g"],
        sm["ws_masked"], sm["bias_full"], ts)
    rs.finish("w_q")
    rs.finish("w_k")
    rs.finish("w_v")
    rs.reduce("w_out")
    rs.push("pool_w", d_pool_w.reshape(4, N_DEV, POOL_GROUP // N_DEV, POOL_GROUP).transpose(1, 0, 2, 3)
            .reshape(4 * POOL_GROUP, POOL_GROUP).astype(BF16))
    rs.small("early", dict(
        pool_scale=d_pool_scale, sgu_norm_g=d_sgu_g, w_spatial=d_ws, b_spatial=d_b[:, :, 0],
        norm_xattn_g=d_xattn_g, norm_mem_g=d_mem_g, norm_ffn_g=d_ffn_g, final_norm_g=d_final_g))
    rs.push("w_in_t", _mm1("proj_in_wgrad", dproj, h1, "tn", 1024, 1024, tk_p, BF16))
    rs.finish("w_out")
    rs.reduce("pool_w")
    rs.reduce("w_in_t")
    grad_x, d_mix_g = _mm_rows(
        "proj_in_dgrad", [dproj], [W["w_in_t"]], [(0, 0, 0)], "nn", tq, 3072, [x, dx1], [sm["norm_mix_g"]],
        [F32], 1, _ep_norm_bwd)
    rs.small("late", dict(norm_mix_g=d_mix_g, loss=jnp.pad(loss, ((0, 0), (0, _LANES - 1)))))
    rs.finish_small("early")
    rs.finish("pool_w")
    rs.finish("w_in_t")
    rs.finish_small("late")
    return loss, grad_x


def _mesh_pos():
    return lax.axis_index("x"), lax.axis_index("y"), lax.axis_index("c")


def _handshake(peers):
    barrier = pltpu.get_barrier_semaphore()
    for peer in peers:
        pl.semaphore_signal(barrier, inc=1, device_id=peer, device_id_type=MESH)
    pl.semaphore_wait(barrier, len(peers))


def _seq_all_gather(name, shards, collective_id):
    n = len(shards)

    def body(*refs):
        ins = refs[:n]
        outs = refs[n:2 * n]
        send_sems, recv_sems, local_sems = refs[2 * n:]
        x, y, c = _mesh_pos()
        me, sibling = (x, y, c), (x, y, 1 - c)
        xn, yn, dg = (1 - x, y), (x, 1 - y), (1 - x, 1 - y)
        north = c == 1
        via = (jnp.where(north, xn[0], yn[0]), jnp.where(north, xn[1], yn[1]))
        to = (jnp.where(north, yn[0], xn[0]), jnp.where(north, yn[1], xn[1]))
        _handshake([sibling, (*xn, c), (*yn, c)])

        def copy(a, k, block, target, src=None):
            bx, by, bc = block
            dst = outs[a].at[4 * bx + 2 * by + bc]
            return pltpu.make_async_remote_copy(
                src_ref=dst if src is None else src, dst_ref=dst,
                send_sem=send_sems.at[a, k], recv_sem=recv_sems.at[a, k],
                device_id=target, device_id_type=MESH)

        mine = [pltpu.make_async_copy(ins[a], outs[a].at[4 * x + 2 * y + c], local_sems.at[a]) for a in range(n)]
        for cp in mine:
            cp.start()
        started = []
        for a in range(n):
            first = [copy(a, 0, me, sibling, src=ins[a]), copy(a, 1, me, (*xn, c), src=ins[a]),
                     copy(a, 2, me, (*yn, c), src=ins[a])]
            for cp in first:
                cp.start()
            started += first
        for a in range(n):
            copy(a, 1, (*xn, c), me).wait_recv()
            copy(a, 2, (*yn, c), me).wait_recv()
            second = [copy(a, 3, (*via, c), (*to, c)), copy(a, 4, (*xn, c), sibling), copy(a, 5, (*yn, c), sibling)]
            for cp in second:
                cp.start()
            started += second
        for a in range(n):
            copy(a, 3, (*dg, c), me).wait_recv()
            last = copy(a, 6, (*dg, c), sibling)
            last.start()
            started.append(last)
        for a in range(n):
            copy(a, 0, sibling, me).wait_recv()
            for k, chip in ((4, xn), (5, yn), (6, dg)):
                copy(a, k, (*chip, 1 - c), me).wait_recv()
        for cp in started:
            cp.wait_send()
        for cp in mine:
            cp.wait()

    return _sc_call(
        body, name=name,
        out_type=[_sds((N_DEV,) + s.shape, s.dtype) for s in shards],
        scratch_types=[pltpu.SemaphoreType.DMA((n, 7)), pltpu.SemaphoreType.DMA((n, 7)),
                       pltpu.SemaphoreType.DMA((n,))],
        compiler_params=pltpu.CompilerParams(collective_id=collective_id),
    )(*shards)


def _seq_pair_exchange(name, gview, collective_id):
    def body(g_ref, theirs_ref, send_sems, recv_sems):
        x, y, c = _mesh_pos()
        sibling = (x, y, 1 - c)
        _handshake([sibling])
        copies = [pltpu.make_async_remote_copy(
            src_ref=g_ref.at[k, 1 - c], dst_ref=theirs_ref.at[k],
            send_sem=send_sems.at[k], recv_sem=recv_sems.at[k],
            device_id=sibling, device_id_type=MESH) for k in range(4)]
        for cp in copies:
            cp.start()
        for cp in copies:
            cp.wait()

    return _sc_call(
        body, name=name, out_type=_sds((4,) + gview.shape[2:], gview.dtype),
        scratch_types=[pltpu.SemaphoreType.DMA((4,)), pltpu.SemaphoreType.DMA((4,))],
        compiler_params=pltpu.CompilerParams(collective_id=collective_id),
    )(gview)


def _pair_sum(name, gview, theirs, pos, tr):
    _, _, r, C = gview.shape

    def body(pos_ref, a_ref, b_ref, o_ref):
        o_ref[...] = (a_ref[...].astype(F32) + b_ref[...].astype(F32)).astype(o_ref.dtype)

    grid_spec = pltpu.PrefetchScalarGridSpec(
        num_scalar_prefetch=1, grid=(4, r // tr),
        in_specs=[pl.BlockSpec((None, None, tr, C), lambda k, t, pos_ref: (k, pos_ref[0], t, 0)),
                  pl.BlockSpec((None, tr, C), lambda k, t, pos_ref: (k, t, 0))],
        out_specs=pl.BlockSpec((None, tr, C), lambda k, t, pos_ref: (k, t, 0)))
    return _tc_call(
        body, name=name, grid_spec=grid_spec, out_shape=_sds(theirs.shape, theirs.dtype),
        compiler_params=_cparams(("parallel", "parallel")),
    )(pos, gview, theirs)


def _seq_chip_exchange(name, pair, collective_id):
    def body(p_ref, land_ref, send_sems, recv_sems):
        x, y, c = _mesh_pos()
        my_chip = 2 * x + y
        chips = [(1 - x, y), (x, 1 - y), (1 - x, 1 - y)]
        _handshake([(cx, cy, c) for cx, cy in chips])
        copies = [pltpu.make_async_remote_copy(
            src_ref=p_ref.at[2 * cx + cy], dst_ref=land_ref.at[my_chip],
            send_sem=send_sems.at[j], recv_sem=recv_sems.at[j],
            device_id=(cx, cy, c), device_id_type=MESH) for j, (cx, cy) in enumerate(chips)]
        for cp in copies:
            cp.start()
        for cp in copies:
            cp.wait_send()
        for j, (cx, cy) in enumerate(chips):
            pltpu.make_async_remote_copy(
                src_ref=p_ref.at[my_chip], dst_ref=land_ref.at[2 * cx + cy],
                send_sem=send_sems.at[j], recv_sem=recv_sems.at[j],
                device_id=(cx, cy, c), device_id_type=MESH).wait_recv()

    return _sc_call(
        body, name=name, out_type=_sds(pair.shape, pair.dtype),
        scratch_types=[pltpu.SemaphoreType.DMA((3,)), pltpu.SemaphoreType.DMA((3,))],
        compiler_params=pltpu.CompilerParams(collective_id=collective_id),
    )(pair)


def _sum_leading(name, parts, tr, out_dtype=F32):
    n, r, C = parts.shape

    def body(p_ref, o_ref):
        acc = p_ref[0].astype(F32)
        for k in range(1, n):
            acc = acc + p_ref[k].astype(F32)
        o_ref[...] = acc.astype(o_ref.dtype)

    return _tc_call(
        body, name=name, grid=(r // tr,),
        in_specs=[pl.BlockSpec((n, tr, C), lambda t: (0, t, 0))],
        out_specs=pl.BlockSpec((tr, C), lambda t: (t, 0)),
        out_shape=_sds((r, C), out_dtype), compiler_params=_cparams(("parallel",)),
    )(parts)


def _row_tile(r):
    for t in (512, 384, 352, 256, 128, 64, 32, 16, 8):
        if r % t == 0:
            return t
    return r


def _adamw_math(w, g, m, v):
    c1 = 1.0 - ADAM_B1 ** ADAM_STEP
    c2 = 1.0 - ADAM_B2 ** ADAM_STEP
    nm = ADAM_B1 * m + (1.0 - ADAM_B1) * g
    nv = ADAM_B2 * v + (1.0 - ADAM_B2) * (g * g)
    m_hat = nm / c1
    v_hat = nv / c2
    return -ADAM_LR * (m_hat / (jnp.sqrt(v_hat) + ADAM_EPS) + ADAM_WD * w), nm, nv


def _chip_sum_adamw(name, pair, landed, pos, w, m, v, transposed):
    _, r, C = pair.shape
    if transposed:
        tr, tc = r, 512
        r_pad = -r % _LANES
        wspec = pl.BlockSpec((tc, r), lambda t, k, pos_ref: (t, 0))
        shape = (C, r)
        scratch = [pltpu.VMEM((tr, tc), F32), pltpu.VMEM((tc, r + r_pad), F32)]
    else:
        tr, tc = _row_tile(r), C
        wspec = pl.BlockSpec((tr, C), lambda t, k, pos_ref: (t, 0))
        shape = (r, C)
        scratch = [pltpu.VMEM((tr, tc), F32)]
    n_t = (C // tc) if transposed else (r // tr)

    def block(chip, t):
        return (chip, 0, t) if transposed else (chip, t, 0)

    def body(pos_ref, own_ref, land_ref, w_ref, m_ref, v_ref, g_ref, d_ref, nm_ref, nv_ref, acc_ref, *turn):
        k = pl.program_id(1)
        val = jnp.where(k == pos_ref[1], own_ref[...], land_ref[...]).astype(F32)

        @pl.when(k == 0)
        def _():
            acc_ref[...] = val

        @pl.when(k > 0)
        def _():
            acc_ref[...] += val

        @pl.when(k == 3)
        def _():
            if transposed:
                g_t = acc_ref[...]
                if r_pad:
                    g_t = jnp.concatenate([g_t, jnp.zeros((r_pad, tc), F32)], axis=0)
                turn[0][...] = g_t.T
                g = turn[0][:, 0:r]
            else:
                g = acc_ref[...]
            d, nm, nv = _adamw_math(w_ref[...], g, m_ref[...], v_ref[...])
            g_ref[...] = g
            d_ref[...] = d
            nm_ref[...] = nm
            nv_ref[...] = nv

    def land_index(t, k, pos_ref):
        return block(jnp.where(k == pos_ref[1], (k + 1) % 4, k), t)

    grid_spec = pltpu.PrefetchScalarGridSpec(
        num_scalar_prefetch=1, grid=(n_t, 4),
        in_specs=[pl.BlockSpec((None, tr, tc), lambda t, k, pos_ref: block(pos_ref[1], t)),
                  pl.BlockSpec((None, tr, tc), land_index), wspec, wspec, wspec],
        out_specs=[wspec] * 4, scratch_shapes=scratch)
    return _tc_call(
        body, name=name, grid_spec=grid_spec, out_shape=[_sds(shape, F32)] * 4,
        compiler_params=_cparams(("parallel", "arbitrary")),
    )(pos, pair, landed, w, m, v)


def _adamw(name, w, g, m, v):
    R, C = w.shape
    tr = _row_tile(R)

    def body(w_ref, g_ref, m_ref, v_ref, d_ref, nm_ref, nv_ref):
        d_ref[...], nm_ref[...], nv_ref[...] = _adamw_math(w_ref[...], g_ref[...], m_ref[...], v_ref[...])

    spec = pl.BlockSpec((tr, C), lambda i: (i, 0))
    return _tc_call(
        body, name=name, grid=(R // tr,), in_specs=[spec] * 4, out_specs=[spec] * 3,
        out_shape=[_sds((R, C), F32)] * 3, compiler_params=_cparams(("parallel",)),
    )(w, g, m, v)


_BIG = ("w_in_t", "w_out", "w_q", "w_k", "w_v", "w_o", "w_gate_t", "w_up_t", "w_down")
_SMALL = ("norm_mix_g", "pool_scale", "sgu_norm_g", "w_spatial", "b_spatial", "norm_xattn_g",
          "norm_mem_g", "norm_ffn_g", "final_norm_g")
_LANES = 128
_GATHER_GROUPS = (("w_in_t", "pool_w"), ("w_out",), ("w_q",), ("w_k", "w_v"), ("w_o",), ("w_gate_t",),
                  ("w_up_t",), ("w_down",))
_SMALL_GROUPS = dict(
    early=("pool_scale", "sgu_norm_g", "w_spatial", "b_spatial", "norm_xattn_g", "norm_mem_g",
           "norm_ffn_g", "final_norm_g"),
    late=("norm_mix_g", "loss"))
_TURN_OUTSIDE = ("w_gate_t", "w_up_t")
_ID_GATHER, _ID_PAIR, _ID_CHIP = 0, 1, 2


_PACK_ROWS = 512


def _pack(parts):
    rows = [p.reshape(-1, _LANES) for p in parts]
    n = sum(r.shape[0] for r in rows)
    pad = -n % (_PACK_ROWS if n > _PACK_ROWS else 8)
    if pad:
        rows.append(jnp.zeros((pad, _LANES), rows[0].dtype))
    return jnp.concatenate(rows, axis=0)


class _GradReducer:
    def __init__(self, pos, apply, apply_small):
        self.pos, self.apply, self.apply_small = pos, apply, apply_small
        self.view, self.theirs, self.pair, self.landed = {}, {}, {}, {}
        self.small_gathered = {}

    def push(self, k, g):
        r = g.shape[0] // N_DEV
        self.view[k] = g.reshape(4, 2, r, g.shape[1])
        self.theirs[k] = _seq_pair_exchange("grad_pair_exchange_" + k, self.view[k], _ID_PAIR)

    def reduce(self, k):
        r = self.view[k].shape[2]
        self.pair[k] = _pair_sum("grad_pair_sum_" + k, self.view[k], self.theirs[k], self.pos, r)
        self.landed[k] = _seq_chip_exchange("grad_chip_exchange_" + k, self.pair[k], _ID_CHIP)

    def finish(self, k):
        self.apply(k, self.pair[k], self.landed[k])

    def small(self, tag, parts):
        packed = _pack([parts[k] for k in _SMALL_GROUPS[tag]])
        (self.small_gathered[tag],) = _seq_all_gather("gather_small_grads_" + tag, [packed], _ID_GATHER)

    def finish_small(self, tag):
        allp = self.small_gathered[tag]
        self.apply_small(tag, _sum_leading("sum_small_grads_" + tag, allp, min(_PACK_ROWS, allp.shape[1])))


def _unpack(packed, like):
    out, row = [], 0
    for ref in like:
        rows = ref.size // _LANES
        out.append(packed[row:row + rows].reshape(ref.shape))
        row += rows
    return out


def kernel(x, mem, norm_mix_g, w_in, pool_w, pool_scale, sgu_norm_g, w_spatial, b_spatial, w_out, norm_xattn_g, norm_mem_g, w_q, w_k, w_v, w_o, norm_ffn_g, w_gate, w_up, w_down, final_norm_g, loss_target, m_norm_mix_g, m_w_in, m_pool_w, m_pool_scale, m_sgu_norm_g, m_w_spatial, m_b_spatial, m_w_out, m_norm_xattn_g, m_norm_mem_g, m_w_q, m_w_k, m_w_v, m_w_o, m_norm_ffn_g, m_w_gate, m_w_up, m_w_down, m_final_norm_g, v_norm_mix_g, v_w_in, v_pool_w, v_pool_scale, v_sgu_norm_g, v_w_spatial, v_b_spatial, v_w_out, v_norm_xattn_g, v_norm_mem_g, v_w_q, v_w_k, v_w_v, v_w_o, v_norm_ffn_g, v_w_gate, v_w_up, v_w_down, v_final_norm_g):
    args = dict(locals())
    names = ("norm_mix_g", "w_in", "pool_w", "pool_scale", "sgu_norm_g", "w_spatial", "b_spatial", "w_out",
             "norm_xattn_g", "norm_mem_g", "w_q", "w_k", "w_v", "w_o", "norm_ffn_g", "w_gate", "w_up",
             "w_down", "final_norm_g")
    w = {k: args[k] for k in names}
    m = {k: args["m_" + k] for k in names}
    v = {k: args["v_" + k] for k in names}
    _CHAIN.__init__()

    shards = dict(
        w_in_t=w["w_in"][0].T, w_out=w["w_out"][0], w_q=w["w_q"][0], w_k=w["w_k"][0], w_v=w["w_v"][0],
        w_o=w["w_o"][0], w_gate_t=w["w_gate"][0].T, w_up_t=w["w_up"][0].T, w_down=w["w_down"][0])
    send = {k: shards[k].astype(BF16) for k in _BIG}
    send["pool_w"] = w["pool_w"][0].reshape(4 * 32, POOL_GROUP).astype(BF16)
    W = {}
    for gi, group in enumerate(_GATHER_GROUPS):
        gathered = _seq_all_gather("gather_weights_%d" % gi, [send[k] for k in group], _ID_GATHER)
        for k, g in zip(group, gathered):
            W[k] = g.reshape(-1, g.shape[-1])
    W["pool_w"] = W["pool_w"].reshape(N_DEV, 4, 32, POOL_GROUP).transpose(1, 0, 2, 3).reshape(4, POOL_GROUP, POOL_GROUP)

    t = jnp.arange(SGU_BLOCK)
    mask = (t[None, :] // SGU_CHUNK) <= (t[:, None] // SGU_CHUNK)
    sm = dict(
        norm_mix_g=w["norm_mix_g"], pool_scale=w["pool_scale"], sgu_norm_g=w["sgu_norm_g"],
        norm_xattn_g=w["norm_xattn_g"], norm_mem_g=w["norm_mem_g"], norm_ffn_g=w["norm_ffn_g"],
        final_norm_g=w["final_norm_g"].reshape(1, D_MODEL),
        ws_masked=jnp.where(mask[None], w["w_spatial"][0], 0.0).astype(BF16),
        bias_full=jnp.repeat(w["b_spatial"][0].T, SGU_BLOCK, axis=1))

    natural = dict(w_in_t="w_in", w_gate_t="w_gate", w_up_t="w_up")
    grads, delta, new_m, new_v = {}, {}, {}, {}

    def apply(k, pair, landed):
        name = natural.get(k, k)
        if k == "pool_w":
            flat = (4 * POOL_GROUP // N_DEV, POOL_GROUP)
            res = _chip_sum_adamw("grad_finish_" + k, pair, landed, pos, w[k].reshape(flat), m[k].reshape(flat),
                                  v[k].reshape(flat), False)
            grads[k], delta[k], new_m[k], new_v[k] = (a.reshape(w[k].shape) for a in res)
            return
        if k in _TURN_OUTSIDE:
            res = _chip_sum_adamw("grad_finish_" + k, pair, landed, pos, w[name][0].T, m[name][0].T, v[name][0].T,
                                  False)
            res = [a.T for a in res]
        else:
            res = _chip_sum_adamw("grad_finish_" + k, pair, landed, pos, w[name][0], m[name][0], v[name][0],
                                  k in natural)
        grads[name], delta[name], new_m[name], new_v[name] = (a[None] for a in res)

    like = dict(w)
    like["loss"] = _sds((1, _LANES), F32)

    def apply_small(tag, total):
        group = _SMALL_GROUPS[tag]
        grads.update(zip(group, _unpack(total, [like[k] for k in group])))
        if tag == "late":
            d_, m_, v_ = _adamw("adamw_small", _pack([w[k] for k in _SMALL]), _pack([grads[k] for k in _SMALL]),
                                _pack([m[k] for k in _SMALL]), _pack([v[k] for k in _SMALL]))
            shapes = [w[k] for k in _SMALL]
            for k, a, b, c_ in zip(_SMALL, _unpack(d_, shapes), _unpack(m_, shapes), _unpack(v_, shapes)):
                delta[k], new_m[k], new_v[k] = a, b, c_

    pos = jnp.stack([lax.axis_index("c"), 2 * lax.axis_index("x") + lax.axis_index("y")]).astype(jnp.int32)
    rs = _GradReducer(pos, apply, apply_small)
    _, grad_x = _local_step(x[0], mem[0], loss_target[0], W, sm, rs)

    outs = [grads["loss"][0, 0], grad_x[None]]
    outs += [grads[k].reshape(w[k].shape) for k in names]
    outs += [delta[k] for k in names]
    outs += [new_m[k] for k in names]
    outs += [new_v[k] for k in names]
    return tuple(outs)
```

```python
import jax
import jax.numpy as jnp
from jax import lax
from jax.experimental import pallas as pl
from jax.experimental.pallas import tpu as pltpu
from jax.experimental.pallas import tpu_sc as plsc

F32 = jnp.float32
BF16 = jnp.bfloat16
MESH = pl.DeviceIdType.MESH

EPS = 1e-6
D_MODEL = 2048
D_POOL = 1024
D_SGU = 1024
POOL_WINDOWS = (2, 4, 8, 16)
POOL_GROUP = 256
POOL_HALO = 16
SGU_BLOCK = 128
SGU_CHUNK = 64
N_SGU_HEADS = 8
N_HEADS = 4
HEAD_DIM = 512
N_DEV = 8

ADAM_LR = 0.001
ADAM_B1 = 0.9
ADAM_B2 = 0.999
ADAM_EPS = 1e-08
ADAM_WD = 0.01
ADAM_STEP = 10

VMEM_LIMIT = 56 * 1024 * 1024


def _cparams(sem=None):
    return pltpu.CompilerParams(dimension_semantics=sem, vmem_limit_bytes=VMEM_LIMIT)


def _sds(shape, dtype):
    return jax.ShapeDtypeStruct(shape, dtype)


_ANY = pl.BlockSpec(memory_space=pl.ANY)


class _Chain:
    def __init__(self):
        self.tc = None


_CHAIN = _Chain()


def _first(out):
    return out[0] if isinstance(out, (list, tuple)) else out


def _tc_call(body, *, in_specs=None, grid_spec=None, **kw):
    def run(*args):
        prev, n = _CHAIN.tc, len(args)
        fn, specs, spec, operands = body, in_specs, grid_spec, args
        if prev is not None:
            def fn(*refs):
                return body(*refs[:n], *refs[n + 1:])
            operands = args + (prev,)
            if grid_spec is None:
                specs = list(in_specs) + [_ANY]
            else:
                spec = pltpu.PrefetchScalarGridSpec(
                    num_scalar_prefetch=grid_spec.num_scalar_prefetch, grid=grid_spec.grid,
                    in_specs=list(grid_spec.in_specs) + [_ANY], out_specs=grid_spec.out_specs,
                    scratch_shapes=grid_spec.scratch_shapes)
        if spec is None:
            out = pl.pallas_call(fn, in_specs=specs, **kw)(*operands)
        else:
            out = pl.pallas_call(fn, grid_spec=spec, **kw)(*operands)
        _CHAIN.tc = _first(out)
        return out
    return run


def _sc_call(body, **kw):
    return pl.kernel(body, mesh=plsc.ScalarSubcoreMesh(axis_name="seq", num_cores=1), **kw)


def _rowsum8(v):
    r, c = v.shape
    return v.reshape(r // 8, 8, c).sum(axis=0)


_DN = {
    "nn": (((1,), (0,)), ((), ())),
    "nt": (((1,), (1,)), ((), ())),
    "tn": (((0,), (0,)), ((), ())),
}


def _mm(name, a_list, b_list, terms, mode, tm, tn, tk, out_dtypes, epilogue=None, extras=(), n_acc=1):
    a0, b0 = a_list[0], b_list[0]
    if mode == "tn":
        K, M = a0.shape
    else:
        M, K = a0.shape
    N = b0.shape[0] if mode == "nt" else b0.shape[1]
    assert M % tm == 0 and N % tn == 0 and K % tk == 0, (name, M, N, K, tm, tn, tk)
    nk = K // tk
    na, nb, ne, no = len(a_list), len(b_list), len(extras), len(out_dtypes)
    dn = _DN[mode]

    if mode == "tn":
        a_spec = pl.BlockSpec((tk, tm), lambda i, j, k: (k, i))
    else:
        a_spec = pl.BlockSpec((tm, tk), lambda i, j, k: (i, k))
    if mode == "nt":
        b_spec = pl.BlockSpec((tn, tk), lambda i, j, k: (j, k))
    else:
        b_spec = pl.BlockSpec((tk, tn), lambda i, j, k: (k, j))
    o_spec = pl.BlockSpec((tm, tn), lambda i, j, k: (i, j))

    def body(*refs):
        a_refs = refs[:na]
        b_refs = refs[na:na + nb]
        e_refs = refs[na + nb:na + nb + ne]
        o_refs = refs[na + nb + ne:na + nb + ne + no]
        acc_refs = refs[na + nb + ne + no:]

        parts = [None] * n_acc
        for ai, bi, ci in terms:
            d = lax.dot_general(a_refs[ai][...].astype(BF16), b_refs[bi][...].astype(BF16), dn,
                                preferred_element_type=F32)
            parts[ci] = d if parts[ci] is None else parts[ci] + d

        def finish(accs):
            outs = epilogue(accs, [e[...] for e in e_refs]) if epilogue is not None else accs
            for o_ref, v in zip(o_refs, outs):
                o_ref[...] = v.astype(o_ref.dtype)

        if nk == 1:
            finish(parts)
        else:
            k = pl.program_id(2)

            @pl.when(k == 0)
            def _():
                for c in range(n_acc):
                    acc_refs[c][...] = parts[c]

            @pl.when(k > 0)
            def _():
                for c in range(n_acc):
                    acc_refs[c][...] += parts[c]

            @pl.when(k == nk - 1)
            def _():
                finish([acc_refs[c][...] for c in range(n_acc)])

    scratch = [pltpu.VMEM((tm, tn), F32) for _ in range(n_acc)] if nk > 1 else []
    return _tc_call(
        body, name=name, grid=(M // tm, N // tn, nk),
        in_specs=[a_spec] * na + [b_spec] * nb + [o_spec] * ne,
        out_specs=[o_spec] * no,
        out_shape=[_sds((M, N), dt) for dt in out_dtypes],
        scratch_shapes=scratch,
        compiler_params=_cparams(("parallel", "parallel", "arbitrary")),
    )(*a_list, *b_list, *extras)


def _mm_rows(name, a_list, b_list, terms, mode, tm, tk, rows, vecs, row_dtypes, n_vec_out, epilogue,
             n_scalar_out=0):
    a0, b0 = a_list[0], b_list[0]
    M, K = a0.shape
    N = b0.shape[0] if mode == "nt" else b0.shape[1]
    assert mode in ("nn", "nt") and M % tm == 0 and K % tk == 0, (name, M, N, K, tm, tk)
    nm, nk = M // tm, K // tk
    slab = min(128, tm)
    na, nb, nr, nv, no = len(a_list), len(b_list), len(rows), len(vecs), len(row_dtypes)
    dn = _DN[mode]
    a_spec = pl.BlockSpec((tm, tk), lambda i, k: (i, k))
    b_mode = dict(pipeline_mode=pl.Buffered(1)) if nk == 1 else {}
    b_spec = (pl.BlockSpec((N, tk), lambda i, k: (0, k), **b_mode) if mode == "nt"
              else pl.BlockSpec((tk, N), lambda i, k: (k, 0), **b_mode))
    row_spec = pl.BlockSpec((tm, N), lambda i, k: (i, 0))
    vec_spec = pl.BlockSpec((1, N), lambda i, k: (0, 0))
    one_spec = pl.BlockSpec((1, 1), lambda i, k: (0, 0))

    def body(*refs):
        pos = 0
        a_refs = refs[pos:pos + na]; pos += na
        b_refs = refs[pos:pos + nb]; pos += nb
        r_refs = refs[pos:pos + nr]; pos += nr
        v_refs = refs[pos:pos + nv]; pos += nv
        o_refs = refs[pos:pos + no]; pos += no
        s_refs = refs[pos:pos + n_vec_out]; pos += n_vec_out
        vacc_refs = refs[pos:pos + n_vec_out]; pos += n_vec_out
        acc_ref = refs[pos] if nk > 1 else None
        i, k = pl.program_id(0), pl.program_id(1)
        part = None
        for ai, bi, _ in terms:
            d = lax.dot_general(a_refs[ai][...].astype(BF16), b_refs[bi][...].astype(BF16), dn,
                                preferred_element_type=F32)
            part = d if part is None else part + d

        def finish(acc):
            vecs_now = [v[...] for v in v_refs]
            vparts = None
            for r0 in range(0, tm, slab):
                rs_ = slice(r0, r0 + slab)
                outs, vp = epilogue(acc[rs_, :], [r[rs_, :] for r in r_refs], vecs_now)
                for o_ref, val in zip(o_refs, outs):
                    o_ref[rs_, :] = val.astype(o_ref.dtype)
                vparts = vp if vparts is None else [a + b for a, b in zip(vparts, vp)]

            @pl.when(i == 0)
            def _():
                for vacc, vp in zip(vacc_refs, vparts):
                    vacc[...] = vp

            @pl.when(i > 0)
            def _():
                for vacc, vp in zip(vacc_refs, vparts):
                    vacc[...] += vp

            @pl.when(i == nm - 1)
            def _():
                for j, (s_ref, vacc) in enumerate(zip(s_refs, vacc_refs)):
                    col = jnp.sum(vacc[...], axis=0, keepdims=True)
                    s_ref[...] = jnp.sum(col, axis=1, keepdims=True) if j >= n_vec_out - n_scalar_out else col

        if nk == 1:
            finish(part)
        else:
            @pl.when(k == 0)
            def _():
                acc_ref[...] = part

            @pl.when(k > 0)
            def _():
                acc_ref[...] += part

            @pl.when(k == nk - 1)
            def _():
                finish(acc_ref)

    n_plain = n_vec_out - n_scalar_out
    return _tc_call(
        body, name=name, grid=(nm, nk),
        in_specs=[a_spec] * na + [b_spec] * nb + [row_spec] * nr + [vec_spec] * nv,
        out_specs=[row_spec] * no + [vec_spec] * n_plain + [one_spec] * n_scalar_out,
        out_shape=[_sds((M, N), dt) for dt in row_dtypes] + [_sds((1, N), F32)] * n_plain
        + [_sds((1, 1), F32)] * n_scalar_out,
        scratch_shapes=[pltpu.VMEM((8, N), F32)] * n_vec_out + ([pltpu.VMEM((tm, N), F32)] if nk > 1 else []),
        compiler_params=_cparams(("arbitrary", "arbitrary")),
    )(*a_list, *b_list, *rows, *vecs)


def _ep_residual_norm(acc, rows, vecs):
    x_new = rows[0] + acc
    r = lax.rsqrt(jnp.mean(x_new * x_new, axis=-1, keepdims=True) + EPS)
    return [x_new, x_new * r * vecs[0]], []


def _ep_norm_bwd(acc, rows, vecs):
    xv, dres = rows
    r = lax.rsqrt(jnp.mean(xv * xv, axis=-1, keepdims=True) + EPS)
    xh = xv * r
    dxh = acc * vecs[0]
    m = jnp.mean(dxh * xh, axis=-1, keepdims=True)
    dx = dres + r * (dxh - xh * m)
    return [dx, dx], [_rowsum8(acc * xh)]


def _ep_final_loss(acc, rows, vecs):
    x2, target = rows
    gv = vecs[0]
    xv = x2 + acc
    inv_d = 1.0 / xv.shape[-1]
    r = lax.rsqrt(jnp.mean(xv * xv, axis=-1, keepdims=True) + EPS)
    xh = xv * r
    e = xh * gv - target
    dy = e * inv_d
    dxh = dy * gv
    m = jnp.mean(dxh * xh, axis=-1, keepdims=True)
    dx = r * (dxh - xh * m)
    return [dx, dx], [_rowsum8(dy * xh), _rowsum8(e * e) * (0.5 * inv_d)]


def _mm1(name, a, b, mode, tm, tn, tk, out_dtype, **kw):
    return _mm(name, [a], [b], [(0, 0, 0)], mode, tm, tn, tk, [out_dtype], **kw)[0]


def _rms_fwd(name, x, g, tr):
    S, Dm = x.shape

    def body(x_ref, g_ref, h_ref):
        xv = x_ref[...]
        r = lax.rsqrt(jnp.mean(xv * xv, axis=-1, keepdims=True) + EPS)
        h_ref[...] = (xv * r * g_ref[...]).astype(h_ref.dtype)

    return _tc_call(
        body, name=name, grid=(S // tr,),
        in_specs=[pl.BlockSpec((tr, Dm), lambda i: (i, 0)), pl.BlockSpec((1, Dm), lambda i: (0, 0))],
        out_specs=pl.BlockSpec((tr, Dm), lambda i: (i, 0)),
        out_shape=_sds((S, Dm), BF16),
        compiler_params=_cparams(("parallel",)),
    )(x, g)


def _rms_bwd(name, dh, x, g, dres, tr, want_dx=True):
    S, Dm = x.shape
    nsteps = S // tr

    def body(*refs):
        if want_dx:
            dh_ref, x_ref, g_ref, dres_ref, dx_ref, dxb_ref, dg_ref, acc_ref = refs
        else:
            dh_ref, x_ref, g_ref, dg_ref, acc_ref = refs
        i = pl.program_id(0)
        xv = x_ref[...]
        r = lax.rsqrt(jnp.mean(xv * xv, axis=-1, keepdims=True) + EPS)
        xh = xv * r
        dhv = dh_ref[...]
        part = _rowsum8(dhv * xh)

        @pl.when(i == 0)
        def _():
            acc_ref[...] = part

        @pl.when(i > 0)
        def _():
            acc_ref[...] += part

        @pl.when(i == nsteps - 1)
        def _():
            dg_ref[...] = jnp.sum(acc_ref[...], axis=0, keepdims=True)

        if want_dx:
            dxh = dhv * g_ref[...]
            m = jnp.mean(dxh * xh, axis=-1, keepdims=True)
            dx = dres_ref[...] + r * (dxh - xh * m)
            dx_ref[...] = dx
            dxb_ref[...] = dx.astype(BF16)

    row = pl.BlockSpec((tr, Dm), lambda i: (i, 0))
    vec = pl.BlockSpec((1, Dm), lambda i: (0, 0))
    if want_dx:
        in_specs = [row, row, vec, row]
        out_specs = [row, row, vec]
        out_shape = [_sds((S, Dm), F32), _sds((S, Dm), BF16), _sds((1, Dm), F32)]
        args = (dh, x, g, dres)
    else:
        in_specs = [row, row, vec]
        out_specs = [vec]
        out_shape = [_sds((1, Dm), F32)]
        args = (dh, x, g)
    return _tc_call(
        body, name=name, grid=(nsteps,), in_specs=in_specs, out_specs=out_specs, out_shape=out_shape,
        scratch_shapes=[pltpu.VMEM((8, Dm), F32)],
        compiler_params=_cparams(("arbitrary",)),
    )(*args)


def _softmax_rows(s):
    e = jnp.exp(s - jnp.max(s, axis=-1, keepdims=True))
    return e / jnp.sum(e, axis=-1, keepdims=True)


def _attn_fwd(name, q, k, v, ts):
    S, Dm = q.shape
    M = k.shape[0]
    scale = HEAD_DIM ** -0.5

    def body(q_ref, k_ref, v_ref, o_ref):
        for h in range(N_HEADS):
            sl = slice(h * HEAD_DIM, (h + 1) * HEAD_DIM)
            s = lax.dot_general(q_ref[:, sl], k_ref[:, sl], _DN["nt"], preferred_element_type=F32) * scale
            p = _softmax_rows(s)
            o_ref[:, sl] = jnp.dot(p.astype(BF16), v_ref[:, sl], preferred_element_type=F32).astype(o_ref.dtype)

    row = pl.BlockSpec((ts, Dm), lambda i: (i, 0))
    mem = pl.BlockSpec((M, Dm), lambda i: (0, 0))
    return _tc_call(
        body, name=name, grid=(S // ts,), in_specs=[row, mem, mem], out_specs=row,
        out_shape=_sds((S, Dm), BF16), compiler_params=_cparams(("parallel",)),
    )(q, k, v)


def _attn_bwd(name, q, k, v, do, ts):
    S, Dm = q.shape
    M = k.shape[0]
    scale = HEAD_DIM ** -0.5

    def body(q_ref, k_ref, v_ref, do_ref, dq_ref, dk_ref, dv_ref):
        i = pl.program_id(0)

        @pl.when(i == 0)
        def _():
            dk_ref[...] = jnp.zeros_like(dk_ref)
            dv_ref[...] = jnp.zeros_like(dv_ref)

        for h in range(N_HEADS):
            sl = slice(h * HEAD_DIM, (h + 1) * HEAD_DIM)
            qh = q_ref[:, sl]
            kh = k_ref[:, sl]
            doh = do_ref[:, sl]
            s = lax.dot_general(qh, kh, _DN["nt"], preferred_element_type=F32) * scale
            p = _softmax_rows(s)
            dp = lax.dot_general(doh, v_ref[:, sl], _DN["nt"], preferred_element_type=F32)
            ds = p * (dp - jnp.sum(dp * p, axis=-1, keepdims=True)) * scale
            dsb = ds.astype(BF16)
            dq_ref[:, sl] = jnp.dot(dsb, kh, preferred_element_type=F32).astype(dq_ref.dtype)
            dk_ref[:, sl] += lax.dot_general(dsb, qh, _DN["tn"], preferred_element_type=F32)
            dv_ref[:, sl] += lax.dot_general(p.astype(BF16), doh, _DN["tn"], preferred_element_type=F32)

    row = pl.BlockSpec((ts, Dm), lambda i: (i, 0))
    mem = pl.BlockSpec((M, Dm), lambda i: (0, 0))
    return _tc_call(
        body, name=name, grid=(S // ts,), in_specs=[row, mem, mem, row], out_specs=[row, mem, mem],
        out_shape=[_sds((S, Dm), BF16), _sds((M, Dm), F32), _sds((M, Dm), F32)],
        compiler_params=_cparams(("arbitrary",)),
    )(q, k, v, do)


def _pool_denominators(row0, ts):
    return (row0 + lax.broadcasted_iota(jnp.int32, (ts, 1), 0) + 1).astype(F32)


def _mixer_fwd(name, proj, pool_w, pool_scale, sgu_g, ws, bias_full, ts):
    S = proj.shape[0]
    nblk = ts // SGU_BLOCK
    halo_blocks = ts // POOL_HALO

    def body(proj_ref, halo_ref, pw_ref, sc_ref, g_ref, ws_ref, b_ref, y_ref, p_ref, vn_ref, ext_ref):
        i = pl.program_id(0)
        a = proj_ref[:, 0:D_POOL]
        ext_ref[0:POOL_HALO, :] = jnp.where(i > 0, halo_ref[...], 0.0)
        ext_ref[POOL_HALO:POOL_HALO + ts, :] = a
        pos = _pool_denominators(i * ts, ts)
        for gi, w in enumerate(POOL_WINDOWS):
            cs = slice(gi * POOL_GROUP, (gi + 1) * POOL_GROUP)
            acc = a[:, cs]
            for j in range(1, w):
                acc = acc + ext_ref[POOL_HALO - j:POOL_HALO - j + ts, cs]
            pg = (acc / jnp.minimum(pos, float(w)) - a[:, cs]).astype(BF16)
            p_ref[:, cs] = pg
            ypre = jnp.dot(pg, pw_ref[gi], preferred_element_type=F32)
            y_ref[:, cs] = (ypre * sc_ref[:, cs]).astype(y_ref.dtype)

        v = proj_ref[:, D_POOL + D_SGU:D_POOL + 2 * D_SGU]
        r = lax.rsqrt(jnp.mean(v * v, axis=-1, keepdims=True) + EPS)
        vn_ref[...] = (v * r * g_ref[...]).astype(BF16)
        for n in range(nblk):
            rs = slice(n * SGU_BLOCK, (n + 1) * SGU_BLOCK)
            for h in range(N_SGU_HEADS):
                cs = slice(h * SGU_BLOCK, (h + 1) * SGU_BLOCK)
                mixed = jnp.dot(ws_ref[h], vn_ref[rs, cs], preferred_element_type=F32) + b_ref[:, cs]
                u = proj_ref[rs, D_POOL + h * SGU_BLOCK:D_POOL + (h + 1) * SGU_BLOCK]
                y_ref[rs, D_POOL + h * SGU_BLOCK:D_POOL + (h + 1) * SGU_BLOCK] = (u * mixed).astype(y_ref.dtype)

    return _tc_call(
        body, name=name, grid=(S // ts,),
        in_specs=[
            pl.BlockSpec((ts, D_POOL + 2 * D_SGU), lambda i: (i, 0)),
            pl.BlockSpec((POOL_HALO, D_POOL), lambda i: (jnp.maximum(i * halo_blocks - 1, 0), 0)),
            pl.BlockSpec((4, POOL_GROUP, POOL_GROUP), lambda i: (0, 0, 0)),
            pl.BlockSpec((1, D_POOL), lambda i: (0, 0)),
            pl.BlockSpec((1, D_SGU), lambda i: (0, 0)),
            pl.BlockSpec((N_SGU_HEADS, SGU_BLOCK, SGU_BLOCK), lambda i: (0, 0, 0)),
            pl.BlockSpec((SGU_BLOCK, D_SGU), lambda i: (0, 0)),
        ],
        out_specs=[
            pl.BlockSpec((ts, D_MODEL), lambda i: (i, 0)),
            pl.BlockSpec((ts, D_POOL), lambda i: (i, 0)),
            pl.BlockSpec((ts, D_SGU), lambda i: (i, 0)),
        ],
        out_shape=[_sds((S, D_MODEL), BF16), _sds((S, D_POOL), BF16), _sds((S, D_SGU), BF16)],
        scratch_shapes=[pltpu.VMEM((ts + POOL_HALO, D_POOL), F32)],
        compiler_params=_cparams(("parallel",)),
    )(proj, proj, pool_w, pool_scale, sgu_g, ws, bias_full)


def _mixer_bwd(name, dymix, proj, p, vn, pool_w, pool_scale, sgu_g, ws, bias_full, ts):
    S = proj.shape[0]
    nsteps = S // ts
    nblk = ts // SGU_BLOCK
    halo_blocks = ts // POOL_HALO

    def body(dy_ref, dyh_ref, u_ref, v_ref, p_ref, vn_ref, pw_ref, sc_ref, g_ref, ws_ref, b_ref,
             dproj_ref, dpw_ref, dsc_ref, dg_ref, dws_ref, db_ref,
             ext_ref, dvn_ref, acc_sc, acc_g, acc_b):
        i = pl.program_id(0)

        @pl.when(i == 0)
        def _():
            dpw_ref[...] = jnp.zeros_like(dpw_ref)
            dws_ref[...] = jnp.zeros_like(dws_ref)
            acc_sc[...] = jnp.zeros_like(acc_sc)
            acc_g[...] = jnp.zeros_like(acc_g)
            acc_b[...] = jnp.zeros_like(acc_b)

        pos = _pool_denominators(i * ts, ts)
        pos_h = _pool_denominators((i + 1) * ts, POOL_HALO)
        for gi, w in enumerate(POOL_WINDOWS):
            cs = slice(gi * POOL_GROUP, (gi + 1) * POOL_GROUP)
            pg = p_ref[:, cs]
            wg = pw_ref[gi]
            dyp = dy_ref[:, cs]
            ypre = jnp.dot(pg, wg, preferred_element_type=F32)
            acc_sc[:, cs] += _rowsum8(dyp * ypre)
            dz = (dyp * sc_ref[:, cs]).astype(BF16)
            dpw_ref[gi] += lax.dot_general(pg, dz, _DN["tn"], preferred_element_type=F32)
            dp = lax.dot_general(dz, wg, _DN["nt"], preferred_element_type=F32)
            dzh = (dyh_ref[:, cs] * sc_ref[:, cs]).astype(BF16)
            dph = lax.dot_general(dzh, wg, _DN["nt"], preferred_element_type=F32)
            ext_ref[0:ts, cs] = dp / jnp.minimum(pos, float(w))
            ext_ref[ts:ts + POOL_HALO, cs] = jnp.where(i < nsteps - 1, dph / jnp.minimum(pos_h, float(w)), 0.0)
            acc = ext_ref[0:ts, cs]
            for j in range(1, w):
                acc = acc + ext_ref[j:j + ts, cs]
            dproj_ref[:, cs] = (acc - dp).astype(dproj_ref.dtype)

        for n in range(nblk):
            rs = slice(n * SGU_BLOCK, (n + 1) * SGU_BLOCK)
            for h in range(N_SGU_HEADS):
                cs = slice(h * SGU_BLOCK, (h + 1) * SGU_BLOCK)
                vnb = vn_ref[rs, cs]
                wh = ws_ref[h]
                mixed = jnp.dot(wh, vnb, preferred_element_type=F32) + b_ref[:, cs]
                dys = dy_ref[rs, D_POOL + h * SGU_BLOCK:D_POOL + (h + 1) * SGU_BLOCK]
                dproj_ref[rs, D_POOL + h * SGU_BLOCK:D_POOL + (h + 1) * SGU_BLOCK] = (dys * mixed).astype(dproj_ref.dtype)
                dmix = dys * u_ref[rs, cs]
                acc_b[:, cs] += dmix
                dmb = dmix.astype(BF16)
                dws_ref[h] += lax.dot_general(dmb, vnb, _DN["nt"], preferred_element_type=F32)
                dvn_ref[rs, cs] = lax.dot_general(wh, dmb, _DN["tn"], preferred_element_type=F32)
        v = v_ref[...]
        r = lax.rsqrt(jnp.mean(v * v, axis=-1, keepdims=True) + EPS)
        vh = v * r
        dvn = dvn_ref[...]
        acc_g[...] += _rowsum8(dvn * vh)
        dxh = dvn * g_ref[...]
        m = jnp.mean(dxh * vh, axis=-1, keepdims=True)
        dproj_ref[:, D_POOL + D_SGU:D_POOL + 2 * D_SGU] = (r * (dxh - vh * m)).astype(dproj_ref.dtype)

        @pl.when(i == nsteps - 1)
        def _():
            dsc_ref[...] = jnp.sum(acc_sc[...], axis=0, keepdims=True)
            dg_ref[...] = jnp.sum(acc_g[...], axis=0, keepdims=True)
            t_idx = lax.broadcasted_iota(jnp.int32, (SGU_BLOCK, SGU_BLOCK), 0) // SGU_CHUNK
            s_idx = lax.broadcasted_iota(jnp.int32, (SGU_BLOCK, SGU_BLOCK), 1) // SGU_CHUNK
            mask = s_idx <= t_idx
            for h in range(N_SGU_HEADS):
                cs = slice(h * SGU_BLOCK, (h + 1) * SGU_BLOCK)
                dws_ref[h] = jnp.where(mask, dws_ref[h], 0.0)
                col = jnp.sum(acc_b[:, cs], axis=1, keepdims=True)
                db_ref[h] = jnp.broadcast_to(col, (SGU_BLOCK, SGU_BLOCK))

    const2 = lambda i: (0, 0)
    const3 = lambda i: (0, 0, 0)
    last_halo = S // POOL_HALO - 1
    return _tc_call(
        body, name=name, grid=(nsteps,),
        in_specs=[
            pl.BlockSpec((ts, D_MODEL), lambda i: (i, 0)),
            pl.BlockSpec((POOL_HALO, D_POOL), lambda i: (jnp.minimum((i + 1) * halo_blocks, last_halo), 0)),
            pl.BlockSpec((ts, D_SGU), lambda i: (i, 1)),
            pl.BlockSpec((ts, D_SGU), lambda i: (i, 2)),
            pl.BlockSpec((ts, D_POOL), lambda i: (i, 0)),
            pl.BlockSpec((ts, D_SGU), lambda i: (i, 0)),
            pl.BlockSpec((4, POOL_GROUP, POOL_GROUP), const3),
            pl.BlockSpec((1, D_POOL), const2),
            pl.BlockSpec((1, D_SGU), const2),
            pl.BlockSpec((N_SGU_HEADS, SGU_BLOCK, SGU_BLOCK), const3),
            pl.BlockSpec((SGU_BLOCK, D_SGU), const2),
        ],
        out_specs=[
            pl.BlockSpec((ts, D_POOL + 2 * D_SGU), lambda i: (i, 0)),
            pl.BlockSpec((4, POOL_GROUP, POOL_GROUP), const3),
            pl.BlockSpec((1, D_POOL), const2),
            pl.BlockSpec((1, D_SGU), const2),
            pl.BlockSpec((N_SGU_HEADS, SGU_BLOCK, SGU_BLOCK), const3),
            pl.BlockSpec((N_SGU_HEADS, SGU_BLOCK, SGU_BLOCK), const3),
        ],
        out_shape=[
            _sds((S, D_POOL + 2 * D_SGU), BF16),
            _sds((4, POOL_GROUP, POOL_GROUP), F32),
            _sds((1, D_POOL), F32),
            _sds((1, D_SGU), F32),
            _sds((N_SGU_HEADS, SGU_BLOCK, SGU_BLOCK), F32),
            _sds((N_SGU_HEADS, SGU_BLOCK, SGU_BLOCK), F32),
        ],
        scratch_shapes=[
            pltpu.VMEM((ts + POOL_HALO, D_POOL), F32),
            pltpu.VMEM((ts, D_SGU), F32),
            pltpu.VMEM((8, D_POOL), F32),
            pltpu.VMEM((8, D_SGU), F32),
            pltpu.VMEM((SGU_BLOCK, D_SGU), F32),
        ],
        compiler_params=_cparams(("arbitrary",)),
    )(dymix, dymix, proj, proj, p, vn, pool_w, pool_scale, sgu_g, ws, bias_full)


def _silu_mul(accs, extras):
    (up,) = accs
    gt = extras[0]
    sig = 1.0 / (1.0 + jnp.exp(-gt))
    return gt, up, gt * sig * up


def _silu_mul_bwd(accs, extras):
    (dact,) = accs
    gt = extras[0].astype(F32)
    up = extras[1].astype(F32)
    sig = 1.0 / (1.0 + jnp.exp(-gt))
    silu = gt * sig
    dgt = dact * up * (sig * (1.0 + gt * (1.0 - sig)))
    dup = dact * silu
    return dgt, dup


def _local_step(x, mem, target, W, sm, rs):
    S = x.shape[0]
    tm = min(1024, S)
    th = min(512, S)
    tq = min(256, S)
    ts = min(512, S)
    tr = min(512, S)
    tk_s = min(2048, S)
    tk_p = min(4096, S)
    M = mem.shape[0]

    h1 = _rms_fwd("rms_mix", x, sm["norm_mix_g"], tr)
    proj = _mm1("proj_in", h1, W["w_in_t"], "nt", tm, 1024, 2048, F32)
    ymix, p, vn = _mixer_fwd("mixer_fwd", proj, W["pool_w"], sm["pool_scale"], sm["sgu_norm_g"],
                             sm["ws_masked"], sm["bias_full"], ts)
    x1, h2 = _mm_rows("proj_out", [ymix], [W["w_out"]], [(0, 0, 0)], "nn", th, 2048, [x], [sm["norm_xattn_g"]],
                      [F32, BF16], 0, _ep_residual_norm)

    mb = _rms_fwd("rms_mem", mem, sm["norm_mem_g"], M)
    q = _mm1("proj_q", h2, W["w_q"], "nn", tm, 1024, 2048, BF16)
    kk, vv = _mm("proj_kv", [mb], [W["w_k"], W["w_v"]], [(0, 0, 0), (0, 1, 1)], "nn", M, 1024, 2048, [BF16, BF16],
                 n_acc=2)
    o = _attn_fwd("attn_fwd", q, kk, vv, tm)
    x2, h3 = _mm_rows("proj_o", [o], [W["w_o"]], [(0, 0, 0)], "nn", th, 2048, [x1], [sm["norm_ffn_g"]],
                      [F32, BF16], 0, _ep_residual_norm)

    gt32 = _mm1("ffn_gate", h3, W["w_gate_t"], "nt", tm, 1408, 2048, F32)
    gt, up, act = _mm("ffn_up", [h3], [W["w_up_t"]], [(0, 0, 0)], "nt", tm, 1408, 2048, [BF16, BF16, BF16],
                      epilogue=_silu_mul, extras=(gt32,))
    dx3, dx3b, d_final_g, loss = _mm_rows(
        "ffn_down", [act], [W["w_down"]], [(0, 0, 0)], "nn", tq, 5632, [x2, target], [sm["final_norm_g"]],
        [F32, BF16], 2, _ep_final_loss, n_scalar_out=1)

    dgt, dup = _mm("ffn_down_dgrad", [dx3b], [W["w_down"]], [(0, 0, 0)], "nt", tm, 1408, 2048, [BF16, BF16],
                   epilogue=_silu_mul_bwd, extras=(gt, up))
    rs.push("w_down", _mm1("ffn_down_wgrad", act, dx3b, "tn", 1408, 1024, tk_s, BF16))
    rs.push("w_gate_t", _mm1("ffn_gate_wgrad", dgt, h3, "tn", 1408, 1024, tk_s, BF16))
    rs.reduce("w_down")
    rs.push("w_up_t", _mm1("ffn_up_wgrad", dup, h3, "tn", 1408, 1024, tk_s, BF16))
    rs.reduce("w_gate_t")
    dh3 = _mm("ffn_gate_up_dgrad", [dgt, dup], [W["w_gate_t"], W["w_up_t"]], [(0, 0, 0), (1, 1, 0)], "nn",
              th, 512, 5632, [F32])[0]
    rs.reduce("w_up_t")
    dx2, dx2b, d_ffn_g = _rms_bwd("rms_ffn_bwd", dh3, x2, sm["norm_ffn_g"], dx3, tr)
    rs.finish("w_down")

    rs.push("w_o", _mm1("proj_o_wgrad", o, dx2b, "tn", 1024, 1024, tk_p, BF16))
    rs.finish("w_gate_t")
    do = _mm1("proj_o_dgrad", dx2b, W["w_o"], "nt", tm, 1024, 2048, BF16)
    rs.reduce("w_o")
    dq, dk, dv = _attn_bwd("attn_bwd", q, kk, vv, do, tm)
    rs.push("w_q", _mm1("proj_q_wgrad", h2, dq, "tn", 1024, 1024, tk_p, BF16))
    rs.push("w_k", _mm1("proj_k_wgrad", mb, dk, "tn", 1024, 1024, M, BF16))
    rs.push("w_v", _mm1("proj_v_wgrad", mb, dv, "tn", 1024, 1024, M, BF16))
    rs.finish("w_up_t")
    dx1, dx1b, d_xattn_g = _mm_rows(
        "proj_q_dgrad", [dq], [W["w_q"]], [(0, 0, 0)], "nt", tq, 2048, [x1, dx2], [sm["norm_xattn_g"]],
        [F32, BF16], 1, _ep_norm_bwd)
    rs.reduce("w_q")
    rs.reduce("w_k")
    rs.reduce("w_v")
    dmb = _mm("proj_kv_dgrad", [dk, dv], [W["w_k"], W["w_v"]], [(0, 0, 0), (1, 1, 0)], "nt",
              M, 1024, 2048, [F32])[0]
    (d_mem_g,) = _rms_bwd("rms_mem_bwd", dmb, mem, sm["norm_mem_g"], None, M, want_dx=False)

    rs.push("w_out", _mm1("proj_out_wgrad", ymix, dx1b, "tn", 1024, 1024, tk_p, BF16))
    dymix = _mm1("proj_out_dgrad", dx1b, W["w_out"], "nt", tm, 1024, 2048, F32)
    rs.finish("w_o")
    dproj, d_pool_w, d_pool_scale, d_sgu_g, d_ws, d_b = _mixer_bwd(
        "mixer_bwd", dymix, proj, p, vn, W["pool_w"], sm["pool_scale"], sm["sgu_norm_g"],
        sm["ws_masked"], sm["bias_full"], ts)
    rs.finish("w_q")
    rs.finish("w_k")
    rs.finish("w_v")
    rs.reduce("w_out")
    rs.push("pool_w", d_pool_w.reshape(4, N_DEV, POOL_GROUP // N_DEV, POOL_GROUP).transpose(1, 0, 2, 3)
            .reshape(4 * POOL_GROUP, POOL_GROUP).astype(BF16))
    rs.small("early", dict(
        pool_scale=d_pool_scale, sgu_norm_g=d_sgu_g, w_spatial=d_ws, b_spatial=d_b[:, :, 0],
        norm_xattn_g=d_xattn_g, norm_mem_g=d_mem_g, norm_ffn_g=d_ffn_g, final_norm_g=d_final_g))
    rs.push("w_in_t", _mm1("proj_in_wgrad", dproj, h1, "tn", 1024, 1024, tk_p, BF16))
    rs.finish("w_out")
    rs.reduce("pool_w")
    rs.reduce("w_in_t")
    grad_x, d_mix_g = _mm_rows(
        "proj_in_dgrad", [dproj], [W["w_in_t"]], [(0, 0, 0)], "nn", tq, 3072, [x, dx1], [sm["norm_mix_g"]],
        [F32], 1, _ep_norm_bwd)
    rs.small("late", dict(norm_mix_g=d_mix_g, loss=jnp.pad(loss, ((0, 0), (0, _LANES - 1)))))
    rs.finish_small("early")
    rs.finish("pool_w")
    rs.finish("w_in_t")
    rs.finish_small("late")
    return loss, grad_x


def _mesh_pos():
    return lax.axis_index("x"), lax.axis_index("y"), lax.axis_index("c")


def _handshake(peers):
    barrier = pltpu.get_barrier_semaphore()
    for peer in peers:
        pl.semaphore_signal(barrier, inc=1, device_id=peer, device_id_type=MESH)
    pl.semaphore_wait(barrier, len(peers))


def _seq_all_gather(name, shards, collective_id):
    n = len(shards)

    def body(*refs):
        ins = refs[:n]
        outs = refs[n:2 * n]
        send_sems, recv_sems, local_sems = refs[2 * n:]
        x, y, c = _mesh_pos()
        me, sibling = (x, y, c), (x, y, 1 - c)
        xn, yn, dg = (1 - x, y), (x, 1 - y), (1 - x, 1 - y)
        north = c == 1
        via = (jnp.where(north, xn[0], yn[0]), jnp.where(north, xn[1], yn[1]))
        to = (jnp.where(north, yn[0], xn[0]), jnp.where(north, yn[1], xn[1]))
        _handshake([sibling, (*xn, c), (*yn, c)])

        def copy(a, k, block, target, src=None):
            bx, by, bc = block
            dst = outs[a].at[4 * bx + 2 * by + bc]
            return pltpu.make_async_remote_copy(
                src_ref=dst if src is None else src, dst_ref=dst,
                send_sem=send_sems.at[a, k], recv_sem=recv_sems.at[a, k],
                device_id=target, device_id_type=MESH)

        mine = [pltpu.make_async_copy(ins[a], outs[a].at[4 * x + 2 * y + c], local_sems.at[a]) for a in range(n)]
        for cp in mine:
            cp.start()
        started = []
        for a in range(n):
            first = [copy(a, 0, me, sibling, src=ins[a]), copy(a, 1, me, (*xn, c), src=ins[a]),
                     copy(a, 2, me, (*yn, c), src=ins[a])]
            for cp in first:
                cp.start()
            started += first
        for a in range(n):
            copy(a, 1, (*xn, c), me).wait_recv()
            copy(a, 2, (*yn, c), me).wait_recv()
            second = [copy(a, 3, (*via, c), (*to, c)), copy(a, 4, (*xn, c), sibling), copy(a, 5, (*yn, c), sibling)]
            for cp in second:
                cp.start()
            started += second
        for a in range(n):
            copy(a, 3, (*dg, c), me).wait_recv()
            last = copy(a, 6, (*dg, c), sibling)
            last.start()
            started.append(last)
        for a in range(n):
            copy(a, 0, sibling, me).wait_recv()
            for k, chip in ((4, xn), (5, yn), (6, dg)):
                copy(a, k, (*chip, 1 - c), me).wait_recv()
        for cp in started:
            cp.wait_send()
        for cp in mine:
            cp.wait()

    return _sc_call(
        body, name=name,
        out_type=[_sds((N_DEV,) + s.shape, s.dtype) for s in shards],
        scratch_types=[pltpu.SemaphoreType.DMA((n, 7)), pltpu.SemaphoreType.DMA((n, 7)),
                       pltpu.SemaphoreType.DMA((n,))],
        compiler_params=pltpu.CompilerParams(collective_id=collective_id),
    )(*shards)


def _seq_pair_exchange(name, gview, collective_id):
    def body(g_ref, theirs_ref, send_sems, recv_sems):
        x, y, c = _mesh_pos()
        sibling = (x, y, 1 - c)
        _handshake([sibling])
        copies = [pltpu.make_async_remote_copy(
            src_ref=g_ref.at[k, 1 - c], dst_ref=theirs_ref.at[k],
            send_sem=send_sems.at[k], recv_sem=recv_sems.at[k],
            device_id=sibling, device_id_type=MESH) for k in range(4)]
        for cp in copies:
            cp.start()
        for cp in copies:
            cp.wait()

    return _sc_call(
        body, name=name, out_type=_sds((4,) + gview.shape[2:], gview.dtype),
        scratch_types=[pltpu.SemaphoreType.DMA((4,)), pltpu.SemaphoreType.DMA((4,))],
        compiler_params=pltpu.CompilerParams(collective_id=collective_id),
    )(gview)


_RING = 3


def _pair_sum(name, gview, theirs, pos, tr):
    _, _, r, C = gview.shape
    nt = r // tr
    n = 4 * nt

    def body(pos_ref, g_ref, t_ref, o_ref, a_buf, b_buf, o_buf, in_sems, out_sems):
        c = pos_ref[0]

        def reads(s):
            k, rows = s // nt, pl.ds((s % nt) * tr, tr)
            slot = s % _RING
            return (pltpu.make_async_copy(g_ref.at[k, c, rows], a_buf.at[slot], in_sems.at[0, slot]),
                    pltpu.make_async_copy(t_ref.at[k, rows], b_buf.at[slot], in_sems.at[1, slot]))

        def write(s):
            k, rows = s // nt, pl.ds((s % nt) * tr, tr)
            return pltpu.make_async_copy(o_buf.at[s % 2], o_ref.at[k, rows], out_sems.at[s % 2])

        for s in range(min(_RING - 1, n)):
            for cp in reads(s):
                cp.start()
        for s in range(n):
            if s + _RING - 1 < n:
                for cp in reads(s + _RING - 1):
                    cp.start()
            for cp in reads(s):
                cp.wait()
            if s >= 2:
                write(s - 2).wait()
            slot = s % _RING
            o_buf[s % 2] = (a_buf[slot].astype(F32) + b_buf[slot].astype(F32)).astype(o_buf.dtype)
            write(s).start()
        for s in range(max(n - 2, 0), n):
            write(s).wait()

    grid_spec = pltpu.PrefetchScalarGridSpec(
        num_scalar_prefetch=1, grid=(1,), in_specs=[_ANY, _ANY], out_specs=_ANY,
        scratch_shapes=[pltpu.VMEM((_RING, tr, C), gview.dtype), pltpu.VMEM((_RING, tr, C), theirs.dtype),
                        pltpu.VMEM((2, tr, C), theirs.dtype),
                        pltpu.SemaphoreType.DMA((2, _RING)), pltpu.SemaphoreType.DMA((2,))])
    return _tc_call(
        body, name=name, grid_spec=grid_spec, out_shape=_sds(theirs.shape, theirs.dtype),
        compiler_params=_cparams(("arbitrary",)),
    )(pos, gview, theirs)


def _seq_chip_exchange(name, pair, collective_id):
    def body(p_ref, land_ref, send_sems, recv_sems):
        x, y, c = _mesh_pos()
        my_chip = 2 * x + y
        chips = [(1 - x, y), (x, 1 - y), (1 - x, 1 - y)]
        _handshake([(cx, cy, c) for cx, cy in chips])
        copies = [pltpu.make_async_remote_copy(
            src_ref=p_ref.at[2 * cx + cy], dst_ref=land_ref.at[my_chip],
            send_sem=send_sems.at[j], recv_sem=recv_sems.at[j],
            device_id=(cx, cy, c), device_id_type=MESH) for j, (cx, cy) in enumerate(chips)]
        for cp in copies:
            cp.start()
        for cp in copies:
            cp.wait_send()
        for j, (cx, cy) in enumerate(chips):
            pltpu.make_async_remote_copy(
                src_ref=p_ref.at[my_chip], dst_ref=land_ref.at[2 * cx + cy],
                send_sem=send_sems.at[j], recv_sem=recv_sems.at[j],
                device_id=(cx, cy, c), device_id_type=MESH).wait_recv()

    return _sc_call(
        body, name=name, out_type=_sds(pair.shape, pair.dtype),
        scratch_types=[pltpu.SemaphoreType.DMA((3,)), pltpu.SemaphoreType.DMA((3,))],
        compiler_params=pltpu.CompilerParams(collective_id=collective_id),
    )(pair)


def _sum_leading(name, parts, tr, out_dtype=F32):
    n, r, C = parts.shape

    def body(p_ref, o_ref):
        acc = p_ref[0].astype(F32)
        for k in range(1, n):
            acc = acc + p_ref[k].astype(F32)
        o_ref[...] = acc.astype(o_ref.dtype)

    return _tc_call(
        body, name=name, grid=(r // tr,),
        in_specs=[pl.BlockSpec((n, tr, C), lambda t: (0, t, 0))],
        out_specs=pl.BlockSpec((tr, C), lambda t: (t, 0)),
        out_shape=_sds((r, C), out_dtype), compiler_params=_cparams(("parallel",)),
    )(parts)


def _row_tile(r):
    for t in (512, 384, 352, 256, 128, 64, 32, 16, 8):
        if r % t == 0:
            return t
    return r


def _adamw_math(w, g, m, v):
    c1 = 1.0 - ADAM_B1 ** ADAM_STEP
    c2 = 1.0 - ADAM_B2 ** ADAM_STEP
    nm = ADAM_B1 * m + (1.0 - ADAM_B1) * g
    nv = ADAM_B2 * v + (1.0 - ADAM_B2) * (g * g)
    m_hat = nm / c1
    v_hat = nv / c2
    return -ADAM_LR * (m_hat / (jnp.sqrt(v_hat) + ADAM_EPS) + ADAM_WD * w), nm, nv


def _chip_sum_adamw(name, pair, landed, pos, w, m, v, transposed):
    _, r, C = pair.shape
    if transposed:
        tr, tc = r, 512
        r_pad = -r % _LANES
        wspec = pl.BlockSpec((tc, r), lambda t, k, pos_ref: (t, 0))
        shape = (C, r)
        scratch = [pltpu.VMEM((tr, tc), F32), pltpu.VMEM((tc, r + r_pad), F32)]
    else:
        tr, tc = _row_tile(r), C
        wspec = pl.BlockSpec((tr, C), lambda t, k, pos_ref: (t, 0))
        shape = (r, C)
        scratch = [pltpu.VMEM((tr, tc), F32)]
    n_t = (C // tc) if transposed else (r // tr)

    def block(chip, t):
        return (chip, 0, t) if transposed else (chip, t, 0)

    def body(pos_ref, own_ref, land_ref, w_ref, m_ref, v_ref, g_ref, d_ref, nm_ref, nv_ref, acc_ref, *turn):
        k = pl.program_id(1)
        val = jnp.where(k == pos_ref[1], own_ref[...], land_ref[...]).astype(F32)

        @pl.when(k == 0)
        def _():
            acc_ref[...] = val

        @pl.when(k > 0)
        def _():
            acc_ref[...] += val

        @pl.when(k == 3)
        def _():
            if transposed:
                g_t = acc_ref[...]
                if r_pad:
                    g_t = jnp.concatenate([g_t, jnp.zeros((r_pad, tc), F32)], axis=0)
                turn[0][...] = g_t.T
                g = turn[0][:, 0:r]
            else:
                g = acc_ref[...]
            d, nm, nv = _adamw_math(w_ref[...], g, m_ref[...], v_ref[...])
            g_ref[...] = g
            d_ref[...] = d
            nm_ref[...] = nm
            nv_ref[...] = nv

    def land_index(t, k, pos_ref):
        return block(jnp.where(k == pos_ref[1], (k + 1) % 4, k), t)

    grid_spec = pltpu.PrefetchScalarGridSpec(
        num_scalar_prefetch=1, grid=(n_t, 4),
        in_specs=[pl.BlockSpec((None, tr, tc), lambda t, k, pos_ref: block(pos_ref[1], t)),
                  pl.BlockSpec((None, tr, tc), land_index), wspec, wspec, wspec],
        out_specs=[wspec] * 4, scratch_shapes=scratch)
    return _tc_call(
        body, name=name, grid_spec=grid_spec, out_shape=[_sds(shape, F32)] * 4,
        compiler_params=_cparams(("parallel", "arbitrary")),
    )(pos, pair, landed, w, m, v)


def _adamw(name, w, g, m, v):
    R, C = w.shape
    tr = _row_tile(R)

    def body(w_ref, g_ref, m_ref, v_ref, d_ref, nm_ref, nv_ref):
        d_ref[...], nm_ref[...], nv_ref[...] = _adamw_math(w_ref[...], g_ref[...], m_ref[...], v_ref[...])

    spec = pl.BlockSpec((tr, C), lambda i: (i, 0))
    return _tc_call(
        body, name=name, grid=(R // tr,), in_specs=[spec] * 4, out_specs=[spec] * 3,
        out_shape=[_sds((R, C), F32)] * 3, compiler_params=_cparams(("parallel",)),
    )(w, g, m, v)


_BIG = ("w_in_t", "w_out", "w_q", "w_k", "w_v", "w_o", "w_gate_t", "w_up_t", "w_down")
_SMALL = ("norm_mix_g", "pool_scale", "sgu_norm_g", "w_spatial", "b_spatial", "norm_xattn_g",
          "norm_mem_g", "norm_ffn_g", "final_norm_g")
_LANES = 128
_GATHER_GROUPS = (("w_in_t", "pool_w"), ("w_out",), ("w_q",), ("w_k", "w_v"), ("w_o",), ("w_gate_t",),
                  ("w_up_t",), ("w_down",))
_SMALL_GROUPS = dict(
    early=("pool_scale", "sgu_norm_g", "w_spatial", "b_spatial", "norm_xattn_g", "norm_mem_g",
           "norm_ffn_g", "final_norm_g"),
    late=("norm_mix_g", "loss"))
_TURN_OUTSIDE = ("w_gate_t", "w_up_t")
_ID_GATHER, _ID_PAIR, _ID_CHIP = 0, 1, 2


_PACK_ROWS = 512


def _pack(parts):
    rows = [p.reshape(-1, _LANES) for p in parts]
    n = sum(r.shape[0] for r in rows)
    pad = -n % (_PACK_ROWS if n > _PACK_ROWS else 8)
    if pad:
        rows.append(jnp.zeros((pad, _LANES), rows[0].dtype))
    return jnp.concatenate(rows, axis=0)


class _GradReducer:
    def __init__(self, pos, apply, apply_small):
        self.pos, self.apply, self.apply_small = pos, apply, apply_small
        self.view, self.theirs, self.pair, self.landed = {}, {}, {}, {}
        self.small_gathered = {}

    def push(self, k, g):
        r = g.shape[0] // N_DEV
        self.view[k] = g.reshape(4, 2, r, g.shape[1])
        self.theirs[k] = _seq_pair_exchange("grad_pair_exchange_" + k, self.view[k], _ID_PAIR)

    def reduce(self, k):
        r = self.view[k].shape[2]
        self.pair[k] = _pair_sum("grad_pair_sum_" + k, self.view[k], self.theirs[k], self.pos, _row_tile(r))
        self.landed[k] = _seq_chip_exchange("grad_chip_exchange_" + k, self.pair[k], _ID_CHIP)

    def finish(self, k):
        self.apply(k, self.pair[k], self.landed[k])

    def small(self, tag, parts):
        packed = _pack([parts[k] for k in _SMALL_GROUPS[tag]])
        (self.small_gathered[tag],) = _seq_all_gather("gather_small_grads_" + tag, [packed], _ID_GATHER)

    def finish_small(self, tag):
        allp = self.small_gathered[tag]
        self.apply_small(tag, _sum_leading("sum_small_grads_" + tag, allp, min(_PACK_ROWS, allp.shape[1])))


def _unpack(packed, like):
    out, row = [], 0
    for ref in like:
        rows = ref.size // _LANES
        out.append(packed[row:row + rows].reshape(ref.shape))
        row += rows
    return out


def kernel(x, mem, norm_mix_g, w_in, pool_w, pool_scale, sgu_norm_g, w_spatial, b_spatial, w_out, norm_xattn_g, norm_mem_g, w_q, w_k, w_v, w_o, norm_ffn_g, w_gate, w_up, w_down, final_norm_g, loss_target, m_norm_mix_g, m_w_in, m_pool_w, m_pool_scale, m_sgu_norm_g, m_w_spatial, m_b_spatial, m_w_out, m_norm_xattn_g, m_norm_mem_g, m_w_q, m_w_k, m_w_v, m_w_o, m_norm_ffn_g, m_w_gate, m_w_up, m_w_down, m_final_norm_g, v_norm_mix_g, v_w_in, v_pool_w, v_pool_scale, v_sgu_norm_g, v_w_spatial, v_b_spatial, v_w_out, v_norm_xattn_g, v_norm_mem_g, v_w_q, v_w_k, v_w_v, v_w_o, v_norm_ffn_g, v_w_gate, v_w_up, v_w_down, v_final_norm_g):
    args = dict(locals())
    names = ("norm_mix_g", "w_in", "pool_w", "pool_scale", "sgu_norm_g", "w_spatial", "b_spatial", "w_out",
             "norm_xattn_g", "norm_mem_g", "w_q", "w_k", "w_v", "w_o", "norm_ffn_g", "w_gate", "w_up",
             "w_down", "final_norm_g")
    w = {k: args[k] for k in names}
    m = {k: args["m_" + k] for k in names}
    v = {k: args["v_" + k] for k in names}
    _CHAIN.__init__()

    shards = dict(
        w_in_t=w["w_in"][0].T, w_out=w["w_out"][0], w_q=w["w_q"][0], w_k=w["w_k"][0], w_v=w["w_v"][0],
        w_o=w["w_o"][0], w_gate_t=w["w_gate"][0].T, w_up_t=w["w_up"][0].T, w_down=w["w_down"][0])
    send = {k: shards[k].astype(BF16) for k in _BIG}
    send["pool_w"] = w["pool_w"][0].reshape(4 * 32, POOL_GROUP).astype(BF16)
    W = {}
    for gi, group in enumerate(_GATHER_GROUPS):
        gathered = _seq_all_gather("gather_weights_%d" % gi, [send[k] for k in group], _ID_GATHER)
        for k, g in zip(group, gathered):
            W[k] = g.reshape(-1, g.shape[-1])
    W["pool_w"] = W["pool_w"].reshape(N_DEV, 4, 32, POOL_GROUP).transpose(1, 0, 2, 3).reshape(4, POOL_GROUP, POOL_GROUP)

    t = jnp.arange(SGU_BLOCK)
    mask = (t[None, :] // SGU_CHUNK) <= (t[:, None] // SGU_CHUNK)
    sm = dict(
        norm_mix_g=w["norm_mix_g"], pool_scale=w["pool_scale"], sgu_norm_g=w["sgu_norm_g"],
        norm_xattn_g=w["norm_xattn_g"], norm_mem_g=w["norm_mem_g"], norm_ffn_g=w["norm_ffn_g"],
        final_norm_g=w["final_norm_g"].reshape(1, D_MODEL),
        ws_masked=jnp.where(mask[None], w["w_spatial"][0], 0.0).astype(BF16),
        bias_full=jnp.repeat(w["b_spatial"][0].T, SGU_BLOCK, axis=1))

    natural = dict(w_in_t="w_in", w_gate_t="w_gate", w_up_t="w_up")
    grads, delta, new_m, new_v = {}, {}, {}, {}

    def apply(k, pair, landed):
        name = natural.get(k, k)
        if k == "pool_w":
            flat = (4 * POOL_GROUP // N_DEV, POOL_GROUP)
            res = _chip_sum_adamw("grad_finish_" + k, pair, landed, pos, w[k].reshape(flat), m[k].reshape(flat),
                                  v[k].reshape(flat), False)
            grads[k], delta[k], new_m[k], new_v[k] = (a.reshape(w[k].shape) for a in res)
            return
        if k in _TURN_OUTSIDE:
            res = _chip_sum_adamw("grad_finish_" + k, pair, landed, pos, w[name][0].T, m[name][0].T, v[name][0].T,
                                  False)
            res = [a.T for a in res]
        else:
            res = _chip_sum_adamw("grad_finish_" + k, pair, landed, pos, w[name][0], m[name][0], v[name][0],
                                  k in natural)
        grads[name], delta[name], new_m[name], new_v[name] = (a[None] for a in res)

    like = dict(w)
    like["loss"] = _sds((1, _LANES), F32)

    def apply_small(tag, total):
        group = _SMALL_GROUPS[tag]
        grads.update(zip(group, _unpack(total, [like[k] for k in group])))
        if tag == "late":
            d_, m_, v_ = _adamw("adamw_small", _pack([w[k] for k in _SMALL]), _pack([grads[k] for k in _SMALL]),
                                _pack([m[k] for k in _SMALL]), _pack([v[k] for k in _SMALL]))
            shapes = [w[k] for k in _SMALL]
            for k, a, b, c_ in zip(_SMALL, _unpack(d_, shapes), _unpack(m_, shapes), _unpack(v_, shapes)):
                delta[k], new_m[k], new_v[k] = a, b, c_

    pos = jnp.stack([lax.axis_index("c"), 2 * lax.axis_index("x") + lax.axis_index("y")]).astype(jnp.int32)
    rs = _GradReducer(pos, apply, apply_small)
    _, grad_x = _local_step(x[0], mem[0], loss_target[0], W, sm, rs)

    outs = [grads["loss"][0, 0], grad_x[None]]
    outs += [grads[k].reshape(w[k].shape) for k in names]
    outs += [delta[k] for k in names]
    outs += [new_m[k] for k in names]
    outs += [new_v[k] for k in names]
    return tuple(outs)
```

```python
import jax
import jax.numpy as jnp
from jax import lax
from jax.experimental import pallas as pl
from jax.experimental.pallas import tpu as pltpu
from jax.experimental.pallas import tpu_sc as plsc

F32 = jnp.float32
BF16 = jnp.bfloat16
MESH = pl.DeviceIdType.MESH

EPS = 1e-6
D_MODEL = 2048
D_POOL = 1024
D_SGU = 1024
POOL_WINDOWS = (2, 4, 8, 16)
POOL_GROUP = 256
POOL_HALO = 16
SGU_BLOCK = 128
SGU_CHUNK = 64
N_SGU_HEADS = 8
N_HEADS = 4
HEAD_DIM = 512
N_DEV = 8

ADAM_LR = 0.001
ADAM_B1 = 0.9
ADAM_B2 = 0.999
ADAM_EPS = 1e-08
ADAM_WD = 0.01
ADAM_STEP = 10

VMEM_LIMIT = 56 * 1024 * 1024


def _cparams(sem=None):
    return pltpu.CompilerParams(dimension_semantics=sem, vmem_limit_bytes=VMEM_LIMIT)


def _sds(shape, dtype):
    return jax.ShapeDtypeStruct(shape, dtype)


_ANY = pl.BlockSpec(memory_space=pl.ANY)


class _Chain:
    def __init__(self):
        self.tc = None


_CHAIN = _Chain()


def _first(out):
    return out[0] if isinstance(out, (list, tuple)) else out


def _tc_call(body, *, in_specs=None, grid_spec=None, **kw):
    def run(*args):
        prev, n = _CHAIN.tc, len(args)
        fn, specs, spec, operands = body, in_specs, grid_spec, args
        if prev is not None:
            def fn(*refs):
                return body(*refs[:n], *refs[n + 1:])
            operands = args + (prev,)
            if grid_spec is None:
                specs = list(in_specs) + [_ANY]
            else:
                spec = pltpu.PrefetchScalarGridSpec(
                    num_scalar_prefetch=grid_spec.num_scalar_prefetch, grid=grid_spec.grid,
                    in_specs=list(grid_spec.in_specs) + [_ANY], out_specs=grid_spec.out_specs,
                    scratch_shapes=grid_spec.scratch_shapes)
        if spec is None:
            out = pl.pallas_call(fn, in_specs=specs, **kw)(*operands)
        else:
            out = pl.pallas_call(fn, grid_spec=spec, **kw)(*operands)
        _CHAIN.tc = _first(out)
        return out
    return run


def _sc_call(body, **kw):
    return pl.kernel(body, mesh=plsc.ScalarSubcoreMesh(axis_name="seq", num_cores=1), **kw)


def _rowsum8(v):
    r, c = v.shape
    return v.reshape(r // 8, 8, c).sum(axis=0)


_DN = {
    "nn": (((1,), (0,)), ((), ())),
    "nt": (((1,), (1,)), ((), ())),
    "tn": (((0,), (0,)), ((), ())),
}


def _mm(name, a_list, b_list, terms, mode, tm, tn, tk, out_dtypes, epilogue=None, extras=(), n_acc=1):
    a0, b0 = a_list[0], b_list[0]
    if mode == "tn":
        K, M = a0.shape
    else:
        M, K = a0.shape
    N = b0.shape[0] if mode == "nt" else b0.shape[1]
    assert M % tm == 0 and N % tn == 0 and K % tk == 0, (name, M, N, K, tm, tn, tk)
    nk = K // tk
    na, nb, ne, no = len(a_list), len(b_list), len(extras), len(out_dtypes)
    dn = _DN[mode]

    if mode == "tn":
        a_spec = pl.BlockSpec((tk, tm), lambda i, j, k: (k, i))
    else:
        a_spec = pl.BlockSpec((tm, tk), lambda i, j, k: (i, k))
    if mode == "nt":
        b_spec = pl.BlockSpec((tn, tk), lambda i, j, k: (j, k))
    else:
        b_spec = pl.BlockSpec((tk, tn), lambda i, j, k: (k, j))
    o_spec = pl.BlockSpec((tm, tn), lambda i, j, k: (i, j))

    def body(*refs):
        a_refs = refs[:na]
        b_refs = refs[na:na + nb]
        e_refs = refs[na + nb:na + nb + ne]
        o_refs = refs[na + nb + ne:na + nb + ne + no]
        acc_refs = refs[na + nb + ne + no:]

        parts = [None] * n_acc
        for ai, bi, ci in terms:
            d = lax.dot_general(a_refs[ai][...].astype(BF16), b_refs[bi][...].astype(BF16), dn,
                                preferred_element_type=F32)
            parts[ci] = d if parts[ci] is None else parts[ci] + d

        def finish(accs):
            outs = epilogue(accs, [e[...] for e in e_refs]) if epilogue is not None else accs
            for o_ref, v in zip(o_refs, outs):
                o_ref[...] = v.astype(o_ref.dtype)

        if nk == 1:
            finish(parts)
        else:
            k = pl.program_id(2)

            @pl.when(k == 0)
            def _():
                for c in range(n_acc):
                    acc_refs[c][...] = parts[c]

            @pl.when(k > 0)
            def _():
                for c in range(n_acc):
                    acc_refs[c][...] += parts[c]

            @pl.when(k == nk - 1)
            def _():
                finish([acc_refs[c][...] for c in range(n_acc)])

    scratch = [pltpu.VMEM((tm, tn), F32) for _ in range(n_acc)] if nk > 1 else []
    return _tc_call(
        body, name=name, grid=(M // tm, N // tn, nk),
        in_specs=[a_spec] * na + [b_spec] * nb + [o_spec] * ne,
        out_specs=[o_spec] * no,
        out_shape=[_sds((M, N), dt) for dt in out_dtypes],
        scratch_shapes=scratch,
        compiler_params=_cparams(("parallel", "parallel", "arbitrary")),
    )(*a_list, *b_list, *extras)


def _mm_rows(name, a_list, b_list, terms, mode, tm, tk, rows, vecs, row_dtypes, n_vec_out, epilogue,
             n_scalar_out=0):
    a0, b0 = a_list[0], b_list[0]
    M, K = a0.shape
    N = b0.shape[0] if mode == "nt" else b0.shape[1]
    assert mode in ("nn", "nt") and M % tm == 0 and K % tk == 0, (name, M, N, K, tm, tk)
    nm, nk = M // tm, K // tk
    slab = min(128, tm)
    na, nb, nr, nv, no = len(a_list), len(b_list), len(rows), len(vecs), len(row_dtypes)
    dn = _DN[mode]
    a_spec = pl.BlockSpec((tm, tk), lambda i, k: (i, k))
    b_mode = dict(pipeline_mode=pl.Buffered(1)) if nk == 1 else {}
    b_spec = (pl.BlockSpec((N, tk), lambda i, k: (0, k), **b_mode) if mode == "nt"
              else pl.BlockSpec((tk, N), lambda i, k: (k, 0), **b_mode))
    row_spec = pl.BlockSpec((tm, N), lambda i, k: (i, 0))
    vec_spec = pl.BlockSpec((1, N), lambda i, k: (0, 0))
    one_spec = pl.BlockSpec((1, 1), lambda i, k: (0, 0))

    def body(*refs):
        pos = 0
        a_refs = refs[pos:pos + na]; pos += na
        b_refs = refs[pos:pos + nb]; pos += nb
        r_refs = refs[pos:pos + nr]; pos += nr
        v_refs = refs[pos:pos + nv]; pos += nv
        o_refs = refs[pos:pos + no]; pos += no
        s_refs = refs[pos:pos + n_vec_out]; pos += n_vec_out
        vacc_refs = refs[pos:pos + n_vec_out]; pos += n_vec_out
        acc_ref = refs[pos] if nk > 1 else None
        i, k = pl.program_id(0), pl.program_id(1)
        part = None
        for ai, bi, _ in terms:
            d = lax.dot_general(a_refs[ai][...].astype(BF16), b_refs[bi][...].astype(BF16), dn,
                                preferred_element_type=F32)
            part = d if part is None else part + d

        def finish(acc):
            vecs_now = [v[...] for v in v_refs]
            vparts = None
            for r0 in range(0, tm, slab):
                rs_ = slice(r0, r0 + slab)
                outs, vp = epilogue(acc[rs_, :], [r[rs_, :] for r in r_refs], vecs_now)
                for o_ref, val in zip(o_refs, outs):
                    o_ref[rs_, :] = val.astype(o_ref.dtype)
                vparts = vp if vparts is None else [a + b for a, b in zip(vparts, vp)]

            @pl.when(i == 0)
            def _():
                for vacc, vp in zip(vacc_refs, vparts):
                    vacc[...] = vp

            @pl.when(i > 0)
            def _():
                for vacc, vp in zip(vacc_refs, vparts):
                    vacc[...] += vp

            @pl.when(i == nm - 1)
            def _():
                for j, (s_ref, vacc) in enumerate(zip(s_refs, vacc_refs)):
                    col = jnp.sum(vacc[...], axis=0, keepdims=True)
                    s_ref[...] = jnp.sum(col, axis=1, keepdims=True) if j >= n_vec_out - n_scalar_out else col

        if nk == 1:
            finish(part)
        else:
            @pl.when(k == 0)
            def _():
                acc_ref[...] = part

            @pl.when(k > 0)
            def _():
                acc_ref[...] += part

            @pl.when(k == nk - 1)
            def _():
                finish(acc_ref)

    n_plain = n_vec_out - n_scalar_out
    return _tc_call(
        body, name=name, grid=(nm, nk),
        in_specs=[a_spec] * na + [b_spec] * nb + [row_spec] * nr + [vec_spec] * nv,
        out_specs=[row_spec] * no + [vec_spec] * n_plain + [one_spec] * n_scalar_out,
        out_shape=[_sds((M, N), dt) for dt in row_dtypes] + [_sds((1, N), F32)] * n_plain
        + [_sds((1, 1), F32)] * n_scalar_out,
        scratch_shapes=[pltpu.VMEM((8, N), F32)] * n_vec_out + ([pltpu.VMEM((tm, N), F32)] if nk > 1 else []),
        compiler_params=_cparams(("arbitrary", "arbitrary")),
    )(*a_list, *b_list, *rows, *vecs)


def _ep_residual_norm(acc, rows, vecs):
    x_new = rows[0] + acc
    r = lax.rsqrt(jnp.mean(x_new * x_new, axis=-1, keepdims=True) + EPS)
    return [x_new, x_new * r * vecs[0]], []


def _ep_norm_bwd(acc, rows, vecs):
    xv, dres = rows
    r = lax.rsqrt(jnp.mean(xv * xv, axis=-1, keepdims=True) + EPS)
    xh = xv * r
    dxh = acc * vecs[0]
    m = jnp.mean(dxh * xh, axis=-1, keepdims=True)
    dx = dres + r * (dxh - xh * m)
    return [dx, dx], [_rowsum8(acc * xh)]


def _ep_final_loss(acc, rows, vecs):
    x2, target = rows
    gv = vecs[0]
    xv = x2 + acc
    inv_d = 1.0 / xv.shape[-1]
    r = lax.rsqrt(jnp.mean(xv * xv, axis=-1, keepdims=True) + EPS)
    xh = xv * r
    e = xh * gv - target
    dy = e * inv_d
    dxh = dy * gv
    m = jnp.mean(dxh * xh, axis=-1, keepdims=True)
    dx = r * (dxh - xh * m)
    return [dx, dx], [_rowsum8(dy * xh), _rowsum8(e * e) * (0.5 * inv_d)]


def _mm1(name, a, b, mode, tm, tn, tk, out_dtype, **kw):
    return _mm(name, [a], [b], [(0, 0, 0)], mode, tm, tn, tk, [out_dtype], **kw)[0]


def _rms_fwd(name, x, g, tr):
    S, Dm = x.shape

    def body(x_ref, g_ref, h_ref):
        xv = x_ref[...]
        r = lax.rsqrt(jnp.mean(xv * xv, axis=-1, keepdims=True) + EPS)
        h_ref[...] = (xv * r * g_ref[...]).astype(h_ref.dtype)

    return _tc_call(
        body, name=name, grid=(S // tr,),
        in_specs=[pl.BlockSpec((tr, Dm), lambda i: (i, 0)), pl.BlockSpec((1, Dm), lambda i: (0, 0))],
        out_specs=pl.BlockSpec((tr, Dm), lambda i: (i, 0)),
        out_shape=_sds((S, Dm), BF16),
        compiler_params=_cparams(("parallel",)),
    )(x, g)


def _rms_bwd(name, dh, x, g, dres, tr, want_dx=True):
    S, Dm = x.shape
    nsteps = S // tr

    def body(*refs):
        if want_dx:
            dh_ref, x_ref, g_ref, dres_ref, dx_ref, dxb_ref, dg_ref, acc_ref = refs
        else:
            dh_ref, x_ref, g_ref, dg_ref, acc_ref = refs
        i = pl.program_id(0)
        xv = x_ref[...]
        r = lax.rsqrt(jnp.mean(xv * xv, axis=-1, keepdims=True) + EPS)
        xh = xv * r
        dhv = dh_ref[...]
        part = _rowsum8(dhv * xh)

        @pl.when(i == 0)
        def _():
            acc_ref[...] = part

        @pl.when(i > 0)
        def _():
            acc_ref[...] += part

        @pl.when(i == nsteps - 1)
        def _():
            dg_ref[...] = jnp.sum(acc_ref[...], axis=0, keepdims=True)

        if want_dx:
            dxh = dhv * g_ref[...]
            m = jnp.mean(dxh * xh, axis=-1, keepdims=True)
            dx = dres_ref[...] + r * (dxh - xh * m)
            dx_ref[...] = dx
            dxb_ref[...] = dx.astype(BF16)

    row = pl.BlockSpec((tr, Dm), lambda i: (i, 0))
    vec = pl.BlockSpec((1, Dm), lambda i: (0, 0))
    if want_dx:
        in_specs = [row, row, vec, row]
        out_specs = [row, row, vec]
        out_shape = [_sds((S, Dm), F32), _sds((S, Dm), BF16), _sds((1, Dm), F32)]
        args = (dh, x, g, dres)
    else:
        in_specs = [row, row, vec]
        out_specs = [vec]
        out_shape = [_sds((1, Dm), F32)]
        args = (dh, x, g)
    return _tc_call(
        body, name=name, grid=(nsteps,), in_specs=in_specs, out_specs=out_specs, out_shape=out_shape,
        scratch_shapes=[pltpu.VMEM((8, Dm), F32)],
        compiler_params=_cparams(("arbitrary",)),
    )(*args)


def _softmax_rows(s):
    e = jnp.exp(s - jnp.max(s, axis=-1, keepdims=True))
    return e / jnp.sum(e, axis=-1, keepdims=True)


def _attn_fwd(name, q, k, v, ts):
    S, Dm = q.shape
    M = k.shape[0]
    scale = HEAD_DIM ** -0.5

    def body(q_ref, k_ref, v_ref, o_ref):
        for h in range(N_HEADS):
            sl = slice(h * HEAD_DIM, (h + 1) * HEAD_DIM)
            s = lax.dot_general(q_ref[:, sl], k_ref[:, sl], _DN["nt"], preferred_element_type=F32) * scale
            p = _softmax_rows(s)
            o_ref[:, sl] = jnp.dot(p.astype(BF16), v_ref[:, sl], preferred_element_type=F32).astype(o_ref.dtype)

    row = pl.BlockSpec((ts, Dm), lambda i: (i, 0))
    mem = pl.BlockSpec((M, Dm), lambda i: (0, 0))
    return _tc_call(
        body, name=name, grid=(S // ts,), in_specs=[row, mem, mem], out_specs=row,
        out_shape=_sds((S, Dm), BF16), compiler_params=_cparams(("parallel",)),
    )(q, k, v)


def _attn_bwd(name, q, k, v, do, ts):
    S, Dm = q.shape
    M = k.shape[0]
    scale = HEAD_DIM ** -0.5

    def body(q_ref, k_ref, v_ref, do_ref, dq_ref, dk_ref, dv_ref):
        i = pl.program_id(0)

        @pl.when(i == 0)
        def _():
            dk_ref[...] = jnp.zeros_like(dk_ref)
            dv_ref[...] = jnp.zeros_like(dv_ref)

        for h in range(N_HEADS):
            sl = slice(h * HEAD_DIM, (h + 1) * HEAD_DIM)
            qh = q_ref[:, sl]
            kh = k_ref[:, sl]
            doh = do_ref[:, sl]
            s = lax.dot_general(qh, kh, _DN["nt"], preferred_element_type=F32) * scale
            p = _softmax_rows(s)
            dp = lax.dot_general(doh, v_ref[:, sl], _DN["nt"], preferred_element_type=F32)
            ds = p * (dp - jnp.sum(dp * p, axis=-1, keepdims=True)) * scale
            dsb = ds.astype(BF16)
            dq_ref[:, sl] = jnp.dot(dsb, kh, preferred_element_type=F32).astype(dq_ref.dtype)
            dk_ref[:, sl] += lax.dot_general(dsb, qh, _DN["tn"], preferred_element_type=F32)
            dv_ref[:, sl] += lax.dot_general(p.astype(BF16), doh, _DN["tn"], preferred_element_type=F32)

    row = pl.BlockSpec((ts, Dm), lambda i: (i, 0))
    mem = pl.BlockSpec((M, Dm), lambda i: (0, 0))
    return _tc_call(
        body, name=name, grid=(S // ts,), in_specs=[row, mem, mem, row], out_specs=[row, mem, mem],
        out_shape=[_sds((S, Dm), BF16), _sds((M, Dm), F32), _sds((M, Dm), F32)],
        compiler_params=_cparams(("arbitrary",)),
    )(q, k, v, do)


def _pool_denominators(row0, ts):
    return (row0 + lax.broadcasted_iota(jnp.int32, (ts, 1), 0) + 1).astype(F32)


def _mixer_fwd(name, proj, pool_w, pool_scale, sgu_g, ws, bias_full, ts):
    S = proj.shape[0]
    nblk = ts // SGU_BLOCK
    halo_blocks = ts // POOL_HALO

    def body(proj_ref, halo_ref, pw_ref, sc_ref, g_ref, ws_ref, b_ref, y_ref, p_ref, vn_ref, ext_ref):
        i = pl.program_id(0)
        a = proj_ref[:, 0:D_POOL]
        ext_ref[0:POOL_HALO, :] = jnp.where(i > 0, halo_ref[...], 0.0)
        ext_ref[POOL_HALO:POOL_HALO + ts, :] = a
        pos = _pool_denominators(i * ts, ts)
        for gi, w in enumerate(POOL_WINDOWS):
            cs = slice(gi * POOL_GROUP, (gi + 1) * POOL_GROUP)
            acc = a[:, cs]
            for j in range(1, w):
                acc = acc + ext_ref[POOL_HALO - j:POOL_HALO - j + ts, cs]
            pg = (acc / jnp.minimum(pos, float(w)) - a[:, cs]).astype(BF16)
            p_ref[:, cs] = pg
            ypre = jnp.dot(pg, pw_ref[gi], preferred_element_type=F32)
            y_ref[:, cs] = (ypre * sc_ref[:, cs]).astype(y_ref.dtype)

        v = proj_ref[:, D_POOL + D_SGU:D_POOL + 2 * D_SGU]
        r = lax.rsqrt(jnp.mean(v * v, axis=-1, keepdims=True) + EPS)
        vn_ref[...] = (v * r * g_ref[...]).astype(BF16)
        for n in range(nblk):
            rs = slice(n * SGU_BLOCK, (n + 1) * SGU_BLOCK)
            for h in range(N_SGU_HEADS):
                cs = slice(h * SGU_BLOCK, (h + 1) * SGU_BLOCK)
                mixed = jnp.dot(ws_ref[h], vn_ref[rs, cs], preferred_element_type=F32) + b_ref[:, cs]
                u = proj_ref[rs, D_POOL + h * SGU_BLOCK:D_POOL + (h + 1) * SGU_BLOCK]
                y_ref[rs, D_POOL + h * SGU_BLOCK:D_POOL + (h + 1) * SGU_BLOCK] = (u * mixed).astype(y_ref.dtype)

    return _tc_call(
        body, name=name, grid=(S // ts,),
        in_specs=[
            pl.BlockSpec((ts, D_POOL + 2 * D_SGU), lambda i: (i, 0)),
            pl.BlockSpec((POOL_HALO, D_POOL), lambda i: (jnp.maximum(i * halo_blocks - 1, 0), 0)),
            pl.BlockSpec((4, POOL_GROUP, POOL_GROUP), lambda i: (0, 0, 0)),
            pl.BlockSpec((1, D_POOL), lambda i: (0, 0)),
            pl.BlockSpec((1, D_SGU), lambda i: (0, 0)),
            pl.BlockSpec((N_SGU_HEADS, SGU_BLOCK, SGU_BLOCK), lambda i: (0, 0, 0)),
            pl.BlockSpec((SGU_BLOCK, D_SGU), lambda i: (0, 0)),
        ],
        out_specs=[
            pl.BlockSpec((ts, D_MODEL), lambda i: (i, 0)),
            pl.BlockSpec((ts, D_POOL), lambda i: (i, 0)),
            pl.BlockSpec((ts, D_SGU), lambda i: (i, 0)),
        ],
        out_shape=[_sds((S, D_MODEL), BF16), _sds((S, D_POOL), BF16), _sds((S, D_SGU), BF16)],
        scratch_shapes=[pltpu.VMEM((ts + POOL_HALO, D_POOL), F32)],
        compiler_params=_cparams(("parallel",)),
    )(proj, proj, pool_w, pool_scale, sgu_g, ws, bias_full)


def _mixer_bwd(name, dymix, proj, p, vn, pool_w, pool_scale, sgu_g, ws, bias_full, ts):
    S = proj.shape[0]
    nsteps = S // ts
    nblk = ts // SGU_BLOCK
    halo_blocks = ts // POOL_HALO

    def body(dy_ref, dyh_ref, u_ref, v_ref, p_ref, vn_ref, pw_ref, sc_ref, g_ref, ws_ref, b_ref,
             dproj_ref, dpw_ref, dsc_ref, dg_ref, dws_ref, db_ref,
             ext_ref, dvn_ref, acc_sc, acc_g, acc_b):
        i = pl.program_id(0)

        @pl.when(i == 0)
        def _():
            dpw_ref[...] = jnp.zeros_like(dpw_ref)
            dws_ref[...] = jnp.zeros_like(dws_ref)
            acc_sc[...] = jnp.zeros_like(acc_sc)
            acc_g[...] = jnp.zeros_like(acc_g)
            acc_b[...] = jnp.zeros_like(acc_b)

        pos = _pool_denominators(i * ts, ts)
        pos_h = _pool_denominators((i + 1) * ts, POOL_HALO)
        for gi, w in enumerate(POOL_WINDOWS):
            cs = slice(gi * POOL_GROUP, (gi + 1) * POOL_GROUP)
            pg = p_ref[:, cs]
            wg = pw_ref[gi]
            dyp = dy_ref[:, cs]
            ypre = jnp.dot(pg, wg, preferred_element_type=F32)
            acc_sc[:, cs] += _rowsum8(dyp * ypre)
            dz = (dyp * sc_ref[:, cs]).astype(BF16)
            dpw_ref[gi] += lax.dot_general(pg, dz, _DN["tn"], preferred_element_type=F32)
            dp = lax.dot_general(dz, wg, _DN["nt"], preferred_element_type=F32)
            dzh = (dyh_ref[:, cs] * sc_ref[:, cs]).astype(BF16)
            dph = lax.dot_general(dzh, wg, _DN["nt"], preferred_element_type=F32)
            ext_ref[0:ts, cs] = dp / jnp.minimum(pos, float(w))
            ext_ref[ts:ts + POOL_HALO, cs] = jnp.where(i < nsteps - 1, dph / jnp.minimum(pos_h, float(w)), 0.0)
            acc = ext_ref[0:ts, cs]
            for j in range(1, w):
                acc = acc + ext_ref[j:j + ts, cs]
            dproj_ref[:, cs] = (acc - dp).astype(dproj_ref.dtype)

        for n in range(nblk):
            rs = slice(n * SGU_BLOCK, (n + 1) * SGU_BLOCK)
            for h in range(N_SGU_HEADS):
                cs = slice(h * SGU_BLOCK, (h + 1) * SGU_BLOCK)
                vnb = vn_ref[rs, cs]
                wh = ws_ref[h]
                mixed = jnp.dot(wh, vnb, preferred_element_type=F32) + b_ref[:, cs]
                dys = dy_ref[rs, D_POOL + h * SGU_BLOCK:D_POOL + (h + 1) * SGU_BLOCK]
                dproj_ref[rs, D_POOL + h * SGU_BLOCK:D_POOL + (h + 1) * SGU_BLOCK] = (dys * mixed).astype(dproj_ref.dtype)
                dmix = dys * u_ref[rs, cs]
                acc_b[:, cs] += dmix
                dmb = dmix.astype(BF16)
                dws_ref[h] += lax.dot_general(dmb, vnb, _DN["nt"], preferred_element_type=F32)
                dvn_ref[rs, cs] = lax.dot_general(wh, dmb, _DN["tn"], preferred_element_type=F32)
        v = v_ref[...]
        r = lax.rsqrt(jnp.mean(v * v, axis=-1, keepdims=True) + EPS)
        vh = v * r
        dvn = dvn_ref[...]
        acc_g[...] += _rowsum8(dvn * vh)
        dxh = dvn * g_ref[...]
        m = jnp.mean(dxh * vh, axis=-1, keepdims=True)
        dproj_ref[:, D_POOL + D_SGU:D_POOL + 2 * D_SGU] = (r * (dxh - vh * m)).astype(dproj_ref.dtype)

        @pl.when(i == nsteps - 1)
        def _():
            dsc_ref[...] = jnp.sum(acc_sc[...], axis=0, keepdims=True)
            dg_ref[...] = jnp.sum(acc_g[...], axis=0, keepdims=True)
            t_idx = lax.broadcasted_iota(jnp.int32, (SGU_BLOCK, SGU_BLOCK), 0) // SGU_CHUNK
            s_idx = lax.broadcasted_iota(jnp.int32, (SGU_BLOCK, SGU_BLOCK), 1) // SGU_CHUNK
            mask = s_idx <= t_idx
            for h in range(N_SGU_HEADS):
                cs = slice(h * SGU_BLOCK, (h + 1) * SGU_BLOCK)
                dws_ref[h] = jnp.where(mask, dws_ref[h], 0.0)
                col = jnp.sum(acc_b[:, cs], axis=1, keepdims=True)
                db_ref[h] = jnp.broadcast_to(col, (SGU_BLOCK, SGU_BLOCK))

    const2 = lambda i: (0, 0)
    const3 = lambda i: (0, 0, 0)
    last_halo = S // POOL_HALO - 1
    return _tc_call(
        body, name=name, grid=(nsteps,),
        in_specs=[
            pl.BlockSpec((ts, D_MODEL), lambda i: (i, 0)),
            pl.BlockSpec((POOL_HALO, D_POOL), lambda i: (jnp.minimum((i + 1) * halo_blocks, last_halo), 0)),
            pl.BlockSpec((ts, D_SGU), lambda i: (i, 1)),
            pl.BlockSpec((ts, D_SGU), lambda i: (i, 2)),
            pl.BlockSpec((ts, D_POOL), lambda i: (i, 0)),
            pl.BlockSpec((ts, D_SGU), lambda i: (i, 0)),
            pl.BlockSpec((4, POOL_GROUP, POOL_GROUP), const3),
            pl.BlockSpec((1, D_POOL), const2),
            pl.BlockSpec((1, D_SGU), const2),
            pl.BlockSpec((N_SGU_HEADS, SGU_BLOCK, SGU_BLOCK), const3),
            pl.BlockSpec((SGU_BLOCK, D_SGU), const2),
        ],
        out_specs=[
            pl.BlockSpec((ts, D_POOL + 2 * D_SGU), lambda i: (i, 0)),
            pl.BlockSpec((4, POOL_GROUP, POOL_GROUP), const3),
            pl.BlockSpec((1, D_POOL), const2),
            pl.BlockSpec((1, D_SGU), const2),
            pl.BlockSpec((N_SGU_HEADS, SGU_BLOCK, SGU_BLOCK), const3),
            pl.BlockSpec((N_SGU_HEADS, SGU_BLOCK, SGU_BLOCK), const3),
        ],
        out_shape=[
            _sds((S, D_POOL + 2 * D_SGU), BF16),
            _sds((4, POOL_GROUP, POOL_GROUP), F32),
            _sds((1, D_POOL), F32),
            _sds((1, D_SGU), F32),
            _sds((N_SGU_HEADS, SGU_BLOCK, SGU_BLOCK), F32),
            _sds((N_SGU_HEADS, SGU_BLOCK, SGU_BLOCK), F32),
        ],
        scratch_shapes=[
            pltpu.VMEM((ts + POOL_HALO, D_POOL), F32),
            pltpu.VMEM((ts, D_SGU), F32),
            pltpu.VMEM((8, D_POOL), F32),
            pltpu.VMEM((8, D_SGU), F32),
            pltpu.VMEM((SGU_BLOCK, D_SGU), F32),
        ],
        compiler_params=_cparams(("arbitrary",)),
    )(dymix, dymix, proj, proj, p, vn, pool_w, pool_scale, sgu_g, ws, bias_full)


def _silu_mul(accs, extras):
    (up,) = accs
    gt = extras[0]
    sig = 1.0 / (1.0 + jnp.exp(-gt))
    return gt, up, gt * sig * up


def _silu_mul_bwd(accs, extras):
    (dact,) = accs
    gt = extras[0].astype(F32)
    up = extras[1].astype(F32)
    sig = 1.0 / (1.0 + jnp.exp(-gt))
    silu = gt * sig
    dgt = dact * up * (sig * (1.0 + gt * (1.0 - sig)))
    dup = dact * silu
    return dgt, dup


def _local_step(x, mem, target, W, sm, rs):
    S = x.shape[0]
    tm = min(1024, S)
    th = min(512, S)
    tq = min(256, S)
    ts = min(512, S)
    tr = min(512, S)
    tk_s = min(2048, S)
    tk_p = min(4096, S)
    M = mem.shape[0]

    h1 = _rms_fwd("rms_mix", x, sm["norm_mix_g"], tr)
    mb = _rms_fwd("rms_mem", mem, sm["norm_mem_g"], M)
    proj = _mm1("proj_in", h1, W["w_in_t"], "nt", tm, 1024, 2048, F32)
    ymix, p, vn = _mixer_fwd("mixer_fwd", proj, W["pool_w"], sm["pool_scale"], sm["sgu_norm_g"],
                             sm["ws_masked"], sm["bias_full"], ts)
    x1, h2 = _mm_rows("proj_out", [ymix], [W["w_out"]], [(0, 0, 0)], "nn", th, 2048, [x], [sm["norm_xattn_g"]],
                      [F32, BF16], 0, _ep_residual_norm)

    q = _mm1("proj_q", h2, W["w_q"], "nn", tm, 1024, 2048, BF16)
    kk, vv = _mm("proj_kv", [mb], [W["w_k"], W["w_v"]], [(0, 0, 0), (0, 1, 1)], "nn", M, 1024, 2048, [BF16, BF16],
                 n_acc=2)
    o = _attn_fwd("attn_fwd", q, kk, vv, tm)
    x2, h3 = _mm_rows("proj_o", [o], [W["w_o"]], [(0, 0, 0)], "nn", th, 2048, [x1], [sm["norm_ffn_g"]],
                      [F32, BF16], 0, _ep_residual_norm)

    gt32 = _mm1("ffn_gate", h3, W["w_gate_t"], "nt", tm, 1408, 2048, F32)
    gt, up, act = _mm("ffn_up", [h3], [W["w_up_t"]], [(0, 0, 0)], "nt", tm, 1408, 2048, [BF16, BF16, BF16],
                      epilogue=_silu_mul, extras=(gt32,))
    dx3, dx3b, d_final_g, loss = _mm_rows(
        "ffn_down", [act], [W["w_down"]], [(0, 0, 0)], "nn", tq, 5632, [x2, target], [sm["final_norm_g"]],
        [F32, BF16], 2, _ep_final_loss, n_scalar_out=1)

    dgt, dup = _mm("ffn_down_dgrad", [dx3b], [W["w_down"]], [(0, 0, 0)], "nt", tm, 1408, 2048, [BF16, BF16],
                   epilogue=_silu_mul_bwd, extras=(gt, up))
    rs.push("w_down", _mm1("ffn_down_wgrad", act, dx3b, "tn", 1408, 1024, tk_s, BF16))
    rs.push("w_gate_t", _mm1("ffn_gate_wgrad", dgt, h3, "tn", 1408, 1024, tk_s, BF16))
    rs.reduce("w_down")
    rs.push("w_up_t", _mm1("ffn_up_wgrad", dup, h3, "tn", 1408, 1024, tk_s, BF16))
    rs.reduce("w_gate_t")
    dh3 = _mm("ffn_gate_up_dgrad", [dgt, dup], [W["w_gate_t"], W["w_up_t"]], [(0, 0, 0), (1, 1, 0)], "nn",
              th, 512, 5632, [F32])[0]
    rs.reduce("w_up_t")
    dx2, dx2b, d_ffn_g = _rms_bwd("rms_ffn_bwd", dh3, x2, sm["norm_ffn_g"], dx3, tr)
    rs.finish("w_down")

    rs.push("w_o", _mm1("proj_o_wgrad", o, dx2b, "tn", 1024, 1024, tk_p, BF16))
    rs.finish("w_gate_t")
    do = _mm1("proj_o_dgrad", dx2b, W["w_o"], "nt", tm, 1024, 2048, BF16)
    rs.reduce("w_o")
    dq, dk, dv = _attn_bwd("attn_bwd", q, kk, vv, do, tm)
    rs.push("w_q", _mm1("proj_q_wgrad", h2, dq, "tn", 1024, 1024, tk_p, BF16))
    rs.push("w_k", _mm1("proj_k_wgrad", mb, dk, "tn", 1024, 1024, M, BF16))
    rs.push("w_v", _mm1("proj_v_wgrad", mb, dv, "tn", 1024, 1024, M, BF16))
    rs.finish("w_up_t")
    dx1, dx1b, d_xattn_g = _mm_rows(
        "proj_q_dgrad", [dq], [W["w_q"]], [(0, 0, 0)], "nt", th, 2048, [x1, dx2], [sm["norm_xattn_g"]],
        [F32, BF16], 1, _ep_norm_bwd)
    rs.reduce("w_q")
    rs.reduce("w_k")
    rs.reduce("w_v")
    dmb = _mm("proj_kv_dgrad", [dk, dv], [W["w_k"], W["w_v"]], [(0, 0, 0), (1, 1, 0)], "nt",
              M, 1024, 2048, [F32])[0]
    (d_mem_g,) = _rms_bwd("rms_mem_bwd", dmb, mem, sm["norm_mem_g"], None, M, want_dx=False)

    rs.push("w_out", _mm1("proj_out_wgrad", ymix, dx1b, "tn", 1024, 1024, tk_p, BF16))
    dymix = _mm1("proj_out_dgrad", dx1b, W["w_out"], "nt", tm, 1024, 2048, F32)
    rs.finish("w_o")
    dproj, d_pool_w, d_pool_scale, d_sgu_g, d_ws, d_b = _mixer_bwd(
        "mixer_bwd", dymix, proj, p, vn, W["pool_w"], sm["pool_scale"], sm["sgu_norm_g"],
        sm["ws_masked"], sm["bias_full"], ts)
    rs.finish("w_q")
    rs.finish("w_k")
    rs.finish("w_v")
    rs.reduce("w_out")
    rs.push("pool_w", d_pool_w.reshape(4, N_DEV, POOL_GROUP // N_DEV, POOL_GROUP).transpose(1, 0, 2, 3)
            .reshape(4 * POOL_GROUP, POOL_GROUP).astype(BF16))
    rs.small("early", dict(
        pool_scale=d_pool_scale, sgu_norm_g=d_sgu_g, w_spatial=d_ws, b_spatial=d_b[:, :, 0],
        norm_xattn_g=d_xattn_g, norm_mem_g=d_mem_g, norm_ffn_g=d_ffn_g, final_norm_g=d_final_g))
    rs.push("w_in_t", _mm1("proj_in_wgrad", dproj, h1, "tn", 1024, 1024, tk_p, BF16))
    rs.finish("w_out")
    rs.reduce("pool_w")
    rs.reduce("w_in_t")
    grad_x, d_mix_g = _mm_rows(
        "proj_in_dgrad", [dproj], [W["w_in_t"]], [(0, 0, 0)], "nn", tq, 3072, [x, dx1], [sm["norm_mix_g"]],
        [F32], 1, _ep_norm_bwd)
    rs.small("late", dict(norm_mix_g=d_mix_g, loss=jnp.pad(loss, ((0, 0), (0, _LANES - 1)))))
    rs.finish_small("early")
    rs.finish("pool_w")
    rs.finish("w_in_t")
    rs.finish_small("late")
    return loss, grad_x


def _mesh_pos():
    return lax.axis_index("x"), lax.axis_index("y"), lax.axis_index("c")


def _handshake(peers):
    barrier = pltpu.get_barrier_semaphore()
    for peer in peers:
        pl.semaphore_signal(barrier, inc=1, device_id=peer, device_id_type=MESH)
    pl.semaphore_wait(barrier, len(peers))


def _seq_all_gather(name, shards, collective_id):
    n = len(shards)

    def body(*refs):
        ins = refs[:n]
        outs = refs[n:2 * n]
        send_sems, recv_sems, local_sems = refs[2 * n:]
        x, y, c = _mesh_pos()
        me, sibling = (x, y, c), (x, y, 1 - c)
        xn, yn, dg = (1 - x, y), (x, 1 - y), (1 - x, 1 - y)
        north = c == 1
        via = (jnp.where(north, xn[0], yn[0]), jnp.where(north, xn[1], yn[1]))
        to = (jnp.where(north, yn[0], xn[0]), jnp.where(north, yn[1], xn[1]))
        _handshake([sibling, (*xn, c), (*yn, c)])

        def copy(a, k, block, target, src=None):
            bx, by, bc = block
            dst = outs[a].at[4 * bx + 2 * by + bc]
            return pltpu.make_async_remote_copy(
                src_ref=dst if src is None else src, dst_ref=dst,
                send_sem=send_sems.at[a, k], recv_sem=recv_sems.at[a, k],
                device_id=target, device_id_type=MESH)

        mine = [pltpu.make_async_copy(ins[a], outs[a].at[4 * x + 2 * y + c], local_sems.at[a]) for a in range(n)]
        for cp in mine:
            cp.start()
        started = []
        for a in range(n):
            first = [copy(a, 0, me, sibling, src=ins[a]), copy(a, 1, me, (*xn, c), src=ins[a]),
                     copy(a, 2, me, (*yn, c), src=ins[a])]
            for cp in first:
                cp.start()
            started += first
        for a in range(n):
            copy(a, 1, (*xn, c), me).wait_recv()
            copy(a, 2, (*yn, c), me).wait_recv()
            second = [copy(a, 3, (*via, c), (*to, c)), copy(a, 4, (*xn, c), sibling), copy(a, 5, (*yn, c), sibling)]
            for cp in second:
                cp.start()
            started += second
        for a in range(n):
            copy(a, 3, (*dg, c), me).wait_recv()
            last = copy(a, 6, (*dg, c), sibling)
            last.start()
            started.append(last)
        for a in range(n):
            copy(a, 0, sibling, me).wait_recv()
            for k, chip in ((4, xn), (5, yn), (6, dg)):
                copy(a, k, (*chip, 1 - c), me).wait_recv()
        for cp in started:
            cp.wait_send()
        for cp in mine:
            cp.wait()

    return _sc_call(
        body, name=name,
        out_type=[_sds((N_DEV,) + s.shape, s.dtype) for s in shards],
        scratch_types=[pltpu.SemaphoreType.DMA((n, 7)), pltpu.SemaphoreType.DMA((n, 7)),
                       pltpu.SemaphoreType.DMA((n,))],
        compiler_params=pltpu.CompilerParams(collective_id=collective_id),
    )(*shards)


def _seq_pair_exchange(name, gview, collective_id):
    def body(g_ref, theirs_ref, send_sems, recv_sems):
        x, y, c = _mesh_pos()
        sibling = (x, y, 1 - c)
        _handshake([sibling])
        copies = [pltpu.make_async_remote_copy(
            src_ref=g_ref.at[k, 1 - c], dst_ref=theirs_ref.at[k],
            send_sem=send_sems.at[k], recv_sem=recv_sems.at[k],
            device_id=sibling, device_id_type=MESH) for k in range(4)]
        for cp in copies:
            cp.start()
        for cp in copies:
            cp.wait()

    return _sc_call(
        body, name=name, out_type=_sds((4,) + gview.shape[2:], gview.dtype),
        scratch_types=[pltpu.SemaphoreType.DMA((4,)), pltpu.SemaphoreType.DMA((4,))],
        compiler_params=pltpu.CompilerParams(collective_id=collective_id),
    )(gview)


def _pair_sum(name, gview, theirs, pos, tr):
    _, _, r, C = gview.shape

    def body(pos_ref, a_ref, b_ref, o_ref):
        o_ref[...] = (a_ref[...].astype(F32) + b_ref[...].astype(F32)).astype(o_ref.dtype)

    grid_spec = pltpu.PrefetchScalarGridSpec(
        num_scalar_prefetch=1, grid=(4, r // tr),
        in_specs=[pl.BlockSpec((None, None, tr, C), lambda k, t, pos_ref: (k, pos_ref[0], t, 0)),
                  pl.BlockSpec((None, tr, C), lambda k, t, pos_ref: (k, t, 0))],
        out_specs=pl.BlockSpec((None, tr, C), lambda k, t, pos_ref: (k, t, 0)))
    return _tc_call(
        body, name=name, grid_spec=grid_spec, out_shape=_sds(theirs.shape, theirs.dtype),
        compiler_params=_cparams(("parallel", "parallel")),
    )(pos, gview, theirs)


def _seq_chip_exchange(name, pair, collective_id):
    def body(p_ref, land_ref, send_sems, recv_sems):
        x, y, c = _mesh_pos()
        my_chip = 2 * x + y
        chips = [(1 - x, y), (x, 1 - y), (1 - x, 1 - y)]
        _handshake([(cx, cy, c) for cx, cy in chips])
        copies = [pltpu.make_async_remote_copy(
            src_ref=p_ref.at[2 * cx + cy], dst_ref=land_ref.at[my_chip],
            send_sem=send_sems.at[j], recv_sem=recv_sems.at[j],
            device_id=(cx, cy, c), device_id_type=MESH) for j, (cx, cy) in enumerate(chips)]
        for cp in copies:
            cp.start()
        for cp in copies:
            cp.wait_send()
        for j, (cx, cy) in enumerate(chips):
            pltpu.make_async_remote_copy(
                src_ref=p_ref.at[my_chip], dst_ref=land_ref.at[2 * cx + cy],
                send_sem=send_sems.at[j], recv_sem=recv_sems.at[j],
                device_id=(cx, cy, c), device_id_type=MESH).wait_recv()

    return _sc_call(
        body, name=name, out_type=_sds(pair.shape, pair.dtype),
        scratch_types=[pltpu.SemaphoreType.DMA((3,)), pltpu.SemaphoreType.DMA((3,))],
        compiler_params=pltpu.CompilerParams(collective_id=collective_id),
    )(pair)


def _sum_leading(name, parts, tr, out_dtype=F32):
    n, r, C = parts.shape

    def body(p_ref, o_ref):
        acc = p_ref[0].astype(F32)
        for k in range(1, n):
            acc = acc + p_ref[k].astype(F32)
        o_ref[...] = acc.astype(o_ref.dtype)

    return _tc_call(
        body, name=name, grid=(r // tr,),
        in_specs=[pl.BlockSpec((n, tr, C), lambda t: (0, t, 0))],
        out_specs=pl.BlockSpec((tr, C), lambda t: (t, 0)),
        out_shape=_sds((r, C), out_dtype), compiler_params=_cparams(("parallel",)),
    )(parts)


def _row_tile(r):
    for t in (512, 384, 352, 256, 128, 64, 32, 16, 8):
        if r % t == 0:
            return t
    return r


def _adamw_math(w, g, m, v):
    c1 = 1.0 - ADAM_B1 ** ADAM_STEP
    c2 = 1.0 - ADAM_B2 ** ADAM_STEP
    nm = ADAM_B1 * m + (1.0 - ADAM_B1) * g
    nv = ADAM_B2 * v + (1.0 - ADAM_B2) * (g * g)
    m_hat = nm / c1
    v_hat = nv / c2
    return -ADAM_LR * (m_hat / (jnp.sqrt(v_hat) + ADAM_EPS) + ADAM_WD * w), nm, nv


def _chip_sum_adamw(name, pair, landed, pos, w, m, v, transposed):
    _, r, C = pair.shape
    if transposed:
        tr, tc = r, 512
        r_pad = -r % _LANES
        wspec = pl.BlockSpec((tc, r), lambda t, k, pos_ref: (t, 0))
        shape = (C, r)
        scratch = [pltpu.VMEM((tr, tc), F32), pltpu.VMEM((tc, r + r_pad), F32)]
    else:
        tr, tc = _row_tile(r), C
        wspec = pl.BlockSpec((tr, C), lambda t, k, pos_ref: (t, 0))
        shape = (r, C)
        scratch = [pltpu.VMEM((tr, tc), F32)]
    n_t = (C // tc) if transposed else (r // tr)

    def block(chip, t):
        return (chip, 0, t) if transposed else (chip, t, 0)

    def body(pos_ref, own_ref, land_ref, w_ref, m_ref, v_ref, g_ref, d_ref, nm_ref, nv_ref, acc_ref, *turn):
        k = pl.program_id(1)
        val = jnp.where(k == pos_ref[1], own_ref[...], land_ref[...]).astype(F32)

        @pl.when(k == 0)
        def _():
            acc_ref[...] = val

        @pl.when(k > 0)
        def _():
            acc_ref[...] += val

        @pl.when(k == 3)
        def _():
            if transposed:
                g_t = acc_ref[...]
                if r_pad:
                    g_t = jnp.concatenate([g_t, jnp.zeros((r_pad, tc), F32)], axis=0)
                turn[0][...] = g_t.T
                g = turn[0][:, 0:r]
            else:
                g = acc_ref[...]
            d, nm, nv = _adamw_math(w_ref[...], g, m_ref[...], v_ref[...])
            g_ref[...] = g
            d_ref[...] = d
            nm_ref[...] = nm
            nv_ref[...] = nv

    def land_index(t, k, pos_ref):
        return block(jnp.where(k == pos_ref[1], (k + 1) % 4, k), t)

    grid_spec = pltpu.PrefetchScalarGridSpec(
        num_scalar_prefetch=1, grid=(n_t, 4),
        in_specs=[pl.BlockSpec((None, tr, tc), lambda t, k, pos_ref: block(pos_ref[1], t)),
                  pl.BlockSpec((None, tr, tc), land_index), wspec, wspec, wspec],
        out_specs=[wspec] * 4, scratch_shapes=scratch)
    return _tc_call(
        body, name=name, grid_spec=grid_spec, out_shape=[_sds(shape, F32)] * 4,
        compiler_params=_cparams(("parallel", "arbitrary")),
    )(pos, pair, landed, w, m, v)


def _adamw(name, w, g, m, v):
    R, C = w.shape
    tr = _row_tile(R)

    def body(w_ref, g_ref, m_ref, v_ref, d_ref, nm_ref, nv_ref):
        d_ref[...], nm_ref[...], nv_ref[...] = _adamw_math(w_ref[...], g_ref[...], m_ref[...], v_ref[...])

    spec = pl.BlockSpec((tr, C), lambda i: (i, 0))
    return _tc_call(
        body, name=name, grid=(R // tr,), in_specs=[spec] * 4, out_specs=[spec] * 3,
        out_shape=[_sds((R, C), F32)] * 3, compiler_params=_cparams(("parallel",)),
    )(w, g, m, v)


_BIG = ("w_in_t", "w_out", "w_q", "w_k", "w_v", "w_o", "w_gate_t", "w_up_t", "w_down")
_SMALL = ("norm_mix_g", "pool_scale", "sgu_norm_g", "w_spatial", "b_spatial", "norm_xattn_g",
          "norm_mem_g", "norm_ffn_g", "final_norm_g")
_LANES = 128
_GATHER_GROUPS = (("w_in_t", "pool_w"), ("w_out",), ("w_q",), ("w_k", "w_v"), ("w_o",), ("w_gate_t",),
                  ("w_up_t",), ("w_down",))
_SMALL_GROUPS = dict(
    early=("pool_scale", "sgu_norm_g", "w_spatial", "b_spatial", "norm_xattn_g", "norm_mem_g",
           "norm_ffn_g", "final_norm_g"),
    late=("norm_mix_g", "loss"))
_TURN_OUTSIDE = ("w_gate_t", "w_up_t")
_ID_GATHER, _ID_PAIR, _ID_CHIP = 0, 1, 2


_PACK_ROWS = 512


def _pack(parts):
    rows = [p.reshape(-1, _LANES) for p in parts]
    n = sum(r.shape[0] for r in rows)
    pad = -n % (_PACK_ROWS if n > _PACK_ROWS else 8)
    if pad:
        rows.append(jnp.zeros((pad, _LANES), rows[0].dtype))
    return jnp.concatenate(rows, axis=0)


class _GradReducer:
    def __init__(self, pos, apply, apply_small):
        self.pos, self.apply, self.apply_small = pos, apply, apply_small
        self.view, self.theirs, self.pair, self.landed = {}, {}, {}, {}
        self.small_gathered = {}

    def push(self, k, g):
        r = g.shape[0] // N_DEV
        self.view[k] = g.reshape(4, 2, r, g.shape[1])
        self.theirs[k] = _seq_pair_exchange("grad_pair_exchange_" + k, self.view[k], _ID_PAIR)

    def reduce(self, k):
        r = self.view[k].shape[2]
        self.pair[k] = _pair_sum("grad_pair_sum_" + k, self.view[k], self.theirs[k], self.pos, r)
        self.landed[k] = _seq_chip_exchange("grad_chip_exchange_" + k, self.pair[k], _ID_CHIP)

    def finish(self, k):
        self.apply(k, self.pair[k], self.landed[k])

    def small(self, tag, parts):
        packed = _pack([parts[k] for k in _SMALL_GROUPS[tag]])
        (self.small_gathered[tag],) = _seq_all_gather("gather_small_grads_" + tag, [packed], _ID_GATHER)

    def finish_small(self, tag):
        allp = self.small_gathered[tag]
        self.apply_small(tag, _sum_leading("sum_small_grads_" + tag, allp, min(_PACK_ROWS, allp.shape[1])))


def _unpack(packed, like):
    out, row = [], 0
    for ref in like:
        rows = ref.size // _LANES
        out.append(packed[row:row + rows].reshape(ref.shape))
        row += rows
    return out


def kernel(x, mem, norm_mix_g, w_in, pool_w, pool_scale, sgu_norm_g, w_spatial, b_spatial, w_out, norm_xattn_g, norm_mem_g, w_q, w_k, w_v, w_o, norm_ffn_g, w_gate, w_up, w_down, final_norm_g, loss_target, m_norm_mix_g, m_w_in, m_pool_w, m_pool_scale, m_sgu_norm_g, m_w_spatial, m_b_spatial, m_w_out, m_norm_xattn_g, m_norm_mem_g, m_w_q, m_w_k, m_w_v, m_w_o, m_norm_ffn_g, m_w_gate, m_w_up, m_w_down, m_final_norm_g, v_norm_mix_g, v_w_in, v_pool_w, v_pool_scale, v_sgu_norm_g, v_w_spatial, v_b_spatial, v_w_out, v_norm_xattn_g, v_norm_mem_g, v_w_q, v_w_k, v_w_v, v_w_o, v_norm_ffn_g, v_w_gate, v_w_up, v_w_down, v_final_norm_g):
    args = dict(locals())
    names = ("norm_mix_g", "w_in", "pool_w", "pool_scale", "sgu_norm_g", "w_spatial", "b_spatial", "w_out",
             "norm_xattn_g", "norm_mem_g", "w_q", "w_k", "w_v", "w_o", "norm_ffn_g", "w_gate", "w_up",
             "w_down", "final_norm_g")
    w = {k: args[k] for k in names}
    m = {k: args["m_" + k] for k in names}
    v = {k: args["v_" + k] for k in names}
    _CHAIN.__init__()

    shards = dict(
        w_in_t=w["w_in"][0].T, w_out=w["w_out"][0], w_q=w["w_q"][0], w_k=w["w_k"][0], w_v=w["w_v"][0],
        w_o=w["w_o"][0], w_gate_t=w["w_gate"][0].T, w_up_t=w["w_up"][0].T, w_down=w["w_down"][0])
    send = {k: shards[k].astype(BF16) for k in _BIG}
    send["pool_w"] = w["pool_w"][0].reshape(4 * 32, POOL_GROUP).astype(BF16)
    W = {}
    for gi, group in enumerate(_GATHER_GROUPS):
        gathered = _seq_all_gather("gather_weights_%d" % gi, [send[k] for k in group], _ID_GATHER)
        for k, g in zip(group, gathered):
            W[k] = g.reshape(-1, g.shape[-1])
    W["pool_w"] = W["pool_w"].reshape(N_DEV, 4, 32, POOL_GROUP).transpose(1, 0, 2, 3).reshape(4, POOL_GROUP, POOL_GROUP)

    t = jnp.arange(SGU_BLOCK)
    mask = (t[None, :] // SGU_CHUNK) <= (t[:, None] // SGU_CHUNK)
    sm = dict(
        norm_mix_g=w["norm_mix_g"], pool_scale=w["pool_scale"], sgu_norm_g=w["sgu_norm_g"],
        norm_xattn_g=w["norm_xattn_g"], norm_mem_g=w["norm_mem_g"], norm_ffn_g=w["norm_ffn_g"],
        final_norm_g=w["final_norm_g"].reshape(1, D_MODEL),
        ws_masked=jnp.where(mask[None], w["w_spatial"][0], 0.0).astype(BF16),
        bias_full=jnp.repeat(w["b_spatial"][0].T, SGU_BLOCK, axis=1))

    natural = dict(w_in_t="w_in", w_gate_t="w_gate", w_up_t="w_up")
    grads, delta, new_m, new_v = {}, {}, {}, {}

    def apply(k, pair, landed):
        name = natural.get(k, k)
        if k == "pool_w":
            flat = (4 * POOL_GROUP // N_DEV, POOL_GROUP)
            res = _chip_sum_adamw("grad_finish_" + k, pair, landed, pos, w[k].reshape(flat), m[k].reshape(flat),
                                  v[k].reshape(flat), False)
            grads[k], delta[k], new_m[k], new_v[k] = (a.reshape(w[k].shape) for a in res)
            return
        if k in _TURN_OUTSIDE:
            res = _chip_sum_adamw("grad_finish_" + k, pair, landed, pos, w[name][0].T, m[name][0].T, v[name][0].T,
                                  False)
            res = [a.T for a in res]
        else:
            res = _chip_sum_adamw("grad_finish_" + k, pair, landed, pos, w[name][0], m[name][0], v[name][0],
                                  k in natural)
        grads[name], delta[name], new_m[name], new_v[name] = (a[None] for a in res)

    like = dict(w)
    like["loss"] = _sds((1, _LANES), F32)

    def apply_small(tag, total):
        group = _SMALL_GROUPS[tag]
        grads.update(zip(group, _unpack(total, [like[k] for k in group])))
        if tag == "late":
            d_, m_, v_ = _adamw("adamw_small", _pack([w[k] for k in _SMALL]), _pack([grads[k] for k in _SMALL]),
                                _pack([m[k] for k in _SMALL]), _pack([v[k] for k in _SMALL]))
            shapes = [w[k] for k in _SMALL]
            for k, a, b, c_ in zip(_SMALL, _unpack(d_, shapes), _unpack(m_, shapes), _unpack(v_, shapes)):
                delta[k], new_m[k], new_v[k] = a, b, c_

    pos = jnp.stack([lax.axis_index("c"), 2 * lax.axis_index("x") + lax.axis_index("y")]).astype(jnp.int32)
    rs = _GradReducer(pos, apply, apply_small)
    _, grad_x = _local_step(x[0], mem[0], loss_target[0], W, sm, rs)

    outs = [grads["loss"][0, 0], grad_x[None]]
    outs += [grads[k].reshape(w[k].shape) for k in names]
    outs += [delta[k] for k in names]
    outs += [new_m[k] for k in names]
    outs += [new_v[k] for k in names]
    return tuple(outs)
```
